```python
import math
import jax
import jax.numpy as jnp
from jax import lax
import numpy as np

D_MODEL = 2048
BATCH = 4
SEQ = 4096
DEPTH = 2

GRID_W = 64
CTX_LEN = 256
N_MIXERS = 4
D_MIX = D_MODEL
W_GRP = D_MIX // N_MIXERS

S5_P = 16
S5_G = W_GRP // S5_P
S5_N = 64

POOL_WINDOWS = (2, 4, 8, 16)
POOL_C = W_GRP // len(POOL_WINDOWS)

NA_HD = 64
NA_H = W_GRP // NA_HD
NA_WIN_R = 8
NA_WIN_C = 16

SSM_HD = 64
SSM_H = W_GRP // SSM_HD
SSM_G = 2
SSM_N = 128
SSM_CONV = 4
SSM_CHUNK = 128
SSM_CONV_CH = W_GRP + 2 * SSM_G * SSM_N

PEER_HEADS = 8
PEER_NK = 128
PEER_NE = PEER_NK * PEER_NK
PEER_QDIM = 256
PEER_TOPK = 16
PEER_BLOCK = 128

IN_SPLITS = (W_GRP, W_GRP, W_GRP, W_GRP, W_GRP, W_GRP, SSM_CONV_CH, 2 * SSM_H)
D_IN = sum(IN_SPLITS)

DEEPNORM_ALPHA = (2 * DEPTH) ** 0.25
DEEPNORM_BETA = (8 * DEPTH) ** -0.25
LN_EPS = 1e-5
F32 = jnp.float32

kernel_name = "hybrid_s5_pool_natten_ssd_peer_dit"


def _flip(t, rev):
    return t[:, ::-1] if rev else t


def modulate(x, shift, scale):
    return x * (1 + scale) + shift


def layer_norm(x, g, b):
    xf = x.astype(F32)
    mu = jnp.mean(xf, -1, keepdims=True)
    var = jnp.mean(jnp.square(xf - mu), -1, keepdims=True)
    return ((xf - mu) * lax.rsqrt(var + LN_EPS)).astype(x.dtype) * g + b


def s5_discretize(a_re, a_im, log_dt, b_re, b_im):
    a_re, a_im = a_re.astype(F32), a_im.astype(F32)
    b_re, b_im = b_re.astype(F32), b_im.astype(F32)
    dt = jnp.exp(log_dt.astype(F32))[:, None]
    mag = jnp.exp(a_re * dt)
    ab_re = mag * jnp.cos(a_im * dt)
    ab_im = mag * jnp.sin(a_im * dt)
    den = a_re * a_re + a_im * a_im
    f_re = ((ab_re - 1) * a_re + ab_im * a_im) / den
    f_im = (ab_im * a_re - (ab_re - 1) * a_im) / den
    bb_re = f_re[..., None] * b_re - f_im[..., None] * b_im
    bb_im = f_re[..., None] * b_im + f_im[..., None] * b_re
    return ab_re, ab_im, bb_re, bb_im


def _cscan_combine(e1, e2):
    a1r, a1i, b1r, b1i = e1
    a2r, a2i, b2r, b2i = e2
    return (a2r * a1r - a2i * a1i, a2r * a1i + a2i * a1r,
            a2r * b1r - a2i * b1i + b2r, a2r * b1i + a2i * b1r + b2i)


def s5_states(u, ab_re, ab_im, bb_re, bb_im, h0_re, h0_im):
    bu_re = jnp.einsum('gnp,blgp->blgn', bb_re, u)
    bu_im = jnp.einsum('gnp,blgp->blgn', bb_im, u)
    bu_re = bu_re.at[:, 0].add(ab_re * h0_re - ab_im * h0_im)
    bu_im = bu_im.at[:, 0].add(ab_re * h0_im + ab_im * h0_re)
    a_re = jnp.broadcast_to(ab_re, bu_re.shape)
    a_im = jnp.broadcast_to(ab_im, bu_im.shape)
    _, _, h_re, h_im = lax.associative_scan(_cscan_combine, (a_re, a_im, bu_re, bu_im), axis=1)
    return h_re, h_im


def s5_readout(c_re, c_im, h_re, h_im):
    return (jnp.einsum('gpn,blgn->blgp', c_re.astype(F32), h_re)
            - jnp.einsum('gpn,blgn->blgp', c_im.astype(F32), h_im))


def s5_glu(y, w_glu, b_glu, dtype):
    B_, L = y.shape[:2]
    g = jax.nn.gelu(y.reshape(B_, L, W_GRP))
    return (g * jax.nn.sigmoid(g @ w_glu.astype(F32) + b_glu.astype(F32))).astype(dtype)


def s5_mixer(u_lat, u_ctx, a_re, a_im, log_dt, b_re, b_im, c_re, c_im, d, w_glu, b_glu, with_ctx_out):
    B_ = u_lat.shape[0]
    ul = u_lat.astype(F32).reshape(B_, -1, S5_G, S5_P)
    uc = u_ctx.astype(F32).reshape(B_, -1, S5_G, S5_P)
    dg = d.astype(F32).reshape(S5_G, S5_P)
    zero = jnp.zeros((B_, S5_G, S5_N), F32)
    y_lat = dg * ul
    y_ctx = dg * uc if with_ctx_out else None
    for di in range(2):
        rev = di == 1
        ab_re, ab_im, bb_re, bb_im = s5_discretize(a_re[di], a_im[di], log_dt[di], b_re[di], b_im[di])
        hc_re, hc_im = s5_states(_flip(uc, rev), ab_re, ab_im, bb_re, bb_im, zero, zero)
        hl_re, hl_im = s5_states(_flip(ul, rev), ab_re, ab_im, bb_re, bb_im, hc_re[:, -1], hc_im[:, -1])
        y_lat = y_lat + _flip(s5_readout(c_re[di], c_im[di], hl_re, hl_im), rev)
        if with_ctx_out:
            y_ctx = y_ctx + _flip(s5_readout(c_re[di], c_im[di], hc_re, hc_im), rev)
    out_lat = s5_glu(y_lat, w_glu, b_glu, u_lat.dtype)
    out_ctx = s5_glu(y_ctx, w_glu, b_glu, u_ctx.dtype) if with_ctx_out else None
    return out_lat, out_ctx


def pool_mix(u, pool_w, pool_scale):
    B_, L, _ = u.shape
    uf = u.astype(F32)
    csum = jnp.pad(jnp.cumsum(uf, axis=1), ((0, 0), (1, 0), (0, 0)))
    t = jnp.arange(L)
    outs = []
    for j, w in enumerate(POOL_WINDOWS):
        lo = jnp.clip(t - w // 2, 0, L - 1)
        hi = jnp.clip(t - w // 2 + w - 1, 0, L - 1)
        ch = slice(j * POOL_C, (j + 1) * POOL_C)
        cnt = (hi - lo + 1).astype(F32)[None, :, None]
        outs.append((csum[:, hi + 1, ch] - csum[:, lo, ch]) / cnt - uf[:, :, ch])
    pooled = jnp.stack(outs, axis=2)
    y = jnp.einsum('blgc,gcd->blgd', pooled, pool_w.astype(F32)).reshape(B_, L, W_GRP)
    return (y * pool_scale.astype(F32)).astype(u.dtype)


def na_latent(q, k, v, k_ctx, v_ctx, rpb):
    B_, L, _ = q.shape
    R = L // GRID_W
    KR = min(NA_WIN_R, R)
    KC = NA_WIN_C
    shp = (B_, R, GRID_W, NA_H, NA_HD)
    q, k, v = q.reshape(shp), k.reshape(shp), v.reshape(shp)
    r = jnp.arange(R)
    row_idx = jnp.clip(r - KR // 2, 0, R - KR)[:, None] + jnp.arange(KR)[None, :]
    k_rows = k[:, row_idx]
    v_rows = v[:, row_idx]
    col = jnp.arange(GRID_W)
    c0 = jnp.clip(col - KC // 2, 0, GRID_W - KC)
    in_win = (col[None, :] >= c0[:, None]) & (col[None, :] < c0[:, None] + KC)
    col_rel = jnp.clip(col[None, :] - col[:, None], -(KC - 1), KC - 1) + KC - 1
    row_rel = row_idx - r[:, None] + NA_WIN_R - 1
    bias = rpb.astype(F32)[:, row_rel][..., col_rel]
    bias = bias.transpose(1, 0, 3, 2, 4)
    scale = NA_HD ** -0.5
    s_nb = jnp.einsum('brqhd,brkwhd->brhqkw', q, k_rows).astype(F32) * scale + bias
    s_nb = jnp.where(in_win[:, None, :], s_nb, -jnp.inf).reshape(B_, R, NA_H, GRID_W, KR * GRID_W)
    s_cx = jnp.einsum('brqhd,bchd->brhqc', q, k_ctx).astype(F32) * scale
    p = jax.nn.softmax(jnp.concatenate([s_nb, s_cx], axis=-1), axis=-1).astype(v.dtype)
    p_nb = p[..., :KR * GRID_W].reshape(B_, R, NA_H, GRID_W, KR, GRID_W)
    p_cx = p[..., KR * GRID_W:]
    o = (jnp.einsum('brhqkw,brkwhd->brqhd', p_nb, v_rows)
         + jnp.einsum('brhqc,bchd->brqhd', p_cx, v_ctx))
    return o.reshape(B_, L, W_GRP)


def ctx_attn(q, k, v):
    s = jnp.einsum('bqhd,bkhd->bhqk', q, k).astype(F32) * NA_HD ** -0.5
    p = jax.nn.softmax(s, axis=-1).astype(v.dtype)
    return jnp.einsum('bhqk,bkhd->bqhd', p, v)


def na_mixer(q_l, k_l, v_l, q_c, k_c, v_c, rpb, with_ctx_out):
    B_, C_ = k_c.shape[:2]
    kc = k_c.reshape(B_, C_, NA_H, NA_HD)
    vc = v_c.reshape(B_, C_, NA_H, NA_HD)
    out_l = na_latent(q_l, k_l, v_l, kc, vc, rpb)
    out_c = (ctx_attn(q_c.reshape(B_, C_, NA_H, NA_HD), kc, vc).reshape(B_, C_, W_GRP)
             if with_ctx_out else None)
    return out_l, out_c


def dwconv(x, w, b):
    y = lax.conv_general_dilated(x, w[:, None, :].astype(x.dtype), window_strides=(1,),
                                 padding=[((SSM_CONV - 1) // 2, SSM_CONV // 2)],
                                 dimension_numbers=('NWC', 'WIO', 'NWC'),
                                 feature_group_count=x.shape[-1])
    return y + b


def segsum(x):
    T = x.shape[-1]
    cs = jnp.cumsum(x, axis=-1)
    d = cs[..., :, None] - cs[..., None, :]
    return jnp.where(jnp.tril(jnp.ones((T, T), bool)), d, -jnp.inf)


def ssd(X, A, Bm, Cm, h0):
    b, L, H, P = X.shape
    N = Bm.shape[-1]
    nc = L // SSM_CHUNK
    X = X.reshape(b, nc, SSM_CHUNK, H, P)
    Bm = Bm.reshape(b, nc, SSM_CHUNK, H, N)
    Cm = Cm.reshape(b, nc, SSM_CHUNK, H, N)
    A = A.reshape(b, nc, SSM_CHUNK, H).transpose(0, 3, 1, 2)
    A_cs = jnp.cumsum(A, axis=-1)
    cb = jnp.einsum('bclhn,bcshn->bhcls', Cm, Bm)
    y_diag = jnp.einsum('bhcls,bcshp->bclhp', cb * jnp.exp(segsum(A)), X)
    decay_states = jnp.exp(A_cs[..., -1:] - A_cs).transpose(0, 2, 3, 1)[..., None]
    states = jnp.einsum('bclhn,bclhp->bchpn', Bm, X * decay_states)
    states = jnp.concatenate([h0[:, None], states], axis=1)
    chunk_decay = jnp.exp(segsum(jnp.pad(A_cs[..., -1], ((0, 0), (0, 0), (1, 0)))))
    new_states = jnp.einsum('bhzc,bchpn->bzhpn', chunk_decay, states)
    prev, final = new_states[:, :-1], new_states[:, -1]
    y_off = (jnp.einsum('bclhn,bchpn->bclhp', Cm, prev)
             * jnp.exp(A_cs).transpose(0, 2, 3, 1)[..., None])
    return (y_diag + y_off).reshape(b, L, H, P), final


def ssm_prepare(pxbc, pdt, conv_w, conv_b, dt_bias):
    B_, L, _ = pxbc.shape
    xbc = jax.nn.silu(dwconv(pxbc, conv_w, conv_b).astype(F32))
    hpg = SSM_H // SSM_G
    xs = xbc[..., :W_GRP].reshape(B_, L, SSM_H, SSM_HD)
    bm = jnp.repeat(xbc[..., W_GRP:W_GRP + SSM_G * SSM_N].reshape(B_, L, SSM_G, SSM_N), hpg, axis=2)
    cm = jnp.repeat(xbc[..., W_GRP + SSM_G * SSM_N:].reshape(B_, L, SSM_G, SSM_N), hpg, axis=2)
    dt = jax.nn.softplus(pdt.astype(F32).reshape(B_, L, 2, SSM_H) + dt_bias.astype(F32))
    return xs, bm, cm, dt


def ssm_direction(xs, bm, cm, dt, a, h0, rev):
    y, h_fin = ssd(_flip(xs * dt[..., None], rev), _flip(dt * a, rev), _flip(bm, rev), _flip(cm, rev), h0)
    return _flip(y, rev), h_fin


def ssm_gate_norm(y, z, norm_w):
    B_, L = z.shape[:2]
    g = (y.reshape(B_, L, W_GRP) * jax.nn.silu(z.astype(F32))).reshape(B_, L, SSM_G, W_GRP // SSM_G)
    g = g * lax.rsqrt(jnp.mean(jnp.square(g), -1, keepdims=True) + LN_EPS)
    return (g.reshape(B_, L, W_GRP) * norm_w.astype(F32)).astype(z.dtype)


def ssm_mixer(z_l, xbc_l, dtr_l, z_c, xbc_c, dtr_c, conv_w, conv_b, dt_bias, a_log, d, norm_w, with_ctx_out):
    xs_l, bm_l, cm_l, dt_l = ssm_prepare(xbc_l, dtr_l, conv_w, conv_b, dt_bias)
    xs_c, bm_c, cm_c, dt_c = ssm_prepare(xbc_c, dtr_c, conv_w, conv_b, dt_bias)
    a = -jnp.exp(a_log.astype(F32))
    dh = d.astype(F32)[:, None]
    zero = jnp.zeros((xs_c.shape[0], SSM_H, SSM_HD, SSM_N), F32)
    y_l = dh * xs_l
    y_c = dh * xs_c if with_ctx_out else None
    for di in range(2):
        rev = di == 1
        yc, hc = ssm_direction(xs_c, bm_c, cm_c, dt_c[:, :, di], a[di], zero, rev)
        yl, _ = ssm_direction(xs_l, bm_l, cm_l, dt_l[:, :, di], a[di], hc, rev)
        y_l = y_l + yl
        if with_ctx_out:
            y_c = y_c + yc
    out_l = ssm_gate_norm(y_l, z_l, norm_w)
    out_c = ssm_gate_norm(y_c, z_c, norm_w) if with_ctx_out else None
    return out_l, out_c


def split_in(p):
    parts, o = [], 0
    for s in IN_SPLITS:
        parts.append(p[..., o:o + s])
        o += s
    return parts


def hybrid_mixer(p_lat, p_ctx, s5_a_re, s5_a_im, s5_log_dt, s5_b_re, s5_b_im, s5_c_re, s5_c_im,
                 s5_d, s5_w_glu, s5_b_glu, pool_w, pool_scale, na_rpb, ssm_conv_w, ssm_conv_b,
                 ssm_dt_bias, ssm_a_log, ssm_d, ssm_norm_w, with_ctx_out):
    s5_l, pool_l, q_l, k_l, v_l, z_l, xbc_l, dt_l = split_in(p_lat)
    s5_c, pool_c, q_c, k_c, v_c, z_c, xbc_c, dt_c = split_in(p_ctx)
    ya_l, ya_c = s5_mixer(s5_l, s5_c, s5_a_re, s5_a_im, s5_log_dt, s5_b_re, s5_b_im, s5_c_re, s5_c_im,
                          s5_d, s5_w_glu, s5_b_glu, with_ctx_out)
    yb_l = pool_mix(pool_l, pool_w, pool_scale)
    yb_c = pool_mix(pool_c, pool_w, pool_scale) if with_ctx_out else None
    yc_l, yc_c = na_mixer(q_l, k_l, v_l, q_c, k_c, v_c, na_rpb, with_ctx_out)
    yd_l, yd_c = ssm_mixer(z_l, xbc_l, dt_l, z_c, xbc_c, dt_c, ssm_conv_w, ssm_conv_b, ssm_dt_bias,
                           ssm_a_log, ssm_d, ssm_norm_w, with_ctx_out)
    y_lat = jnp.concatenate([ya_l, yb_l, yc_l, yd_l], axis=-1)
    y_ctx = jnp.concatenate([ya_c, yb_c, yc_c, yd_c], axis=-1) if with_ctx_out else None
    return y_lat, y_ctx


def peer_ffn(h, w_q, sub_keys, u_tab, v_tab):
    shp = h.shape
    t = h.reshape(-1, shp[-1])
    n_tok = t.shape[0]
    q = (t @ w_q).reshape(n_tok, PEER_HEADS, 2, PEER_QDIM // 2)
    s = jnp.einsum('thsd,hskd->thsk', q, sub_keys).astype(F32)
    sv, si = lax.top_k(s, PEER_TOPK)
    cand_s = (sv[:, :, 0, :, None] + sv[:, :, 1, None, :]).reshape(n_tok, PEER_HEADS, PEER_TOPK * PEER_TOPK)
    cand_i = (si[:, :, 0, :, None] * PEER_NK + si[:, :, 1, None, :]).reshape(n_tok, PEER_HEADS, PEER_TOPK * PEER_TOPK)
    top_s, pos = lax.top_k(cand_s, PEER_TOPK)
    idx = jnp.take_along_axis(cand_i, pos, axis=-1).reshape(n_tok, PEER_HEADS * PEER_TOPK)
    gate = jax.nn.softmax(top_s, axis=-1).reshape(n_tok, PEER_HEADS * PEER_TOPK)
    n_blk = n_tok // PEER_BLOCK

    def expert_block(args):
        xb, ib, gb = args
        act = jnp.einsum('tkd,td->tk', u_tab[ib], xb).astype(F32)
        return jnp.einsum('tk,tkd->td', (jax.nn.gelu(act) * gb).astype(xb.dtype), v_tab[ib])

    out = lax.map(expert_block, (t.reshape(n_blk, PEER_BLOCK, -1),
                                 idx.reshape(n_blk, PEER_BLOCK, -1),
                                 gate.reshape(n_blk, PEER_BLOCK, -1)))
    return out.reshape(shp)


def setup_inputs(seed: int = 0) -> dict:
    key = jax.random.key(seed)
    ks = iter(jax.random.split(key, 48))

    def nrm(shape, s):
        return s * jax.random.normal(next(ks), shape, F32)

    def unif(shape, lo, hi):
        return jax.random.uniform(next(ks), shape, F32, lo, hi)

    D = D_MODEL
    ssm_dt = jnp.exp(unif((DEPTH, 2, SSM_H), math.log(1e-3), math.log(1e-1)))
    return {
        "x": nrm((BATCH, SEQ, D), 1.0),
        "c": nrm((BATCH, D), 1.0),
        "ctx": nrm((BATCH, CTX_LEN, D), 1.0),
        "c_ctx": nrm((D,), 1.0),
        "w_ada": nrm((DEPTH, D, 6 * D), 0.5 * D ** -0.5),
        "b_ada": nrm((DEPTH, 6 * D), 0.02),
        "w_in": nrm((DEPTH, D, D_IN), D ** -0.5),
        "w_out": nrm((DEPTH, D_MIX, D), DEEPNORM_BETA * D_MIX ** -0.5),
        "s5_a_re": -0.5 * jnp.exp(nrm((DEPTH, 2, S5_G, S5_N), 0.02)),
        "s5_a_im": jnp.broadcast_to(jnp.pi * jnp.arange(S5_N, dtype=F32), (DEPTH, 2, S5_G, S5_N)),
        "s5_log_dt": unif((DEPTH, 2, S5_G), math.log(1e-3), math.log(1e-1)),
        "s5_b_re": nrm((DEPTH, 2, S5_G, S5_N, S5_P), (2 * S5_P) ** -0.5),
        "s5_b_im": nrm((DEPTH, 2, S5_G, S5_N, S5_P), (2 * S5_P) ** -0.5),
        "s5_c_re": nrm((DEPTH, 2, S5_G, S5_P, S5_N), S5_N ** -0.5),
        "s5_c_im": nrm((DEPTH, 2, S5_G, S5_P, S5_N), S5_N ** -0.5),
        "s5_d": nrm((DEPTH, W_GRP), 1.0),
        "s5_w_glu": nrm((DEPTH, W_GRP, W_GRP), W_GRP ** -0.5),
        "s5_b_glu": nrm((DEPTH, W_GRP), 0.02),
        "pool_w": nrm((DEPTH, len(POOL_WINDOWS), POOL_C, POOL_C), POOL_C ** -0.5),
        "pool_scale": 1.0 + nrm((DEPTH, W_GRP), 0.02),
        "na_rpb": nrm((DEPTH, NA_H, 2 * NA_WIN_R - 1, 2 * NA_WIN_C - 1), 0.02),
        "ssm_conv_w": nrm((DEPTH, SSM_CONV, SSM_CONV_CH), SSM_CONV ** -0.5),
        "ssm_conv_b": nrm((DEPTH, SSM_CONV_CH), 0.02),
        "ssm_dt_bias": ssm_dt + jnp.log(-jnp.expm1(-ssm_dt)),
        "ssm_a_log": jnp.log(unif((DEPTH, 2, SSM_H), 1.0, 16.0)),
        "ssm_d": 1.0 + nrm((DEPTH, SSM_H), 0.02),
        "ssm_norm_w": 1.0 + nrm((DEPTH, W_GRP), 0.02),
        "ln1_g": 1.0 + nrm((DEPTH, D), 0.02),
        "ln1_b": nrm((DEPTH, D), 0.02),
        "ln2_g": 1.0 + nrm((DEPTH, D), 0.02),
        "ln2_b": nrm((DEPTH, D), 0.02),
        "peer_w_q": nrm((DEPTH, D, PEER_HEADS * PEER_QDIM), D ** -0.5),
        "peer_sub_keys": nrm((DEPTH, PEER_HEADS, 2, PEER_NK, PEER_QDIM // 2), (PEER_QDIM // 2) ** -0.5),
        "peer_u": nrm((DEPTH, PEER_NE, D), D ** -0.5),
        "peer_v": nrm((DEPTH, PEER_NE, D), DEEPNORM_BETA),
    }


def reference(x, c, ctx, c_ctx, w_ada, b_ada, w_in, w_out, s5_a_re, s5_a_im, s5_log_dt, s5_b_re,
              s5_b_im, s5_c_re, s5_c_im, s5_d, s5_w_glu, s5_b_glu, pool_w, pool_scale, na_rpb,
              ssm_conv_w, ssm_conv_b, ssm_dt_bias, ssm_a_log, ssm_d, ssm_norm_w, ln1_g, ln1_b,
              ln2_g, ln2_b, peer_w_q, peer_sub_keys, peer_u, peer_v):
    x_lat, x_ctx = x, ctx
    act_lat = jax.nn.silu(c)
    act_ctx = jax.nn.silu(c_ctx)
    for l in range(DEPTH):
        last = l == DEPTH - 1
        m_lat = jnp.split((act_lat @ w_ada[l] + b_ada[l])[:, None, :], 6, axis=-1)
        m_ctx = jnp.split(act_ctx @ w_ada[l] + b_ada[l], 6, axis=-1)
        p_lat = modulate(x_lat, m_lat[0], m_lat[1]) @ w_in[l]
        p_ctx = modulate(x_ctx, m_ctx[0], m_ctx[1]) @ w_in[l]
        y_lat, y_ctx = hybrid_mixer(p_lat, p_ctx, s5_a_re[l], s5_a_im[l], s5_log_dt[l], s5_b_re[l],
                                    s5_b_im[l], s5_c_re[l], s5_c_im[l], s5_d[l], s5_w_glu[l],
                                    s5_b_glu[l], pool_w[l], pool_scale[l], na_rpb[l], ssm_conv_w[l],
                                    ssm_conv_b[l], ssm_dt_bias[l], ssm_a_log[l], ssm_d[l],
                                    ssm_norm_w[l], not last)
        x_lat = layer_norm(DEEPNORM_ALPHA * x_lat + m_lat[2] * (y_lat @ w_out[l]), ln1_g[l], ln1_b[l])
        f_lat = peer_ffn(modulate(x_lat, m_lat[3], m_lat[4]), peer_w_q[l], peer_sub_keys[l], peer_u[l], peer_v[l])
        x_lat = layer_norm(DEEPNORM_ALPHA * x_lat + m_lat[5] * f_lat, ln2_g[l], ln2_b[l])
        if not last:
            x_ctx = layer_norm(DEEPNORM_ALPHA * x_ctx + m_ctx[2] * (y_ctx @ w_out[l]), ln1_g[l], ln1_b[l])
            f_ctx = peer_ffn(modulate(x_ctx, m_ctx[3], m_ctx[4]), peer_w_q[l], peer_sub_keys[l], peer_u[l], peer_v[l])
            x_ctx = layer_norm(DEEPNORM_ALPHA * x_ctx + m_ctx[5] * f_ctx, ln2_g[l], ln2_b[l])
    return x_lat
```

```python
import functools
import math

import jax
import jax.numpy as jnp
from jax import lax
from jax.experimental import pallas as pl
from jax.experimental.pallas import tpu as pltpu

D_MODEL = 2048
BATCH = 4
SEQ = 4096
DEPTH = 2

GRID_W = 64
CTX_LEN = 256
N_MIXERS = 4
D_MIX = D_MODEL
W_GRP = D_MIX // N_MIXERS

S5_P = 16
S5_G = W_GRP // S5_P
S5_N = 64

POOL_WINDOWS = (2, 4, 8, 16)
POOL_C = W_GRP // len(POOL_WINDOWS)

NA_HD = 64
NA_H = W_GRP // NA_HD
NA_WIN_R = 8
NA_WIN_C = 16

SSM_HD = 64
SSM_H = W_GRP // SSM_HD
SSM_G = 2
SSM_N = 128
SSM_CONV = 4
SSM_CHUNK = 128
SSM_CONV_CH = W_GRP + 2 * SSM_G * SSM_N

PEER_HEADS = 8
PEER_NK = 128
PEER_NE = PEER_NK * PEER_NK
PEER_QDIM = 256
PEER_TOPK = 16
PEER_BLOCK = 128

IN_SPLITS = (W_GRP, W_GRP, W_GRP, W_GRP, W_GRP, W_GRP, SSM_CONV_CH, 2 * SSM_H)
D_IN = sum(IN_SPLITS)

DEEPNORM_ALPHA = (2 * DEPTH) ** 0.25
DEEPNORM_BETA = (8 * DEPTH) ** -0.25
LN_EPS = 1e-5
F32 = jnp.float32
BF16 = jnp.bfloat16

V7X_LANES = 128
V7X_VMEM_LIMIT_BYTES = 56 * 1024 * 1024


def _flip(t, rev):
    return t[:, ::-1] if rev else t


def _modmm_body(x_ref, shift_ref, scale_ref, w_ref, *out_refs, col_splits):
    xm = (x_ref[...] * (1.0 + scale_ref[0]) + shift_ref[0]).astype(BF16)
    o = 0
    for ref, n in zip(out_refs, col_splits):
        ref[...] = jnp.dot(xm, w_ref[:, o:o + n], preferred_element_type=F32)
        o += n


def modulated_matmul(x, shift, scale, w_bf16, col_splits, rows_per_mod, tm):
    T, K = x.shape
    assert T % tm == 0 and rows_per_mod % tm == 0
    tiles_per_mod = rows_per_mod // tm
    n_tot = sum(col_splits)
    assert w_bf16.shape == (K, n_tot)
    mod_spec = pl.BlockSpec((1, 1, K), lambda i: (i // tiles_per_mod, 0, 0))
    return pl.pallas_call(
        functools.partial(_modmm_body, col_splits=tuple(col_splits)),
        out_shape=[jax.ShapeDtypeStruct((T, n), F32) for n in col_splits],
        grid=(T // tm,),
        in_specs=[pl.BlockSpec((tm, K), lambda i: (i, 0)), mod_spec, mod_spec,
                  pl.BlockSpec((K, n_tot), lambda i: (0, 0))],
        out_specs=[pl.BlockSpec((tm, n), lambda i: (i, 0)) for n in col_splits],
        compiler_params=pltpu.CompilerParams(dimension_semantics=("arbitrary",),
                                             vmem_limit_bytes=V7X_VMEM_LIMIT_BYTES),
        name="modulated_matmul",
    )(x, shift, scale, w_bf16)


def _proj_ln_body(*refs, n_parts, alpha):
    part_refs = refs[:n_parts]
    x_ref, gate_ref, w_ref, g_ref, b_ref, o_ref = refs[n_parts:]
    acc = None
    o = 0
    for pr in part_refs:
        n = pr.shape[-1]
        d = jnp.dot(pr[...].astype(BF16), w_ref[o:o + n, :], preferred_element_type=F32)
        acc = d if acc is None else acc + d
        o += n
    h = alpha * x_ref[...] + gate_ref[0] * acc
    mu = jnp.mean(h, -1, keepdims=True)
    hc = h - mu
    var = jnp.mean(hc * hc, -1, keepdims=True)
    o_ref[...] = hc * lax.rsqrt(var + LN_EPS) * g_ref[...] + b_ref[...]


def proj_residual_ln(parts, x, gate, w_bf16, g, b, rows_per_mod, tm):
    T, D = x.shape
    assert T % tm == 0 and rows_per_mod % tm == 0
    tiles_per_mod = rows_per_mod // tm
    k_tot = sum(p.shape[-1] for p in parts)
    assert w_bf16.shape == (k_tot, D)
    row = lambda n: pl.BlockSpec((tm, n), lambda i: (i, 0))
    vec = pl.BlockSpec((1, D), lambda i: (0, 0))
    return pl.pallas_call(
        functools.partial(_proj_ln_body, n_parts=len(parts), alpha=DEEPNORM_ALPHA),
        out_shape=jax.ShapeDtypeStruct((T, D), F32),
        grid=(T // tm,),
        in_specs=[row(p.shape[-1]) for p in parts] + [
            row(D), pl.BlockSpec((1, 1, D), lambda i: (i // tiles_per_mod, 0, 0)),
            pl.BlockSpec((k_tot, D), lambda i: (0, 0)), vec, vec],
        out_specs=row(D),
        compiler_params=pltpu.CompilerParams(dimension_semantics=("arbitrary",),
                                             vmem_limit_bytes=V7X_VMEM_LIMIT_BYTES),
        name="proj_residual_ln",
    )(*parts, x, gate, w_bf16, g.reshape(1, D), b.reshape(1, D))


def _res_ln_body(x_ref, f_ref, gate_ref, g_ref, b_ref, o_ref, *, alpha):
    h = alpha * x_ref[...] + gate_ref[0] * f_ref[...]
    mu = jnp.mean(h, -1, keepdims=True)
    hc = h - mu
    var = jnp.mean(hc * hc, -1, keepdims=True)
    o_ref[...] = hc * lax.rsqrt(var + LN_EPS) * g_ref[...] + b_ref[...]


def residual_ln(x, f, gate, g, b, rows_per_mod, tm):
    T, D = x.shape
    assert T % tm == 0 and rows_per_mod % tm == 0
    tiles_per_mod = rows_per_mod // tm
    row = pl.BlockSpec((tm, D), lambda i: (i, 0))
    vec = pl.BlockSpec((1, D), lambda i: (0, 0))
    return pl.pallas_call(
        functools.partial(_res_ln_body, alpha=DEEPNORM_ALPHA),
        out_shape=jax.ShapeDtypeStruct((T, D), F32),
        grid=(T // tm,),
        in_specs=[row, row, pl.BlockSpec((1, 1, D), lambda i: (i // tiles_per_mod, 0, 0)), vec, vec],
        out_specs=row,
        compiler_params=pltpu.CompilerParams(dimension_semantics=("arbitrary",),
                                             vmem_limit_bytes=V7X_VMEM_LIMIT_BYTES),
        name="residual_ln",
    )(x, f, gate, g.reshape(1, D), b.reshape(1, D))


def s5_discretize(a_re, a_im, log_dt, b_re, b_im):
    a_re, a_im = a_re.astype(F32), a_im.astype(F32)
    b_re, b_im = b_re.astype(F32), b_im.astype(F32)
    dt = jnp.exp(log_dt.astype(F32))[:, None]
    mag = jnp.exp(a_re * dt)
    ab_re = mag * jnp.cos(a_im * dt)
    ab_im = mag * jnp.sin(a_im * dt)
    den = a_re * a_re + a_im * a_im
    f_re = ((ab_re - 1) * a_re + ab_im * a_im) / den
    f_im = (ab_im * a_re - (ab_re - 1) * a_im) / den
    bb_re = f_re[..., None] * b_re - f_im[..., None] * b_im
    bb_im = f_re[..., None] * b_im + f_im[..., None] * b_re
    return ab_re, ab_im, bb_re, bb_im


def _cscan_combine(e1, e2):
    a1r, a1i, b1r, b1i = e1
    a2r, a2i, b2r, b2i = e2
    return (a2r * a1r - a2i * a1i, a2r * a1i + a2i * a1r,
            a2r * b1r - a2i * b1i + b2r, a2r * b1i + a2i * b1r + b2i)


def s5_states(u, ab_re, ab_im, bb_re, bb_im, h0_re, h0_im):
    bu_re = jnp.einsum('gnp,blgp->blgn', bb_re, u)
    bu_im = jnp.einsum('gnp,blgp->blgn', bb_im, u)
    bu_re = bu_re.at[:, 0].add(ab_re * h0_re - ab_im * h0_im)
    bu_im = bu_im.at[:, 0].add(ab_re * h0_im + ab_im * h0_re)
    a_re = jnp.broadcast_to(ab_re, bu_re.shape)
    a_im = jnp.broadcast_to(ab_im, bu_im.shape)
    _, _, h_re, h_im = lax.associative_scan(_cscan_combine, (a_re, a_im, bu_re, bu_im), axis=1)
    return h_re, h_im


def s5_readout(c_re, c_im, h_re, h_im):
    return (jnp.einsum('gpn,blgn->blgp', c_re.astype(F32), h_re)
            - jnp.einsum('gpn,blgn->blgp', c_im.astype(F32), h_im))


def s5_glu(y, w_glu, b_glu, dtype):
    B_, L = y.shape[:2]
    g = jax.nn.gelu(y.reshape(B_, L, W_GRP))
    return (g * jax.nn.sigmoid(g @ w_glu.astype(F32) + b_glu.astype(F32))).astype(dtype)


def s5_mixer(u_lat, u_ctx, a_re, a_im, log_dt, b_re, b_im, c_re, c_im, d, w_glu, b_glu, with_ctx_out):
    B_ = u_lat.shape[0]
    ul = u_lat.astype(F32).reshape(B_, -1, S5_G, S5_P)
    uc = u_ctx.astype(F32).reshape(B_, -1, S5_G, S5_P)
    dg = d.astype(F32).reshape(S5_G, S5_P)
    zero = jnp.zeros((B_, S5_G, S5_N), F32)
    y_lat = dg * ul
    y_ctx = dg * uc if with_ctx_out else None
    for di in range(2):
        rev = di == 1
        ab_re, ab_im, bb_re, bb_im = s5_discretize(a_re[di], a_im[di], log_dt[di], b_re[di], b_im[di])
        hc_re, hc_im = s5_states(_flip(uc, rev), ab_re, ab_im, bb_re, bb_im, zero, zero)
        hl_re, hl_im = s5_states(_flip(ul, rev), ab_re, ab_im, bb_re, bb_im, hc_re[:, -1], hc_im[:, -1])
        y_lat = y_lat + _flip(s5_readout(c_re[di], c_im[di], hl_re, hl_im), rev)
        if with_ctx_out:
            y_ctx = y_ctx + _flip(s5_readout(c_re[di], c_im[di], hc_re, hc_im), rev)
    out_lat = s5_glu(y_lat, w_glu, b_glu, u_lat.dtype)
    out_ctx = s5_glu(y_ctx, w_glu, b_glu, u_ctx.dtype) if with_ctx_out else None
    return out_lat, out_ctx


def pool_mix(u, pool_w, pool_scale):
    B_, L, _ = u.shape
    uf = u.astype(F32)
    csum = jnp.pad(jnp.cumsum(uf, axis=1), ((0, 0), (1, 0), (0, 0)))
    t = jnp.arange(L)
    outs = []
    for j, w in enumerate(POOL_WINDOWS):
        lo = jnp.clip(t - w // 2, 0, L - 1)
        hi = jnp.clip(t - w // 2 + w - 1, 0, L - 1)
        ch = slice(j * POOL_C, (j + 1) * POOL_C)
        cnt = (hi - lo + 1).astype(F32)[None, :, None]
        outs.append((csum[:, hi + 1, ch] - csum[:, lo, ch]) / cnt - uf[:, :, ch])
    pooled = jnp.stack(outs, axis=2)
    y = jnp.einsum('blgc,gcd->blgd', pooled, pool_w.astype(F32)).reshape(B_, L, W_GRP)
    return (y * pool_scale.astype(F32)).astype(u.dtype)


def na_latent(q, k, v, k_ctx, v_ctx, rpb):
    B_, L, _ = q.shape
    R = L // GRID_W
    KR = min(NA_WIN_R, R)
    KC = NA_WIN_C
    shp = (B_, R, GRID_W, NA_H, NA_HD)
    q, k, v = q.reshape(shp), k.reshape(shp), v.reshape(shp)
    r = jnp.arange(R)
    row_idx = jnp.clip(r - KR // 2, 0, R - KR)[:, None] + jnp.arange(KR)[None, :]
    k_rows = k[:, row_idx]
    v_rows = v[:, row_idx]
    col = jnp.arange(GRID_W)
    c0 = jnp.clip(col - KC // 2, 0, GRID_W - KC)
    in_win = (col[None, :] >= c0[:, None]) & (col[None, :] < c0[:, None] + KC)
    col_rel = jnp.clip(col[None, :] - col[:, None], -(KC - 1), KC - 1) + KC - 1
    row_rel = row_idx - r[:, None] + NA_WIN_R - 1
    bias = rpb.astype(F32)[:, row_rel][..., col_rel]
    bias = bias.transpose(1, 0, 3, 2, 4)
    scale = NA_HD ** -0.5
    s_nb = jnp.einsum('brqhd,brkwhd->brhqkw', q, k_rows).astype(F32) * scale + bias
    s_nb = jnp.where(in_win[:, None, :], s_nb, -jnp.inf).reshape(B_, R, NA_H, GRID_W, KR * GRID_W)
    s_cx = jnp.einsum('brqhd,bchd->brhqc', q, k_ctx).astype(F32) * scale
    p = jax.nn.softmax(jnp.concatenate([s_nb, s_cx], axis=-1), axis=-1).astype(v.dtype)
    p_nb = p[..., :KR * GRID_W].reshape(B_, R, NA_H, GRID_W, KR, GRID_W)
    p_cx = p[..., KR * GRID_W:]
    o = (jnp.einsum('brhqkw,brkwhd->brqhd', p_nb, v_rows)
         + jnp.einsum('brhqc,bchd->brqhd', p_cx, v_ctx))
    return o.reshape(B_, L, W_GRP)


def ctx_attn(q, k, v):
    s = jnp.einsum('bqhd,bkhd->bhqk', q, k).astype(F32) * NA_HD ** -0.5
    p = jax.nn.softmax(s, axis=-1).astype(v.dtype)
    return jnp.einsum('bhqk,bkhd->bqhd', p, v)


def na_mixer(q_l, k_l, v_l, q_c, k_c, v_c, rpb, with_ctx_out):
    B_, C_ = k_c.shape[:2]
    kc = k_c.reshape(B_, C_, NA_H, NA_HD)
    vc = v_c.reshape(B_, C_, NA_H, NA_HD)
    out_l = na_latent(q_l, k_l, v_l, kc, vc, rpb)
    out_c = (ctx_attn(q_c.reshape(B_, C_, NA_H, NA_HD), kc, vc).reshape(B_, C_, W_GRP)
             if with_ctx_out else None)
    return out_l, out_c


def dwconv(x, w, b):
    y = lax.conv_general_dilated(x, w[:, None, :].astype(x.dtype), window_strides=(1,),
                                 padding=[((SSM_CONV - 1) // 2, SSM_CONV // 2)],
                                 dimension_numbers=('NWC', 'WIO', 'NWC'),
                                 feature_group_count=x.shape[-1])
    return y + b


def segsum(x):
    T = x.shape[-1]
    cs = jnp.cumsum(x, axis=-1)
    d = cs[..., :, None] - cs[..., None, :]
    return jnp.where(jnp.tril(jnp.ones((T, T), bool)), d, -jnp.inf)


def ssd(X, A, Bm, Cm, h0):
    b, L, H, P = X.shape
    N = Bm.shape[-1]
    nc = L // SSM_CHUNK
    X = X.reshape(b, nc, SSM_CHUNK, H, P)
    Bm = Bm.reshape(b, nc, SSM_CHUNK, H, N)
    Cm = Cm.reshape(b, nc, SSM_CHUNK, H, N)
    A = A.reshape(b, nc, SSM_CHUNK, H).transpose(0, 3, 1, 2)
    A_cs = jnp.cumsum(A, axis=-1)
    cb = jnp.einsum('bclhn,bcshn->bhcls', Cm, Bm)
    y_diag = jnp.einsum('bhcls,bcshp->bclhp', cb * jnp.exp(segsum(A)), X)
    decay_states = jnp.exp(A_cs[..., -1:] - A_cs).transpose(0, 2, 3, 1)[..., None]
    states = jnp.einsum('bclhn,bclhp->bchpn', Bm, X * decay_states)
    states = jnp.concatenate([h0[:, None], states], axis=1)
    chunk_decay = jnp.exp(segsum(jnp.pad(A_cs[..., -1], ((0, 0), (0, 0), (1, 0)))))
    new_states = jnp.einsum('bhzc,bchpn->bzhpn', chunk_decay, states)
    prev, final = new_states[:, :-1], new_states[:, -1]
    y_off = (jnp.einsum('bclhn,bchpn->bclhp', Cm, prev)
             * jnp.exp(A_cs).transpose(0, 2, 3, 1)[..., None])
    return (y_diag + y_off).reshape(b, L, H, P), final


def ssm_prepare(pxbc, pdt, conv_w, conv_b, dt_bias):
    B_, L, _ = pxbc.shape
    xbc = jax.nn.silu(dwconv(pxbc, conv_w, conv_b).astype(F32))
    hpg = SSM_H // SSM_G
    xs = xbc[..., :W_GRP].reshape(B_, L, SSM_H, SSM_HD)
    bm = jnp.repeat(xbc[..., W_GRP:W_GRP + SSM_G * SSM_N].reshape(B_, L, SSM_G, SSM_N), hpg, axis=2)
    cm = jnp.repeat(xbc[..., W_GRP + SSM_G * SSM_N:].reshape(B_, L, SSM_G, SSM_N), hpg, axis=2)
    dt = jax.nn.softplus(pdt.astype(F32).reshape(B_, L, 2, SSM_H) + dt_bias.astype(F32))
    return xs, bm, cm, dt


def ssm_direction(xs, bm, cm, dt, a, h0, rev):
    y, h_fin = ssd(_flip(xs * dt[..., None], rev), _flip(dt * a, rev), _flip(bm, rev), _flip(cm, rev), h0)
    return _flip(y, rev), h_fin


def ssm_gate_norm(y, z, norm_w):
    B_, L = z.shape[:2]
    g = (y.reshape(B_, L, W_GRP) * jax.nn.silu(z.astype(F32))).reshape(B_, L, SSM_G, W_GRP // SSM_G)
    g = g * lax.rsqrt(jnp.mean(jnp.square(g), -1, keepdims=True) + LN_EPS)
    return (g.reshape(B_, L, W_GRP) * norm_w.astype(F32)).astype(z.dtype)


def ssm_mixer(z_l, xbc_l, dtr_l, z_c, xbc_c, dtr_c, conv_w, conv_b, dt_bias, a_log, d, norm_w, with_ctx_out):
    xs_l, bm_l, cm_l, dt_l = ssm_prepare(xbc_l, dtr_l, conv_w, conv_b, dt_bias)
    xs_c, bm_c, cm_c, dt_c = ssm_prepare(xbc_c, dtr_c, conv_w, conv_b, dt_bias)
    a = -jnp.exp(a_log.astype(F32))
    dh = d.astype(F32)[:, None]
    zero = jnp.zeros((xs_c.shape[0], SSM_H, SSM_HD, SSM_N), F32)
    y_l = dh * xs_l
    y_c = dh * xs_c if with_ctx_out else None
    for di in range(2):
        rev = di == 1
        yc, hc = ssm_direction(xs_c, bm_c, cm_c, dt_c[:, :, di], a[di], zero, rev)
        yl, _ = ssm_direction(xs_l, bm_l, cm_l, dt_l[:, :, di], a[di], hc, rev)
        y_l = y_l + yl
        if with_ctx_out:
            y_c = y_c + yc
    out_l = ssm_gate_norm(y_l, z_l, norm_w)
    out_c = ssm_gate_norm(y_c, z_c, norm_w) if with_ctx_out else None
    return out_l, out_c


def peer_ffn(t, w_q, sub_keys, u_tab, v_tab):
    n_tok = t.shape[0]
    q = (t @ w_q).reshape(n_tok, PEER_HEADS, 2, PEER_QDIM // 2)
    s = jnp.einsum('thsd,hskd->thsk', q, sub_keys).astype(F32)
    sv, si = lax.top_k(s, PEER_TOPK)
    cand_s = (sv[:, :, 0, :, None] + sv[:, :, 1, None, :]).reshape(n_tok, PEER_HEADS, PEER_TOPK * PEER_TOPK)
    cand_i = (si[:, :, 0, :, None] * PEER_NK + si[:, :, 1, None, :]).reshape(n_tok, PEER_HEADS, PEER_TOPK * PEER_TOPK)
    top_s, pos = lax.top_k(cand_s, PEER_TOPK)
    idx = jnp.take_along_axis(cand_i, pos, axis=-1).reshape(n_tok, PEER_HEADS * PEER_TOPK)
    gate = jax.nn.softmax(top_s, axis=-1).reshape(n_tok, PEER_HEADS * PEER_TOPK)
    n_blk = n_tok // PEER_BLOCK

    def expert_block(args):
        xb, ib, gb = args
        act = jnp.einsum('tkd,td->tk', u_tab[ib], xb).astype(F32)
        return jnp.einsum('tk,tkd->td', (jax.nn.gelu(act) * gb).astype(xb.dtype), v_tab[ib])

    out = lax.map(expert_block, (t.reshape(n_blk, PEER_BLOCK, -1),
                                 idx.reshape(n_blk, PEER_BLOCK, -1),
                                 gate.reshape(n_blk, PEER_BLOCK, -1)))
    return out.reshape(n_tok, -1)


def _mixers(p_l, p_c, prm, with_ctx_out):
    s5_l, pool_l, q_l, k_l, v_l, z_l, xbc_l, dt_l = p_l
    s5_c, pool_c, q_c, k_c, v_c, z_c, xbc_c, dt_c = p_c
    ya_l, ya_c = s5_mixer(s5_l, s5_c, prm["s5_a_re"], prm["s5_a_im"], prm["s5_log_dt"], prm["s5_b_re"],
                          prm["s5_b_im"], prm["s5_c_re"], prm["s5_c_im"], prm["s5_d"], prm["s5_w_glu"],
                          prm["s5_b_glu"], with_ctx_out)
    yb_l = pool_mix(pool_l, prm["pool_w"], prm["pool_scale"])
    yb_c = pool_mix(pool_c, prm["pool_w"], prm["pool_scale"]) if with_ctx_out else None
    yc_l, yc_c = na_mixer(q_l, k_l, v_l, q_c, k_c, v_c, prm["na_rpb"], with_ctx_out)
    yd_l, yd_c = ssm_mixer(z_l, xbc_l, dt_l, z_c, xbc_c, dt_c, prm["ssm_conv_w"], prm["ssm_conv_b"],
                           prm["ssm_dt_bias"], prm["ssm_a_log"], prm["ssm_d"], prm["ssm_norm_w"], with_ctx_out)
    return (ya_l, yb_l, yc_l, yd_l), (ya_c, yb_c, yc_c, yd_c)


def kernel(x, c, ctx, c_ctx, w_ada, b_ada, w_in, w_out, s5_a_re, s5_a_im, s5_log_dt, s5_b_re, s5_b_im, s5_c_re, s5_c_im, s5_d, s5_w_glu, s5_b_glu, pool_w, pool_scale, na_rpb, ssm_conv_w, ssm_conv_b, ssm_dt_bias, ssm_a_log, ssm_d, ssm_norm_w, ln1_g, ln1_b, ln2_g, ln2_b, peer_w_q, peer_sub_keys, peer_u, peer_v):
    B_, L, D = x.shape
    C_ = ctx.shape[1]
    x_lat = x.reshape(B_ * L, D)
    x_ctx = ctx.reshape(B_ * C_, D)
    act_lat = jax.nn.silu(c)
    act_ctx = jax.nn.silu(c_ctx)
    dt_pad = V7X_LANES - IN_SPLITS[-1]
    splits = IN_SPLITS[:-1] + (V7X_LANES,)
    TM = 256
    for l in range(DEPTH):
        last = l == DEPTH - 1
        prm = dict(s5_a_re=s5_a_re[l], s5_a_im=s5_a_im[l], s5_log_dt=s5_log_dt[l], s5_b_re=s5_b_re[l],
                   s5_b_im=s5_b_im[l], s5_c_re=s5_c_re[l], s5_c_im=s5_c_im[l], s5_d=s5_d[l],
                   s5_w_glu=s5_w_glu[l], s5_b_glu=s5_b_glu[l], pool_w=pool_w[l], pool_scale=pool_scale[l],
                   na_rpb=na_rpb[l], ssm_conv_w=ssm_conv_w[l], ssm_conv_b=ssm_conv_b[l],
                   ssm_dt_bias=ssm_dt_bias[l], ssm_a_log=ssm_a_log[l], ssm_d=ssm_d[l],
                   ssm_norm_w=ssm_norm_w[l])
        m_lat = (act_lat @ w_ada[l] + b_ada[l]).reshape(B_, 6, 1, D)
        m_ctx = (act_ctx @ w_ada[l] + b_ada[l]).reshape(1, 6, 1, D)
        w_in_b = jnp.pad(w_in[l], ((0, 0), (0, dt_pad))).astype(BF16)
        w_out_b = w_out[l].astype(BF16)

        p_l = modulated_matmul(x_lat, m_lat[:, 0], m_lat[:, 1], w_in_b, splits, L, TM)
        p_c = modulated_matmul(x_ctx, m_ctx[:, 0], m_ctx[:, 1], w_in_b, splits, B_ * C_, TM)
        p_l = [a.reshape(B_, L, -1) for a in p_l]
        p_c = [a.reshape(B_, C_, -1) for a in p_c]
        p_l[-1] = p_l[-1][..., :IN_SPLITS[-1]]
        p_c[-1] = p_c[-1][..., :IN_SPLITS[-1]]
        y_l, y_c = _mixers(p_l, p_c, prm, not last)

        x_lat = proj_residual_ln([a.reshape(B_ * L, -1) for a in y_l], x_lat, m_lat[:, 2], w_out_b,
                                 ln1_g[l], ln1_b[l], L, TM)
        h_lat = x_lat.reshape(B_, L, D) * (1 + m_lat[:, 4]) + m_lat[:, 3]
        f_lat = peer_ffn(h_lat.reshape(B_ * L, D), peer_w_q[l], peer_sub_keys[l], peer_u[l], peer_v[l])
        x_lat = residual_ln(x_lat, f_lat, m_lat[:, 5], ln2_g[l], ln2_b[l], L, TM)
        if not last:
            x_ctx = proj_residual_ln([a.reshape(B_ * C_, -1) for a in y_c], x_ctx, m_ctx[:, 2], w_out_b,
                                     ln1_g[l], ln1_b[l], B_ * C_, TM)
            h_ctx = x_ctx * (1 + m_ctx[0, 4]) + m_ctx[0, 3]
            f_ctx = peer_ffn(h_ctx, peer_w_q[l], peer_sub_keys[l], peer_u[l], peer_v[l])
            x_ctx = residual_ln(x_ctx, f_ctx, m_ctx[:, 5], ln2_g[l], ln2_b[l], B_ * C_, TM)
    return x_lat.reshape(B_, L, D)
```

```python
import functools
import math

import jax
import jax.numpy as jnp
from jax import lax
from jax.experimental import pallas as pl
from jax.experimental.pallas import tpu as pltpu

D_MODEL = 2048
BATCH = 4
SEQ = 4096
DEPTH = 2

GRID_W = 64
CTX_LEN = 256
N_MIXERS = 4
D_MIX = D_MODEL
W_GRP = D_MIX // N_MIXERS

S5_P = 16
S5_G = W_GRP // S5_P
S5_N = 64

POOL_WINDOWS = (2, 4, 8, 16)
POOL_C = W_GRP // len(POOL_WINDOWS)

NA_HD = 64
NA_H = W_GRP // NA_HD
NA_WIN_R = 8
NA_WIN_C = 16

SSM_HD = 64
SSM_H = W_GRP // SSM_HD
SSM_G = 2
SSM_N = 128
SSM_CONV = 4
SSM_CHUNK = 128
SSM_CONV_CH = W_GRP + 2 * SSM_G * SSM_N

PEER_HEADS = 8
PEER_NK = 128
PEER_NE = PEER_NK * PEER_NK
PEER_QDIM = 256
PEER_TOPK = 16
PEER_BLOCK = 128

IN_SPLITS = (W_GRP, W_GRP, W_GRP, W_GRP, W_GRP, W_GRP, SSM_CONV_CH, 2 * SSM_H)
D_IN = sum(IN_SPLITS)

DEEPNORM_ALPHA = (2 * DEPTH) ** 0.25
DEEPNORM_BETA = (8 * DEPTH) ** -0.25
LN_EPS = 1e-5
F32 = jnp.float32
BF16 = jnp.bfloat16

V7X_LANES = 128
V7X_VMEM_LIMIT_BYTES = 56 * 1024 * 1024


def _flip(t, rev):
    return t[:, ::-1] if rev else t


def _modmm_body(x_ref, shift_ref, scale_ref, w_ref, *out_refs, col_splits):
    xm = (x_ref[...] * (1.0 + scale_ref[0]) + shift_ref[0]).astype(BF16)
    o = 0
    for ref, n in zip(out_refs, col_splits):
        ref[...] = jnp.dot(xm, w_ref[:, o:o + n], preferred_element_type=F32)
        o += n


def modulated_matmul(x, shift, scale, w_bf16, col_splits, rows_per_mod, tm):
    T, K = x.shape
    assert T % tm == 0 and rows_per_mod % tm == 0
    tiles_per_mod = rows_per_mod // tm
    n_tot = sum(col_splits)
    assert w_bf16.shape == (K, n_tot)
    mod_spec = pl.BlockSpec((1, 1, K), lambda i: (i // tiles_per_mod, 0, 0))
    return pl.pallas_call(
        functools.partial(_modmm_body, col_splits=tuple(col_splits)),
        out_shape=[jax.ShapeDtypeStruct((T, n), F32) for n in col_splits],
        grid=(T // tm,),
        in_specs=[pl.BlockSpec((tm, K), lambda i: (i, 0)), mod_spec, mod_spec,
                  pl.BlockSpec((K, n_tot), lambda i: (0, 0))],
        out_specs=[pl.BlockSpec((tm, n), lambda i: (i, 0)) for n in col_splits],
        compiler_params=pltpu.CompilerParams(dimension_semantics=("arbitrary",),
                                             vmem_limit_bytes=V7X_VMEM_LIMIT_BYTES),
        name="modulated_matmul",
    )(x, shift, scale, w_bf16)


def _proj_ln_body(*refs, n_parts, alpha):
    part_refs = refs[:n_parts]
    x_ref, gate_ref, w_ref, g_ref, b_ref, o_ref = refs[n_parts:]
    acc = None
    o = 0
    for pr in part_refs:
        n = pr.shape[-1]
        d = jnp.dot(pr[...].astype(BF16), w_ref[o:o + n, :], preferred_element_type=F32)
        acc = d if acc is None else acc + d
        o += n
    h = alpha * x_ref[...] + gate_ref[0] * acc
    mu = jnp.mean(h, -1, keepdims=True)
    hc = h - mu
    var = jnp.mean(hc * hc, -1, keepdims=True)
    o_ref[...] = hc * lax.rsqrt(var + LN_EPS) * g_ref[...] + b_ref[...]


def proj_residual_ln(parts, x, gate, w_bf16, g, b, rows_per_mod, tm):
    T, D = x.shape
    assert T % tm == 0 and rows_per_mod % tm == 0
    tiles_per_mod = rows_per_mod // tm
    k_tot = sum(p.shape[-1] for p in parts)
    assert w_bf16.shape == (k_tot, D)
    row = lambda n: pl.BlockSpec((tm, n), lambda i: (i, 0))
    vec = pl.BlockSpec((1, D), lambda i: (0, 0))
    return pl.pallas_call(
        functools.partial(_proj_ln_body, n_parts=len(parts), alpha=DEEPNORM_ALPHA),
        out_shape=jax.ShapeDtypeStruct((T, D), F32),
        grid=(T // tm,),
        in_specs=[row(p.shape[-1]) for p in parts] + [
            row(D), pl.BlockSpec((1, 1, D), lambda i: (i // tiles_per_mod, 0, 0)),
            pl.BlockSpec((k_tot, D), lambda i: (0, 0)), vec, vec],
        out_specs=row(D),
        compiler_params=pltpu.CompilerParams(dimension_semantics=("arbitrary",),
                                             vmem_limit_bytes=V7X_VMEM_LIMIT_BYTES),
        name="proj_residual_ln",
    )(*parts, x, gate, w_bf16, g.reshape(1, D), b.reshape(1, D))


def _res_ln_body(x_ref, f_ref, gate_ref, g_ref, b_ref, o_ref, *, alpha):
    h = alpha * x_ref[...] + gate_ref[0] * f_ref[...]
    mu = jnp.mean(h, -1, keepdims=True)
    hc = h - mu
    var = jnp.mean(hc * hc, -1, keepdims=True)
    o_ref[...] = hc * lax.rsqrt(var + LN_EPS) * g_ref[...] + b_ref[...]


def residual_ln(x, f, gate, g, b, rows_per_mod, tm):
    T, D = x.shape
    assert T % tm == 0 and rows_per_mod % tm == 0
    tiles_per_mod = rows_per_mod // tm
    row = pl.BlockSpec((tm, D), lambda i: (i, 0))
    vec = pl.BlockSpec((1, D), lambda i: (0, 0))
    return pl.pallas_call(
        functools.partial(_res_ln_body, alpha=DEEPNORM_ALPHA),
        out_shape=jax.ShapeDtypeStruct((T, D), F32),
        grid=(T // tm,),
        in_specs=[row, row, pl.BlockSpec((1, 1, D), lambda i: (i // tiles_per_mod, 0, 0)), vec, vec],
        out_specs=row,
        compiler_params=pltpu.CompilerParams(dimension_semantics=("arbitrary",),
                                             vmem_limit_bytes=V7X_VMEM_LIMIT_BYTES),
        name="residual_ln",
    )(x, f, gate, g.reshape(1, D), b.reshape(1, D))


S5_CHAINS = 2 * BATCH
S5_STATE = S5_G * S5_N
S5_SCAN_COLS = 512


def _s5_scan_body(u_ref, wbf_ref, wbb_ref, wc_ref, are_ref, aim_ref, y_ref, bu_ref, h_ref, *, steps):
    rows = steps * S5_CHAINS

    @pl.when(pl.program_id(0) == 0)
    def _():
        h_ref[...] = jnp.zeros_like(h_ref)

    u = u_ref[...]
    chain = lax.broadcasted_iota(jnp.int32, u.shape, 0) % S5_CHAINS
    fwd = chain < BATCH
    zero = jnp.zeros_like(u)
    bu_ref[...] = (jnp.dot(jnp.where(fwd, u, zero), wbf_ref[...], preferred_element_type=F32)
                   + jnp.dot(jnp.where(fwd, zero, u), wbb_ref[...], preferred_element_type=F32))

    for cb in range(S5_STATE // S5_SCAN_COLS):
        re = pl.ds(cb * S5_SCAN_COLS, S5_SCAN_COLS)
        im = pl.ds(S5_STATE + cb * S5_SCAN_COLS, S5_SCAN_COLS)
        a_re = are_ref[:, re]
        a_im = aim_ref[:, re]

        def step(s, carry):
            h_re, h_im = carry
            r = pl.ds(pl.multiple_of(s * S5_CHAINS, S5_CHAINS), S5_CHAINS)
            n_re = a_re * h_re - a_im * h_im + bu_ref[r, re]
            n_im = a_re * h_im + a_im * h_re + bu_ref[r, im]
            bu_ref[r, re] = n_re
            bu_ref[r, im] = n_im
            return n_re, n_im

        h_re, h_im = lax.fori_loop(0, steps, step, (h_ref[:, re], h_ref[:, im]), unroll=4)
        h_ref[:, re] = h_re
        h_ref[:, im] = h_im

    y2 = jnp.dot(bu_ref[...].astype(BF16), wc_ref[...], preferred_element_type=F32)
    chain_y = lax.broadcasted_iota(jnp.int32, (rows, W_GRP), 0) % S5_CHAINS
    y_ref[...] = jnp.where(chain_y < BATCH, y2[:, :W_GRP], y2[:, W_GRP:])


def s5_scan(u8, wbf, wbb, wc, a_re8, a_im8, steps):
    n_rows = u8.shape[0]
    rows = steps * S5_CHAINS
    assert n_rows % rows == 0
    full = lambda a: pl.BlockSpec(a.shape, lambda i: (0,) * a.ndim)
    return pl.pallas_call(
        functools.partial(_s5_scan_body, steps=steps),
        out_shape=jax.ShapeDtypeStruct((n_rows, W_GRP), F32),
        grid=(n_rows // rows,),
        in_specs=[pl.BlockSpec((rows, W_GRP), lambda i: (i, 0)), full(wbf), full(wbb), full(wc),
                  full(a_re8), full(a_im8)],
        out_specs=pl.BlockSpec((rows, W_GRP), lambda i: (i, 0)),
        scratch_shapes=[pltpu.VMEM((rows, 2 * S5_STATE), F32), pltpu.VMEM((S5_CHAINS, 2 * S5_STATE), F32)],
        compiler_params=pltpu.CompilerParams(dimension_semantics=("arbitrary",),
                                             vmem_limit_bytes=V7X_VMEM_LIMIT_BYTES),
        name="s5_scan",
    )(u8, wbf, wbb, wc, a_re8, a_im8)


def _s5_glu_body(u_ref, yf_ref, yb_ref, d_ref, w_ref, b_ref, o_ref):
    y = d_ref[...] * u_ref[...] + yf_ref[...] + yb_ref[...]
    g = jax.nn.gelu(y)
    z = jnp.dot(g.astype(BF16), w_ref[...], preferred_element_type=F32) + b_ref[...]
    o_ref[...] = g * jax.nn.sigmoid(z)


def s5_glu_pallas(u, yf, yb, d, w_bf16, b, tm):
    T, W = u.shape
    assert T % tm == 0
    row = pl.BlockSpec((tm, W), lambda i: (i, 0))
    vec = pl.BlockSpec((1, W), lambda i: (0, 0))
    return pl.pallas_call(
        _s5_glu_body,
        out_shape=jax.ShapeDtypeStruct((T, W), F32),
        grid=(T // tm,),
        in_specs=[row, row, row, vec, pl.BlockSpec((W, W), lambda i: (0, 0)), vec],
        out_specs=row,
        compiler_params=pltpu.CompilerParams(dimension_semantics=("arbitrary",),
                                             vmem_limit_bytes=V7X_VMEM_LIMIT_BYTES),
        name="s5_glu",
    )(u, yf, yb, d.reshape(1, W), w_bf16, b.reshape(1, W))


def _s5_weights(a_re, a_im, log_dt, b_re, b_im, c_re, c_im):
    eye = jnp.eye(S5_G, dtype=F32)
    wb, a8 = [], []
    wc = []
    for di in range(2):
        ab_re, ab_im, bb_re, bb_im = s5_discretize(a_re[di], a_im[di], log_dt[di], b_re[di], b_im[di])
        blk = lambda m: jnp.einsum('gnp,gh->gphn', m, eye).reshape(W_GRP, S5_STATE)
        wb.append(jnp.concatenate([blk(bb_re), blk(bb_im)], axis=1).astype(BF16))
        a8.append((jnp.broadcast_to(ab_re.reshape(1, S5_STATE), (BATCH, S5_STATE)),
                   jnp.broadcast_to(ab_im.reshape(1, S5_STATE), (BATCH, S5_STATE))))
        cblk = lambda m: jnp.einsum('gpn,gh->gnhp', m.astype(F32), eye).reshape(S5_STATE, W_GRP)
        wc.append(jnp.concatenate([cblk(c_re[di]), -cblk(c_im[di])], axis=0))
    wc = jnp.concatenate(wc, axis=1).astype(BF16)
    a_re8 = jnp.concatenate([a8[0][0], a8[1][0]], axis=0)
    a_im8 = jnp.concatenate([a8[0][1], a8[1][1]], axis=0)
    return wb[0], wb[1], wc, a_re8, a_im8


def s5_mixer_pallas(u_lat, u_ctx, a_re, a_im, log_dt, b_re, b_im, c_re, c_im, d, w_glu, b_glu,
                    with_ctx_out, steps=64, tm=512):
    B_, L, W = u_lat.shape
    C_ = u_ctx.shape[1]
    assert B_ == BATCH and W == W_GRP
    wbf, wbb, wc, a_re8, a_im8 = _s5_weights(a_re, a_im, log_dt, b_re, b_im, c_re, c_im)
    seq_f = jnp.concatenate([u_ctx, u_lat], axis=1)
    seq_b = jnp.concatenate([u_ctx[:, ::-1], u_lat[:, ::-1]], axis=1)
    u8 = jnp.concatenate([seq_f, seq_b], axis=0).transpose(1, 0, 2).reshape((C_ + L) * S5_CHAINS, W)
    y8 = s5_scan(u8.astype(BF16), wbf, wbb, wc, a_re8, a_im8, steps)
    y8 = y8.reshape(C_ + L, S5_CHAINS, W).transpose(1, 0, 2)
    yf, yb = y8[:B_], y8[B_:]
    w_glu_b = w_glu.astype(BF16)
    out_lat = s5_glu_pallas(u_lat.reshape(B_ * L, W), yf[:, C_:].reshape(B_ * L, W),
                            yb[:, C_:][:, ::-1].reshape(B_ * L, W), d, w_glu_b, b_glu, tm)
    out_ctx = None
    if with_ctx_out:
        out_ctx = s5_glu_pallas(u_ctx.reshape(B_ * C_, W), yf[:, :C_].reshape(B_ * C_, W),
                                yb[:, :C_][:, ::-1].reshape(B_ * C_, W), d, w_glu_b, b_glu,
                                min(tm, B_ * C_))
    return out_lat, out_ctx


def s5_discretize(a_re, a_im, log_dt, b_re, b_im):
    a_re, a_im = a_re.astype(F32), a_im.astype(F32)
    b_re, b_im = b_re.astype(F32), b_im.astype(F32)
    dt = jnp.exp(log_dt.astype(F32))[:, None]
    mag = jnp.exp(a_re * dt)
    ab_re = mag * jnp.cos(a_im * dt)
    ab_im = mag * jnp.sin(a_im * dt)
    den = a_re * a_re + a_im * a_im
    f_re = ((ab_re - 1) * a_re + ab_im * a_im) / den
    f_im = (ab_im * a_re - (ab_re - 1) * a_im) / den
    bb_re = f_re[..., None] * b_re - f_im[..., None] * b_im
    bb_im = f_re[..., None] * b_im + f_im[..., None] * b_re
    return ab_re, ab_im, bb_re, bb_im


def _cscan_combine(e1, e2):
    a1r, a1i, b1r, b1i = e1
    a2r, a2i, b2r, b2i = e2
    return (a2r * a1r - a2i * a1i, a2r * a1i + a2i * a1r,
            a2r * b1r - a2i * b1i + b2r, a2r * b1i + a2i * b1r + b2i)


def s5_states(u, ab_re, ab_im, bb_re, bb_im, h0_re, h0_im):
    bu_re = jnp.einsum('gnp,blgp->blgn', bb_re, u)
    bu_im = jnp.einsum('gnp,blgp->blgn', bb_im, u)
    bu_re = bu_re.at[:, 0].add(ab_re * h0_re - ab_im * h0_im)
    bu_im = bu_im.at[:, 0].add(ab_re * h0_im + ab_im * h0_re)
    a_re = jnp.broadcast_to(ab_re, bu_re.shape)
    a_im = jnp.broadcast_to(ab_im, bu_im.shape)
    _, _, h_re, h_im = lax.associative_scan(_cscan_combine, (a_re, a_im, bu_re, bu_im), axis=1)
    return h_re, h_im


def s5_readout(c_re, c_im, h_re, h_im):
    return (jnp.einsum('gpn,blgn->blgp', c_re.astype(F32), h_re)
            - jnp.einsum('gpn,blgn->blgp', c_im.astype(F32), h_im))


def s5_glu(y, w_glu, b_glu, dtype):
    B_, L = y.shape[:2]
    g = jax.nn.gelu(y.reshape(B_, L, W_GRP))
    return (g * jax.nn.sigmoid(g @ w_glu.astype(F32) + b_glu.astype(F32))).astype(dtype)


def s5_mixer(u_lat, u_ctx, a_re, a_im, log_dt, b_re, b_im, c_re, c_im, d, w_glu, b_glu, with_ctx_out):
    B_ = u_lat.shape[0]
    ul = u_lat.astype(F32).reshape(B_, -1, S5_G, S5_P)
    uc = u_ctx.astype(F32).reshape(B_, -1, S5_G, S5_P)
    dg = d.astype(F32).reshape(S5_G, S5_P)
    zero = jnp.zeros((B_, S5_G, S5_N), F32)
    y_lat = dg * ul
    y_ctx = dg * uc if with_ctx_out else None
    for di in range(2):
        rev = di == 1
        ab_re, ab_im, bb_re, bb_im = s5_discretize(a_re[di], a_im[di], log_dt[di], b_re[di], b_im[di])
        hc_re, hc_im = s5_states(_flip(uc, rev), ab_re, ab_im, bb_re, bb_im, zero, zero)
        hl_re, hl_im = s5_states(_flip(ul, rev), ab_re, ab_im, bb_re, bb_im, hc_re[:, -1], hc_im[:, -1])
        y_lat = y_lat + _flip(s5_readout(c_re[di], c_im[di], hl_re, hl_im), rev)
        if with_ctx_out:
            y_ctx = y_ctx + _flip(s5_readout(c_re[di], c_im[di], hc_re, hc_im), rev)
    out_lat = s5_glu(y_lat, w_glu, b_glu, u_lat.dtype)
    out_ctx = s5_glu(y_ctx, w_glu, b_glu, u_ctx.dtype) if with_ctx_out else None
    return out_lat, out_ctx


def pool_mix(u, pool_w, pool_scale):
    B_, L, _ = u.shape
    uf = u.astype(F32)
    csum = jnp.pad(jnp.cumsum(uf, axis=1), ((0, 0), (1, 0), (0, 0)))
    t = jnp.arange(L)
    outs = []
    for j, w in enumerate(POOL_WINDOWS):
        lo = jnp.clip(t - w // 2, 0, L - 1)
        hi = jnp.clip(t - w // 2 + w - 1, 0, L - 1)
        ch = slice(j * POOL_C, (j + 1) * POOL_C)
        cnt = (hi - lo + 1).astype(F32)[None, :, None]
        outs.append((csum[:, hi + 1, ch] - csum[:, lo, ch]) / cnt - uf[:, :, ch])
    pooled = jnp.stack(outs, axis=2)
    y = jnp.einsum('blgc,gcd->blgd', pooled, pool_w.astype(F32)).reshape(B_, L, W_GRP)
    return (y * pool_scale.astype(F32)).astype(u.dtype)


def na_latent(q, k, v, k_ctx, v_ctx, rpb):
    B_, L, _ = q.shape
    R = L // GRID_W
    KR = min(NA_WIN_R, R)
    KC = NA_WIN_C
    shp = (B_, R, GRID_W, NA_H, NA_HD)
    q, k, v = q.reshape(shp), k.reshape(shp), v.reshape(shp)
    r = jnp.arange(R)
    row_idx = jnp.clip(r - KR // 2, 0, R - KR)[:, None] + jnp.arange(KR)[None, :]
    k_rows = k[:, row_idx]
    v_rows = v[:, row_idx]
    col = jnp.arange(GRID_W)
    c0 = jnp.clip(col - KC // 2, 0, GRID_W - KC)
    in_win = (col[None, :] >= c0[:, None]) & (col[None, :] < c0[:, None] + KC)
    col_rel = jnp.clip(col[None, :] - col[:, None], -(KC - 1), KC - 1) + KC - 1
    row_rel = row_idx - r[:, None] + NA_WIN_R - 1
    bias = rpb.astype(F32)[:, row_rel][..., col_rel]
    bias = bias.transpose(1, 0, 3, 2, 4)
    scale = NA_HD ** -0.5
    s_nb = jnp.einsum('brqhd,brkwhd->brhqkw', q, k_rows).astype(F32) * scale + bias
    s_nb = jnp.where(in_win[:, None, :], s_nb, -jnp.inf).reshape(B_, R, NA_H, GRID_W, KR * GRID_W)
    s_cx = jnp.einsum('brqhd,bchd->brhqc', q, k_ctx).astype(F32) * scale
    p = jax.nn.softmax(jnp.concatenate([s_nb, s_cx], axis=-1), axis=-1).astype(v.dtype)
    p_nb = p[..., :KR * GRID_W].reshape(B_, R, NA_H, GRID_W, KR, GRID_W)
    p_cx = p[..., KR * GRID_W:]
    o = (jnp.einsum('brhqkw,brkwhd->brqhd', p_nb, v_rows)
         + jnp.einsum('brhqc,bchd->brqhd', p_cx, v_ctx))
    return o.reshape(B_, L, W_GRP)


def ctx_attn(q, k, v):
    s = jnp.einsum('bqhd,bkhd->bhqk', q, k).astype(F32) * NA_HD ** -0.5
    p = jax.nn.softmax(s, axis=-1).astype(v.dtype)
    return jnp.einsum('bhqk,bkhd->bqhd', p, v)


def na_mixer(q_l, k_l, v_l, q_c, k_c, v_c, rpb, with_ctx_out):
    B_, C_ = k_c.shape[:2]
    kc = k_c.reshape(B_, C_, NA_H, NA_HD)
    vc = v_c.reshape(B_, C_, NA_H, NA_HD)
    out_l = na_latent(q_l, k_l, v_l, kc, vc, rpb)
    out_c = (ctx_attn(q_c.reshape(B_, C_, NA_H, NA_HD), kc, vc).reshape(B_, C_, W_GRP)
             if with_ctx_out else None)
    return out_l, out_c


def dwconv(x, w, b):
    y = lax.conv_general_dilated(x, w[:, None, :].astype(x.dtype), window_strides=(1,),
                                 padding=[((SSM_CONV - 1) // 2, SSM_CONV // 2)],
                                 dimension_numbers=('NWC', 'WIO', 'NWC'),
                                 feature_group_count=x.shape[-1])
    return y + b


def segsum(x):
    T = x.shape[-1]
    cs = jnp.cumsum(x, axis=-1)
    d = cs[..., :, None] - cs[..., None, :]
    return jnp.where(jnp.tril(jnp.ones((T, T), bool)), d, -jnp.inf)


def ssd(X, A, Bm, Cm, h0):
    b, L, H, P = X.shape
    N = Bm.shape[-1]
    nc = L // SSM_CHUNK
    X = X.reshape(b, nc, SSM_CHUNK, H, P)
    Bm = Bm.reshape(b, nc, SSM_CHUNK, H, N)
    Cm = Cm.reshape(b, nc, SSM_CHUNK, H, N)
    A = A.reshape(b, nc, SSM_CHUNK, H).transpose(0, 3, 1, 2)
    A_cs = jnp.cumsum(A, axis=-1)
    cb = jnp.einsum('bclhn,bcshn->bhcls', Cm, Bm)
    y_diag = jnp.einsum('bhcls,bcshp->bclhp', cb * jnp.exp(segsum(A)), X)
    decay_states = jnp.exp(A_cs[..., -1:] - A_cs).transpose(0, 2, 3, 1)[..., None]
    states = jnp.einsum('bclhn,bclhp->bchpn', Bm, X * decay_states)
    states = jnp.concatenate([h0[:, None], states], axis=1)
    chunk_decay = jnp.exp(segsum(jnp.pad(A_cs[..., -1], ((0, 0), (0, 0), (1, 0)))))
    new_states = jnp.einsum('bhzc,bchpn->bzhpn', chunk_decay, states)
    prev, final = new_states[:, :-1], new_states[:, -1]
    y_off = (jnp.einsum('bclhn,bchpn->bclhp', Cm, prev)
             * jnp.exp(A_cs).transpose(0, 2, 3, 1)[..., None])
    return (y_diag + y_off).reshape(b, L, H, P), final


def ssm_prepare(pxbc, pdt, conv_w, conv_b, dt_bias):
    B_, L, _ = pxbc.shape
    xbc = jax.nn.silu(dwconv(pxbc, conv_w, conv_b).astype(F32))
    hpg = SSM_H // SSM_G
    xs = xbc[..., :W_GRP].reshape(B_, L, SSM_H, SSM_HD)
    bm = jnp.repeat(xbc[..., W_GRP:W_GRP + SSM_G * SSM_N].reshape(B_, L, SSM_G, SSM_N), hpg, axis=2)
    cm = jnp.repeat(xbc[..., W_GRP + SSM_G * SSM_N:].reshape(B_, L, SSM_G, SSM_N), hpg, axis=2)
    dt = jax.nn.softplus(pdt.astype(F32).reshape(B_, L, 2, SSM_H) + dt_bias.astype(F32))
    return xs, bm, cm, dt


def ssm_direction(xs, bm, cm, dt, a, h0, rev):
    y, h_fin = ssd(_flip(xs * dt[..., None], rev), _flip(dt * a, rev), _flip(bm, rev), _flip(cm, rev), h0)
    return _flip(y, rev), h_fin


def ssm_gate_norm(y, z, norm_w):
    B_, L = z.shape[:2]
    g = (y.reshape(B_, L, W_GRP) * jax.nn.silu(z.astype(F32))).reshape(B_, L, SSM_G, W_GRP // SSM_G)
    g = g * lax.rsqrt(jnp.mean(jnp.square(g), -1, keepdims=True) + LN_EPS)
    return (g.reshape(B_, L, W_GRP) * norm_w.astype(F32)).astype(z.dtype)


def ssm_mixer(z_l, xbc_l, dtr_l, z_c, xbc_c, dtr_c, conv_w, conv_b, dt_bias, a_log, d, norm_w, with_ctx_out):
    xs_l, bm_l, cm_l, dt_l = ssm_prepare(xbc_l, dtr_l, conv_w, conv_b, dt_bias)
    xs_c, bm_c, cm_c, dt_c = ssm_prepare(xbc_c, dtr_c, conv_w, conv_b, dt_bias)
    a = -jnp.exp(a_log.astype(F32))
    dh = d.astype(F32)[:, None]
    zero = jnp.zeros((xs_c.shape[0], SSM_H, SSM_HD, SSM_N), F32)
    y_l = dh * xs_l
    y_c = dh * xs_c if with_ctx_out else None
    for di in range(2):
        rev = di == 1
        yc, hc = ssm_direction(xs_c, bm_c, cm_c, dt_c[:, :, di], a[di], zero, rev)
        yl, _ = ssm_direction(xs_l, bm_l, cm_l, dt_l[:, :, di], a[di], hc, rev)
        y_l = y_l + yl
        if with_ctx_out:
            y_c = y_c + yc
    out_l = ssm_gate_norm(y_l, z_l, norm_w)
    out_c = ssm_gate_norm(y_c, z_c, norm_w) if with_ctx_out else None
    return out_l, out_c


def peer_ffn(t, w_q, sub_keys, u_tab, v_tab):
    n_tok = t.shape[0]
    q = (t @ w_q).reshape(n_tok, PEER_HEADS, 2, PEER_QDIM // 2)
    s = jnp.einsum('thsd,hskd->thsk', q, sub_keys).astype(F32)
    sv, si = lax.top_k(s, PEER_TOPK)
    cand_s = (sv[:, :, 0, :, None] + sv[:, :, 1, None, :]).reshape(n_tok, PEER_HEADS, PEER_TOPK * PEER_TOPK)
    cand_i = (si[:, :, 0, :, None] * PEER_NK + si[:, :, 1, None, :]).reshape(n_tok, PEER_HEADS, PEER_TOPK * PEER_TOPK)
    top_s, pos = lax.top_k(cand_s, PEER_TOPK)
    idx = jnp.take_along_axis(cand_i, pos, axis=-1).reshape(n_tok, PEER_HEADS * PEER_TOPK)
    gate = jax.nn.softmax(top_s, axis=-1).reshape(n_tok, PEER_HEADS * PEER_TOPK)
    n_blk = n_tok // PEER_BLOCK

    def expert_block(args):
        xb, ib, gb = args
        act = jnp.einsum('tkd,td->tk', u_tab[ib], xb).astype(F32)
        return jnp.einsum('tk,tkd->td', (jax.nn.gelu(act) * gb).astype(xb.dtype), v_tab[ib])

    out = lax.map(expert_block, (t.reshape(n_blk, PEER_BLOCK, -1),
                                 idx.reshape(n_blk, PEER_BLOCK, -1),
                                 gate.reshape(n_blk, PEER_BLOCK, -1)))
    return out.reshape(n_tok, -1)


def _mixers(p_l, p_c, prm, with_ctx_out):
    s5_l, pool_l, q_l, k_l, v_l, z_l, xbc_l, dt_l = p_l
    s5_c, pool_c, q_c, k_c, v_c, z_c, xbc_c, dt_c = p_c
    ya_l, ya_c = s5_mixer_pallas(s5_l, s5_c, prm["s5_a_re"], prm["s5_a_im"], prm["s5_log_dt"],
                                 prm["s5_b_re"], prm["s5_b_im"], prm["s5_c_re"], prm["s5_c_im"], prm["s5_d"],
                                 prm["s5_w_glu"], prm["s5_b_glu"], with_ctx_out)
    yb_l = pool_mix(pool_l, prm["pool_w"], prm["pool_scale"])
    yb_c = pool_mix(pool_c, prm["pool_w"], prm["pool_scale"]) if with_ctx_out else None
    yc_l, yc_c = na_mixer(q_l, k_l, v_l, q_c, k_c, v_c, prm["na_rpb"], with_ctx_out)
    yd_l, yd_c = ssm_mixer(z_l, xbc_l, dt_l, z_c, xbc_c, dt_c, prm["ssm_conv_w"], prm["ssm_conv_b"],
                           prm["ssm_dt_bias"], prm["ssm_a_log"], prm["ssm_d"], prm["ssm_norm_w"], with_ctx_out)
    return (ya_l, yb_l, yc_l, yd_l), (ya_c, yb_c, yc_c, yd_c)


def kernel(x, c, ctx, c_ctx, w_ada, b_ada, w_in, w_out, s5_a_re, s5_a_im, s5_log_dt, s5_b_re, s5_b_im, s5_c_re, s5_c_im, s5_d, s5_w_glu, s5_b_glu, pool_w, pool_scale, na_rpb, ssm_conv_w, ssm_conv_b, ssm_dt_bias, ssm_a_log, ssm_d, ssm_norm_w, ln1_g, ln1_b, ln2_g, ln2_b, peer_w_q, peer_sub_keys, peer_u, peer_v):
    B_, L, D = x.shape
    C_ = ctx.shape[1]
    x_lat = x.reshape(B_ * L, D)
    x_ctx = ctx.reshape(B_ * C_, D)
    act_lat = jax.nn.silu(c)
    act_ctx = jax.nn.silu(c_ctx)
    dt_pad = V7X_LANES - IN_SPLITS[-1]
    splits = IN_SPLITS[:-1] + (V7X_LANES,)
    TM = 256
    for l in range(DEPTH):
        last = l == DEPTH - 1
        prm = dict(s5_a_re=s5_a_re[l], s5_a_im=s5_a_im[l], s5_log_dt=s5_log_dt[l], s5_b_re=s5_b_re[l],
                   s5_b_im=s5_b_im[l], s5_c_re=s5_c_re[l], s5_c_im=s5_c_im[l], s5_d=s5_d[l],
                   s5_w_glu=s5_w_glu[l], s5_b_glu=s5_b_glu[l], pool_w=pool_w[l], pool_scale=pool_scale[l],
                   na_rpb=na_rpb[l], ssm_conv_w=ssm_conv_w[l], ssm_conv_b=ssm_conv_b[l],
                   ssm_dt_bias=ssm_dt_bias[l], ssm_a_log=ssm_a_log[l], ssm_d=ssm_d[l],
                   ssm_norm_w=ssm_norm_w[l])
        m_lat = (act_lat @ w_ada[l] + b_ada[l]).reshape(B_, 6, 1, D)
        m_ctx = (act_ctx @ w_ada[l] + b_ada[l]).reshape(1, 6, 1, D)
        w_in_b = jnp.pad(w_in[l], ((0, 0), (0, dt_pad))).astype(BF16)
        w_out_b = w_out[l].astype(BF16)

        p_l = modulated_matmul(x_lat, m_lat[:, 0], m_lat[:, 1], w_in_b, splits, L, TM)
        p_c = modulated_matmul(x_ctx, m_ctx[:, 0], m_ctx[:, 1], w_in_b, splits, B_ * C_, TM)
        p_l = [a.reshape(B_, L, -1) for a in p_l]
        p_c = [a.reshape(B_, C_, -1) for a in p_c]
        p_l[-1] = p_l[-1][..., :IN_SPLITS[-1]]
        p_c[-1] = p_c[-1][..., :IN_SPLITS[-1]]
        y_l, y_c = _mixers(p_l, p_c, prm, not last)

        x_lat = proj_residual_ln([a.reshape(B_ * L, -1) for a in y_l], x_lat, m_lat[:, 2], w_out_b,
                                 ln1_g[l], ln1_b[l], L, TM)
        h_lat = x_lat.reshape(B_, L, D) * (1 + m_lat[:, 4]) + m_lat[:, 3]
        f_lat = peer_ffn(h_lat.reshape(B_ * L, D), peer_w_q[l], peer_sub_keys[l], peer_u[l], peer_v[l])
        x_lat = residual_ln(x_lat, f_lat, m_lat[:, 5], ln2_g[l], ln2_b[l], L, TM)
        if not last:
            x_ctx = proj_residual_ln([a.reshape(B_ * C_, -1) for a in y_c], x_ctx, m_ctx[:, 2], w_out_b,
                                     ln1_g[l], ln1_b[l], B_ * C_, TM)
            h_ctx = x_ctx * (1 + m_ctx[0, 4]) + m_ctx[0, 3]
            f_ctx = peer_ffn(h_ctx, peer_w_q[l], peer_sub_keys[l], peer_u[l], peer_v[l])
            x_ctx = residual_ln(x_ctx, f_ctx, m_ctx[:, 5], ln2_g[l], ln2_b[l], B_ * C_, TM)
    return x_lat.reshape(B_, L, D)
```

```python
import functools
import math

import jax
import jax.numpy as jnp
from jax import lax
from jax.experimental import pallas as pl
from jax.experimental.pallas import tpu as pltpu

D_MODEL = 2048
BATCH = 4
SEQ = 4096
DEPTH = 2

GRID_W = 64
CTX_LEN = 256
N_MIXERS = 4
D_MIX = D_MODEL
W_GRP = D_MIX // N_MIXERS

S5_P = 16
S5_G = W_GRP // S5_P
S5_N = 64

POOL_WINDOWS = (2, 4, 8, 16)
POOL_C = W_GRP // len(POOL_WINDOWS)

NA_HD = 64
NA_H = W_GRP // NA_HD
NA_WIN_R = 8
NA_WIN_C = 16

SSM_HD = 64
SSM_H = W_GRP // SSM_HD
SSM_G = 2
SSM_N = 128
SSM_CONV = 4
SSM_CHUNK = 128
SSM_CONV_CH = W_GRP + 2 * SSM_G * SSM_N

PEER_HEADS = 8
PEER_NK = 128
PEER_NE = PEER_NK * PEER_NK
PEER_QDIM = 256
PEER_TOPK = 16
PEER_BLOCK = 128

IN_SPLITS = (W_GRP, W_GRP, W_GRP, W_GRP, W_GRP, W_GRP, SSM_CONV_CH, 2 * SSM_H)
D_IN = sum(IN_SPLITS)

DEEPNORM_ALPHA = (2 * DEPTH) ** 0.25
DEEPNORM_BETA = (8 * DEPTH) ** -0.25
LN_EPS = 1e-5
F32 = jnp.float32
BF16 = jnp.bfloat16

V7X_LANES = 128
V7X_VMEM_LIMIT_BYTES = 56 * 1024 * 1024


def _flip(t, rev):
    return t[:, ::-1] if rev else t


def _modmm_body(x_ref, shift_ref, scale_ref, w_ref, *out_refs, col_splits):
    xm = (x_ref[...] * (1.0 + scale_ref[0]) + shift_ref[0]).astype(BF16)
    o = 0
    for ref, n in zip(out_refs, col_splits):
        ref[...] = jnp.dot(xm, w_ref[:, o:o + n], preferred_element_type=F32)
        o += n


def modulated_matmul(x, shift, scale, w_bf16, col_splits, rows_per_mod, tm):
    T, K = x.shape
    assert T % tm == 0 and rows_per_mod % tm == 0
    tiles_per_mod = rows_per_mod // tm
    n_tot = sum(col_splits)
    assert w_bf16.shape == (K, n_tot)
    mod_spec = pl.BlockSpec((1, 1, K), lambda i: (i // tiles_per_mod, 0, 0))
    return pl.pallas_call(
        functools.partial(_modmm_body, col_splits=tuple(col_splits)),
        out_shape=[jax.ShapeDtypeStruct((T, n), F32) for n in col_splits],
        grid=(T // tm,),
        in_specs=[pl.BlockSpec((tm, K), lambda i: (i, 0)), mod_spec, mod_spec,
                  pl.BlockSpec((K, n_tot), lambda i: (0, 0))],
        out_specs=[pl.BlockSpec((tm, n), lambda i: (i, 0)) for n in col_splits],
        compiler_params=pltpu.CompilerParams(dimension_semantics=("arbitrary",),
                                             vmem_limit_bytes=V7X_VMEM_LIMIT_BYTES),
        name="modulated_matmul",
    )(x, shift, scale, w_bf16)


def _proj_ln_body(*refs, n_parts, alpha):
    part_refs = refs[:n_parts]
    x_ref, gate_ref, w_ref, g_ref, b_ref, o_ref = refs[n_parts:]
    acc = None
    o = 0
    for pr in part_refs:
        n = pr.shape[-1]
        d = jnp.dot(pr[...].astype(BF16), w_ref[o:o + n, :], preferred_element_type=F32)
        acc = d if acc is None else acc + d
        o += n
    h = alpha * x_ref[...] + gate_ref[0] * acc
    mu = jnp.mean(h, -1, keepdims=True)
    hc = h - mu
    var = jnp.mean(hc * hc, -1, keepdims=True)
    o_ref[...] = hc * lax.rsqrt(var + LN_EPS) * g_ref[...] + b_ref[...]


def proj_residual_ln(parts, x, gate, w_bf16, g, b, rows_per_mod, tm):
    T, D = x.shape
    assert T % tm == 0 and rows_per_mod % tm == 0
    tiles_per_mod = rows_per_mod // tm
    k_tot = sum(p.shape[-1] for p in parts)
    assert w_bf16.shape == (k_tot, D)
    row = lambda n: pl.BlockSpec((tm, n), lambda i: (i, 0))
    vec = pl.BlockSpec((1, D), lambda i: (0, 0))
    return pl.pallas_call(
        functools.partial(_proj_ln_body, n_parts=len(parts), alpha=DEEPNORM_ALPHA),
        out_shape=jax.ShapeDtypeStruct((T, D), F32),
        grid=(T // tm,),
        in_specs=[row(p.shape[-1]) for p in parts] + [
            row(D), pl.BlockSpec((1, 1, D), lambda i: (i // tiles_per_mod, 0, 0)),
            pl.BlockSpec((k_tot, D), lambda i: (0, 0)), vec, vec],
        out_specs=row(D),
        compiler_params=pltpu.CompilerParams(dimension_semantics=("arbitrary",),
                                             vmem_limit_bytes=V7X_VMEM_LIMIT_BYTES),
        name="proj_residual_ln",
    )(*parts, x, gate, w_bf16, g.reshape(1, D), b.reshape(1, D))


def _res_ln_body(x_ref, f_ref, gate_ref, g_ref, b_ref, o_ref, *, alpha):
    h = alpha * x_ref[...] + gate_ref[0] * f_ref[...]
    mu = jnp.mean(h, -1, keepdims=True)
    hc = h - mu
    var = jnp.mean(hc * hc, -1, keepdims=True)
    o_ref[...] = hc * lax.rsqrt(var + LN_EPS) * g_ref[...] + b_ref[...]


def residual_ln(x, f, gate, g, b, rows_per_mod, tm):
    T, D = x.shape
    assert T % tm == 0 and rows_per_mod % tm == 0
    tiles_per_mod = rows_per_mod // tm
    row = pl.BlockSpec((tm, D), lambda i: (i, 0))
    vec = pl.BlockSpec((1, D), lambda i: (0, 0))
    return pl.pallas_call(
        functools.partial(_res_ln_body, alpha=DEEPNORM_ALPHA),
        out_shape=jax.ShapeDtypeStruct((T, D), F32),
        grid=(T // tm,),
        in_specs=[row, row, pl.BlockSpec((1, 1, D), lambda i: (i // tiles_per_mod, 0, 0)), vec, vec],
        out_specs=row,
        compiler_params=pltpu.CompilerParams(dimension_semantics=("arbitrary",),
                                             vmem_limit_bytes=V7X_VMEM_LIMIT_BYTES),
        name="residual_ln",
    )(x, f, gate, g.reshape(1, D), b.reshape(1, D))


S5_CHAINS = 2 * BATCH
S5_STATE = S5_G * S5_N
S5_SCAN_COLS = 512


def _s5_scan_body(u_ref, wbf_ref, wbb_ref, wc_ref, are_ref, aim_ref, y_ref, bu_ref, h_ref, *, steps):
    rows = steps * S5_CHAINS

    @pl.when(pl.program_id(0) == 0)
    def _():
        h_ref[...] = jnp.zeros_like(h_ref)

    u = u_ref[...]
    chain = lax.broadcasted_iota(jnp.int32, u.shape, 0) % S5_CHAINS
    fwd = chain < BATCH
    zero = jnp.zeros_like(u)
    bu_ref[...] = (jnp.dot(jnp.where(fwd, u, zero), wbf_ref[...], preferred_element_type=F32)
                   + jnp.dot(jnp.where(fwd, zero, u), wbb_ref[...], preferred_element_type=F32))

    for cb in range(S5_STATE // S5_SCAN_COLS):
        re = pl.ds(cb * S5_SCAN_COLS, S5_SCAN_COLS)
        im = pl.ds(S5_STATE + cb * S5_SCAN_COLS, S5_SCAN_COLS)
        a_re = are_ref[:, re]
        a_im = aim_ref[:, re]

        def step(s, carry):
            h_re, h_im = carry
            r = pl.ds(pl.multiple_of(s * S5_CHAINS, S5_CHAINS), S5_CHAINS)
            n_re = a_re * h_re - a_im * h_im + bu_ref[r, re]
            n_im = a_re * h_im + a_im * h_re + bu_ref[r, im]
            bu_ref[r, re] = n_re
            bu_ref[r, im] = n_im
            return n_re, n_im

        h_re, h_im = lax.fori_loop(0, steps, step, (h_ref[:, re], h_ref[:, im]), unroll=4)
        h_ref[:, re] = h_re
        h_ref[:, im] = h_im

    y2 = jnp.dot(bu_ref[...].astype(BF16), wc_ref[...], preferred_element_type=F32)
    chain_y = lax.broadcasted_iota(jnp.int32, (rows, W_GRP), 0) % S5_CHAINS
    y_ref[...] = jnp.where(chain_y < BATCH, y2[:, :W_GRP], y2[:, W_GRP:])


def s5_scan(u8, wbf, wbb, wc, a_re8, a_im8, steps):
    n_rows = u8.shape[0]
    rows = steps * S5_CHAINS
    assert n_rows % rows == 0
    full = lambda a: pl.BlockSpec(a.shape, lambda i: (0,) * a.ndim)
    return pl.pallas_call(
        functools.partial(_s5_scan_body, steps=steps),
        out_shape=jax.ShapeDtypeStruct((n_rows, W_GRP), F32),
        grid=(n_rows // rows,),
        in_specs=[pl.BlockSpec((rows, W_GRP), lambda i: (i, 0)), full(wbf), full(wbb), full(wc),
                  full(a_re8), full(a_im8)],
        out_specs=pl.BlockSpec((rows, W_GRP), lambda i: (i, 0)),
        scratch_shapes=[pltpu.VMEM((rows, 2 * S5_STATE), F32), pltpu.VMEM((S5_CHAINS, 2 * S5_STATE), F32)],
        compiler_params=pltpu.CompilerParams(dimension_semantics=("arbitrary",),
                                             vmem_limit_bytes=V7X_VMEM_LIMIT_BYTES),
        name="s5_scan",
    )(u8, wbf, wbb, wc, a_re8, a_im8)


def _s5_glu_body(u_ref, yf_ref, yb_ref, d_ref, w_ref, b_ref, o_ref):
    y = d_ref[...] * u_ref[...] + yf_ref[...] + yb_ref[...]
    g = jax.nn.gelu(y)
    z = jnp.dot(g.astype(BF16), w_ref[...], preferred_element_type=F32) + b_ref[...]
    o_ref[...] = g * jax.nn.sigmoid(z)


def s5_glu_pallas(u, yf, yb, d, w_bf16, b, tm):
    T, W = u.shape
    assert T % tm == 0
    row = pl.BlockSpec((tm, W), lambda i: (i, 0))
    vec = pl.BlockSpec((1, W), lambda i: (0, 0))
    return pl.pallas_call(
        _s5_glu_body,
        out_shape=jax.ShapeDtypeStruct((T, W), F32),
        grid=(T // tm,),
        in_specs=[row, row, row, vec, pl.BlockSpec((W, W), lambda i: (0, 0)), vec],
        out_specs=row,
        compiler_params=pltpu.CompilerParams(dimension_semantics=("arbitrary",),
                                             vmem_limit_bytes=V7X_VMEM_LIMIT_BYTES),
        name="s5_glu",
    )(u, yf, yb, d.reshape(1, W), w_bf16, b.reshape(1, W))


def _s5_weights(a_re, a_im, log_dt, b_re, b_im, c_re, c_im):
    eye = jnp.eye(S5_G, dtype=F32)
    wb, a8 = [], []
    wc = []
    for di in range(2):
        ab_re, ab_im, bb_re, bb_im = s5_discretize(a_re[di], a_im[di], log_dt[di], b_re[di], b_im[di])
        blk = lambda m: jnp.einsum('gnp,gh->gphn', m, eye).reshape(W_GRP, S5_STATE)
        wb.append(jnp.concatenate([blk(bb_re), blk(bb_im)], axis=1).astype(BF16))
        a8.append((jnp.broadcast_to(ab_re.reshape(1, S5_STATE), (BATCH, S5_STATE)),
                   jnp.broadcast_to(ab_im.reshape(1, S5_STATE), (BATCH, S5_STATE))))
        cblk = lambda m: jnp.einsum('gpn,gh->gnhp', m.astype(F32), eye).reshape(S5_STATE, W_GRP)
        wc.append(jnp.concatenate([cblk(c_re[di]), -cblk(c_im[di])], axis=0))
    wc = jnp.concatenate(wc, axis=1).astype(BF16)
    a_re8 = jnp.concatenate([a8[0][0], a8[1][0]], axis=0)
    a_im8 = jnp.concatenate([a8[0][1], a8[1][1]], axis=0)
    return wb[0], wb[1], wc, a_re8, a_im8


def s5_mixer_pallas(u_lat, u_ctx, a_re, a_im, log_dt, b_re, b_im, c_re, c_im, d, w_glu, b_glu,
                    with_ctx_out, steps=64, tm=512):
    B_, L, W = u_lat.shape
    C_ = u_ctx.shape[1]
    assert B_ == BATCH and W == W_GRP
    wbf, wbb, wc, a_re8, a_im8 = _s5_weights(a_re, a_im, log_dt, b_re, b_im, c_re, c_im)
    seq_f = jnp.concatenate([u_ctx, u_lat], axis=1)
    seq_b = jnp.concatenate([u_ctx[:, ::-1], u_lat[:, ::-1]], axis=1)
    u8 = jnp.concatenate([seq_f, seq_b], axis=0).transpose(1, 0, 2).reshape((C_ + L) * S5_CHAINS, W)
    y8 = s5_scan(u8.astype(BF16), wbf, wbb, wc, a_re8, a_im8, steps)
    y8 = y8.reshape(C_ + L, S5_CHAINS, W).transpose(1, 0, 2)
    yf, yb = y8[:B_], y8[B_:]
    w_glu_b = w_glu.astype(BF16)
    out_lat = s5_glu_pallas(u_lat.reshape(B_ * L, W), yf[:, C_:].reshape(B_ * L, W),
                            yb[:, C_:][:, ::-1].reshape(B_ * L, W), d, w_glu_b, b_glu, tm)
    out_ctx = None
    if with_ctx_out:
        out_ctx = s5_glu_pallas(u_ctx.reshape(B_ * C_, W), yf[:, :C_].reshape(B_ * C_, W),
                                yb[:, :C_][:, ::-1].reshape(B_ * C_, W), d, w_glu_b, b_glu,
                                min(tm, B_ * C_))
    return out_lat, out_ctx


PEER_HALF = PEER_QDIM // 2


def _topk_rows(x, k):
    n = x.shape[0]
    row = lax.broadcasted_iota(jnp.int32, x.shape, 0).astype(F32)
    krow = lax.broadcasted_iota(jnp.int32, (k, x.shape[1]), 0)

    def body(it, carry):
        x, rank, vals = carry
        m = jnp.max(x, axis=0, keepdims=True)
        first = jnp.min(jnp.where(x == m, row, float(n)), axis=0, keepdims=True)
        hit = row == first
        itf = it.astype(F32)
        return (jnp.where(hit, -jnp.inf, x), jnp.where(hit, itf, rank), jnp.where(krow == it, m, vals))

    init = (x, jnp.full(x.shape, float(k), F32), jnp.zeros((k, x.shape[1]), F32))
    _, rank, vals = lax.fori_loop(0, k, body, init)
    return vals, rank


def _peer_route_body(x_ref, shift_ref, scale_ref, wq_ref, keys_ref, h_ref, n1_ref, c1_ref, r2_ref, e2_ref,
                     q_ref, *, lane_tiles):
    hm = (x_ref[...] * (1.0 + scale_ref[0]) + shift_ref[0]).astype(BF16)
    h_ref[...] = hm
    q_ref[...] = jnp.dot(hm, wq_ref[...], preferred_element_type=F32).astype(BF16)
    K = PEER_TOPK
    for hd in range(PEER_HEADS):
        for lt in range(lane_tiles):
            tok = pl.ds(lt * V7X_LANES, V7X_LANES)
            sc = []
            for side in range(2):
                col = (2 * hd + side) * PEER_HALF
                qs = q_ref[tok, col:col + PEER_HALF]
                sc.append(lax.dot_general(keys_ref[2 * hd + side], qs, (((1,), (1,)), ((), ())),
                                          preferred_element_type=F32))
            v1, r1 = _topk_rows(sc[0], K)
            v2, r2 = _topk_rows(sc[1], K)
            cand = jnp.concatenate([v1[i:i + 1] + v2 for i in range(K)], axis=0)
            vc, rc = _topk_rows(cand, K)
            z = jnp.sum(jnp.exp(vc - vc[0:1]), axis=0, keepdims=True)
            chosen = jnp.where(rc < float(K), 1.0, 0.0)
            n1 = jnp.zeros_like(r1)
            for i in range(K):
                n_i = jnp.sum(chosen[i * K:(i + 1) * K], axis=0, keepdims=True)
                n1 = jnp.where(r1 == float(i), n_i, n1)
            n1_ref[hd, :, tok] = n1
            c1_ref[hd, :, tok] = jnp.exp(sc[0] - v1[0:1]) / z
            r2_ref[hd, :, tok] = r2
            e2_ref[hd, :, tok] = jnp.exp(sc[1] - v2[0:1])


def peer_route(x, shift, scale, wq_bf16, keys_bf16, rows_per_mod, tm):
    T, D = x.shape
    assert T % tm == 0 and rows_per_mod % tm == 0 and tm % V7X_LANES == 0
    tiles_per_mod = rows_per_mod // tm
    mod_spec = pl.BlockSpec((1, 1, D), lambda i: (i // tiles_per_mod, 0, 0))
    tab = jax.ShapeDtypeStruct((PEER_HEADS, PEER_NK, T), F32)
    tab_spec = pl.BlockSpec((PEER_HEADS, PEER_NK, tm), lambda i: (0, 0, i))
    return pl.pallas_call(
        functools.partial(_peer_route_body, lane_tiles=tm // V7X_LANES),
        out_shape=[jax.ShapeDtypeStruct((T, D), BF16), tab, tab, tab, tab],
        grid=(T // tm,),
        in_specs=[pl.BlockSpec((tm, D), lambda i: (i, 0)), mod_spec, mod_spec,
                  pl.BlockSpec(wq_bf16.shape, lambda i: (0, 0)),
                  pl.BlockSpec(keys_bf16.shape, lambda i: (0, 0, 0))],
        out_specs=[pl.BlockSpec((tm, D), lambda i: (i, 0)), tab_spec, tab_spec, tab_spec, tab_spec],
        scratch_shapes=[pltpu.VMEM((tm, PEER_HEADS * PEER_QDIM), BF16)],
        compiler_params=pltpu.CompilerParams(dimension_semantics=("arbitrary",),
                                             vmem_limit_bytes=V7X_VMEM_LIMIT_BYTES),
        name="peer_route",
    )(x, shift, scale, wq_bf16, keys_bf16)


def _peer_dense_body(h_ref, u_ref, vt_ref, n1_ref, c1_ref, r2_ref, e2_ref, o_ref, acc_ref, a_ref, *, n_slab):
    j = pl.program_id(1)

    @pl.when(j == 0)
    def _():
        acc_ref[...] = jnp.zeros_like(acc_ref)

    s = lax.dot_general(u_ref[...], h_ref[...], (((1,), (1,)), ((), ())), preferred_element_type=F32)
    for k in range(n_slab):
        g = None
        for hd in range(PEER_HEADS):
            gh = jnp.where(r2_ref[hd] < n1_ref[hd, k:k + 1, :], c1_ref[hd, k:k + 1, :] * e2_ref[hd], 0.0)
            g = gh if g is None else g + gh
        sk = s[k * PEER_NK:(k + 1) * PEER_NK]
        a_ref[k * PEER_NK:(k + 1) * PEER_NK, :] = (jax.nn.gelu(sk) * g).astype(BF16)
    acc_ref[...] += jnp.dot(vt_ref[...], a_ref[...], preferred_element_type=F32)

    @pl.when(j == pl.num_programs(1) - 1)
    def _():
        o_ref[...] = acc_ref[...].T


def peer_dense(h_bf16, u_bf16, vt_bf16, n1, c1, r2, e2, tm, n_slab):
    T, D = h_bf16.shape
    NE = u_bf16.shape[0]
    e_tile = n_slab * PEER_NK
    assert T % tm == 0 and NE % e_tile == 0 and vt_bf16.shape == (D, NE)
    slab_spec = pl.BlockSpec((PEER_HEADS, n_slab, tm), lambda i, j: (0, j, i))
    tok_spec = pl.BlockSpec((PEER_HEADS, PEER_NK, tm), lambda i, j: (0, 0, i))
    return pl.pallas_call(
        functools.partial(_peer_dense_body, n_slab=n_slab),
        out_shape=jax.ShapeDtypeStruct((T, D), F32),
        grid=(T // tm, NE // e_tile),
        in_specs=[pl.BlockSpec((tm, D), lambda i, j: (i, 0)),
                  pl.BlockSpec((e_tile, D), lambda i, j: (j, 0)),
                  pl.BlockSpec((D, e_tile), lambda i, j: (0, j)),
                  slab_spec, slab_spec, tok_spec, tok_spec],
        out_specs=pl.BlockSpec((tm, D), lambda i, j: (i, 0)),
        scratch_shapes=[pltpu.VMEM((D, tm), F32), pltpu.VMEM((e_tile, tm), BF16)],
        compiler_params=pltpu.CompilerParams(dimension_semantics=("arbitrary", "arbitrary"),
                                             vmem_limit_bytes=V7X_VMEM_LIMIT_BYTES),
        name="peer_dense",
    )(h_bf16, u_bf16, vt_bf16, n1, c1, r2, e2)


def peer_ffn_pallas(x, shift, scale, wq_bf16, keys_bf16, u_bf16, vt_bf16, rows_per_mod, tm_route=256,
                    tm_dense=512, n_slab=8):
    h, n1, c1, r2, e2 = peer_route(x, shift, scale, wq_bf16, keys_bf16, rows_per_mod, tm_route)
    return peer_dense(h, u_bf16, vt_bf16, n1, c1, r2, e2, tm_dense, n_slab)


POOL_HALO = 8
POOL_ROWS = 256


def _pool_body(u_ref, w_ref, scale_ref, o_ref, pad_ref, *, seq_len):
    L = seq_len
    chunk = min(POOL_ROWS, L)
    zeros = jnp.zeros((POOL_HALO, W_GRP), F32)
    pad_ref[0:POOL_HALO, :] = zeros
    pad_ref[POOL_HALO + L:2 * POOL_HALO + L, :] = zeros
    pad_ref[POOL_HALO:POOL_HALO + L, :] = u_ref[0]
    for r0 in range(0, L, chunk):
        t = r0 + lax.broadcasted_iota(jnp.int32, (chunk, POOL_C), 0)
        for j, w in enumerate(POOL_WINDOWS):
            cols = slice(j * POOL_C, (j + 1) * POOL_C)
            acc = None
            for o in range(-(w // 2), w - w // 2):
                s = pad_ref[POOL_HALO + r0 + o:POOL_HALO + r0 + o + chunk, cols]
                acc = s if acc is None else acc + s
            lo = jnp.maximum(t - w // 2, 0)
            hi = jnp.minimum(t - w // 2 + w - 1, L - 1)
            cnt = (hi - lo + 1).astype(F32)
            pooled = acc / cnt - u_ref[0, r0:r0 + chunk, cols]
            y = jnp.dot(pooled.astype(BF16), w_ref[j], preferred_element_type=F32)
            o_ref[0, r0:r0 + chunk, cols] = y * scale_ref[:, cols]


def pool_mix_pallas(u, pool_w, pool_scale):
    B_, L, W = u.shape
    assert max(POOL_WINDOWS) // 2 <= POOL_HALO and L % min(POOL_ROWS, L) == 0
    blk = pl.BlockSpec((1, L, W), lambda b: (b, 0, 0))
    return pl.pallas_call(
        functools.partial(_pool_body, seq_len=L),
        out_shape=jax.ShapeDtypeStruct((B_, L, W), F32),
        grid=(B_,),
        in_specs=[blk, pl.BlockSpec(pool_w.shape, lambda b: (0, 0, 0)), pl.BlockSpec((1, W), lambda b: (0, 0))],
        out_specs=blk,
        scratch_shapes=[pltpu.VMEM((L + 2 * POOL_HALO, W), F32)],
        compiler_params=pltpu.CompilerParams(dimension_semantics=("arbitrary",),
                                             vmem_limit_bytes=V7X_VMEM_LIMIT_BYTES),
        name="pool_mix",
    )(u, pool_w.astype(BF16), pool_scale.reshape(1, W))


def s5_discretize(a_re, a_im, log_dt, b_re, b_im):
    a_re, a_im = a_re.astype(F32), a_im.astype(F32)
    b_re, b_im = b_re.astype(F32), b_im.astype(F32)
    dt = jnp.exp(log_dt.astype(F32))[:, None]
    mag = jnp.exp(a_re * dt)
    ab_re = mag * jnp.cos(a_im * dt)
    ab_im = mag * jnp.sin(a_im * dt)
    den = a_re * a_re + a_im * a_im
    f_re = ((ab_re - 1) * a_re + ab_im * a_im) / den
    f_im = (ab_im * a_re - (ab_re - 1) * a_im) / den
    bb_re = f_re[..., None] * b_re - f_im[..., None] * b_im
    bb_im = f_re[..., None] * b_im + f_im[..., None] * b_re
    return ab_re, ab_im, bb_re, bb_im


def _cscan_combine(e1, e2):
    a1r, a1i, b1r, b1i = e1
    a2r, a2i, b2r, b2i = e2
    return (a2r * a1r - a2i * a1i, a2r * a1i + a2i * a1r,
            a2r * b1r - a2i * b1i + b2r, a2r * b1i + a2i * b1r + b2i)


def s5_states(u, ab_re, ab_im, bb_re, bb_im, h0_re, h0_im):
    bu_re = jnp.einsum('gnp,blgp->blgn', bb_re, u)
    bu_im = jnp.einsum('gnp,blgp->blgn', bb_im, u)
    bu_re = bu_re.at[:, 0].add(ab_re * h0_re - ab_im * h0_im)
    bu_im = bu_im.at[:, 0].add(ab_re * h0_im + ab_im * h0_re)
    a_re = jnp.broadcast_to(ab_re, bu_re.shape)
    a_im = jnp.broadcast_to(ab_im, bu_im.shape)
    _, _, h_re, h_im = lax.associative_scan(_cscan_combine, (a_re, a_im, bu_re, bu_im), axis=1)
    return h_re, h_im


def s5_readout(c_re, c_im, h_re, h_im):
    return (jnp.einsum('gpn,blgn->blgp', c_re.astype(F32), h_re)
            - jnp.einsum('gpn,blgn->blgp', c_im.astype(F32), h_im))


def s5_glu(y, w_glu, b_glu, dtype):
    B_, L = y.shape[:2]
    g = jax.nn.gelu(y.reshape(B_, L, W_GRP))
    return (g * jax.nn.sigmoid(g @ w_glu.astype(F32) + b_glu.astype(F32))).astype(dtype)


def s5_mixer(u_lat, u_ctx, a_re, a_im, log_dt, b_re, b_im, c_re, c_im, d, w_glu, b_glu, with_ctx_out):
    B_ = u_lat.shape[0]
    ul = u_lat.astype(F32).reshape(B_, -1, S5_G, S5_P)
    uc = u_ctx.astype(F32).reshape(B_, -1, S5_G, S5_P)
    dg = d.astype(F32).reshape(S5_G, S5_P)
    zero = jnp.zeros((B_, S5_G, S5_N), F32)
    y_lat = dg * ul
    y_ctx = dg * uc if with_ctx_out else None
    for di in range(2):
        rev = di == 1
        ab_re, ab_im, bb_re, bb_im = s5_discretize(a_re[di], a_im[di], log_dt[di], b_re[di], b_im[di])
        hc_re, hc_im = s5_states(_flip(uc, rev), ab_re, ab_im, bb_re, bb_im, zero, zero)
        hl_re, hl_im = s5_states(_flip(ul, rev), ab_re, ab_im, bb_re, bb_im, hc_re[:, -1], hc_im[:, -1])
        y_lat = y_lat + _flip(s5_readout(c_re[di], c_im[di], hl_re, hl_im), rev)
        if with_ctx_out:
            y_ctx = y_ctx + _flip(s5_readout(c_re[di], c_im[di], hc_re, hc_im), rev)
    out_lat = s5_glu(y_lat, w_glu, b_glu, u_lat.dtype)
    out_ctx = s5_glu(y_ctx, w_glu, b_glu, u_ctx.dtype) if with_ctx_out else None
    return out_lat, out_ctx


def pool_mix(u, pool_w, pool_scale):
    B_, L, _ = u.shape
    uf = u.astype(F32)
    csum = jnp.pad(jnp.cumsum(uf, axis=1), ((0, 0), (1, 0), (0, 0)))
    t = jnp.arange(L)
    outs = []
    for j, w in enumerate(POOL_WINDOWS):
        lo = jnp.clip(t - w // 2, 0, L - 1)
        hi = jnp.clip(t - w // 2 + w - 1, 0, L - 1)
        ch = slice(j * POOL_C, (j + 1) * POOL_C)
        cnt = (hi - lo + 1).astype(F32)[None, :, None]
        outs.append((csum[:, hi + 1, ch] - csum[:, lo, ch]) / cnt - uf[:, :, ch])
    pooled = jnp.stack(outs, axis=2)
    y = jnp.einsum('blgc,gcd->blgd', pooled, pool_w.astype(F32)).reshape(B_, L, W_GRP)
    return (y * pool_scale.astype(F32)).astype(u.dtype)


def na_latent(q, k, v, k_ctx, v_ctx, rpb):
    B_, L, _ = q.shape
    R = L // GRID_W
    KR = min(NA_WIN_R, R)
    KC = NA_WIN_C
    shp = (B_, R, GRID_W, NA_H, NA_HD)
    q, k, v = q.reshape(shp), k.reshape(shp), v.reshape(shp)
    r = jnp.arange(R)
    row_idx = jnp.clip(r - KR // 2, 0, R - KR)[:, None] + jnp.arange(KR)[None, :]
    k_rows = k[:, row_idx]
    v_rows = v[:, row_idx]
    col = jnp.arange(GRID_W)
    c0 = jnp.clip(col - KC // 2, 0, GRID_W - KC)
    in_win = (col[None, :] >= c0[:, None]) & (col[None, :] < c0[:, None] + KC)
    col_rel = jnp.clip(col[None, :] - col[:, None], -(KC - 1), KC - 1) + KC - 1
    row_rel = row_idx - r[:, None] + NA_WIN_R - 1
    bias = rpb.astype(F32)[:, row_rel][..., col_rel]
    bias = bias.transpose(1, 0, 3, 2, 4)
    scale = NA_HD ** -0.5
    s_nb = jnp.einsum('brqhd,brkwhd->brhqkw', q, k_rows).astype(F32) * scale + bias
    s_nb = jnp.where(in_win[:, None, :], s_nb, -jnp.inf).reshape(B_, R, NA_H, GRID_W, KR * GRID_W)
    s_cx = jnp.einsum('brqhd,bchd->brhqc', q, k_ctx).astype(F32) * scale
    p = jax.nn.softmax(jnp.concatenate([s_nb, s_cx], axis=-1), axis=-1).astype(v.dtype)
    p_nb = p[..., :KR * GRID_W].reshape(B_, R, NA_H, GRID_W, KR, GRID_W)
    p_cx = p[..., KR * GRID_W:]
    o = (jnp.einsum('brhqkw,brkwhd->brqhd', p_nb, v_rows)
         + jnp.einsum('brhqc,bchd->brqhd', p_cx, v_ctx))
    return o.reshape(B_, L, W_GRP)


def ctx_attn(q, k, v):
    s = jnp.einsum('bqhd,bkhd->bhqk', q, k).astype(F32) * NA_HD ** -0.5
    p = jax.nn.softmax(s, axis=-1).astype(v.dtype)
    return jnp.einsum('bhqk,bkhd->bqhd', p, v)


def na_mixer(q_l, k_l, v_l, q_c, k_c, v_c, rpb, with_ctx_out):
    B_, C_ = k_c.shape[:2]
    kc = k_c.reshape(B_, C_, NA_H, NA_HD)
    vc = v_c.reshape(B_, C_, NA_H, NA_HD)
    out_l = na_latent(q_l, k_l, v_l, kc, vc, rpb)
    out_c = (ctx_attn(q_c.reshape(B_, C_, NA_H, NA_HD), kc, vc).reshape(B_, C_, W_GRP)
             if with_ctx_out else None)
    return out_l, out_c


def dwconv(x, w, b):
    y = lax.conv_general_dilated(x, w[:, None, :].astype(x.dtype), window_strides=(1,),
                                 padding=[((SSM_CONV - 1) // 2, SSM_CONV // 2)],
                                 dimension_numbers=('NWC', 'WIO', 'NWC'),
                                 feature_group_count=x.shape[-1])
    return y + b


def segsum(x):
    T = x.shape[-1]
    cs = jnp.cumsum(x, axis=-1)
    d = cs[..., :, None] - cs[..., None, :]
    return jnp.where(jnp.tril(jnp.ones((T, T), bool)), d, -jnp.inf)


def ssd(X, A, Bm, Cm, h0):
    b, L, H, P = X.shape
    N = Bm.shape[-1]
    nc = L // SSM_CHUNK
    X = X.reshape(b, nc, SSM_CHUNK, H, P)
    Bm = Bm.reshape(b, nc, SSM_CHUNK, H, N)
    Cm = Cm.reshape(b, nc, SSM_CHUNK, H, N)
    A = A.reshape(b, nc, SSM_CHUNK, H).transpose(0, 3, 1, 2)
    A_cs = jnp.cumsum(A, axis=-1)
    cb = jnp.einsum('bclhn,bcshn->bhcls', Cm, Bm)
    y_diag = jnp.einsum('bhcls,bcshp->bclhp', cb * jnp.exp(segsum(A)), X)
    decay_states = jnp.exp(A_cs[..., -1:] - A_cs).transpose(0, 2, 3, 1)[..., None]
    states = jnp.einsum('bclhn,bclhp->bchpn', Bm, X * decay_states)
    states = jnp.concatenate([h0[:, None], states], axis=1)
    chunk_decay = jnp.exp(segsum(jnp.pad(A_cs[..., -1], ((0, 0), (0, 0), (1, 0)))))
    new_states = jnp.einsum('bhzc,bchpn->bzhpn', chunk_decay, states)
    prev, final = new_states[:, :-1], new_states[:, -1]
    y_off = (jnp.einsum('bclhn,bchpn->bclhp', Cm, prev)
             * jnp.exp(A_cs).transpose(0, 2, 3, 1)[..., None])
    return (y_diag + y_off).reshape(b, L, H, P), final


def ssm_prepare(pxbc, pdt, conv_w, conv_b, dt_bias):
    B_, L, _ = pxbc.shape
    xbc = jax.nn.silu(dwconv(pxbc, conv_w, conv_b).astype(F32))
    hpg = SSM_H // SSM_G
    xs = xbc[..., :W_GRP].reshape(B_, L, SSM_H, SSM_HD)
    bm = jnp.repeat(xbc[..., W_GRP:W_GRP + SSM_G * SSM_N].reshape(B_, L, SSM_G, SSM_N), hpg, axis=2)
    cm = jnp.repeat(xbc[..., W_GRP + SSM_G * SSM_N:].reshape(B_, L, SSM_G, SSM_N), hpg, axis=2)
    dt = jax.nn.softplus(pdt.astype(F32).reshape(B_, L, 2, SSM_H) + dt_bias.astype(F32))
    return xs, bm, cm, dt


def ssm_direction(xs, bm, cm, dt, a, h0, rev):
    y, h_fin = ssd(_flip(xs * dt[..., None], rev), _flip(dt * a, rev), _flip(bm, rev), _flip(cm, rev), h0)
    return _flip(y, rev), h_fin


def ssm_gate_norm(y, z, norm_w):
    B_, L = z.shape[:2]
    g = (y.reshape(B_, L, W_GRP) * jax.nn.silu(z.astype(F32))).reshape(B_, L, SSM_G, W_GRP // SSM_G)
    g = g * lax.rsqrt(jnp.mean(jnp.square(g), -1, keepdims=True) + LN_EPS)
    return (g.reshape(B_, L, W_GRP) * norm_w.astype(F32)).astype(z.dtype)


def ssm_mixer(z_l, xbc_l, dtr_l, z_c, xbc_c, dtr_c, conv_w, conv_b, dt_bias, a_log, d, norm_w, with_ctx_out):
    xs_l, bm_l, cm_l, dt_l = ssm_prepare(xbc_l, dtr_l, conv_w, conv_b, dt_bias)
    xs_c, bm_c, cm_c, dt_c = ssm_prepare(xbc_c, dtr_c, conv_w, conv_b, dt_bias)
    a = -jnp.exp(a_log.astype(F32))
    dh = d.astype(F32)[:, None]
    zero = jnp.zeros((xs_c.shape[0], SSM_H, SSM_HD, SSM_N), F32)
    y_l = dh * xs_l
    y_c = dh * xs_c if with_ctx_out else None
    for di in range(2):
        rev = di == 1
        yc, hc = ssm_direction(xs_c, bm_c, cm_c, dt_c[:, :, di], a[di], zero, rev)
        yl, _ = ssm_direction(xs_l, bm_l, cm_l, dt_l[:, :, di], a[di], hc, rev)
        y_l = y_l + yl
        if with_ctx_out:
            y_c = y_c + yc
    out_l = ssm_gate_norm(y_l, z_l, norm_w)
    out_c = ssm_gate_norm(y_c, z_c, norm_w) if with_ctx_out else None
    return out_l, out_c


def peer_ffn(t, w_q, sub_keys, u_tab, v_tab):
    n_tok = t.shape[0]
    q = (t @ w_q).reshape(n_tok, PEER_HEADS, 2, PEER_QDIM // 2)
    s = jnp.einsum('thsd,hskd->thsk', q, sub_keys).astype(F32)
    sv, si = lax.top_k(s, PEER_TOPK)
    cand_s = (sv[:, :, 0, :, None] + sv[:, :, 1, None, :]).reshape(n_tok, PEER_HEADS, PEER_TOPK * PEER_TOPK)
    cand_i = (si[:, :, 0, :, None] * PEER_NK + si[:, :, 1, None, :]).reshape(n_tok, PEER_HEADS, PEER_TOPK * PEER_TOPK)
    top_s, pos = lax.top_k(cand_s, PEER_TOPK)
    idx = jnp.take_along_axis(cand_i, pos, axis=-1).reshape(n_tok, PEER_HEADS * PEER_TOPK)
    gate = jax.nn.softmax(top_s, axis=-1).reshape(n_tok, PEER_HEADS * PEER_TOPK)
    n_blk = n_tok // PEER_BLOCK

    def expert_block(args):
        xb, ib, gb = args
        act = jnp.einsum('tkd,td->tk', u_tab[ib], xb).astype(F32)
        return jnp.einsum('tk,tkd->td', (jax.nn.gelu(act) * gb).astype(xb.dtype), v_tab[ib])

    out = lax.map(expert_block, (t.reshape(n_blk, PEER_BLOCK, -1),
                                 idx.reshape(n_blk, PEER_BLOCK, -1),
                                 gate.reshape(n_blk, PEER_BLOCK, -1)))
    return out.reshape(n_tok, -1)


def _mixers(p_l, p_c, prm, with_ctx_out):
    s5_l, pool_l, q_l, k_l, v_l, z_l, xbc_l, dt_l = p_l
    s5_c, pool_c, q_c, k_c, v_c, z_c, xbc_c, dt_c = p_c
    ya_l, ya_c = s5_mixer_pallas(s5_l, s5_c, prm["s5_a_re"], prm["s5_a_im"], prm["s5_log_dt"],
                                 prm["s5_b_re"], prm["s5_b_im"], prm["s5_c_re"], prm["s5_c_im"], prm["s5_d"],
                                 prm["s5_w_glu"], prm["s5_b_glu"], with_ctx_out)
    yb_l = pool_mix_pallas(pool_l, prm["pool_w"], prm["pool_scale"])
    yb_c = pool_mix_pallas(pool_c, prm["pool_w"], prm["pool_scale"]) if with_ctx_out else None
    yc_l, yc_c = na_mixer(q_l, k_l, v_l, q_c, k_c, v_c, prm["na_rpb"], with_ctx_out)
    yd_l, yd_c = ssm_mixer(z_l, xbc_l, dt_l, z_c, xbc_c, dt_c, prm["ssm_conv_w"], prm["ssm_conv_b"],
                           prm["ssm_dt_bias"], prm["ssm_a_log"], prm["ssm_d"], prm["ssm_norm_w"], with_ctx_out)
    return (ya_l, yb_l, yc_l, yd_l), (ya_c, yb_c, yc_c, yd_c)


def kernel(x, c, ctx, c_ctx, w_ada, b_ada, w_in, w_out, s5_a_re, s5_a_im, s5_log_dt, s5_b_re, s5_b_im, s5_c_re, s5_c_im, s5_d, s5_w_glu, s5_b_glu, pool_w, pool_scale, na_rpb, ssm_conv_w, ssm_conv_b, ssm_dt_bias, ssm_a_log, ssm_d, ssm_norm_w, ln1_g, ln1_b, ln2_g, ln2_b, peer_w_q, peer_sub_keys, peer_u, peer_v):
    B_, L, D = x.shape
    C_ = ctx.shape[1]
    x_lat = x.reshape(B_ * L, D)
    x_ctx = ctx.reshape(B_ * C_, D)
    act_lat = jax.nn.silu(c)
    act_ctx = jax.nn.silu(c_ctx)
    dt_pad = V7X_LANES - IN_SPLITS[-1]
    splits = IN_SPLITS[:-1] + (V7X_LANES,)
    TM = 256
    for l in range(DEPTH):
        last = l == DEPTH - 1
        prm = dict(s5_a_re=s5_a_re[l], s5_a_im=s5_a_im[l], s5_log_dt=s5_log_dt[l], s5_b_re=s5_b_re[l],
                   s5_b_im=s5_b_im[l], s5_c_re=s5_c_re[l], s5_c_im=s5_c_im[l], s5_d=s5_d[l],
                   s5_w_glu=s5_w_glu[l], s5_b_glu=s5_b_glu[l], pool_w=pool_w[l], pool_scale=pool_scale[l],
                   na_rpb=na_rpb[l], ssm_conv_w=ssm_conv_w[l], ssm_conv_b=ssm_conv_b[l],
                   ssm_dt_bias=ssm_dt_bias[l], ssm_a_log=ssm_a_log[l], ssm_d=ssm_d[l],
                   ssm_norm_w=ssm_norm_w[l])
        m_lat = (act_lat @ w_ada[l] + b_ada[l]).reshape(B_, 6, 1, D)
        m_ctx = (act_ctx @ w_ada[l] + b_ada[l]).reshape(1, 6, 1, D)
        w_in_b = jnp.pad(w_in[l], ((0, 0), (0, dt_pad))).astype(BF16)
        w_out_b = w_out[l].astype(BF16)

        p_l = modulated_matmul(x_lat, m_lat[:, 0], m_lat[:, 1], w_in_b, splits, L, TM)
        p_c = modulated_matmul(x_ctx, m_ctx[:, 0], m_ctx[:, 1], w_in_b, splits, B_ * C_, TM)
        p_l = [a.reshape(B_, L, -1) for a in p_l]
        p_c = [a.reshape(B_, C_, -1) for a in p_c]
        p_l[-1] = p_l[-1][..., :IN_SPLITS[-1]]
        p_c[-1] = p_c[-1][..., :IN_SPLITS[-1]]
        y_l, y_c = _mixers(p_l, p_c, prm, not last)

        x_lat = proj_residual_ln([a.reshape(B_ * L, -1) for a in y_l], x_lat, m_lat[:, 2], w_out_b,
                                 ln1_g[l], ln1_b[l], L, TM)
        wq_b = peer_w_q[l].astype(BF16)
        keys_b = peer_sub_keys[l].reshape(2 * PEER_HEADS, PEER_NK, PEER_HALF).astype(BF16)
        u_b = peer_u[l].astype(BF16)
        vt_b = peer_v[l].T.astype(BF16)
        f_lat = peer_ffn_pallas(x_lat, m_lat[:, 3], m_lat[:, 4], wq_b, keys_b, u_b, vt_b, L)
        x_lat = residual_ln(x_lat, f_lat, m_lat[:, 5], ln2_g[l], ln2_b[l], L, TM)
        if not last:
            x_ctx = proj_residual_ln([a.reshape(B_ * C_, -1) for a in y_c], x_ctx, m_ctx[:, 2], w_out_b,
                                     ln1_g[l], ln1_b[l], B_ * C_, TM)
            f_ctx = peer_ffn_pallas(x_ctx, m_ctx[:, 3], m_ctx[:, 4], wq_b, keys_b, u_b, vt_b, B_ * C_)
            x_ctx = residual_ln(x_ctx, f_ctx, m_ctx[:, 5], ln2_g[l], ln2_b[l], B_ * C_, TM)
    return x_lat.reshape(B_, L, D)
```

```python
import functools
import math

import jax
import jax.numpy as jnp
from jax import lax
from jax.experimental import pallas as pl
from jax.experimental.pallas import tpu as pltpu

D_MODEL = 2048
BATCH = 4
SEQ = 4096
DEPTH = 2

GRID_W = 64
CTX_LEN = 256
N_MIXERS = 4
D_MIX = D_MODEL
W_GRP = D_MIX // N_MIXERS

S5_P = 16
S5_G = W_GRP // S5_P
S5_N = 64

POOL_WINDOWS = (2, 4, 8, 16)
POOL_C = W_GRP // len(POOL_WINDOWS)

NA_HD = 64
NA_H = W_GRP // NA_HD
NA_WIN_R = 8
NA_WIN_C = 16

SSM_HD = 64
SSM_H = W_GRP // SSM_HD
SSM_G = 2
SSM_N = 128
SSM_CONV = 4
SSM_CHUNK = 128
SSM_CONV_CH = W_GRP + 2 * SSM_G * SSM_N

PEER_HEADS = 8
PEER_NK = 128
PEER_NE = PEER_NK * PEER_NK
PEER_QDIM = 256
PEER_TOPK = 16
PEER_BLOCK = 128

IN_SPLITS = (W_GRP, W_GRP, W_GRP, W_GRP, W_GRP, W_GRP, SSM_CONV_CH, 2 * SSM_H)
D_IN = sum(IN_SPLITS)

DEEPNORM_ALPHA = (2 * DEPTH) ** 0.25
DEEPNORM_BETA = (8 * DEPTH) ** -0.25
LN_EPS = 1e-5
F32 = jnp.float32
BF16 = jnp.bfloat16

V7X_LANES = 128
V7X_VMEM_LIMIT_BYTES = 56 * 1024 * 1024


def _flip(t, rev):
    return t[:, ::-1] if rev else t


def _modmm_body(x_ref, shift_ref, scale_ref, w_ref, *out_refs, col_splits):
    xm = (x_ref[...] * (1.0 + scale_ref[0]) + shift_ref[0]).astype(BF16)
    o = 0
    for ref, n in zip(out_refs, col_splits):
        ref[...] = jnp.dot(xm, w_ref[:, o:o + n], preferred_element_type=F32).astype(ref.dtype)
        o += n


def modulated_matmul(x, shift, scale, w_bf16, col_splits, out_dtypes, rows_per_mod, tm):
    T, K = x.shape
    assert T % tm == 0 and rows_per_mod % tm == 0
    tiles_per_mod = rows_per_mod // tm
    n_tot = sum(col_splits)
    assert w_bf16.shape == (K, n_tot)
    mod_spec = pl.BlockSpec((1, 1, K), lambda i: (i // tiles_per_mod, 0, 0))
    return pl.pallas_call(
        functools.partial(_modmm_body, col_splits=tuple(col_splits)),
        out_shape=[jax.ShapeDtypeStruct((T, n), dt) for n, dt in zip(col_splits, out_dtypes, strict=True)],
        grid=(T // tm,),
        in_specs=[pl.BlockSpec((tm, K), lambda i: (i, 0)), mod_spec, mod_spec,
                  pl.BlockSpec((K, n_tot), lambda i: (0, 0))],
        out_specs=[pl.BlockSpec((tm, n), lambda i: (i, 0)) for n in col_splits],
        compiler_params=pltpu.CompilerParams(dimension_semantics=("arbitrary",),
                                             vmem_limit_bytes=V7X_VMEM_LIMIT_BYTES),
        name="modulated_matmul",
    )(x, shift, scale, w_bf16)


def _proj_ln_body(*refs, n_parts, alpha):
    part_refs = refs[:n_parts]
    x_ref, gate_ref, w_ref, g_ref, b_ref, o_ref = refs[n_parts:]
    acc = None
    o = 0
    for pr in part_refs:
        n = pr.shape[-1]
        d = jnp.dot(pr[...].astype(BF16), w_ref[o:o + n, :], preferred_element_type=F32)
        acc = d if acc is None else acc + d
        o += n
    h = alpha * x_ref[...] + gate_ref[0] * acc
    mu = jnp.mean(h, -1, keepdims=True)
    hc = h - mu
    var = jnp.mean(hc * hc, -1, keepdims=True)
    o_ref[...] = hc * lax.rsqrt(var + LN_EPS) * g_ref[...] + b_ref[...]


def proj_residual_ln(parts, x, gate, w_bf16, g, b, rows_per_mod, tm):
    T, D = x.shape
    assert T % tm == 0 and rows_per_mod % tm == 0
    tiles_per_mod = rows_per_mod // tm
    k_tot = sum(p.shape[-1] for p in parts)
    assert w_bf16.shape == (k_tot, D)
    row = lambda n: pl.BlockSpec((tm, n), lambda i: (i, 0))
    vec = pl.BlockSpec((1, D), lambda i: (0, 0))
    return pl.pallas_call(
        functools.partial(_proj_ln_body, n_parts=len(parts), alpha=DEEPNORM_ALPHA),
        out_shape=jax.ShapeDtypeStruct((T, D), F32),
        grid=(T // tm,),
        in_specs=[row(p.shape[-1]) for p in parts] + [
            row(D), pl.BlockSpec((1, 1, D), lambda i: (i // tiles_per_mod, 0, 0)),
            pl.BlockSpec((k_tot, D), lambda i: (0, 0)), vec, vec],
        out_specs=row(D),
        compiler_params=pltpu.CompilerParams(dimension_semantics=("arbitrary",),
                                             vmem_limit_bytes=V7X_VMEM_LIMIT_BYTES),
        name="proj_residual_ln",
    )(*parts, x, gate, w_bf16, g.reshape(1, D), b.reshape(1, D))


def _res_ln_body(x_ref, f_ref, gate_ref, g_ref, b_ref, o_ref, *, alpha):
    h = alpha * x_ref[...] + gate_ref[0] * f_ref[...]
    mu = jnp.mean(h, -1, keepdims=True)
    hc = h - mu
    var = jnp.mean(hc * hc, -1, keepdims=True)
    o_ref[...] = hc * lax.rsqrt(var + LN_EPS) * g_ref[...] + b_ref[...]


def residual_ln(x, f, gate, g, b, rows_per_mod, tm):
    T, D = x.shape
    assert T % tm == 0 and rows_per_mod % tm == 0
    tiles_per_mod = rows_per_mod // tm
    row = pl.BlockSpec((tm, D), lambda i: (i, 0))
    vec = pl.BlockSpec((1, D), lambda i: (0, 0))
    return pl.pallas_call(
        functools.partial(_res_ln_body, alpha=DEEPNORM_ALPHA),
        out_shape=jax.ShapeDtypeStruct((T, D), F32),
        grid=(T // tm,),
        in_specs=[row, row, pl.BlockSpec((1, 1, D), lambda i: (i // tiles_per_mod, 0, 0)), vec, vec],
        out_specs=row,
        compiler_params=pltpu.CompilerParams(dimension_semantics=("arbitrary",),
                                             vmem_limit_bytes=V7X_VMEM_LIMIT_BYTES),
        name="residual_ln",
    )(x, f, gate, g.reshape(1, D), b.reshape(1, D))


S5_CHAINS = 2 * BATCH
S5_STATE = S5_G * S5_N
S5_SCAN_COLS = 512


def _s5_scan_body(u_ref, wbf_ref, wbb_ref, wc_ref, are_ref, aim_ref, y_ref, bu_ref, h_ref, *, steps):
    rows = steps * S5_CHAINS

    @pl.when(pl.program_id(0) == 0)
    def _():
        h_ref[...] = jnp.zeros_like(h_ref)

    u = u_ref[...]
    chain = lax.broadcasted_iota(jnp.int32, u.shape, 0) % S5_CHAINS
    fwd = chain < BATCH
    zero = jnp.zeros_like(u)
    bu_ref[...] = (jnp.dot(jnp.where(fwd, u, zero), wbf_ref[...], preferred_element_type=F32)
                   + jnp.dot(jnp.where(fwd, zero, u), wbb_ref[...], preferred_element_type=F32))

    for cb in range(S5_STATE // S5_SCAN_COLS):
        re = pl.ds(cb * S5_SCAN_COLS, S5_SCAN_COLS)
        im = pl.ds(S5_STATE + cb * S5_SCAN_COLS, S5_SCAN_COLS)
        a_re = are_ref[:, re]
        a_im = aim_ref[:, re]

        def step(s, carry):
            h_re, h_im = carry
            r = pl.ds(pl.multiple_of(s * S5_CHAINS, S5_CHAINS), S5_CHAINS)
            n_re = a_re * h_re - a_im * h_im + bu_ref[r, re]
            n_im = a_re * h_im + a_im * h_re + bu_ref[r, im]
            bu_ref[r, re] = n_re
            bu_ref[r, im] = n_im
            return n_re, n_im

        h_re, h_im = lax.fori_loop(0, steps, step, (h_ref[:, re], h_ref[:, im]), unroll=4)
        h_ref[:, re] = h_re
        h_ref[:, im] = h_im

    y2 = jnp.dot(bu_ref[...].astype(BF16), wc_ref[...], preferred_element_type=F32)
    chain_y = lax.broadcasted_iota(jnp.int32, (rows, W_GRP), 0) % S5_CHAINS
    y_ref[...] = jnp.where(chain_y < BATCH, y2[:, :W_GRP], y2[:, W_GRP:])


def s5_scan(u8, wbf, wbb, wc, a_re8, a_im8, steps):
    n_rows = u8.shape[0]
    rows = steps * S5_CHAINS
    assert n_rows % rows == 0
    full = lambda a: pl.BlockSpec(a.shape, lambda i: (0,) * a.ndim)
    return pl.pallas_call(
        functools.partial(_s5_scan_body, steps=steps),
        out_shape=jax.ShapeDtypeStruct((n_rows, W_GRP), F32),
        grid=(n_rows // rows,),
        in_specs=[pl.BlockSpec((rows, W_GRP), lambda i: (i, 0)), full(wbf), full(wbb), full(wc),
                  full(a_re8), full(a_im8)],
        out_specs=pl.BlockSpec((rows, W_GRP), lambda i: (i, 0)),
        scratch_shapes=[pltpu.VMEM((rows, 2 * S5_STATE), F32), pltpu.VMEM((S5_CHAINS, 2 * S5_STATE), F32)],
        compiler_params=pltpu.CompilerParams(dimension_semantics=("arbitrary",),
                                             vmem_limit_bytes=V7X_VMEM_LIMIT_BYTES),
        name="s5_scan",
    )(u8, wbf, wbb, wc, a_re8, a_im8)


def _s5_glu_body(u_ref, yf_ref, yb_ref, d_ref, w_ref, b_ref, o_ref):
    y = d_ref[...] * u_ref[...] + yf_ref[...] + yb_ref[...]
    g = jax.nn.gelu(y)
    z = jnp.dot(g.astype(BF16), w_ref[...], preferred_element_type=F32) + b_ref[...]
    o_ref[...] = g * jax.nn.sigmoid(z)


def s5_glu_pallas(u, yf, yb, d, w_bf16, b, tm):
    T, W = u.shape
    assert T % tm == 0
    row = pl.BlockSpec((tm, W), lambda i: (i, 0))
    vec = pl.BlockSpec((1, W), lambda i: (0, 0))
    return pl.pallas_call(
        _s5_glu_body,
        out_shape=jax.ShapeDtypeStruct((T, W), F32),
        grid=(T // tm,),
        in_specs=[row, row, row, vec, pl.BlockSpec((W, W), lambda i: (0, 0)), vec],
        out_specs=row,
        compiler_params=pltpu.CompilerParams(dimension_semantics=("arbitrary",),
                                             vmem_limit_bytes=V7X_VMEM_LIMIT_BYTES),
        name="s5_glu",
    )(u, yf, yb, d.reshape(1, W), w_bf16, b.reshape(1, W))


def _s5_weights(a_re, a_im, log_dt, b_re, b_im, c_re, c_im):
    eye = jnp.eye(S5_G, dtype=F32)
    wb, a8 = [], []
    wc = []
    for di in range(2):
        ab_re, ab_im, bb_re, bb_im = s5_discretize(a_re[di], a_im[di], log_dt[di], b_re[di], b_im[di])
        blk = lambda m: jnp.einsum('gnp,gh->gphn', m, eye).reshape(W_GRP, S5_STATE)
        wb.append(jnp.concatenate([blk(bb_re), blk(bb_im)], axis=1).astype(BF16))
        a8.append((jnp.broadcast_to(ab_re.reshape(1, S5_STATE), (BATCH, S5_STATE)),
                   jnp.broadcast_to(ab_im.reshape(1, S5_STATE), (BATCH, S5_STATE))))
        cblk = lambda m: jnp.einsum('gpn,gh->gnhp', m.astype(F32), eye).reshape(S5_STATE, W_GRP)
        wc.append(jnp.concatenate([cblk(c_re[di]), -cblk(c_im[di])], axis=0))
    wc = jnp.concatenate(wc, axis=1).astype(BF16)
    a_re8 = jnp.concatenate([a8[0][0], a8[1][0]], axis=0)
    a_im8 = jnp.concatenate([a8[0][1], a8[1][1]], axis=0)
    return wb[0], wb[1], wc, a_re8, a_im8


def s5_mixer_pallas(u_lat, u_ctx, a_re, a_im, log_dt, b_re, b_im, c_re, c_im, d, w_glu, b_glu,
                    with_ctx_out, steps=64, tm=512):
    B_, L, W = u_lat.shape
    C_ = u_ctx.shape[1]
    assert B_ == BATCH and W == W_GRP
    wbf, wbb, wc, a_re8, a_im8 = _s5_weights(a_re, a_im, log_dt, b_re, b_im, c_re, c_im)
    seq_f = jnp.concatenate([u_ctx, u_lat], axis=1)
    seq_b = jnp.concatenate([u_ctx[:, ::-1], u_lat[:, ::-1]], axis=1)
    u8 = jnp.concatenate([seq_f, seq_b], axis=0).transpose(1, 0, 2).reshape((C_ + L) * S5_CHAINS, W)
    y8 = s5_scan(u8.astype(BF16), wbf, wbb, wc, a_re8, a_im8, steps)
    y8 = y8.reshape(C_ + L, S5_CHAINS, W).transpose(1, 0, 2)
    yf, yb = y8[:B_], y8[B_:]
    w_glu_b = w_glu.astype(BF16)
    out_lat = s5_glu_pallas(u_lat.reshape(B_ * L, W), yf[:, C_:].reshape(B_ * L, W),
                            yb[:, C_:][:, ::-1].reshape(B_ * L, W), d, w_glu_b, b_glu, tm)
    out_ctx = None
    if with_ctx_out:
        out_ctx = s5_glu_pallas(u_ctx.reshape(B_ * C_, W), yf[:, :C_].reshape(B_ * C_, W),
                                yb[:, :C_][:, ::-1].reshape(B_ * C_, W), d, w_glu_b, b_glu,
                                min(tm, B_ * C_))
    return out_lat, out_ctx


PEER_HALF = PEER_QDIM // 2


def _argmax_rows(v, r):
    while v.shape[0] > 1:
        half = v.shape[0] // 2
        take_hi = v[half:] > v[:half]
        r = jnp.where(take_hi, r[half:], r[:half])
        v = jnp.maximum(v[:half], v[half:])
    return v, r


def _topk_rows(x, k):
    row = lax.broadcasted_iota(jnp.int32, x.shape, 0).astype(F32)
    krow = lax.broadcasted_iota(jnp.int32, (k, x.shape[1]), 0)

    def body(it, carry):
        x, rank, vals = carry
        m, first = _argmax_rows(x, row)
        hit = row == first
        return (jnp.where(hit, -jnp.inf, x), jnp.where(hit, it.astype(F32), rank), jnp.where(krow == it, m, vals))

    init = (x, jnp.full(x.shape, float(k), F32), jnp.zeros((k, x.shape[1]), F32))
    _, rank, vals = lax.fori_loop(0, k, body, init)
    return vals, rank


PEER_GRID_COLS = tuple(PEER_TOPK // (i + 1) for i in range(PEER_TOPK))
PEER_GRID_ROWS = 64


def _peer_route_body(x_ref, shift_ref, scale_ref, wq_ref, keys_ref, h_ref, n1_ref, c1_ref, r2_ref, e2_ref,
                     q_ref, *, lane_tiles):
    hm = (x_ref[...] * (1.0 + scale_ref[0]) + shift_ref[0]).astype(BF16)
    h_ref[...] = hm
    q_ref[...] = jnp.dot(hm, wq_ref[...], preferred_element_type=F32).astype(BF16)
    K = PEER_TOPK
    for hd in range(PEER_HEADS):
        for lt in range(lane_tiles):
            tok = pl.ds(lt * V7X_LANES, V7X_LANES)
            sc = []
            for side in range(2):
                col = (2 * hd + side) * PEER_HALF
                qs = q_ref[tok, col:col + PEER_HALF]
                sc.append(lax.dot_general(keys_ref[2 * hd + side], qs, (((1,), (1,)), ((), ())),
                                          preferred_element_type=F32))
            v1, r1 = _topk_rows(sc[0], K)
            v2, r2 = _topk_rows(sc[1], K)
            cells = [v1[i:i + 1] + v2[:PEER_GRID_COLS[i]] for i in range(K)]
            cells.append(jnp.full((PEER_GRID_ROWS - sum(PEER_GRID_COLS), V7X_LANES), -jnp.inf, F32))
            vc, rc = _topk_rows(jnp.concatenate(cells, axis=0), K)
            z = jnp.sum(jnp.exp(vc - vc[0:1]), axis=0, keepdims=True)
            chosen = jnp.where(rc < float(K), 1.0, 0.0)
            n1 = jnp.zeros_like(r1)
            off = 0
            for i in range(K):
                n_i = jnp.sum(chosen[off:off + PEER_GRID_COLS[i]], axis=0, keepdims=True)
                n1 = jnp.where(r1 == float(i), n_i, n1)
                off += PEER_GRID_COLS[i]
            n1_ref[hd, :, tok] = n1
            c1_ref[hd, :, tok] = jnp.exp(sc[0] - v1[0:1]) / z
            r2_ref[hd, :, tok] = r2.astype(BF16)
            e2_ref[hd, :, tok] = jnp.exp(sc[1] - v2[0:1]).astype(BF16)


def peer_route(x, shift, scale, wq_bf16, keys_bf16, rows_per_mod, tm):
    T, D = x.shape
    assert T % tm == 0 and rows_per_mod % tm == 0 and tm % V7X_LANES == 0
    tiles_per_mod = rows_per_mod // tm
    mod_spec = pl.BlockSpec((1, 1, D), lambda i: (i // tiles_per_mod, 0, 0))
    tab = lambda dt: jax.ShapeDtypeStruct((PEER_HEADS, PEER_NK, T), dt)
    tab_spec = pl.BlockSpec((PEER_HEADS, PEER_NK, tm), lambda i: (0, 0, i))
    return pl.pallas_call(
        functools.partial(_peer_route_body, lane_tiles=tm // V7X_LANES),
        out_shape=[jax.ShapeDtypeStruct((T, D), BF16), tab(F32), tab(F32), tab(BF16), tab(BF16)],
        grid=(T // tm,),
        in_specs=[pl.BlockSpec((tm, D), lambda i: (i, 0)), mod_spec, mod_spec,
                  pl.BlockSpec(wq_bf16.shape, lambda i: (0, 0)),
                  pl.BlockSpec(keys_bf16.shape, lambda i: (0, 0, 0))],
        out_specs=[pl.BlockSpec((tm, D), lambda i: (i, 0)), tab_spec, tab_spec, tab_spec, tab_spec],
        scratch_shapes=[pltpu.VMEM((tm, PEER_HEADS * PEER_QDIM), BF16)],
        compiler_params=pltpu.CompilerParams(dimension_semantics=("arbitrary",),
                                             vmem_limit_bytes=V7X_VMEM_LIMIT_BYTES),
        name="peer_route",
    )(x, shift, scale, wq_bf16, keys_bf16)


def _peer_dense_body(h_ref, u_ref, vt_ref, n1_ref, c1_ref, r2_ref, e2_ref, o_ref, acc_ref, a_ref, *, n_slab):
    j = pl.program_id(1)

    @pl.when(j == 0)
    def _():
        acc_ref[...] = jnp.zeros_like(acc_ref)

    pair = 2 * PEER_NK
    for p in range(n_slab // 2):
        s = lax.dot_general(u_ref[p * pair:(p + 1) * pair, :], h_ref[...], (((1,), (1,)), ((), ())),
                            preferred_element_type=F32)
        for kk in range(2):
            k = 2 * p + kk
            for lt in range(h_ref.shape[0] // V7X_LANES):
                lanes = slice(lt * V7X_LANES, (lt + 1) * V7X_LANES)
                g = None
                for hd in range(PEER_HEADS):
                    n1row = n1_ref[hd, k:k + 1, lanes].astype(BF16)
                    c1row = c1_ref[hd, k:k + 1, lanes].astype(BF16)
                    gh = jnp.where(r2_ref[hd, :, lanes] < n1row, c1row * e2_ref[hd, :, lanes],
                                   jnp.zeros((), BF16))
                    g = gh if g is None else g + gh
                sk = s[kk * PEER_NK:(kk + 1) * PEER_NK, lanes]
                a_ref[k * PEER_NK:(k + 1) * PEER_NK, lanes] = jax.nn.gelu(sk).astype(BF16) * g
    acc_ref[...] += jnp.dot(vt_ref[...], a_ref[...], preferred_element_type=F32)

    @pl.when(j == pl.num_programs(1) - 1)
    def _():
        o_ref[...] = acc_ref[...].T


def peer_dense(h_bf16, u_bf16, vt_bf16, n1, c1, r2, e2, tm, n_slab):
    T, D = h_bf16.shape
    NE = u_bf16.shape[0]
    e_tile = n_slab * PEER_NK
    assert T % tm == 0 and NE % e_tile == 0 and vt_bf16.shape == (D, NE)
    slab_spec = pl.BlockSpec((PEER_HEADS, n_slab, tm), lambda i, j: (0, j, i))
    tok_spec = pl.BlockSpec((PEER_HEADS, PEER_NK, tm), lambda i, j: (0, 0, i))
    return pl.pallas_call(
        functools.partial(_peer_dense_body, n_slab=n_slab),
        out_shape=jax.ShapeDtypeStruct((T, D), F32),
        grid=(T // tm, NE // e_tile),
        in_specs=[pl.BlockSpec((tm, D), lambda i, j: (i, 0)),
                  pl.BlockSpec((e_tile, D), lambda i, j: (j, 0)),
                  pl.BlockSpec((D, e_tile), lambda i, j: (0, j)),
                  slab_spec, slab_spec, tok_spec, tok_spec],
        out_specs=pl.BlockSpec((tm, D), lambda i, j: (i, 0)),
        scratch_shapes=[pltpu.VMEM((D, tm), F32), pltpu.VMEM((e_tile, tm), BF16)],
        compiler_params=pltpu.CompilerParams(dimension_semantics=("arbitrary", "arbitrary"),
                                             vmem_limit_bytes=V7X_VMEM_LIMIT_BYTES),
        name="peer_dense",
    )(h_bf16, u_bf16, vt_bf16, n1, c1, r2, e2)


def peer_ffn_pallas(x, shift, scale, wq_bf16, keys_bf16, u_bf16, vt_bf16, rows_per_mod, tm_route=256,
                    tm_dense=512, n_slab=8):
    h, n1, c1, r2, e2 = peer_route(x, shift, scale, wq_bf16, keys_bf16, rows_per_mod, tm_route)
    return peer_dense(h, u_bf16, vt_bf16, n1, c1, r2, e2, tm_dense, n_slab)


POOL_HALO = 8
POOL_ROWS = 256


def _pool_body(u_ref, w_ref, scale_ref, o_ref, pad_ref, *, seq_len):
    L = seq_len
    chunk = min(POOL_ROWS, L)
    zeros = jnp.zeros((POOL_HALO, W_GRP), F32)
    pad_ref[0:POOL_HALO, :] = zeros
    pad_ref[POOL_HALO + L:2 * POOL_HALO + L, :] = zeros
    pad_ref[POOL_HALO:POOL_HALO + L, :] = u_ref[0]
    for r0 in range(0, L, chunk):
        t = r0 + lax.broadcasted_iota(jnp.int32, (chunk, POOL_C), 0)
        for j, w in enumerate(POOL_WINDOWS):
            cols = slice(j * POOL_C, (j + 1) * POOL_C)
            acc = None
            for o in range(-(w // 2), w - w // 2):
                s = pad_ref[POOL_HALO + r0 + o:POOL_HALO + r0 + o + chunk, cols]
                acc = s if acc is None else acc + s
            lo = jnp.maximum(t - w // 2, 0)
            hi = jnp.minimum(t - w // 2 + w - 1, L - 1)
            cnt = (hi - lo + 1).astype(F32)
            pooled = acc / cnt - u_ref[0, r0:r0 + chunk, cols]
            y = jnp.dot(pooled.astype(BF16), w_ref[j], preferred_element_type=F32)
            o_ref[0, r0:r0 + chunk, cols] = y * scale_ref[:, cols]


def pool_mix_pallas(u, pool_w, pool_scale):
    B_, L, W = u.shape
    assert max(POOL_WINDOWS) // 2 <= POOL_HALO and L % min(POOL_ROWS, L) == 0
    blk = pl.BlockSpec((1, L, W), lambda b: (b, 0, 0))
    return pl.pallas_call(
        functools.partial(_pool_body, seq_len=L),
        out_shape=jax.ShapeDtypeStruct((B_, L, W), F32),
        grid=(B_,),
        in_specs=[blk, pl.BlockSpec(pool_w.shape, lambda b: (0, 0, 0)), pl.BlockSpec((1, W), lambda b: (0, 0))],
        out_specs=blk,
        scratch_shapes=[pltpu.VMEM((L + 2 * POOL_HALO, W), F32)],
        compiler_params=pltpu.CompilerParams(dimension_semantics=("arbitrary",),
                                             vmem_limit_bytes=V7X_VMEM_LIMIT_BYTES),
        name="pool_mix",
    )(u, pool_w.astype(BF16), pool_scale.reshape(1, W))


NA_TILE_R = 4
NA_TILE = NA_TILE_R * GRID_W
NA_SCALE = NA_HD ** -0.5


def _na_bias_table(rpb, n_rows):
    n_tiles = n_rows // NA_TILE_R
    KR = min(NA_WIN_R, n_rows)
    a = jnp.array([0, min(2, n_tiles - 1), n_tiles - 1])[:, None, None, None, None, None]
    d = jnp.arange(3)[None, :, None, None, None, None]
    i = jnp.arange(NA_TILE_R)[None, None, :, None, None, None]
    qc = jnp.arange(GRID_W)[None, None, None, :, None, None]
    j = jnp.arange(NA_TILE_R)[None, None, None, None, :, None]
    kc = jnp.arange(GRID_W)[None, None, None, None, None, :]
    qr = NA_TILE_R * a + i
    kr = NA_TILE_R * (a + d - 1) + j
    rs = jnp.clip(qr - KR // 2, 0, n_rows - KR)
    c0 = jnp.clip(qc - NA_WIN_C // 2, 0, GRID_W - NA_WIN_C)
    ok = (kr >= rs) & (kr < rs + KR) & (kr >= 0) & (kr < n_rows) & (kc >= c0) & (kc < c0 + NA_WIN_C)
    row_rel = jnp.clip(kr - qr + NA_WIN_R - 1, 0, 2 * NA_WIN_R - 2)
    col_rel = jnp.clip(kc - qc, -(NA_WIN_C - 1), NA_WIN_C - 1) + NA_WIN_C - 1
    row_rel, col_rel, ok = jnp.broadcast_arrays(row_rel, col_rel, ok)
    bias = rpb.astype(F32)[:, row_rel, col_rel]
    tab = jnp.where(ok[None], bias, -jnp.inf)
    return tab.transpose(1, 0, 2, 3, 4, 5, 6).reshape(3, NA_H, 3, NA_TILE, NA_TILE)


def _na_body(q_ref, k_ref, v_ref, kc_ref, vc_ref, t_ref, o_ref, *, n_tiles, with_grid):
    a = pl.program_id(1)
    nt = (((1,), (1,)), ((), ()))
    for h in range(NA_H):
        hs = slice(h * NA_HD, (h + 1) * NA_HD)
        qh = q_ref[:, hs]
        scores = [lax.dot_general(qh, kc_ref[:, hs], nt, preferred_element_type=F32) * NA_SCALE]
        vals = [vc_ref[:, hs]]
        if with_grid:
            for d in range(3):
                ti = jnp.clip(a + d - 1, 0, n_tiles - 1)
                rows = pl.ds(pl.multiple_of(ti * NA_TILE, NA_TILE), NA_TILE)
                s = lax.dot_general(qh, k_ref[rows, hs], nt, preferred_element_type=F32)
                scores.append(s * NA_SCALE + t_ref[0, h, d])
                vals.append(v_ref[rows, hs])
        m = scores[0].max(axis=-1, keepdims=True)
        for s in scores[1:]:
            m = jnp.maximum(m, s.max(axis=-1, keepdims=True))
        den = None
        acc = None
        for s, vv in zip(scores, vals):
            p = jnp.exp(s - m)
            l = p.sum(axis=-1, keepdims=True)
            o = jnp.dot(p.astype(BF16), vv, preferred_element_type=F32)
            den = l if den is None else den + l
            acc = o if acc is None else acc + o
        o_ref[:, hs] = acc / den


def na_attention(q, k, v, kc, vc, table, seq_len, ctx_len, with_grid):
    W = q.shape[1]
    n_b = kc.shape[0] // ctx_len
    lq = q.shape[0] // n_b
    assert lq % NA_TILE == 0
    q_tiles = lq // NA_TILE
    n_tiles = seq_len // NA_TILE
    pat = lambda b, a: (jnp.where(a == 0, 0, jnp.where(a == n_tiles - 1, 2, 1)), 0, 0, 0, 0)
    qo_spec = pl.BlockSpec((NA_TILE, W), lambda b, a: (b * q_tiles + a, 0))
    return pl.pallas_call(
        functools.partial(_na_body, n_tiles=n_tiles, with_grid=with_grid),
        out_shape=jax.ShapeDtypeStruct(q.shape, F32),
        grid=(n_b, q_tiles),
        in_specs=[qo_spec,
                  pl.BlockSpec((seq_len, W), lambda b, a: (b, 0)), pl.BlockSpec((seq_len, W), lambda b, a: (b, 0)),
                  pl.BlockSpec((ctx_len, W), lambda b, a: (b, 0)), pl.BlockSpec((ctx_len, W), lambda b, a: (b, 0)),
                  pl.BlockSpec((1,) + table.shape[1:], pat)],
        out_specs=qo_spec,
        compiler_params=pltpu.CompilerParams(dimension_semantics=("arbitrary", "arbitrary"),
                                             vmem_limit_bytes=V7X_VMEM_LIMIT_BYTES),
        name="na_attention",
    )(q, k, v, kc, vc, table)


SSM_HALO = 8
SSM_BC = SSM_G * SSM_N


def _ssm_conv_body(prev_ref, cur_ref, next_ref, w_ref, b_ref, o_ref, pad_ref, *, tiles_per_seq):
    pos = pl.program_id(0) % tiles_per_seq
    tm = cur_ref.shape[0]
    pad_ref[0:SSM_HALO, :] = jnp.where(pos == 0, 0.0, prev_ref[...])
    pad_ref[SSM_HALO:SSM_HALO + tm, :] = cur_ref[...]
    pad_ref[SSM_HALO + tm:2 * SSM_HALO + tm, :] = jnp.where(pos == tiles_per_seq - 1, 0.0, next_ref[...])
    lead = (SSM_CONV - 1) // 2
    y = b_ref[...]
    for k in range(SSM_CONV):
        y = y + w_ref[k:k + 1, :] * pad_ref[SSM_HALO - lead + k:SSM_HALO - lead + k + tm, :]
    o_ref[...] = jax.nn.silu(y)


def ssm_conv(xbc, conv_w, conv_b, seq_len, tm):
    T, CH = xbc.shape
    assert seq_len % tm == 0 and tm % SSM_HALO == 0
    hb = tm // SSM_HALO
    n_hb = T // SSM_HALO
    return pl.pallas_call(
        functools.partial(_ssm_conv_body, tiles_per_seq=seq_len // tm),
        out_shape=jax.ShapeDtypeStruct((T, CH), F32),
        grid=(T // tm,),
        in_specs=[pl.BlockSpec((SSM_HALO, CH), lambda i: (jnp.maximum(i * hb - 1, 0), 0)),
                  pl.BlockSpec((tm, CH), lambda i: (i, 0)),
                  pl.BlockSpec((SSM_HALO, CH), lambda i: (jnp.minimum((i + 1) * hb, n_hb - 1), 0)),
                  pl.BlockSpec((SSM_CONV, CH), lambda i: (0, 0)), pl.BlockSpec((1, CH), lambda i: (0, 0))],
        out_specs=pl.BlockSpec((tm, CH), lambda i: (i, 0)),
        scratch_shapes=[pltpu.VMEM((tm + 2 * SSM_HALO, CH), F32)],
        compiler_params=pltpu.CompilerParams(dimension_semantics=("arbitrary",),
                                             vmem_limit_bytes=V7X_VMEM_LIMIT_BYTES),
        name="ssm_conv",
    )(xbc, xbc, xbc, conv_w, conv_b.reshape(1, CH))


def _ssd_body(xbc_ref, dt_ref, bias_ref, a_ref, h0_ref, y_ref, hfin_ref, h_scr, *, di, rev):
    ci = pl.program_id(1)
    Q = xbc_ref.shape[0]

    @pl.when(ci == 0)
    def _():
        h_scr[...] = h0_ref[0]

    dt_all = jax.nn.softplus(dt_ref[...] + bias_ref[...])
    a_cs = dt_all * a_ref[...]
    row_id = lax.broadcasted_iota(jnp.int32, a_cs.shape, 0)
    sh = 1
    while sh < Q:
        if rev:
            a_cs = a_cs + jnp.where(row_id < Q - sh, pltpu.roll(a_cs, Q - sh, 0), 0.0)
        else:
            a_cs = a_cs + jnp.where(row_id >= sh, pltpu.roll(a_cs, sh, 0), 0.0)
        sh *= 2
    a_cs_t = a_cs.T
    a_tot = a_cs[0:1, :] if rev else a_cs[Q - 1:Q, :]
    l_id = lax.broadcasted_iota(jnp.int32, (Q, Q), 0)
    s_id = lax.broadcasted_iota(jnp.int32, (Q, Q), 1)
    causal = (l_id <= s_id) if rev else (l_id >= s_id)
    nt = (((1,), (1,)), ((), ()))
    tn = (((0,), (0,)), ((), ()))
    cb = []
    for g in range(SSM_G):
        bg = xbc_ref[:, W_GRP + g * SSM_N:W_GRP + (g + 1) * SSM_N].astype(BF16)
        cg = xbc_ref[:, W_GRP + SSM_BC + g * SSM_N:W_GRP + SSM_BC + (g + 1) * SSM_N].astype(BF16)
        cb.append((bg, cg, lax.dot_general(cg, bg, nt, preferred_element_type=F32)))
    for h in range(SSM_H):
        c = di * SSM_H + h
        bg, cg, cbg = cb[h // (SSM_H // SSM_G)]
        col = a_cs[:, c:c + 1]
        lm = jnp.exp(jnp.where(causal, col - a_cs_t[c:c + 1, :], -jnp.inf))
        xh = xbc_ref[:, h * SSM_HD:(h + 1) * SSM_HD] * dt_all[:, c:c + 1]
        hp = h_scr[h]
        yd = jnp.dot((cbg * lm).astype(BF16), xh.astype(BF16), preferred_element_type=F32)
        yo = lax.dot_general(cg, hp.astype(BF16), nt, preferred_element_type=F32) * jnp.exp(col)
        y_ref[:, h * SSM_HD:(h + 1) * SSM_HD] = yd + yo
        tot = a_tot[:, c:c + 1]
        xd = (xh * jnp.exp(tot - col)).astype(BF16)
        h_scr[h] = jnp.exp(tot) * hp + lax.dot_general(xd, bg, tn, preferred_element_type=F32)

    @pl.when(ci == pl.num_programs(1) - 1)
    def _():
        hfin_ref[0] = h_scr[...]


def ssd_scan(xbc_act, dt_raw, bias128, a128, h0, seq_len, di, rev):
    T = xbc_act.shape[0]
    n_b = T // seq_len
    Q = min(SSM_CHUNK, seq_len)
    nc = seq_len // Q
    chunk = (lambda b, i: (b * nc + nc - 1 - i, 0)) if rev else (lambda b, i: (b * nc + i, 0))
    vec = pl.BlockSpec((1, V7X_LANES), lambda b, i: (0, 0))
    st = pl.BlockSpec((1, SSM_H, SSM_HD, SSM_N), lambda b, i: (b, 0, 0, 0))
    return pl.pallas_call(
        functools.partial(_ssd_body, di=di, rev=rev),
        out_shape=[jax.ShapeDtypeStruct((T, W_GRP), F32), jax.ShapeDtypeStruct(h0.shape, F32)],
        grid=(n_b, nc),
        in_specs=[pl.BlockSpec((Q, SSM_CONV_CH), chunk), pl.BlockSpec((Q, V7X_LANES), chunk), vec, vec, st],
        out_specs=[pl.BlockSpec((Q, W_GRP), chunk), st],
        scratch_shapes=[pltpu.VMEM((SSM_H, SSM_HD, SSM_N), F32)],
        compiler_params=pltpu.CompilerParams(dimension_semantics=("arbitrary", "arbitrary"),
                                             vmem_limit_bytes=V7X_VMEM_LIMIT_BYTES),
        name="ssd_scan",
    )(xbc_act, dt_raw, bias128, a128, h0)


def _ssm_out_body(xbc_ref, yf_ref, yb_ref, z_ref, d_ref, nw_ref, o_ref):
    y = d_ref[...] * xbc_ref[:, :W_GRP] + yf_ref[...] + yb_ref[...]
    g = y * jax.nn.silu(z_ref[...])
    gw = W_GRP // SSM_G
    for k in range(SSM_G):
        gk = g[:, k * gw:(k + 1) * gw]
        r = lax.rsqrt(jnp.mean(gk * gk, -1, keepdims=True) + LN_EPS)
        o_ref[:, k * gw:(k + 1) * gw] = gk * r * nw_ref[:, k * gw:(k + 1) * gw]


def ssm_out(xbc_act, yf, yb, z, d512, norm_w, tm):
    T = z.shape[0]
    assert T % tm == 0
    row = pl.BlockSpec((tm, W_GRP), lambda i: (i, 0))
    vec = pl.BlockSpec((1, W_GRP), lambda i: (0, 0))
    return pl.pallas_call(
        _ssm_out_body,
        out_shape=jax.ShapeDtypeStruct((T, W_GRP), F32),
        grid=(T // tm,),
        in_specs=[pl.BlockSpec((tm, SSM_CONV_CH), lambda i: (i, 0)), row, row, row, vec, vec],
        out_specs=row,
        compiler_params=pltpu.CompilerParams(dimension_semantics=("arbitrary",),
                                             vmem_limit_bytes=V7X_VMEM_LIMIT_BYTES),
        name="ssm_out",
    )(xbc_act, yf, yb, z, d512, norm_w.reshape(1, W_GRP))


def ssm_mixer_pallas(z_l, xbc_l, dt_l, z_c, xbc_c, dt_c, conv_w, conv_b, dt_bias, a_log, d, norm_w,
                     seq_len, ctx_len, with_ctx_out, tm=256):
    n_b = z_l.shape[0] // seq_len
    pad = V7X_LANES - 2 * SSM_H
    bias128 = jnp.pad(dt_bias.astype(F32).reshape(1, 2 * SSM_H), ((0, 0), (0, pad)))
    a128 = jnp.pad(-jnp.exp(a_log.astype(F32)).reshape(1, 2 * SSM_H), ((0, 0), (0, pad)))
    d512 = jnp.repeat(d.astype(F32), SSM_HD).reshape(1, W_GRP)
    act_c = ssm_conv(xbc_c, conv_w, conv_b, ctx_len, min(tm, ctx_len))
    act_l = ssm_conv(xbc_l, conv_w, conv_b, seq_len, tm)
    zero = jnp.zeros((n_b, SSM_H, SSM_HD, SSM_N), F32)
    y_c, y_l = [], []
    for di in range(2):
        yc, hc = ssd_scan(act_c, dt_c, bias128, a128, zero, ctx_len, di, di == 1)
        yl, _ = ssd_scan(act_l, dt_l, bias128, a128, hc, seq_len, di, di == 1)
        y_c.append(yc)
        y_l.append(yl)
    out_l = ssm_out(act_l, y_l[0], y_l[1], z_l, d512, norm_w, tm)
    out_c = ssm_out(act_c, y_c[0], y_c[1], z_c, d512, norm_w, min(tm, z_c.shape[0])) if with_ctx_out else None
    return out_l, out_c


def s5_discretize(a_re, a_im, log_dt, b_re, b_im):
    a_re, a_im = a_re.astype(F32), a_im.astype(F32)
    b_re, b_im = b_re.astype(F32), b_im.astype(F32)
    dt = jnp.exp(log_dt.astype(F32))[:, None]
    mag = jnp.exp(a_re * dt)
    ab_re = mag * jnp.cos(a_im * dt)
    ab_im = mag * jnp.sin(a_im * dt)
    den = a_re * a_re + a_im * a_im
    f_re = ((ab_re - 1) * a_re + ab_im * a_im) / den
    f_im = (ab_im * a_re - (ab_re - 1) * a_im) / den
    bb_re = f_re[..., None] * b_re - f_im[..., None] * b_im
    bb_im = f_re[..., None] * b_im + f_im[..., None] * b_re
    return ab_re, ab_im, bb_re, bb_im


def _cscan_combine(e1, e2):
    a1r, a1i, b1r, b1i = e1
    a2r, a2i, b2r, b2i = e2
    return (a2r * a1r - a2i * a1i, a2r * a1i + a2i * a1r,
            a2r * b1r - a2i * b1i + b2r, a2r * b1i + a2i * b1r + b2i)


def s5_states(u, ab_re, ab_im, bb_re, bb_im, h0_re, h0_im):
    bu_re = jnp.einsum('gnp,blgp->blgn', bb_re, u)
    bu_im = jnp.einsum('gnp,blgp->blgn', bb_im, u)
    bu_re = bu_re.at[:, 0].add(ab_re * h0_re - ab_im * h0_im)
    bu_im = bu_im.at[:, 0].add(ab_re * h0_im + ab_im * h0_re)
    a_re = jnp.broadcast_to(ab_re, bu_re.shape)
    a_im = jnp.broadcast_to(ab_im, bu_im.shape)
    _, _, h_re, h_im = lax.associative_scan(_cscan_combine, (a_re, a_im, bu_re, bu_im), axis=1)
    return h_re, h_im


def s5_readout(c_re, c_im, h_re, h_im):
    return (jnp.einsum('gpn,blgn->blgp', c_re.astype(F32), h_re)
            - jnp.einsum('gpn,blgn->blgp', c_im.astype(F32), h_im))


def s5_glu(y, w_glu, b_glu, dtype):
    B_, L = y.shape[:2]
    g = jax.nn.gelu(y.reshape(B_, L, W_GRP))
    return (g * jax.nn.sigmoid(g @ w_glu.astype(F32) + b_glu.astype(F32))).astype(dtype)


def s5_mixer(u_lat, u_ctx, a_re, a_im, log_dt, b_re, b_im, c_re, c_im, d, w_glu, b_glu, with_ctx_out):
    B_ = u_lat.shape[0]
    ul = u_lat.astype(F32).reshape(B_, -1, S5_G, S5_P)
    uc = u_ctx.astype(F32).reshape(B_, -1, S5_G, S5_P)
    dg = d.astype(F32).reshape(S5_G, S5_P)
    zero = jnp.zeros((B_, S5_G, S5_N), F32)
    y_lat = dg * ul
    y_ctx = dg * uc if with_ctx_out else None
    for di in range(2):
        rev = di == 1
        ab_re, ab_im, bb_re, bb_im = s5_discretize(a_re[di], a_im[di], log_dt[di], b_re[di], b_im[di])
        hc_re, hc_im = s5_states(_flip(uc, rev), ab_re, ab_im, bb_re, bb_im, zero, zero)
        hl_re, hl_im = s5_states(_flip(ul, rev), ab_re, ab_im, bb_re, bb_im, hc_re[:, -1], hc_im[:, -1])
        y_lat = y_lat + _flip(s5_readout(c_re[di], c_im[di], hl_re, hl_im), rev)
        if with_ctx_out:
            y_ctx = y_ctx + _flip(s5_readout(c_re[di], c_im[di], hc_re, hc_im), rev)
    out_lat = s5_glu(y_lat, w_glu, b_glu, u_lat.dtype)
    out_ctx = s5_glu(y_ctx, w_glu, b_glu, u_ctx.dtype) if with_ctx_out else None
    return out_lat, out_ctx


def pool_mix(u, pool_w, pool_scale):
    B_, L, _ = u.shape
    uf = u.astype(F32)
    csum = jnp.pad(jnp.cumsum(uf, axis=1), ((0, 0), (1, 0), (0, 0)))
    t = jnp.arange(L)
    outs = []
    for j, w in enumerate(POOL_WINDOWS):
        lo = jnp.clip(t - w // 2, 0, L - 1)
        hi = jnp.clip(t - w // 2 + w - 1, 0, L - 1)
        ch = slice(j * POOL_C, (j + 1) * POOL_C)
        cnt = (hi - lo + 1).astype(F32)[None, :, None]
        outs.append((csum[:, hi + 1, ch] - csum[:, lo, ch]) / cnt - uf[:, :, ch])
    pooled = jnp.stack(outs, axis=2)
    y = jnp.einsum('blgc,gcd->blgd', pooled, pool_w.astype(F32)).reshape(B_, L, W_GRP)
    return (y * pool_scale.astype(F32)).astype(u.dtype)


def na_latent(q, k, v, k_ctx, v_ctx, rpb):
    B_, L, _ = q.shape
    R = L // GRID_W
    KR = min(NA_WIN_R, R)
    KC = NA_WIN_C
    shp = (B_, R, GRID_W, NA_H, NA_HD)
    q, k, v = q.reshape(shp), k.reshape(shp), v.reshape(shp)
    r = jnp.arange(R)
    row_idx = jnp.clip(r - KR // 2, 0, R - KR)[:, None] + jnp.arange(KR)[None, :]
    k_rows = k[:, row_idx]
    v_rows = v[:, row_idx]
    col = jnp.arange(GRID_W)
    c0 = jnp.clip(col - KC // 2, 0, GRID_W - KC)
    in_win = (col[None, :] >= c0[:, None]) & (col[None, :] < c0[:, None] + KC)
    col_rel = jnp.clip(col[None, :] - col[:, None], -(KC - 1), KC - 1) + KC - 1
    row_rel = row_idx - r[:, None] + NA_WIN_R - 1
    bias = rpb.astype(F32)[:, row_rel][..., col_rel]
    bias = bias.transpose(1, 0, 3, 2, 4)
    scale = NA_HD ** -0.5
    s_nb = jnp.einsum('brqhd,brkwhd->brhqkw', q, k_rows).astype(F32) * scale + bias
    s_nb = jnp.where(in_win[:, None, :], s_nb, -jnp.inf).reshape(B_, R, NA_H, GRID_W, KR * GRID_W)
    s_cx = jnp.einsum('brqhd,bchd->brhqc', q, k_ctx).astype(F32) * scale
    p = jax.nn.softmax(jnp.concatenate([s_nb, s_cx], axis=-1), axis=-1).astype(v.dtype)
    p_nb = p[..., :KR * GRID_W].reshape(B_, R, NA_H, GRID_W, KR, GRID_W)
    p_cx = p[..., KR * GRID_W:]
    o = (jnp.einsum('brhqkw,brkwhd->brqhd', p_nb, v_rows)
         + jnp.einsum('brhqc,bchd->brqhd', p_cx, v_ctx))
    return o.reshape(B_, L, W_GRP)


def ctx_attn(q, k, v):
    s = jnp.einsum('bqhd,bkhd->bhqk', q, k).astype(F32) * NA_HD ** -0.5
    p = jax.nn.softmax(s, axis=-1).astype(v.dtype)
    return jnp.einsum('bhqk,bkhd->bqhd', p, v)


def na_mixer(q_l, k_l, v_l, q_c, k_c, v_c, rpb, with_ctx_out):
    B_, C_ = k_c.shape[:2]
    kc = k_c.reshape(B_, C_, NA_H, NA_HD)
    vc = v_c.reshape(B_, C_, NA_H, NA_HD)
    out_l = na_latent(q_l, k_l, v_l, kc, vc, rpb)
    out_c = (ctx_attn(q_c.reshape(B_, C_, NA_H, NA_HD), kc, vc).reshape(B_, C_, W_GRP)
             if with_ctx_out else None)
    return out_l, out_c


def dwconv(x, w, b):
    y = lax.conv_general_dilated(x, w[:, None, :].astype(x.dtype), window_strides=(1,),
                                 padding=[((SSM_CONV - 1) // 2, SSM_CONV // 2)],
                                 dimension_numbers=('NWC', 'WIO', 'NWC'),
                                 feature_group_count=x.shape[-1])
    return y + b


def segsum(x):
    T = x.shape[-1]
    cs = jnp.cumsum(x, axis=-1)
    d = cs[..., :, None] - cs[..., None, :]
    return jnp.where(jnp.tril(jnp.ones((T, T), bool)), d, -jnp.inf)


def ssd(X, A, Bm, Cm, h0):
    b, L, H, P = X.shape
    N = Bm.shape[-1]
    nc = L // SSM_CHUNK
    X = X.reshape(b, nc, SSM_CHUNK, H, P)
    Bm = Bm.reshape(b, nc, SSM_CHUNK, H, N)
    Cm = Cm.reshape(b, nc, SSM_CHUNK, H, N)
    A = A.reshape(b, nc, SSM_CHUNK, H).transpose(0, 3, 1, 2)
    A_cs = jnp.cumsum(A, axis=-1)
    cb = jnp.einsum('bclhn,bcshn->bhcls', Cm, Bm)
    y_diag = jnp.einsum('bhcls,bcshp->bclhp', cb * jnp.exp(segsum(A)), X)
    decay_states = jnp.exp(A_cs[..., -1:] - A_cs).transpose(0, 2, 3, 1)[..., None]
    states = jnp.einsum('bclhn,bclhp->bchpn', Bm, X * decay_states)
    states = jnp.concatenate([h0[:, None], states], axis=1)
    chunk_decay = jnp.exp(segsum(jnp.pad(A_cs[..., -1], ((0, 0), (0, 0), (1, 0)))))
    new_states = jnp.einsum('bhzc,bchpn->bzhpn', chunk_decay, states)
    prev, final = new_states[:, :-1], new_states[:, -1]
    y_off = (jnp.einsum('bclhn,bchpn->bclhp', Cm, prev)
             * jnp.exp(A_cs).transpose(0, 2, 3, 1)[..., None])
    return (y_diag + y_off).reshape(b, L, H, P), final


def ssm_prepare(pxbc, pdt, conv_w, conv_b, dt_bias):
    B_, L, _ = pxbc.shape
    xbc = jax.nn.silu(dwconv(pxbc, conv_w, conv_b).astype(F32))
    hpg = SSM_H // SSM_G
    xs = xbc[..., :W_GRP].reshape(B_, L, SSM_H, SSM_HD)
    bm = jnp.repeat(xbc[..., W_GRP:W_GRP + SSM_G * SSM_N].reshape(B_, L, SSM_G, SSM_N), hpg, axis=2)
    cm = jnp.repeat(xbc[..., W_GRP + SSM_G * SSM_N:].reshape(B_, L, SSM_G, SSM_N), hpg, axis=2)
    dt = jax.nn.softplus(pdt.astype(F32).reshape(B_, L, 2, SSM_H) + dt_bias.astype(F32))
    return xs, bm, cm, dt


def ssm_direction(xs, bm, cm, dt, a, h0, rev):
    y, h_fin = ssd(_flip(xs * dt[..., None], rev), _flip(dt * a, rev), _flip(bm, rev), _flip(cm, rev), h0)
    return _flip(y, rev), h_fin


def ssm_gate_norm(y, z, norm_w):
    B_, L = z.shape[:2]
    g = (y.reshape(B_, L, W_GRP) * jax.nn.silu(z.astype(F32))).reshape(B_, L, SSM_G, W_GRP // SSM_G)
    g = g * lax.rsqrt(jnp.mean(jnp.square(g), -1, keepdims=True) + LN_EPS)
    return (g.reshape(B_, L, W_GRP) * norm_w.astype(F32)).astype(z.dtype)


def ssm_mixer(z_l, xbc_l, dtr_l, z_c, xbc_c, dtr_c, conv_w, conv_b, dt_bias, a_log, d, norm_w, with_ctx_out):
    xs_l, bm_l, cm_l, dt_l = ssm_prepare(xbc_l, dtr_l, conv_w, conv_b, dt_bias)
    xs_c, bm_c, cm_c, dt_c = ssm_prepare(xbc_c, dtr_c, conv_w, conv_b, dt_bias)
    a = -jnp.exp(a_log.astype(F32))
    dh = d.astype(F32)[:, None]
    zero = jnp.zeros((xs_c.shape[0], SSM_H, SSM_HD, SSM_N), F32)
    y_l = dh * xs_l
    y_c = dh * xs_c if with_ctx_out else None
    for di in range(2):
        rev = di == 1
        yc, hc = ssm_direction(xs_c, bm_c, cm_c, dt_c[:, :, di], a[di], zero, rev)
        yl, _ = ssm_direction(xs_l, bm_l, cm_l, dt_l[:, :, di], a[di], hc, rev)
        y_l = y_l + yl
        if with_ctx_out:
            y_c = y_c + yc
    out_l = ssm_gate_norm(y_l, z_l, norm_w)
    out_c = ssm_gate_norm(y_c, z_c, norm_w) if with_ctx_out else None
    return out_l, out_c


def peer_ffn(t, w_q, sub_keys, u_tab, v_tab):
    n_tok = t.shape[0]
    q = (t @ w_q).reshape(n_tok, PEER_HEADS, 2, PEER_QDIM // 2)
    s = jnp.einsum('thsd,hskd->thsk', q, sub_keys).astype(F32)
    sv, si = lax.top_k(s, PEER_TOPK)
    cand_s = (sv[:, :, 0, :, None] + sv[:, :, 1, None, :]).reshape(n_tok, PEER_HEADS, PEER_TOPK * PEER_TOPK)
    cand_i = (si[:, :, 0, :, None] * PEER_NK + si[:, :, 1, None, :]).reshape(n_tok, PEER_HEADS, PEER_TOPK * PEER_TOPK)
    top_s, pos = lax.top_k(cand_s, PEER_TOPK)
    idx = jnp.take_along_axis(cand_i, pos, axis=-1).reshape(n_tok, PEER_HEADS * PEER_TOPK)
    gate = jax.nn.softmax(top_s, axis=-1).reshape(n_tok, PEER_HEADS * PEER_TOPK)
    n_blk = n_tok // PEER_BLOCK

    def expert_block(args):
        xb, ib, gb = args
        act = jnp.einsum('tkd,td->tk', u_tab[ib], xb).astype(F32)
        return jnp.einsum('tk,tkd->td', (jax.nn.gelu(act) * gb).astype(xb.dtype), v_tab[ib])

    out = lax.map(expert_block, (t.reshape(n_blk, PEER_BLOCK, -1),
                                 idx.reshape(n_blk, PEER_BLOCK, -1),
                                 gate.reshape(n_blk, PEER_BLOCK, -1)))
    return out.reshape(n_tok, -1)


def _mixers(p_l, p_c, prm, with_ctx_out):
    s5_l, pool_l, q_l, k_l, v_l, z_l, xbc_l, dt_l = p_l
    s5_c, pool_c, q_c, k_c, v_c, z_c, xbc_c, dt_c = p_c
    ya_l, ya_c = s5_mixer_pallas(s5_l, s5_c, prm["s5_a_re"], prm["s5_a_im"], prm["s5_log_dt"],
                                 prm["s5_b_re"], prm["s5_b_im"], prm["s5_c_re"], prm["s5_c_im"], prm["s5_d"],
                                 prm["s5_w_glu"], prm["s5_b_glu"], with_ctx_out)
    yb_l = pool_mix_pallas(pool_l, prm["pool_w"], prm["pool_scale"])
    yb_c = pool_mix_pallas(pool_c, prm["pool_w"], prm["pool_scale"]) if with_ctx_out else None
    B_, L, _ = s5_l.shape
    C_ = s5_c.shape[1]
    flat = lambda t: t.reshape(-1, t.shape[-1])
    table = _na_bias_table(prm["na_rpb"], L // GRID_W)
    yc_l = na_attention(flat(q_l), flat(k_l), flat(v_l), flat(k_c), flat(v_c), table, L, C_, True)
    yc_c = (na_attention(flat(q_c), flat(k_c), flat(v_c), flat(k_c), flat(v_c), table, C_, C_, False)
            if with_ctx_out else None)
    yd_l, yd_c = ssm_mixer_pallas(flat(z_l), flat(xbc_l), flat(dt_l), flat(z_c), flat(xbc_c), flat(dt_c),
                                  prm["ssm_conv_w"], prm["ssm_conv_b"], prm["ssm_dt_bias"], prm["ssm_a_log"],
                                  prm["ssm_d"], prm["ssm_norm_w"], L, C_, with_ctx_out)
    return (ya_l, yb_l, yc_l, yd_l), (ya_c, yb_c, yc_c, yd_c)


def kernel(x, c, ctx, c_ctx, w_ada, b_ada, w_in, w_out, s5_a_re, s5_a_im, s5_log_dt, s5_b_re, s5_b_im, s5_c_re, s5_c_im, s5_d, s5_w_glu, s5_b_glu, pool_w, pool_scale, na_rpb, ssm_conv_w, ssm_conv_b, ssm_dt_bias, ssm_a_log, ssm_d, ssm_norm_w, ln1_g, ln1_b, ln2_g, ln2_b, peer_w_q, peer_sub_keys, peer_u, peer_v):
    B_, L, D = x.shape
    C_ = ctx.shape[1]
    x_lat = x.reshape(B_ * L, D)
    x_ctx = ctx.reshape(B_ * C_, D)
    act_lat = jax.nn.silu(c)
    act_ctx = jax.nn.silu(c_ctx)
    dt_pad = V7X_LANES - IN_SPLITS[-1]
    splits = IN_SPLITS[:-1] + (V7X_LANES,)
    in_dtypes = (F32, F32, BF16, BF16, BF16, F32, F32, F32)
    TM = 256
    for l in range(DEPTH):
        last = l == DEPTH - 1
        prm = dict(s5_a_re=s5_a_re[l], s5_a_im=s5_a_im[l], s5_log_dt=s5_log_dt[l], s5_b_re=s5_b_re[l],
                   s5_b_im=s5_b_im[l], s5_c_re=s5_c_re[l], s5_c_im=s5_c_im[l], s5_d=s5_d[l],
                   s5_w_glu=s5_w_glu[l], s5_b_glu=s5_b_glu[l], pool_w=pool_w[l], pool_scale=pool_scale[l],
                   na_rpb=na_rpb[l], ssm_conv_w=ssm_conv_w[l], ssm_conv_b=ssm_conv_b[l],
                   ssm_dt_bias=ssm_dt_bias[l], ssm_a_log=ssm_a_log[l], ssm_d=ssm_d[l],
                   ssm_norm_w=ssm_norm_w[l])
        m_lat = (act_lat @ w_ada[l] + b_ada[l]).reshape(B_, 6, 1, D)
        m_ctx = (act_ctx @ w_ada[l] + b_ada[l]).reshape(1, 6, 1, D)
        w_in_b = jnp.pad(w_in[l], ((0, 0), (0, dt_pad))).astype(BF16)
        w_out_b = w_out[l].astype(BF16)

        p_l = modulated_matmul(x_lat, m_lat[:, 0], m_lat[:, 1], w_in_b, splits, in_dtypes, L, TM)
        p_c = modulated_matmul(x_ctx, m_ctx[:, 0], m_ctx[:, 1], w_in_b, splits, in_dtypes, B_ * C_, TM)
        p_l = [a.reshape(B_, L, -1) for a in p_l]
        p_c = [a.reshape(B_, C_, -1) for a in p_c]
        y_l, y_c = _mixers(p_l, p_c, prm, not last)

        x_lat = proj_residual_ln([a.reshape(B_ * L, -1) for a in y_l], x_lat, m_lat[:, 2], w_out_b,
                                 ln1_g[l], ln1_b[l], L, TM)
        wq_b = peer_w_q[l].astype(BF16)
        keys_b = peer_sub_keys[l].reshape(2 * PEER_HEADS, PEER_NK, PEER_HALF).astype(BF16)
        u_b = peer_u[l].astype(BF16)
        vt_b = peer_v[l].T.astype(BF16)
        f_lat = peer_ffn_pallas(x_lat, m_lat[:, 3], m_lat[:, 4], wq_b, keys_b, u_b, vt_b, L)
        x_lat = residual_ln(x_lat, f_lat, m_lat[:, 5], ln2_g[l], ln2_b[l], L, TM)
        if not last:
            x_ctx = proj_residual_ln([a.reshape(B_ * C_, -1) for a in y_c], x_ctx, m_ctx[:, 2], w_out_b,
                                     ln1_g[l], ln1_b[l], B_ * C_, TM)
            f_ctx = peer_ffn_pallas(x_ctx, m_ctx[:, 3], m_ctx[:, 4], wq_b, keys_b, u_b, vt_b, B_ * C_)
            x_ctx = residual_ln(x_ctx, f_ctx, m_ctx[:, 5], ln2_g[l], ln2_b[l], B_ * C_, TM)
    return x_lat.reshape(B_, L, D)
```

```python
import functools
import math

import jax
import jax.numpy as jnp
import numpy as np
from jax import lax
from jax.experimental import pallas as pl
from jax.experimental.pallas import tpu as pltpu

D_MODEL = 2048
BATCH = 4
SEQ = 4096
DEPTH = 2

GRID_W = 64
CTX_LEN = 256
N_MIXERS = 4
D_MIX = D_MODEL
W_GRP = D_MIX // N_MIXERS

S5_P = 16
S5_G = W_GRP // S5_P
S5_N = 64

POOL_WINDOWS = (2, 4, 8, 16)
POOL_C = W_GRP // len(POOL_WINDOWS)

NA_HD = 64
NA_H = W_GRP // NA_HD
NA_WIN_R = 8
NA_WIN_C = 16

SSM_HD = 64
SSM_H = W_GRP // SSM_HD
SSM_G = 2
SSM_N = 128
SSM_CONV = 4
SSM_CHUNK = 128
SSM_CONV_CH = W_GRP + 2 * SSM_G * SSM_N

PEER_HEADS = 8
PEER_NK = 128
PEER_NE = PEER_NK * PEER_NK
PEER_QDIM = 256
PEER_TOPK = 16
PEER_BLOCK = 128

IN_SPLITS = (W_GRP, W_GRP, W_GRP, W_GRP, W_GRP, W_GRP, SSM_CONV_CH, 2 * SSM_H)
D_IN = sum(IN_SPLITS)

DEEPNORM_ALPHA = (2 * DEPTH) ** 0.25
DEEPNORM_BETA = (8 * DEPTH) ** -0.25
LN_EPS = 1e-5
F32 = jnp.float32
BF16 = jnp.bfloat16

V7X_LANES = 128
V7X_VMEM_LIMIT_BYTES = 56 * 1024 * 1024


def _flip(t, rev):
    return t[:, ::-1] if rev else t


def _modmm_body(x_ref, shift_ref, scale_ref, w_ref, *out_refs, col_splits):
    xm = (x_ref[...] * (1.0 + scale_ref[0]) + shift_ref[0]).astype(BF16)
    o = 0
    for ref, n in zip(out_refs, col_splits):
        ref[...] = jnp.dot(xm, w_ref[:, o:o + n], preferred_element_type=F32).astype(ref.dtype)
        o += n


def modulated_matmul(x, shift, scale, w_bf16, col_splits, out_dtypes, rows_per_mod, tm):
    T, K = x.shape
    assert T % tm == 0 and rows_per_mod % tm == 0
    tiles_per_mod = rows_per_mod // tm
    n_tot = sum(col_splits)
    assert w_bf16.shape == (K, n_tot)
    mod_spec = pl.BlockSpec((1, 1, K), lambda i: (i // tiles_per_mod, 0, 0))
    return pl.pallas_call(
        functools.partial(_modmm_body, col_splits=tuple(col_splits)),
        out_shape=[jax.ShapeDtypeStruct((T, n), dt) for n, dt in zip(col_splits, out_dtypes, strict=True)],
        grid=(T // tm,),
        in_specs=[pl.BlockSpec((tm, K), lambda i: (i, 0)), mod_spec, mod_spec,
                  pl.BlockSpec((K, n_tot), lambda i: (0, 0))],
        out_specs=[pl.BlockSpec((tm, n), lambda i: (i, 0)) for n in col_splits],
        compiler_params=pltpu.CompilerParams(dimension_semantics=("arbitrary",),
                                             vmem_limit_bytes=V7X_VMEM_LIMIT_BYTES),
        name="modulated_matmul",
    )(x, shift, scale, w_bf16)


def _proj_ln_body(*refs, n_parts, alpha):
    part_refs = refs[:n_parts]
    x_ref, gate_ref, w_ref, g_ref, b_ref, o_ref = refs[n_parts:]
    acc = None
    o = 0
    for pr in part_refs:
        n = pr.shape[-1]
        d = jnp.dot(pr[...].astype(BF16), w_ref[o:o + n, :], preferred_element_type=F32)
        acc = d if acc is None else acc + d
        o += n
    h = alpha * x_ref[...] + gate_ref[0] * acc
    mu = jnp.mean(h, -1, keepdims=True)
    hc = h - mu
    var = jnp.mean(hc * hc, -1, keepdims=True)
    o_ref[...] = hc * lax.rsqrt(var + LN_EPS) * g_ref[...] + b_ref[...]


def proj_residual_ln(parts, x, gate, w_bf16, g, b, rows_per_mod, tm):
    T, D = x.shape
    assert T % tm == 0 and rows_per_mod % tm == 0
    tiles_per_mod = rows_per_mod // tm
    k_tot = sum(p.shape[-1] for p in parts)
    assert w_bf16.shape == (k_tot, D)
    row = lambda n: pl.BlockSpec((tm, n), lambda i: (i, 0))
    vec = pl.BlockSpec((1, D), lambda i: (0, 0))
    return pl.pallas_call(
        functools.partial(_proj_ln_body, n_parts=len(parts), alpha=DEEPNORM_ALPHA),
        out_shape=jax.ShapeDtypeStruct((T, D), F32),
        grid=(T // tm,),
        in_specs=[row(p.shape[-1]) for p in parts] + [
            row(D), pl.BlockSpec((1, 1, D), lambda i: (i // tiles_per_mod, 0, 0)),
            pl.BlockSpec((k_tot, D), lambda i: (0, 0)), vec, vec],
        out_specs=row(D),
        compiler_params=pltpu.CompilerParams(dimension_semantics=("arbitrary",),
                                             vmem_limit_bytes=V7X_VMEM_LIMIT_BYTES),
        name="proj_residual_ln",
    )(*parts, x, gate, w_bf16, g.reshape(1, D), b.reshape(1, D))


def _res_ln_body(x_ref, f_ref, gate_ref, g_ref, b_ref, o_ref, *, alpha):
    h = alpha * x_ref[...] + gate_ref[0] * f_ref[...]
    mu = jnp.mean(h, -1, keepdims=True)
    hc = h - mu
    var = jnp.mean(hc * hc, -1, keepdims=True)
    o_ref[...] = hc * lax.rsqrt(var + LN_EPS) * g_ref[...] + b_ref[...]


def residual_ln(x, f, gate, g, b, rows_per_mod, tm):
    T, D = x.shape
    assert T % tm == 0 and rows_per_mod % tm == 0
    tiles_per_mod = rows_per_mod // tm
    row = pl.BlockSpec((tm, D), lambda i: (i, 0))
    vec = pl.BlockSpec((1, D), lambda i: (0, 0))
    return pl.pallas_call(
        functools.partial(_res_ln_body, alpha=DEEPNORM_ALPHA),
        out_shape=jax.ShapeDtypeStruct((T, D), F32),
        grid=(T // tm,),
        in_specs=[row, row, pl.BlockSpec((1, 1, D), lambda i: (i // tiles_per_mod, 0, 0)), vec, vec],
        out_specs=row,
        compiler_params=pltpu.CompilerParams(dimension_semantics=("arbitrary",),
                                             vmem_limit_bytes=V7X_VMEM_LIMIT_BYTES),
        name="residual_ln",
    )(x, f, gate, g.reshape(1, D), b.reshape(1, D))


S5_CHAINS = 2 * BATCH
S5_STATE = S5_G * S5_N
S5_SCAN_COLS = 512
S5_SB_IN = S5_SCAN_COLS // S5_N * S5_P


def _s5_scan_body(u_ref, wb_ref, wc_ref, are_ref, aim_ref, y_ref, bu_ref, h_ref, *, steps):
    rows = steps * S5_CHAINS

    @pl.when(pl.program_id(0) == 0)
    def _():
        h_ref[...] = jnp.zeros_like(h_ref)

    u = u_ref[...]
    chain = lax.broadcasted_iota(jnp.int32, u.shape, 0) % S5_CHAINS
    fwd = chain < BATCH
    zero = jnp.zeros_like(u)
    uf = jnp.where(fwd, u, zero)
    ub = jnp.where(fwd, zero, u)
    is_fwd_y = lax.broadcasted_iota(jnp.int32, (rows, S5_SB_IN), 0) % S5_CHAINS < BATCH

    for cb in range(S5_STATE // S5_SCAN_COLS):
        cin = slice(cb * S5_SB_IN, (cb + 1) * S5_SB_IN)
        re = pl.ds(cb * S5_SCAN_COLS, S5_SCAN_COLS)
        im = pl.ds(S5_STATE + cb * S5_SCAN_COLS, S5_SCAN_COLS)
        bu = jnp.dot(jnp.concatenate([uf[:, cin], ub[:, cin]], axis=1), wb_ref[cb], preferred_element_type=F32)
        bu_ref[:, re] = bu[:, :S5_SCAN_COLS]
        bu_ref[:, im] = bu[:, S5_SCAN_COLS:]
        a_re = are_ref[:, re]
        a_im = aim_ref[:, re]

        def step(s, carry):
            h_re, h_im = carry
            r = pl.ds(pl.multiple_of(s * S5_CHAINS, S5_CHAINS), S5_CHAINS)
            n_re = a_re * h_re - a_im * h_im + bu_ref[r, re]
            n_im = a_re * h_im + a_im * h_re + bu_ref[r, im]
            bu_ref[r, re] = n_re
            bu_ref[r, im] = n_im
            return n_re, n_im

        h_re, h_im = lax.fori_loop(0, steps, step, (h_ref[:, re], h_ref[:, im]), unroll=4)
        h_ref[:, re] = h_re
        h_ref[:, im] = h_im
        hb = jnp.concatenate([bu_ref[:, re], bu_ref[:, im]], axis=1).astype(BF16)
        y2 = jnp.dot(hb, wc_ref[cb], preferred_element_type=F32)
        y_ref[:, cin] = jnp.where(is_fwd_y, y2[:, :S5_SB_IN], y2[:, S5_SB_IN:])


def s5_scan(u8, wb, wc, a_re8, a_im8, steps):
    n_rows = u8.shape[0]
    rows = steps * S5_CHAINS
    assert n_rows % rows == 0
    full = lambda a: pl.BlockSpec(a.shape, lambda i: (0,) * a.ndim)
    return pl.pallas_call(
        functools.partial(_s5_scan_body, steps=steps),
        out_shape=jax.ShapeDtypeStruct((n_rows, W_GRP), F32),
        grid=(n_rows // rows,),
        in_specs=[pl.BlockSpec((rows, W_GRP), lambda i: (i, 0)), full(wb), full(wc), full(a_re8), full(a_im8)],
        out_specs=pl.BlockSpec((rows, W_GRP), lambda i: (i, 0)),
        scratch_shapes=[pltpu.VMEM((rows, 2 * S5_STATE), F32), pltpu.VMEM((S5_CHAINS, 2 * S5_STATE), F32)],
        compiler_params=pltpu.CompilerParams(dimension_semantics=("arbitrary",),
                                             vmem_limit_bytes=V7X_VMEM_LIMIT_BYTES),
        name="s5_scan",
    )(u8, wb, wc, a_re8, a_im8)


def _s5_glu_body(u_ref, yf_ref, yb_ref, d_ref, w_ref, b_ref, o_ref):
    y = d_ref[...] * u_ref[...] + yf_ref[...] + yb_ref[...]
    g = jax.nn.gelu(y)
    z = jnp.dot(g.astype(BF16), w_ref[...], preferred_element_type=F32) + b_ref[...]
    o_ref[...] = g * jax.nn.sigmoid(z)


def s5_glu_pallas(u, yf, yb, d, w_bf16, b, tm):
    T, W = u.shape
    assert T % tm == 0
    row = pl.BlockSpec((tm, W), lambda i: (i, 0))
    vec = pl.BlockSpec((1, W), lambda i: (0, 0))
    return pl.pallas_call(
        _s5_glu_body,
        out_shape=jax.ShapeDtypeStruct((T, W), F32),
        grid=(T // tm,),
        in_specs=[row, row, row, vec, pl.BlockSpec((W, W), lambda i: (0, 0)), vec],
        out_specs=row,
        compiler_params=pltpu.CompilerParams(dimension_semantics=("arbitrary",),
                                             vmem_limit_bytes=V7X_VMEM_LIMIT_BYTES),
        name="s5_glu",
    )(u, yf, yb, d.reshape(1, W), w_bf16, b.reshape(1, W))


def _s5_weights(a_re, a_im, log_dt, b_re, b_im, c_re, c_im):
    eye = jnp.eye(S5_G, dtype=F32)
    wb, a8 = [], []
    wc = []
    for di in range(2):
        ab_re, ab_im, bb_re, bb_im = s5_discretize(a_re[di], a_im[di], log_dt[di], b_re[di], b_im[di])
        blk = lambda m: jnp.einsum('gnp,gh->gphn', m, eye).reshape(W_GRP, S5_STATE)
        wb.append(jnp.concatenate([blk(bb_re), blk(bb_im)], axis=1))
        a8.append((jnp.broadcast_to(ab_re.reshape(1, S5_STATE), (BATCH, S5_STATE)),
                   jnp.broadcast_to(ab_im.reshape(1, S5_STATE), (BATCH, S5_STATE))))
        cblk = lambda m: jnp.einsum('gpn,gh->gnhp', m.astype(F32), eye).reshape(S5_STATE, W_GRP)
        wc.append(jnp.concatenate([cblk(c_re[di]), -cblk(c_im[di])], axis=0))
    wb_sb, wc_sb = [], []
    for sb in range(S5_STATE // S5_SCAN_COLS):
        cin = slice(sb * S5_SB_IN, (sb + 1) * S5_SB_IN)
        re = slice(sb * S5_SCAN_COLS, (sb + 1) * S5_SCAN_COLS)
        im = slice(S5_STATE + sb * S5_SCAN_COLS, S5_STATE + (sb + 1) * S5_SCAN_COLS)
        wb_sb.append(jnp.concatenate([jnp.concatenate([w[cin, re], w[cin, im]], axis=1) for w in wb], axis=0))
        wc_sb.append(jnp.concatenate([jnp.concatenate([w[re, cin], w[im, cin]], axis=0) for w in wc], axis=1))
    wb_sb = jnp.stack(wb_sb).astype(BF16)
    wc_sb = jnp.stack(wc_sb).astype(BF16)
    a_re8 = jnp.concatenate([a8[0][0], a8[1][0]], axis=0)
    a_im8 = jnp.concatenate([a8[0][1], a8[1][1]], axis=0)
    return wb_sb, wc_sb, a_re8, a_im8


def s5_mixer_pallas(u_lat, u_ctx, a_re, a_im, log_dt, b_re, b_im, c_re, c_im, d, w_glu, b_glu,
                    with_ctx_out, steps=64, tm=512):
    B_, L, W = u_lat.shape
    C_ = u_ctx.shape[1]
    assert B_ == BATCH and W == W_GRP
    wb, wc, a_re8, a_im8 = _s5_weights(a_re, a_im, log_dt, b_re, b_im, c_re, c_im)
    seq_f = jnp.concatenate([u_ctx, u_lat], axis=1)
    seq_b = jnp.concatenate([u_ctx[:, ::-1], u_lat[:, ::-1]], axis=1)
    u8 = jnp.concatenate([seq_f, seq_b], axis=0).transpose(1, 0, 2).reshape((C_ + L) * S5_CHAINS, W)
    y8 = s5_scan(u8.astype(BF16), wb, wc, a_re8, a_im8, steps)
    y8 = y8.reshape(C_ + L, S5_CHAINS, W).transpose(1, 0, 2)
    yf, yb = y8[:B_], y8[B_:]
    w_glu_b = w_glu.astype(BF16)
    out_lat = s5_glu_pallas(u_lat.reshape(B_ * L, W), yf[:, C_:].reshape(B_ * L, W),
                            yb[:, C_:][:, ::-1].reshape(B_ * L, W), d, w_glu_b, b_glu, tm)
    out_ctx = None
    if with_ctx_out:
        out_ctx = s5_glu_pallas(u_ctx.reshape(B_ * C_, W), yf[:, :C_].reshape(B_ * C_, W),
                                yb[:, :C_][:, ::-1].reshape(B_ * C_, W), d, w_glu_b, b_glu,
                                min(tm, B_ * C_))
    return out_lat, out_ctx


PEER_HALF = PEER_QDIM // 2


def _argmax_rows(v, r):
    while v.shape[0] > 1:
        half = v.shape[0] // 2
        take_hi = v[half:] > v[:half]
        r = jnp.where(take_hi, r[half:], r[:half])
        v = jnp.maximum(v[:half], v[half:])
    return v, r


def _topk_rows(x, k):
    row = lax.broadcasted_iota(jnp.int32, x.shape, 0).astype(F32)
    krow = lax.broadcasted_iota(jnp.int32, (k, x.shape[1]), 0)

    def body(it, carry):
        x, rank, vals = carry
        m, first = _argmax_rows(x, row)
        hit = row == first
        return (jnp.where(hit, -jnp.inf, x), jnp.where(hit, it.astype(F32), rank), jnp.where(krow == it, m, vals))

    init = (x, jnp.full(x.shape, float(k), F32), jnp.zeros((k, x.shape[1]), F32))
    _, rank, vals = lax.fori_loop(0, k, body, init)
    return vals, rank


PEER_GRID_COLS = tuple(PEER_TOPK // (i + 1) for i in range(PEER_TOPK))
PEER_GRID_ROWS = 64


def _peer_route_body(x_ref, shift_ref, scale_ref, wq_ref, keys_ref, h_ref, n1_ref, c1_ref, r2_ref, e2_ref,
                     q_ref, *, lane_tiles):
    hm = (x_ref[...] * (1.0 + scale_ref[0]) + shift_ref[0]).astype(BF16)
    h_ref[...] = hm
    q_ref[...] = jnp.dot(hm, wq_ref[...], preferred_element_type=F32).astype(BF16)
    K = PEER_TOPK
    for hd in range(PEER_HEADS):
        for lt in range(lane_tiles):
            tok = pl.ds(lt * V7X_LANES, V7X_LANES)
            sc = []
            for side in range(2):
                col = (2 * hd + side) * PEER_HALF
                qs = q_ref[tok, col:col + PEER_HALF]
                sc.append(lax.dot_general(keys_ref[2 * hd + side], qs, (((1,), (1,)), ((), ())),
                                          preferred_element_type=F32))
            v1, r1 = _topk_rows(sc[0], K)
            v2, r2 = _topk_rows(sc[1], K)
            cells = [v1[i:i + 1] + v2[:PEER_GRID_COLS[i]] for i in range(K)]
            cells.append(jnp.full((PEER_GRID_ROWS - sum(PEER_GRID_COLS), V7X_LANES), -jnp.inf, F32))
            vc, rc = _topk_rows(jnp.concatenate(cells, axis=0), K)
            z = jnp.sum(jnp.exp(vc - vc[0:1]), axis=0, keepdims=True)
            chosen = jnp.where(rc < float(K), 1.0, 0.0)
            n1 = jnp.zeros_like(r1)
            off = 0
            for i in range(K):
                n_i = jnp.sum(chosen[off:off + PEER_GRID_COLS[i]], axis=0, keepdims=True)
                n1 = jnp.where(r1 == float(i), n_i, n1)
                off += PEER_GRID_COLS[i]
            n1_ref[hd, :, tok] = n1
            c1_ref[hd, :, tok] = jnp.exp(sc[0] - v1[0:1]) / z
            r2_ref[hd, :, tok] = r2.astype(BF16)
            e2_ref[hd, :, tok] = jnp.exp(sc[1] - v2[0:1]).astype(BF16)


def peer_route(x, shift, scale, wq_bf16, keys_bf16, rows_per_mod, tm):
    T, D = x.shape
    assert T % tm == 0 and rows_per_mod % tm == 0 and tm % V7X_LANES == 0
    tiles_per_mod = rows_per_mod // tm
    mod_spec = pl.BlockSpec((1, 1, D), lambda i: (i // tiles_per_mod, 0, 0))
    tab = lambda dt: jax.ShapeDtypeStruct((PEER_HEADS, PEER_NK, T), dt)
    tab_spec = pl.BlockSpec((PEER_HEADS, PEER_NK, tm), lambda i: (0, 0, i))
    return pl.pallas_call(
        functools.partial(_peer_route_body, lane_tiles=tm // V7X_LANES),
        out_shape=[jax.ShapeDtypeStruct((T, D), BF16), tab(F32), tab(F32), tab(BF16), tab(BF16)],
        grid=(T // tm,),
        in_specs=[pl.BlockSpec((tm, D), lambda i: (i, 0)), mod_spec, mod_spec,
                  pl.BlockSpec(wq_bf16.shape, lambda i: (0, 0)),
                  pl.BlockSpec(keys_bf16.shape, lambda i: (0, 0, 0))],
        out_specs=[pl.BlockSpec((tm, D), lambda i: (i, 0)), tab_spec, tab_spec, tab_spec, tab_spec],
        scratch_shapes=[pltpu.VMEM((tm, PEER_HEADS * PEER_QDIM), BF16)],
        compiler_params=pltpu.CompilerParams(dimension_semantics=("arbitrary",),
                                             vmem_limit_bytes=V7X_VMEM_LIMIT_BYTES),
        name="peer_route",
    )(x, shift, scale, wq_bf16, keys_bf16)


def _peer_dense_body(h_ref, u_ref, vt_ref, n1_ref, c1_ref, r2_ref, e2_ref, o_ref, acc_ref, a_ref, *, n_slab):
    j = pl.program_id(1)

    @pl.when(j == 0)
    def _():
        acc_ref[...] = jnp.zeros_like(acc_ref)

    pair = 2 * PEER_NK
    for p in range(n_slab // 2):
        s = lax.dot_general(u_ref[p * pair:(p + 1) * pair, :], h_ref[...], (((1,), (1,)), ((), ())),
                            preferred_element_type=F32)
        for kk in range(2):
            k = 2 * p + kk
            for lt in range(h_ref.shape[0] // V7X_LANES):
                lanes = slice(lt * V7X_LANES, (lt + 1) * V7X_LANES)
                g = None
                for hd in range(PEER_HEADS):
                    n1row = n1_ref[hd, k:k + 1, lanes].astype(BF16)
                    c1row = c1_ref[hd, k:k + 1, lanes].astype(BF16)
                    gh = jnp.where(r2_ref[hd, :, lanes] < n1row, c1row * e2_ref[hd, :, lanes],
                                   jnp.zeros((), BF16))
                    g = gh if g is None else g + gh
                sk = s[kk * PEER_NK:(kk + 1) * PEER_NK, lanes]
                a_ref[k * PEER_NK:(k + 1) * PEER_NK, lanes] = jax.nn.gelu(sk).astype(BF16) * g
    acc_ref[...] += jnp.dot(vt_ref[0], a_ref[...], preferred_element_type=F32)

    @pl.when(j == pl.num_programs(1) - 1)
    def _():
        o_ref[...] = acc_ref[...].T


def peer_dense(h_bf16, u_bf16, vt_tiles, n1, c1, r2, e2, tm, n_slab):
    T, D = h_bf16.shape
    NE = u_bf16.shape[0]
    e_tile = n_slab * PEER_NK
    n_e = NE // e_tile
    assert T % tm == 0 and NE % e_tile == 0 and vt_tiles.shape == (n_e, D, e_tile)
    slab_spec = pl.BlockSpec((PEER_HEADS, n_slab, tm), lambda i, j: (0, j, i))
    tok_spec = pl.BlockSpec((PEER_HEADS, PEER_NK, tm), lambda i, j: (0, 0, i))
    return pl.pallas_call(
        functools.partial(_peer_dense_body, n_slab=n_slab),
        out_shape=jax.ShapeDtypeStruct((T, D), F32),
        grid=(T // tm, n_e),
        in_specs=[pl.BlockSpec((tm, D), lambda i, j: (i, 0)),
                  pl.BlockSpec((e_tile, D), lambda i, j: (j, 0)),
                  pl.BlockSpec((1, D, e_tile), lambda i, j: (j, 0, 0)),
                  slab_spec, slab_spec, tok_spec, tok_spec],
        out_specs=pl.BlockSpec((tm, D), lambda i, j: (i, 0)),
        scratch_shapes=[pltpu.VMEM((D, tm), F32), pltpu.VMEM((e_tile, tm), BF16)],
        compiler_params=pltpu.CompilerParams(dimension_semantics=("arbitrary", "arbitrary"),
                                             vmem_limit_bytes=V7X_VMEM_LIMIT_BYTES),
        name="peer_dense",
    )(h_bf16, u_bf16, vt_tiles, n1, c1, r2, e2)


PEER_SLABS = 8


def peer_v_tiles(v_tab):
    e_tile = PEER_SLABS * PEER_NK
    return v_tab.astype(BF16).reshape(v_tab.shape[0] // e_tile, e_tile, v_tab.shape[1]).transpose(0, 2, 1)


def peer_ffn_pallas(x, shift, scale, wq_bf16, keys_bf16, u_bf16, vt_tiles, rows_per_mod, tm_route=256,
                    tm_dense=512):
    h, n1, c1, r2, e2 = peer_route(x, shift, scale, wq_bf16, keys_bf16, rows_per_mod, tm_route)
    return peer_dense(h, u_bf16, vt_tiles, n1, c1, r2, e2, tm_dense, PEER_SLABS)


POOL_HALO = 8
POOL_ROWS = 256


def _pool_body(u_ref, w_ref, scale_ref, o_ref, pad_ref, *, seq_len):
    L = seq_len
    chunk = min(POOL_ROWS, L)
    zeros = jnp.zeros((POOL_HALO, W_GRP), F32)
    pad_ref[0:POOL_HALO, :] = zeros
    pad_ref[POOL_HALO + L:2 * POOL_HALO + L, :] = zeros
    pad_ref[POOL_HALO:POOL_HALO + L, :] = u_ref[0]
    for r0 in range(0, L, chunk):
        t = r0 + lax.broadcasted_iota(jnp.int32, (chunk, POOL_C), 0)
        for j, w in enumerate(POOL_WINDOWS):
            cols = slice(j * POOL_C, (j + 1) * POOL_C)
            acc = None
            for o in range(-(w // 2), w - w // 2):
                s = pad_ref[POOL_HALO + r0 + o:POOL_HALO + r0 + o + chunk, cols]
                acc = s if acc is None else acc + s
            lo = jnp.maximum(t - w // 2, 0)
            hi = jnp.minimum(t - w // 2 + w - 1, L - 1)
            cnt = (hi - lo + 1).astype(F32)
            pooled = acc / cnt - u_ref[0, r0:r0 + chunk, cols]
            y = jnp.dot(pooled.astype(BF16), w_ref[j], preferred_element_type=F32)
            o_ref[0, r0:r0 + chunk, cols] = y * scale_ref[:, cols]


def pool_mix_pallas(u, pool_w, pool_scale):
    B_, L, W = u.shape
    assert max(POOL_WINDOWS) // 2 <= POOL_HALO and L % min(POOL_ROWS, L) == 0
    blk = pl.BlockSpec((1, L, W), lambda b: (b, 0, 0))
    return pl.pallas_call(
        functools.partial(_pool_body, seq_len=L),
        out_shape=jax.ShapeDtypeStruct((B_, L, W), F32),
        grid=(B_,),
        in_specs=[blk, pl.BlockSpec(pool_w.shape, lambda b: (0, 0, 0)), pl.BlockSpec((1, W), lambda b: (0, 0))],
        out_specs=blk,
        scratch_shapes=[pltpu.VMEM((L + 2 * POOL_HALO, W), F32)],
        compiler_params=pltpu.CompilerParams(dimension_semantics=("arbitrary",),
                                             vmem_limit_bytes=V7X_VMEM_LIMIT_BYTES),
        name="pool_mix",
    )(u, pool_w.astype(BF16), pool_scale.reshape(1, W))


NA_TILE_R = 4
NA_TILE = NA_TILE_R * GRID_W
NA_SCALE = NA_HD ** -0.5


def _na_bias_table(rpb, n_rows):
    n_tiles = n_rows // NA_TILE_R
    KR = min(NA_WIN_R, n_rows)
    a = np.array([0, min(2, n_tiles - 1), n_tiles - 1]).reshape(3, 1, 1, 1, 1, 1)
    d = np.arange(3).reshape(1, 3, 1, 1, 1, 1)
    i = np.arange(NA_TILE_R).reshape(1, 1, NA_TILE_R, 1, 1, 1)
    qc = np.arange(GRID_W).reshape(1, 1, 1, GRID_W, 1, 1)
    j = np.arange(NA_TILE_R).reshape(1, 1, 1, 1, NA_TILE_R, 1)
    kc = np.arange(GRID_W).reshape(1, 1, 1, 1, 1, GRID_W)
    qr = NA_TILE_R * a + i
    kr = NA_TILE_R * (a + d - 1) + j
    rs = np.clip(qr - KR // 2, 0, n_rows - KR)
    c0 = np.clip(qc - NA_WIN_C // 2, 0, GRID_W - NA_WIN_C)
    ok = (kr >= rs) & (kr < rs + KR) & (kr >= 0) & (kr < n_rows) & (kc >= c0) & (kc < c0 + NA_WIN_C)
    row_rel = np.clip(kr - qr + NA_WIN_R - 1, 0, 2 * NA_WIN_R - 2)[:, :, :, 0, :, 0]
    col_rel = (np.clip(kc - qc, -(NA_WIN_C - 1), NA_WIN_C - 1) + NA_WIN_C - 1)[0, 0, 0, :, 0, :]
    onehot = (col_rel[None] == np.arange(2 * NA_WIN_C - 1)[:, None, None]).astype(np.float32)
    bias_rc = jnp.einsum('hrc,cqk->hrqk', rpb.astype(F32), onehot, precision=lax.Precision.HIGHEST)
    tab = jnp.stack([bias_rc[:, int(r)] for r in row_rel.reshape(-1)], axis=1)
    tab = tab.reshape(NA_H, 3, 3, NA_TILE_R, NA_TILE_R, GRID_W, GRID_W).transpose(1, 0, 2, 3, 5, 4, 6)
    tab = jnp.where(ok[:, None], tab, -jnp.inf)
    return tab.reshape(3, NA_H, 3, NA_TILE, NA_TILE)


def _na_body(q_ref, k_ref, v_ref, kc_ref, vc_ref, t_ref, o_ref, *, n_tiles, with_grid):
    a = pl.program_id(1)
    nt = (((1,), (1,)), ((), ()))
    for h in range(NA_H):
        hs = slice(h * NA_HD, (h + 1) * NA_HD)
        qh = q_ref[:, hs]
        scores = [lax.dot_general(qh, kc_ref[:, hs], nt, preferred_element_type=F32) * NA_SCALE]
        vals = [vc_ref[:, hs]]
        if with_grid:
            for d in range(3):
                ti = jnp.clip(a + d - 1, 0, n_tiles - 1)
                rows = pl.ds(pl.multiple_of(ti * NA_TILE, NA_TILE), NA_TILE)
                s = lax.dot_general(qh, k_ref[rows, hs], nt, preferred_element_type=F32)
                scores.append(s * NA_SCALE + t_ref[0, h, d])
                vals.append(v_ref[rows, hs])
        m = scores[0].max(axis=-1, keepdims=True)
        for s in scores[1:]:
            m = jnp.maximum(m, s.max(axis=-1, keepdims=True))
        den = None
        acc = None
        for s, vv in zip(scores, vals):
            p = jnp.exp(s - m)
            l = p.sum(axis=-1, keepdims=True)
            o = jnp.dot(p.astype(BF16), vv, preferred_element_type=F32)
            den = l if den is None else den + l
            acc = o if acc is None else acc + o
        o_ref[:, hs] = acc / den


def na_attention(q, k, v, kc, vc, table, seq_len, ctx_len, with_grid):
    W = q.shape[1]
    n_b = kc.shape[0] // ctx_len
    lq = q.shape[0] // n_b
    assert lq % NA_TILE == 0
    q_tiles = lq // NA_TILE
    n_tiles = seq_len // NA_TILE
    pat = lambda b, a: (jnp.where(a == 0, 0, jnp.where(a == n_tiles - 1, 2, 1)), 0, 0, 0, 0)
    qo_spec = pl.BlockSpec((NA_TILE, W), lambda b, a: (b * q_tiles + a, 0))
    return pl.pallas_call(
        functools.partial(_na_body, n_tiles=n_tiles, with_grid=with_grid),
        out_shape=jax.ShapeDtypeStruct(q.shape, F32),
        grid=(n_b, q_tiles),
        in_specs=[qo_spec,
                  pl.BlockSpec((seq_len, W), lambda b, a: (b, 0)), pl.BlockSpec((seq_len, W), lambda b, a: (b, 0)),
                  pl.BlockSpec((ctx_len, W), lambda b, a: (b, 0)), pl.BlockSpec((ctx_len, W), lambda b, a: (b, 0)),
                  pl.BlockSpec((1,) + table.shape[1:], pat)],
        out_specs=qo_spec,
        compiler_params=pltpu.CompilerParams(dimension_semantics=("arbitrary", "arbitrary"),
                                             vmem_limit_bytes=V7X_VMEM_LIMIT_BYTES),
        name="na_attention",
    )(q, k, v, kc, vc, table)


SSM_HALO = 8
SSM_BC = SSM_G * SSM_N


def _ssm_conv_body(prev_ref, cur_ref, next_ref, w_ref, b_ref, o_ref, pad_ref, *, tiles_per_seq):
    pos = pl.program_id(0) % tiles_per_seq
    tm = cur_ref.shape[0]
    pad_ref[0:SSM_HALO, :] = jnp.where(pos == 0, 0.0, prev_ref[...])
    pad_ref[SSM_HALO:SSM_HALO + tm, :] = cur_ref[...]
    pad_ref[SSM_HALO + tm:2 * SSM_HALO + tm, :] = jnp.where(pos == tiles_per_seq - 1, 0.0, next_ref[...])
    lead = (SSM_CONV - 1) // 2
    y = b_ref[...]
    for k in range(SSM_CONV):
        y = y + w_ref[k:k + 1, :] * pad_ref[SSM_HALO - lead + k:SSM_HALO - lead + k + tm, :]
    o_ref[...] = jax.nn.silu(y)


def ssm_conv(xbc, conv_w, conv_b, seq_len, tm):
    T, CH = xbc.shape
    assert seq_len % tm == 0 and tm % SSM_HALO == 0
    hb = tm // SSM_HALO
    n_hb = T // SSM_HALO
    return pl.pallas_call(
        functools.partial(_ssm_conv_body, tiles_per_seq=seq_len // tm),
        out_shape=jax.ShapeDtypeStruct((T, CH), F32),
        grid=(T // tm,),
        in_specs=[pl.BlockSpec((SSM_HALO, CH), lambda i: (jnp.maximum(i * hb - 1, 0), 0)),
                  pl.BlockSpec((tm, CH), lambda i: (i, 0)),
                  pl.BlockSpec((SSM_HALO, CH), lambda i: (jnp.minimum((i + 1) * hb, n_hb - 1), 0)),
                  pl.BlockSpec((SSM_CONV, CH), lambda i: (0, 0)), pl.BlockSpec((1, CH), lambda i: (0, 0))],
        out_specs=pl.BlockSpec((tm, CH), lambda i: (i, 0)),
        scratch_shapes=[pltpu.VMEM((tm + 2 * SSM_HALO, CH), F32)],
        compiler_params=pltpu.CompilerParams(dimension_semantics=("arbitrary",),
                                             vmem_limit_bytes=V7X_VMEM_LIMIT_BYTES),
        name="ssm_conv",
    )(xbc, xbc, xbc, conv_w, conv_b.reshape(1, CH))


def _ssd_body(xbc_ref, dt_ref, bias_ref, a_ref, h0_ref, y_ref, hfin_ref, h_scr, *, di, rev):
    ci = pl.program_id(1)
    Q = xbc_ref.shape[0]

    @pl.when(ci == 0)
    def _():
        h_scr[...] = h0_ref[0]

    dt_all = jax.nn.softplus(dt_ref[...] + bias_ref[...])
    a_cs = dt_all * a_ref[...]
    row_id = lax.broadcasted_iota(jnp.int32, a_cs.shape, 0)
    sh = 1
    while sh < Q:
        if rev:
            a_cs = a_cs + jnp.where(row_id < Q - sh, pltpu.roll(a_cs, Q - sh, 0), 0.0)
        else:
            a_cs = a_cs + jnp.where(row_id >= sh, pltpu.roll(a_cs, sh, 0), 0.0)
        sh *= 2
    a_cs_t = a_cs.T
    a_tot = a_cs[0:1, :] if rev else a_cs[Q - 1:Q, :]
    l_id = lax.broadcasted_iota(jnp.int32, (Q, Q), 0)
    s_id = lax.broadcasted_iota(jnp.int32, (Q, Q), 1)
    causal = (l_id <= s_id) if rev else (l_id >= s_id)
    nt = (((1,), (1,)), ((), ()))
    tn = (((0,), (0,)), ((), ()))
    cb = []
    for g in range(SSM_G):
        bg = xbc_ref[:, W_GRP + g * SSM_N:W_GRP + (g + 1) * SSM_N].astype(BF16)
        cg = xbc_ref[:, W_GRP + SSM_BC + g * SSM_N:W_GRP + SSM_BC + (g + 1) * SSM_N].astype(BF16)
        cb.append((bg, cg, lax.dot_general(cg, bg, nt, preferred_element_type=F32)))
    for h in range(SSM_H):
        c = di * SSM_H + h
        bg, cg, cbg = cb[h // (SSM_H // SSM_G)]
        col = a_cs[:, c:c + 1]
        lm = jnp.exp(jnp.where(causal, col - a_cs_t[c:c + 1, :], -jnp.inf))
        xh = xbc_ref[:, h * SSM_HD:(h + 1) * SSM_HD] * dt_all[:, c:c + 1]
        hp = h_scr[h]
        yd = jnp.dot((cbg * lm).astype(BF16), xh.astype(BF16), preferred_element_type=F32)
        yo = lax.dot_general(cg, hp.astype(BF16), nt, preferred_element_type=F32) * jnp.exp(col)
        y_ref[:, h * SSM_HD:(h + 1) * SSM_HD] = yd + yo
        tot = a_tot[:, c:c + 1]
        xd = (xh * jnp.exp(tot - col)).astype(BF16)
        h_scr[h] = jnp.exp(tot) * hp + lax.dot_general(xd, bg, tn, preferred_element_type=F32)

    @pl.when(ci == pl.num_programs(1) - 1)
    def _():
        hfin_ref[0] = h_scr[...]


def ssd_scan(xbc_act, dt_raw, bias128, a128, h0, seq_len, di, rev):
    T = xbc_act.shape[0]
    n_b = T // seq_len
    Q = min(SSM_CHUNK, seq_len)
    nc = seq_len // Q
    chunk = (lambda b, i: (b * nc + nc - 1 - i, 0)) if rev else (lambda b, i: (b * nc + i, 0))
    vec = pl.BlockSpec((1, V7X_LANES), lambda b, i: (0, 0))
    st = pl.BlockSpec((1, SSM_H, SSM_HD, SSM_N), lambda b, i: (b, 0, 0, 0))
    return pl.pallas_call(
        functools.partial(_ssd_body, di=di, rev=rev),
        out_shape=[jax.ShapeDtypeStruct((T, W_GRP), F32), jax.ShapeDtypeStruct(h0.shape, F32)],
        grid=(n_b, nc),
        in_specs=[pl.BlockSpec((Q, SSM_CONV_CH), chunk), pl.BlockSpec((Q, V7X_LANES), chunk), vec, vec, st],
        out_specs=[pl.BlockSpec((Q, W_GRP), chunk), st],
        scratch_shapes=[pltpu.VMEM((SSM_H, SSM_HD, SSM_N), F32)],
        compiler_params=pltpu.CompilerParams(dimension_semantics=("arbitrary", "arbitrary"),
                                             vmem_limit_bytes=V7X_VMEM_LIMIT_BYTES),
        name="ssd_scan",
    )(xbc_act, dt_raw, bias128, a128, h0)


def _ssm_out_body(xbc_ref, yf_ref, yb_ref, z_ref, d_ref, nw_ref, o_ref):
    y = d_ref[...] * xbc_ref[:, :W_GRP] + yf_ref[...] + yb_ref[...]
    g = y * jax.nn.silu(z_ref[...])
    gw = W_GRP // SSM_G
    for k in range(SSM_G):
        gk = g[:, k * gw:(k + 1) * gw]
        r = lax.rsqrt(jnp.mean(gk * gk, -1, keepdims=True) + LN_EPS)
        o_ref[:, k * gw:(k + 1) * gw] = gk * r * nw_ref[:, k * gw:(k + 1) * gw]


def ssm_out(xbc_act, yf, yb, z, d512, norm_w, tm):
    T = z.shape[0]
    assert T % tm == 0
    row = pl.BlockSpec((tm, W_GRP), lambda i: (i, 0))
    vec = pl.BlockSpec((1, W_GRP), lambda i: (0, 0))
    return pl.pallas_call(
        _ssm_out_body,
        out_shape=jax.ShapeDtypeStruct((T, W_GRP), F32),
        grid=(T // tm,),
        in_specs=[pl.BlockSpec((tm, SSM_CONV_CH), lambda i: (i, 0)), row, row, row, vec, vec],
        out_specs=row,
        compiler_params=pltpu.CompilerParams(dimension_semantics=("arbitrary",),
                                             vmem_limit_bytes=V7X_VMEM_LIMIT_BYTES),
        name="ssm_out",
    )(xbc_act, yf, yb, z, d512, norm_w.reshape(1, W_GRP))


def ssm_mixer_pallas(z_l, xbc_l, dt_l, z_c, xbc_c, dt_c, conv_w, conv_b, dt_bias, a_log, d, norm_w,
                     seq_len, ctx_len, with_ctx_out, tm=256):
    n_b = z_l.shape[0] // seq_len
    pad = V7X_LANES - 2 * SSM_H
    bias128 = jnp.pad(dt_bias.astype(F32).reshape(1, 2 * SSM_H), ((0, 0), (0, pad)))
    a128 = jnp.pad(-jnp.exp(a_log.astype(F32)).reshape(1, 2 * SSM_H), ((0, 0), (0, pad)))
    d512 = jnp.repeat(d.astype(F32), SSM_HD).reshape(1, W_GRP)
    act_c = ssm_conv(xbc_c, conv_w, conv_b, ctx_len, min(tm, ctx_len))
    act_l = ssm_conv(xbc_l, conv_w, conv_b, seq_len, tm)
    zero = jnp.zeros((n_b, SSM_H, SSM_HD, SSM_N), F32)
    y_c, y_l = [], []
    for di in range(2):
        yc, hc = ssd_scan(act_c, dt_c, bias128, a128, zero, ctx_len, di, di == 1)
        yl, _ = ssd_scan(act_l, dt_l, bias128, a128, hc, seq_len, di, di == 1)
        y_c.append(yc)
        y_l.append(yl)
    out_l = ssm_out(act_l, y_l[0], y_l[1], z_l, d512, norm_w, tm)
    out_c = ssm_out(act_c, y_c[0], y_c[1], z_c, d512, norm_w, min(tm, z_c.shape[0])) if with_ctx_out else None
    return out_l, out_c


def s5_discretize(a_re, a_im, log_dt, b_re, b_im):
    a_re, a_im = a_re.astype(F32), a_im.astype(F32)
    b_re, b_im = b_re.astype(F32), b_im.astype(F32)
    dt = jnp.exp(log_dt.astype(F32))[:, None]
    mag = jnp.exp(a_re * dt)
    ab_re = mag * jnp.cos(a_im * dt)
    ab_im = mag * jnp.sin(a_im * dt)
    den = a_re * a_re + a_im * a_im
    f_re = ((ab_re - 1) * a_re + ab_im * a_im) / den
    f_im = (ab_im * a_re - (ab_re - 1) * a_im) / den
    bb_re = f_re[..., None] * b_re - f_im[..., None] * b_im
    bb_im = f_re[..., None] * b_im + f_im[..., None] * b_re
    return ab_re, ab_im, bb_re, bb_im


def _cscan_combine(e1, e2):
    a1r, a1i, b1r, b1i = e1
    a2r, a2i, b2r, b2i = e2
    return (a2r * a1r - a2i * a1i, a2r * a1i + a2i * a1r,
            a2r * b1r - a2i * b1i + b2r, a2r * b1i + a2i * b1r + b2i)


def s5_states(u, ab_re, ab_im, bb_re, bb_im, h0_re, h0_im):
    bu_re = jnp.einsum('gnp,blgp->blgn', bb_re, u)
    bu_im = jnp.einsum('gnp,blgp->blgn', bb_im, u)
    bu_re = bu_re.at[:, 0].add(ab_re * h0_re - ab_im * h0_im)
    bu_im = bu_im.at[:, 0].add(ab_re * h0_im + ab_im * h0_re)
    a_re = jnp.broadcast_to(ab_re, bu_re.shape)
    a_im = jnp.broadcast_to(ab_im, bu_im.shape)
    _, _, h_re, h_im = lax.associative_scan(_cscan_combine, (a_re, a_im, bu_re, bu_im), axis=1)
    return h_re, h_im


def s5_readout(c_re, c_im, h_re, h_im):
    return (jnp.einsum('gpn,blgn->blgp', c_re.astype(F32), h_re)
            - jnp.einsum('gpn,blgn->blgp', c_im.astype(F32), h_im))


def s5_glu(y, w_glu, b_glu, dtype):
    B_, L = y.shape[:2]
    g = jax.nn.gelu(y.reshape(B_, L, W_GRP))
    return (g * jax.nn.sigmoid(g @ w_glu.astype(F32) + b_glu.astype(F32))).astype(dtype)


def s5_mixer(u_lat, u_ctx, a_re, a_im, log_dt, b_re, b_im, c_re, c_im, d, w_glu, b_glu, with_ctx_out):
    B_ = u_lat.shape[0]
    ul = u_lat.astype(F32).reshape(B_, -1, S5_G, S5_P)
    uc = u_ctx.astype(F32).reshape(B_, -1, S5_G, S5_P)
    dg = d.astype(F32).reshape(S5_G, S5_P)
    zero = jnp.zeros((B_, S5_G, S5_N), F32)
    y_lat = dg * ul
    y_ctx = dg * uc if with_ctx_out else None
    for di in range(2):
        rev = di == 1
        ab_re, ab_im, bb_re, bb_im = s5_discretize(a_re[di], a_im[di], log_dt[di], b_re[di], b_im[di])
        hc_re, hc_im = s5_states(_flip(uc, rev), ab_re, ab_im, bb_re, bb_im, zero, zero)
        hl_re, hl_im = s5_states(_flip(ul, rev), ab_re, ab_im, bb_re, bb_im, hc_re[:, -1], hc_im[:, -1])
        y_lat = y_lat + _flip(s5_readout(c_re[di], c_im[di], hl_re, hl_im), rev)
        if with_ctx_out:
            y_ctx = y_ctx + _flip(s5_readout(c_re[di], c_im[di], hc_re, hc_im), rev)
    out_lat = s5_glu(y_lat, w_glu, b_glu, u_lat.dtype)
    out_ctx = s5_glu(y_ctx, w_glu, b_glu, u_ctx.dtype) if with_ctx_out else None
    return out_lat, out_ctx


def pool_mix(u, pool_w, pool_scale):
    B_, L, _ = u.shape
    uf = u.astype(F32)
    csum = jnp.pad(jnp.cumsum(uf, axis=1), ((0, 0), (1, 0), (0, 0)))
    t = jnp.arange(L)
    outs = []
    for j, w in enumerate(POOL_WINDOWS):
        lo = jnp.clip(t - w // 2, 0, L - 1)
        hi = jnp.clip(t - w // 2 + w - 1, 0, L - 1)
        ch = slice(j * POOL_C, (j + 1) * POOL_C)
        cnt = (hi - lo + 1).astype(F32)[None, :, None]
        outs.append((csum[:, hi + 1, ch] - csum[:, lo, ch]) / cnt - uf[:, :, ch])
    pooled = jnp.stack(outs, axis=2)
    y = jnp.einsum('blgc,gcd->blgd', pooled, pool_w.astype(F32)).reshape(B_, L, W_GRP)
    return (y * pool_scale.astype(F32)).astype(u.dtype)


def na_latent(q, k, v, k_ctx, v_ctx, rpb):
    B_, L, _ = q.shape
    R = L // GRID_W
    KR = min(NA_WIN_R, R)
    KC = NA_WIN_C
    shp = (B_, R, GRID_W, NA_H, NA_HD)
    q, k, v = q.reshape(shp), k.reshape(shp), v.reshape(shp)
    r = jnp.arange(R)
    row_idx = jnp.clip(r - KR // 2, 0, R - KR)[:, None] + jnp.arange(KR)[None, :]
    k_rows = k[:, row_idx]
    v_rows = v[:, row_idx]
    col = jnp.arange(GRID_W)
    c0 = jnp.clip(col - KC // 2, 0, GRID_W - KC)
    in_win = (col[None, :] >= c0[:, None]) & (col[None, :] < c0[:, None] + KC)
    col_rel = jnp.clip(col[None, :] - col[:, None], -(KC - 1), KC - 1) + KC - 1
    row_rel = row_idx - r[:, None] + NA_WIN_R - 1
    bias = rpb.astype(F32)[:, row_rel][..., col_rel]
    bias = bias.transpose(1, 0, 3, 2, 4)
    scale = NA_HD ** -0.5
    s_nb = jnp.einsum('brqhd,brkwhd->brhqkw', q, k_rows).astype(F32) * scale + bias
    s_nb = jnp.where(in_win[:, None, :], s_nb, -jnp.inf).reshape(B_, R, NA_H, GRID_W, KR * GRID_W)
    s_cx = jnp.einsum('brqhd,bchd->brhqc', q, k_ctx).astype(F32) * scale
    p = jax.nn.softmax(jnp.concatenate([s_nb, s_cx], axis=-1), axis=-1).astype(v.dtype)
    p_nb = p[..., :KR * GRID_W].reshape(B_, R, NA_H, GRID_W, KR, GRID_W)
    p_cx = p[..., KR * GRID_W:]
    o = (jnp.einsum('brhqkw,brkwhd->brqhd', p_nb, v_rows)
         + jnp.einsum('brhqc,bchd->brqhd', p_cx, v_ctx))
    return o.reshape(B_, L, W_GRP)


def ctx_attn(q, k, v):
    s = jnp.einsum('bqhd,bkhd->bhqk', q, k).astype(F32) * NA_HD ** -0.5
    p = jax.nn.softmax(s, axis=-1).astype(v.dtype)
    return jnp.einsum('bhqk,bkhd->bqhd', p, v)


def na_mixer(q_l, k_l, v_l, q_c, k_c, v_c, rpb, with_ctx_out):
    B_, C_ = k_c.shape[:2]
    kc = k_c.reshape(B_, C_, NA_H, NA_HD)
    vc = v_c.reshape(B_, C_, NA_H, NA_HD)
    out_l = na_latent(q_l, k_l, v_l, kc, vc, rpb)
    out_c = (ctx_attn(q_c.reshape(B_, C_, NA_H, NA_HD), kc, vc).reshape(B_, C_, W_GRP)
             if with_ctx_out else None)
    return out_l, out_c


def dwconv(x, w, b):
    y = lax.conv_general_dilated(x, w[:, None, :].astype(x.dtype), window_strides=(1,),
                                 padding=[((SSM_CONV - 1) // 2, SSM_CONV // 2)],
                                 dimension_numbers=('NWC', 'WIO', 'NWC'),
                                 feature_group_count=x.shape[-1])
    return y + b


def segsum(x):
    T = x.shape[-1]
    cs = jnp.cumsum(x, axis=-1)
    d = cs[..., :, None] - cs[..., None, :]
    return jnp.where(jnp.tril(jnp.ones((T, T), bool)), d, -jnp.inf)


def ssd(X, A, Bm, Cm, h0):
    b, L, H, P = X.shape
    N = Bm.shape[-1]
    nc = L // SSM_CHUNK
    X = X.reshape(b, nc, SSM_CHUNK, H, P)
    Bm = Bm.reshape(b, nc, SSM_CHUNK, H, N)
    Cm = Cm.reshape(b, nc, SSM_CHUNK, H, N)
    A = A.reshape(b, nc, SSM_CHUNK, H).transpose(0, 3, 1, 2)
    A_cs = jnp.cumsum(A, axis=-1)
    cb = jnp.einsum('bclhn,bcshn->bhcls', Cm, Bm)
    y_diag = jnp.einsum('bhcls,bcshp->bclhp', cb * jnp.exp(segsum(A)), X)
    decay_states = jnp.exp(A_cs[..., -1:] - A_cs).transpose(0, 2, 3, 1)[..., None]
    states = jnp.einsum('bclhn,bclhp->bchpn', Bm, X * decay_states)
    states = jnp.concatenate([h0[:, None], states], axis=1)
    chunk_decay = jnp.exp(segsum(jnp.pad(A_cs[..., -1], ((0, 0), (0, 0), (1, 0)))))
    new_states = jnp.einsum('bhzc,bchpn->bzhpn', chunk_decay, states)
    prev, final = new_states[:, :-1], new_states[:, -1]
    y_off = (jnp.einsum('bclhn,bchpn->bclhp', Cm, prev)
             * jnp.exp(A_cs).transpose(0, 2, 3, 1)[..., None])
    return (y_diag + y_off).reshape(b, L, H, P), final


def ssm_prepare(pxbc, pdt, conv_w, conv_b, dt_bias):
    B_, L, _ = pxbc.shape
    xbc = jax.nn.silu(dwconv(pxbc, conv_w, conv_b).astype(F32))
    hpg = SSM_H // SSM_G
    xs = xbc[..., :W_GRP].reshape(B_, L, SSM_H, SSM_HD)
    bm = jnp.repeat(xbc[..., W_GRP:W_GRP + SSM_G * SSM_N].reshape(B_, L, SSM_G, SSM_N), hpg, axis=2)
    cm = jnp.repeat(xbc[..., W_GRP + SSM_G * SSM_N:].reshape(B_, L, SSM_G, SSM_N), hpg, axis=2)
    dt = jax.nn.softplus(pdt.astype(F32).reshape(B_, L, 2, SSM_H) + dt_bias.astype(F32))
    return xs, bm, cm, dt


def ssm_direction(xs, bm, cm, dt, a, h0, rev):
    y, h_fin = ssd(_flip(xs * dt[..., None], rev), _flip(dt * a, rev), _flip(bm, rev), _flip(cm, rev), h0)
    return _flip(y, rev), h_fin


def ssm_gate_norm(y, z, norm_w):
    B_, L = z.shape[:2]
    g = (y.reshape(B_, L, W_GRP) * jax.nn.silu(z.astype(F32))).reshape(B_, L, SSM_G, W_GRP // SSM_G)
    g = g * lax.rsqrt(jnp.mean(jnp.square(g), -1, keepdims=True) + LN_EPS)
    return (g.reshape(B_, L, W_GRP) * norm_w.astype(F32)).astype(z.dtype)


def ssm_mixer(z_l, xbc_l, dtr_l, z_c, xbc_c, dtr_c, conv_w, conv_b, dt_bias, a_log, d, norm_w, with_ctx_out):
    xs_l, bm_l, cm_l, dt_l = ssm_prepare(xbc_l, dtr_l, conv_w, conv_b, dt_bias)
    xs_c, bm_c, cm_c, dt_c = ssm_prepare(xbc_c, dtr_c, conv_w, conv_b, dt_bias)
    a = -jnp.exp(a_log.astype(F32))
    dh = d.astype(F32)[:, None]
    zero = jnp.zeros((xs_c.shape[0], SSM_H, SSM_HD, SSM_N), F32)
    y_l = dh * xs_l
    y_c = dh * xs_c if with_ctx_out else None
    for di in range(2):
        rev = di == 1
        yc, hc = ssm_direction(xs_c, bm_c, cm_c, dt_c[:, :, di], a[di], zero, rev)
        yl, _ = ssm_direction(xs_l, bm_l, cm_l, dt_l[:, :, di], a[di], hc, rev)
        y_l = y_l + yl
        if with_ctx_out:
            y_c = y_c + yc
    out_l = ssm_gate_norm(y_l, z_l, norm_w)
    out_c = ssm_gate_norm(y_c, z_c, norm_w) if with_ctx_out else None
    return out_l, out_c


def peer_ffn(t, w_q, sub_keys, u_tab, v_tab):
    n_tok = t.shape[0]
    q = (t @ w_q).reshape(n_tok, PEER_HEADS, 2, PEER_QDIM // 2)
    s = jnp.einsum('thsd,hskd->thsk', q, sub_keys).astype(F32)
    sv, si = lax.top_k(s, PEER_TOPK)
    cand_s = (sv[:, :, 0, :, None] + sv[:, :, 1, None, :]).reshape(n_tok, PEER_HEADS, PEER_TOPK * PEER_TOPK)
    cand_i = (si[:, :, 0, :, None] * PEER_NK + si[:, :, 1, None, :]).reshape(n_tok, PEER_HEADS, PEER_TOPK * PEER_TOPK)
    top_s, pos = lax.top_k(cand_s, PEER_TOPK)
    idx = jnp.take_along_axis(cand_i, pos, axis=-1).reshape(n_tok, PEER_HEADS * PEER_TOPK)
    gate = jax.nn.softmax(top_s, axis=-1).reshape(n_tok, PEER_HEADS * PEER_TOPK)
    n_blk = n_tok // PEER_BLOCK

    def expert_block(args):
        xb, ib, gb = args
        act = jnp.einsum('tkd,td->tk', u_tab[ib], xb).astype(F32)
        return jnp.einsum('tk,tkd->td', (jax.nn.gelu(act) * gb).astype(xb.dtype), v_tab[ib])

    out = lax.map(expert_block, (t.reshape(n_blk, PEER_BLOCK, -1),
                                 idx.reshape(n_blk, PEER_BLOCK, -1),
                                 gate.reshape(n_blk, PEER_BLOCK, -1)))
    return out.reshape(n_tok, -1)


def _mixers(p_l, p_c, prm, with_ctx_out):
    s5_l, pool_l, q_l, k_l, v_l, z_l, xbc_l, dt_l = p_l
    s5_c, pool_c, q_c, k_c, v_c, z_c, xbc_c, dt_c = p_c
    ya_l, ya_c = s5_mixer_pallas(s5_l, s5_c, prm["s5_a_re"], prm["s5_a_im"], prm["s5_log_dt"],
                                 prm["s5_b_re"], prm["s5_b_im"], prm["s5_c_re"], prm["s5_c_im"], prm["s5_d"],
                                 prm["s5_w_glu"], prm["s5_b_glu"], with_ctx_out)
    yb_l = pool_mix_pallas(pool_l, prm["pool_w"], prm["pool_scale"])
    yb_c = pool_mix_pallas(pool_c, prm["pool_w"], prm["pool_scale"]) if with_ctx_out else None
    B_, L, _ = s5_l.shape
    C_ = s5_c.shape[1]
    flat = lambda t: t.reshape(-1, t.shape[-1])
    table = _na_bias_table(prm["na_rpb"], L // GRID_W)
    yc_l = na_attention(flat(q_l), flat(k_l), flat(v_l), flat(k_c), flat(v_c), table, L, C_, True)
    yc_c = (na_attention(flat(q_c), flat(k_c), flat(v_c), flat(k_c), flat(v_c), table, C_, C_, False)
            if with_ctx_out else None)
    yd_l, yd_c = ssm_mixer_pallas(flat(z_l), flat(xbc_l), flat(dt_l), flat(z_c), flat(xbc_c), flat(dt_c),
                                  prm["ssm_conv_w"], prm["ssm_conv_b"], prm["ssm_dt_bias"], prm["ssm_a_log"],
                                  prm["ssm_d"], prm["ssm_norm_w"], L, C_, with_ctx_out)
    return (ya_l, yb_l, yc_l, yd_l), (ya_c, yb_c, yc_c, yd_c)


def kernel(x, c, ctx, c_ctx, w_ada, b_ada, w_in, w_out, s5_a_re, s5_a_im, s5_log_dt, s5_b_re, s5_b_im, s5_c_re, s5_c_im, s5_d, s5_w_glu, s5_b_glu, pool_w, pool_scale, na_rpb, ssm_conv_w, ssm_conv_b, ssm_dt_bias, ssm_a_log, ssm_d, ssm_norm_w, ln1_g, ln1_b, ln2_g, ln2_b, peer_w_q, peer_sub_keys, peer_u, peer_v):
    B_, L, D = x.shape
    C_ = ctx.shape[1]
    x_lat = x.reshape(B_ * L, D)
    x_ctx = ctx.reshape(B_ * C_, D)
    act_lat = jax.nn.silu(c)
    act_ctx = jax.nn.silu(c_ctx)
    dt_pad = V7X_LANES - IN_SPLITS[-1]
    splits = IN_SPLITS[:-1] + (V7X_LANES,)
    in_dtypes = (F32, F32, BF16, BF16, BF16, F32, F32, F32)
    TM = 256
    for l in range(DEPTH):
        last = l == DEPTH - 1
        prm = dict(s5_a_re=s5_a_re[l], s5_a_im=s5_a_im[l], s5_log_dt=s5_log_dt[l], s5_b_re=s5_b_re[l],
                   s5_b_im=s5_b_im[l], s5_c_re=s5_c_re[l], s5_c_im=s5_c_im[l], s5_d=s5_d[l],
                   s5_w_glu=s5_w_glu[l], s5_b_glu=s5_b_glu[l], pool_w=pool_w[l], pool_scale=pool_scale[l],
                   na_rpb=na_rpb[l], ssm_conv_w=ssm_conv_w[l], ssm_conv_b=ssm_conv_b[l],
                   ssm_dt_bias=ssm_dt_bias[l], ssm_a_log=ssm_a_log[l], ssm_d=ssm_d[l],
                   ssm_norm_w=ssm_norm_w[l])
        m_lat = (act_lat @ w_ada[l] + b_ada[l]).reshape(B_, 6, 1, D)
        m_ctx = (act_ctx @ w_ada[l] + b_ada[l]).reshape(1, 6, 1, D)
        w_in_b = jnp.pad(w_in[l], ((0, 0), (0, dt_pad))).astype(BF16)
        w_out_b = w_out[l].astype(BF16)

        p_l = modulated_matmul(x_lat, m_lat[:, 0], m_lat[:, 1], w_in_b, splits, in_dtypes, L, TM)
        p_c = modulated_matmul(x_ctx, m_ctx[:, 0], m_ctx[:, 1], w_in_b, splits, in_dtypes, B_ * C_, TM)
        p_l = [a.reshape(B_, L, -1) for a in p_l]
        p_c = [a.reshape(B_, C_, -1) for a in p_c]
        y_l, y_c = _mixers(p_l, p_c, prm, not last)

        x_lat = proj_residual_ln([a.reshape(B_ * L, -1) for a in y_l], x_lat, m_lat[:, 2], w_out_b,
                                 ln1_g[l], ln1_b[l], L, TM)
        wq_b = peer_w_q[l].astype(BF16)
        keys_b = peer_sub_keys[l].reshape(2 * PEER_HEADS, PEER_NK, PEER_HALF).astype(BF16)
        u_b = peer_u[l].astype(BF16)
        vt_b = peer_v_tiles(peer_v[l])
        f_lat = peer_ffn_pallas(x_lat, m_lat[:, 3], m_lat[:, 4], wq_b, keys_b, u_b, vt_b, L)
        x_lat = residual_ln(x_lat, f_lat, m_lat[:, 5], ln2_g[l], ln2_b[l], L, TM)
        if not last:
            x_ctx = proj_residual_ln([a.reshape(B_ * C_, -1) for a in y_c], x_ctx, m_ctx[:, 2], w_out_b,
                                     ln1_g[l], ln1_b[l], B_ * C_, TM)
            f_ctx = peer_ffn_pallas(x_ctx, m_ctx[:, 3], m_ctx[:, 4], wq_b, keys_b, u_b, vt_b, B_ * C_)
            x_ctx = residual_ln(x_ctx, f_ctx, m_ctx[:, 5], ln2_g[l], ln2_b[l], B_ * C_, TM)
    return x_lat.reshape(B_, L, D)
```

```python
import functools
import math

import jax
import jax.numpy as jnp
import numpy as np
from jax import lax
from jax.experimental import pallas as pl
from jax.experimental.pallas import tpu as pltpu

D_MODEL = 2048
BATCH = 4
SEQ = 4096
DEPTH = 2

GRID_W = 64
CTX_LEN = 256
N_MIXERS = 4
D_MIX = D_MODEL
W_GRP = D_MIX // N_MIXERS

S5_P = 16
S5_G = W_GRP // S5_P
S5_N = 64

POOL_WINDOWS = (2, 4, 8, 16)
POOL_C = W_GRP // len(POOL_WINDOWS)

NA_HD = 64
NA_H = W_GRP // NA_HD
NA_WIN_R = 8
NA_WIN_C = 16

SSM_HD = 64
SSM_H = W_GRP // SSM_HD
SSM_G = 2
SSM_N = 128
SSM_CONV = 4
SSM_CHUNK = 128
SSM_CONV_CH = W_GRP + 2 * SSM_G * SSM_N

PEER_HEADS = 8
PEER_NK = 128
PEER_NE = PEER_NK * PEER_NK
PEER_QDIM = 256
PEER_TOPK = 16
PEER_BLOCK = 128

IN_SPLITS = (W_GRP, W_GRP, W_GRP, W_GRP, W_GRP, W_GRP, SSM_CONV_CH, 2 * SSM_H)
D_IN = sum(IN_SPLITS)

DEEPNORM_ALPHA = (2 * DEPTH) ** 0.25
DEEPNORM_BETA = (8 * DEPTH) ** -0.25
LN_EPS = 1e-5
F32 = jnp.float32
BF16 = jnp.bfloat16

V7X_LANES = 128
V7X_VMEM_LIMIT_BYTES = 56 * 1024 * 1024


def _flip(t, rev):
    return t[:, ::-1] if rev else t


def _modmm_body(x_ref, shift_ref, scale_ref, w_ref, *out_refs, col_splits):
    xm = (x_ref[...] * (1.0 + scale_ref[0]) + shift_ref[0]).astype(BF16)
    o = 0
    for ref, n in zip(out_refs, col_splits):
        ref[...] = jnp.dot(xm, w_ref[:, o:o + n], preferred_element_type=F32).astype(ref.dtype)
        o += n


def modulated_matmul(x, shift, scale, w_bf16, col_splits, out_dtypes, rows_per_mod, tm):
    T, K = x.shape
    assert T % tm == 0 and rows_per_mod % tm == 0
    tiles_per_mod = rows_per_mod // tm
    n_tot = sum(col_splits)
    assert w_bf16.shape == (K, n_tot)
    mod_spec = pl.BlockSpec((1, 1, K), lambda i: (i // tiles_per_mod, 0, 0))
    return pl.pallas_call(
        functools.partial(_modmm_body, col_splits=tuple(col_splits)),
        out_shape=[jax.ShapeDtypeStruct((T, n), dt) for n, dt in zip(col_splits, out_dtypes, strict=True)],
        grid=(T // tm,),
        in_specs=[pl.BlockSpec((tm, K), lambda i: (i, 0)), mod_spec, mod_spec,
                  pl.BlockSpec((K, n_tot), lambda i: (0, 0))],
        out_specs=[pl.BlockSpec((tm, n), lambda i: (i, 0)) for n in col_splits],
        compiler_params=pltpu.CompilerParams(dimension_semantics=("arbitrary",),
                                             vmem_limit_bytes=V7X_VMEM_LIMIT_BYTES),
        name="modulated_matmul",
    )(x, shift, scale, w_bf16)


def _proj_ln_body(*refs, n_parts, alpha):
    part_refs = refs[:n_parts]
    x_ref, gate_ref, w_ref, g_ref, b_ref, o_ref = refs[n_parts:]
    acc = None
    o = 0
    for pr in part_refs:
        n = pr.shape[-1]
        d = jnp.dot(pr[...].astype(BF16), w_ref[o:o + n, :], preferred_element_type=F32)
        acc = d if acc is None else acc + d
        o += n
    h = alpha * x_ref[...] + gate_ref[0] * acc
    mu = jnp.mean(h, -1, keepdims=True)
    hc = h - mu
    var = jnp.mean(hc * hc, -1, keepdims=True)
    o_ref[...] = hc * lax.rsqrt(var + LN_EPS) * g_ref[...] + b_ref[...]


def proj_residual_ln(parts, x, gate, w_bf16, g, b, rows_per_mod, tm):
    T, D = x.shape
    assert T % tm == 0 and rows_per_mod % tm == 0
    tiles_per_mod = rows_per_mod // tm
    k_tot = sum(p.shape[-1] for p in parts)
    assert w_bf16.shape == (k_tot, D)
    row = lambda n: pl.BlockSpec((tm, n), lambda i: (i, 0))
    vec = pl.BlockSpec((1, D), lambda i: (0, 0))
    return pl.pallas_call(
        functools.partial(_proj_ln_body, n_parts=len(parts), alpha=DEEPNORM_ALPHA),
        out_shape=jax.ShapeDtypeStruct((T, D), F32),
        grid=(T // tm,),
        in_specs=[row(p.shape[-1]) for p in parts] + [
            row(D), pl.BlockSpec((1, 1, D), lambda i: (i // tiles_per_mod, 0, 0)),
            pl.BlockSpec((k_tot, D), lambda i: (0, 0)), vec, vec],
        out_specs=row(D),
        compiler_params=pltpu.CompilerParams(dimension_semantics=("arbitrary",),
                                             vmem_limit_bytes=V7X_VMEM_LIMIT_BYTES),
        name="proj_residual_ln",
    )(*parts, x, gate, w_bf16, g.reshape(1, D), b.reshape(1, D))


def _res_ln_body(x_ref, f_ref, gate_ref, g_ref, b_ref, o_ref, *, alpha):
    h = alpha * x_ref[...] + gate_ref[0] * f_ref[...]
    mu = jnp.mean(h, -1, keepdims=True)
    hc = h - mu
    var = jnp.mean(hc * hc, -1, keepdims=True)
    o_ref[...] = hc * lax.rsqrt(var + LN_EPS) * g_ref[...] + b_ref[...]


def residual_ln(x, f, gate, g, b, rows_per_mod, tm):
    T, D = x.shape
    assert T % tm == 0 and rows_per_mod % tm == 0
    tiles_per_mod = rows_per_mod // tm
    row = pl.BlockSpec((tm, D), lambda i: (i, 0))
    vec = pl.BlockSpec((1, D), lambda i: (0, 0))
    return pl.pallas_call(
        functools.partial(_res_ln_body, alpha=DEEPNORM_ALPHA),
        out_shape=jax.ShapeDtypeStruct((T, D), F32),
        grid=(T // tm,),
        in_specs=[row, row, pl.BlockSpec((1, 1, D), lambda i: (i // tiles_per_mod, 0, 0)), vec, vec],
        out_specs=row,
        compiler_params=pltpu.CompilerParams(dimension_semantics=("arbitrary",),
                                             vmem_limit_bytes=V7X_VMEM_LIMIT_BYTES),
        name="residual_ln",
    )(x, f, gate, g.reshape(1, D), b.reshape(1, D))


S5_CHAINS = 2 * BATCH
S5_STATE = S5_G * S5_N
S5_SCAN_COLS = 512
S5_SB_IN = S5_SCAN_COLS // S5_N * S5_P


def _s5_scan_body(u_ref, wb_ref, wc_ref, are_ref, aim_ref, y_ref, bu_ref, h_ref, *, steps):
    rows = steps * S5_CHAINS

    @pl.when(pl.program_id(0) == 0)
    def _():
        h_ref[...] = jnp.zeros_like(h_ref)

    u = u_ref[...]
    chain = lax.broadcasted_iota(jnp.int32, u.shape, 0) % S5_CHAINS
    fwd = chain < BATCH
    zero = jnp.zeros_like(u)
    uf = jnp.where(fwd, u, zero)
    ub = jnp.where(fwd, zero, u)
    is_fwd_y = lax.broadcasted_iota(jnp.int32, (rows, S5_SB_IN), 0) % S5_CHAINS < BATCH

    for cb in range(S5_STATE // S5_SCAN_COLS):
        cin = slice(cb * S5_SB_IN, (cb + 1) * S5_SB_IN)
        re = pl.ds(cb * S5_SCAN_COLS, S5_SCAN_COLS)
        im = pl.ds(S5_STATE + cb * S5_SCAN_COLS, S5_SCAN_COLS)
        bu = jnp.dot(jnp.concatenate([uf[:, cin], ub[:, cin]], axis=1), wb_ref[cb], preferred_element_type=F32)
        bu_ref[:, re] = bu[:, :S5_SCAN_COLS]
        bu_ref[:, im] = bu[:, S5_SCAN_COLS:]
        a_re = are_ref[:, re]
        a_im = aim_ref[:, re]

        def step(s, carry):
            h_re, h_im = carry
            r = pl.ds(pl.multiple_of(s * S5_CHAINS, S5_CHAINS), S5_CHAINS)
            n_re = a_re * h_re - a_im * h_im + bu_ref[r, re]
            n_im = a_re * h_im + a_im * h_re + bu_ref[r, im]
            bu_ref[r, re] = n_re
            bu_ref[r, im] = n_im
            return n_re, n_im

        h_re, h_im = lax.fori_loop(0, steps, step, (h_ref[:, re], h_ref[:, im]), unroll=4)
        h_ref[:, re] = h_re
        h_ref[:, im] = h_im
        hb = jnp.concatenate([bu_ref[:, re], bu_ref[:, im]], axis=1).astype(BF16)
        y2 = jnp.dot(hb, wc_ref[cb], preferred_element_type=F32)
        y_ref[cb] = jnp.where(is_fwd_y, y2[:, :S5_SB_IN], y2[:, S5_SB_IN:])


def s5_scan(u8, wb, wc, a_re8, a_im8, steps):
    n_rows = u8.shape[0]
    rows = steps * S5_CHAINS
    assert n_rows % rows == 0
    full = lambda a: pl.BlockSpec(a.shape, lambda i: (0,) * a.ndim)
    return pl.pallas_call(
        functools.partial(_s5_scan_body, steps=steps),
        out_shape=jax.ShapeDtypeStruct((W_GRP // S5_SB_IN, n_rows, S5_SB_IN), F32),
        grid=(n_rows // rows,),
        in_specs=[pl.BlockSpec((rows, W_GRP), lambda i: (i, 0)), full(wb), full(wc), full(a_re8), full(a_im8)],
        out_specs=pl.BlockSpec((W_GRP // S5_SB_IN, rows, S5_SB_IN), lambda i: (0, i, 0)),
        scratch_shapes=[pltpu.VMEM((rows, 2 * S5_STATE), F32), pltpu.VMEM((S5_CHAINS, 2 * S5_STATE), F32)],
        compiler_params=pltpu.CompilerParams(dimension_semantics=("arbitrary",),
                                             vmem_limit_bytes=V7X_VMEM_LIMIT_BYTES),
        name="s5_scan",
    )(u8, wb, wc, a_re8, a_im8)


def _s5_glu_body(u_ref, yf_ref, yb_ref, d_ref, w_ref, b_ref, o_ref):
    tm = u_ref.shape[1]
    r = lax.broadcasted_iota(jnp.int32, (tm, tm), 0)
    c = lax.broadcasted_iota(jnp.int32, (tm, tm), 1)
    flip = jnp.where(r + c == tm - 1, 1.0, 0.0).astype(BF16)
    for b in range(BATCH):
        n_cb = yf_ref.shape[0]
        yf = jnp.concatenate([yf_ref[cb, pl.ds(b, tm, stride=S5_CHAINS), :] for cb in range(n_cb)], axis=1)
        yb = jnp.concatenate([yb_ref[cb, pl.ds(BATCH + b, tm, stride=S5_CHAINS), :] for cb in range(n_cb)],
                             axis=1)
        hi = yb.astype(BF16)
        lo = (yb - hi.astype(F32)).astype(BF16)
        yb = jnp.dot(flip, hi, preferred_element_type=F32) + jnp.dot(flip, lo, preferred_element_type=F32)
        y = d_ref[...] * u_ref[b] + yf + yb
        g = jax.nn.gelu(y)
        z = jnp.dot(g.astype(BF16), w_ref[...], preferred_element_type=F32) + b_ref[...]
        o_ref[b] = g * jax.nn.sigmoid(z)


def s5_glu_pallas(u, y8, first_step, d, w_bf16, b, tm):
    B_, n, W = u.shape
    assert B_ == BATCH and n % tm == 0 and first_step % tm == 0
    nt = n // tm
    off = first_step // tm
    row = pl.BlockSpec((B_, tm, W), lambda i: (0, i, 0))
    vec = pl.BlockSpec((1, W), lambda i: (0, 0))
    return pl.pallas_call(
        _s5_glu_body,
        out_shape=jax.ShapeDtypeStruct((B_, n, W), F32),
        grid=(nt,),
        in_specs=[row, pl.BlockSpec((y8.shape[0], tm * S5_CHAINS, y8.shape[2]), lambda i: (0, off + i, 0)),
                  pl.BlockSpec((y8.shape[0], tm * S5_CHAINS, y8.shape[2]), lambda i: (0, off + nt - 1 - i, 0)),
                  vec, pl.BlockSpec((W, W), lambda i: (0, 0)), vec],
        out_specs=row,
        compiler_params=pltpu.CompilerParams(dimension_semantics=("arbitrary",),
                                             vmem_limit_bytes=V7X_VMEM_LIMIT_BYTES),
        name="s5_glu",
    )(u, y8, y8, d.reshape(1, W), w_bf16, b.reshape(1, W))


def _s5_weights(a_re, a_im, log_dt, b_re, b_im, c_re, c_im):
    eye = jnp.eye(S5_G, dtype=F32)
    wb, a8 = [], []
    wc = []
    for di in range(2):
        ab_re, ab_im, bb_re, bb_im = s5_discretize(a_re[di], a_im[di], log_dt[di], b_re[di], b_im[di])
        blk = lambda m: jnp.einsum('gnp,gh->gphn', m, eye).reshape(W_GRP, S5_STATE)
        wb.append(jnp.concatenate([blk(bb_re), blk(bb_im)], axis=1))
        a8.append((jnp.broadcast_to(ab_re.reshape(1, S5_STATE), (BATCH, S5_STATE)),
                   jnp.broadcast_to(ab_im.reshape(1, S5_STATE), (BATCH, S5_STATE))))
        cblk = lambda m: jnp.einsum('gpn,gh->gnhp', m.astype(F32), eye).reshape(S5_STATE, W_GRP)
        wc.append(jnp.concatenate([cblk(c_re[di]), -cblk(c_im[di])], axis=0))
    wb_sb, wc_sb = [], []
    for sb in range(S5_STATE // S5_SCAN_COLS):
        cin = slice(sb * S5_SB_IN, (sb + 1) * S5_SB_IN)
        re = slice(sb * S5_SCAN_COLS, (sb + 1) * S5_SCAN_COLS)
        im = slice(S5_STATE + sb * S5_SCAN_COLS, S5_STATE + (sb + 1) * S5_SCAN_COLS)
        wb_sb.append(jnp.concatenate([jnp.concatenate([w[cin, re], w[cin, im]], axis=1) for w in wb], axis=0))
        wc_sb.append(jnp.concatenate([jnp.concatenate([w[re, cin], w[im, cin]], axis=0) for w in wc], axis=1))
    wb_sb = jnp.stack(wb_sb).astype(BF16)
    wc_sb = jnp.stack(wc_sb).astype(BF16)
    a_re8 = jnp.concatenate([a8[0][0], a8[1][0]], axis=0)
    a_im8 = jnp.concatenate([a8[0][1], a8[1][1]], axis=0)
    return wb_sb, wc_sb, a_re8, a_im8


def s5_mixer_pallas(u_lat, u_ctx, a_re, a_im, log_dt, b_re, b_im, c_re, c_im, d, w_glu, b_glu,
                    with_ctx_out, steps=64, tm=512):
    B_, L, W = u_lat.shape
    C_ = u_ctx.shape[1]
    assert B_ == BATCH and W == W_GRP
    wb, wc, a_re8, a_im8 = _s5_weights(a_re, a_im, log_dt, b_re, b_im, c_re, c_im)
    seq_f = jnp.concatenate([u_ctx, u_lat], axis=1)
    seq_b = jnp.concatenate([u_ctx[:, ::-1], u_lat[:, ::-1]], axis=1)
    u8 = jnp.concatenate([seq_f, seq_b], axis=0).transpose(1, 0, 2).reshape((C_ + L) * S5_CHAINS, W)
    y8 = s5_scan(u8.astype(BF16), wb, wc, a_re8, a_im8, steps)
    w_glu_b = w_glu.astype(BF16)
    tm = min(tm, C_)
    out_lat = s5_glu_pallas(u_lat, y8, C_, d, w_glu_b, b_glu, tm).reshape(B_ * L, W)
    out_ctx = s5_glu_pallas(u_ctx, y8, 0, d, w_glu_b, b_glu, tm).reshape(B_ * C_, W) if with_ctx_out else None
    return out_lat, out_ctx


PEER_HALF = PEER_QDIM // 2


def _argmax_rows(v, r):
    while v.shape[0] > 1:
        half = v.shape[0] // 2
        take_hi = v[half:] > v[:half]
        r = jnp.where(take_hi, r[half:], r[:half])
        v = jnp.maximum(v[:half], v[half:])
    return v, r


def _topk_rows(x, k):
    row = lax.broadcasted_iota(jnp.int32, x.shape, 0).astype(F32)
    krow = lax.broadcasted_iota(jnp.int32, (k, x.shape[1]), 0)

    def body(it, carry):
        x, rank, vals = carry
        m, first = _argmax_rows(x, row)
        hit = row == first
        return (jnp.where(hit, -jnp.inf, x), jnp.where(hit, it.astype(F32), rank), jnp.where(krow == it, m, vals))

    init = (x, jnp.full(x.shape, float(k), F32), jnp.zeros((k, x.shape[1]), F32))
    _, rank, vals = lax.fori_loop(0, k, body, init)
    return vals, rank


PEER_GRID_COLS = tuple(PEER_TOPK // (i + 1) for i in range(PEER_TOPK))
PEER_GRID_ROWS = 64


def _peer_route_body(x_ref, shift_ref, scale_ref, wq_ref, keys_ref, h_ref, n1_ref, c1_ref, r2_ref, e2_ref,
                     q_ref, *, lane_tiles):
    hm = (x_ref[...] * (1.0 + scale_ref[0]) + shift_ref[0]).astype(BF16)
    h_ref[...] = hm
    q_ref[...] = jnp.dot(hm, wq_ref[...], preferred_element_type=F32).astype(BF16)
    K = PEER_TOPK
    for hd in range(PEER_HEADS):
        for lt in range(lane_tiles):
            tok = pl.ds(lt * V7X_LANES, V7X_LANES)
            sc = []
            for side in range(2):
                col = (2 * hd + side) * PEER_HALF
                qs = q_ref[tok, col:col + PEER_HALF]
                sc.append(lax.dot_general(keys_ref[2 * hd + side], qs, (((1,), (1,)), ((), ())),
                                          preferred_element_type=F32))
            v1, r1 = _topk_rows(sc[0], K)
            v2, r2 = _topk_rows(sc[1], K)
            cells = [v1[i:i + 1] + v2[:PEER_GRID_COLS[i]] for i in range(K)]
            cells.append(jnp.full((PEER_GRID_ROWS - sum(PEER_GRID_COLS), V7X_LANES), -jnp.inf, F32))
            vc, rc = _topk_rows(jnp.concatenate(cells, axis=0), K)
            z = jnp.sum(jnp.exp(vc - vc[0:1]), axis=0, keepdims=True)
            chosen = jnp.where(rc < float(K), 1.0, 0.0)
            n1 = jnp.zeros_like(r1)
            off = 0
            for i in range(K):
                n_i = jnp.sum(chosen[off:off + PEER_GRID_COLS[i]], axis=0, keepdims=True)
                n1 = jnp.where(r1 == float(i), n_i, n1)
                off += PEER_GRID_COLS[i]
            n1_ref[hd, :, tok] = n1
            c1_ref[hd, :, tok] = jnp.exp(sc[0] - v1[0:1]) / z
            r2_ref[hd, :, tok] = r2.astype(BF16)
            e2_ref[hd, :, tok] = jnp.exp(sc[1] - v2[0:1]).astype(BF16)


def peer_route(x, shift, scale, wq_bf16, keys_bf16, rows_per_mod, tm):
    T, D = x.shape
    assert T % tm == 0 and rows_per_mod % tm == 0 and tm % V7X_LANES == 0
    tiles_per_mod = rows_per_mod // tm
    mod_spec = pl.BlockSpec((1, 1, D), lambda i: (i // tiles_per_mod, 0, 0))
    tab = lambda dt: jax.ShapeDtypeStruct((PEER_HEADS, PEER_NK, T), dt)
    tab_spec = pl.BlockSpec((PEER_HEADS, PEER_NK, tm), lambda i: (0, 0, i))
    return pl.pallas_call(
        functools.partial(_peer_route_body, lane_tiles=tm // V7X_LANES),
        out_shape=[jax.ShapeDtypeStruct((T, D), BF16), tab(F32), tab(F32), tab(BF16), tab(BF16)],
        grid=(T // tm,),
        in_specs=[pl.BlockSpec((tm, D), lambda i: (i, 0)), mod_spec, mod_spec,
                  pl.BlockSpec(wq_bf16.shape, lambda i: (0, 0)),
                  pl.BlockSpec(keys_bf16.shape, lambda i: (0, 0, 0))],
        out_specs=[pl.BlockSpec((tm, D), lambda i: (i, 0)), tab_spec, tab_spec, tab_spec, tab_spec],
        scratch_shapes=[pltpu.VMEM((tm, PEER_HEADS * PEER_QDIM), BF16)],
        compiler_params=pltpu.CompilerParams(dimension_semantics=("arbitrary",),
                                             vmem_limit_bytes=V7X_VMEM_LIMIT_BYTES),
        name="peer_route",
    )(x, shift, scale, wq_bf16, keys_bf16)


def _peer_dense_body(h_ref, u_ref, vt_ref, n1_ref, c1_ref, r2_ref, e2_ref, x_ref, gate_ref, g_ref, b_ref,
                     o_ref, acc_ref, a_ref, *, n_slab):
    j = pl.program_id(1)

    @pl.when(j == 0)
    def _():
        acc_ref[...] = jnp.zeros_like(acc_ref)

    pair = 2 * PEER_NK
    for p in range(n_slab // 2):
        s = lax.dot_general(u_ref[p * pair:(p + 1) * pair, :], h_ref[...], (((1,), (1,)), ((), ())),
                            preferred_element_type=F32)
        for kk in range(2):
            k = 2 * p + kk
            for lt in range(h_ref.shape[0] // V7X_LANES):
                lanes = slice(lt * V7X_LANES, (lt + 1) * V7X_LANES)
                g = None
                for hd in range(PEER_HEADS):
                    n1row = n1_ref[hd, k:k + 1, lanes].astype(BF16)
                    c1row = c1_ref[hd, k:k + 1, lanes].astype(BF16)
                    sel = lax.clamp(jnp.zeros((), BF16), n1row - r2_ref[hd, :, lanes], jnp.ones((), BF16))
                    gh = sel * e2_ref[hd, :, lanes] * c1row
                    g = gh if g is None else g + gh
                sk = s[kk * PEER_NK:(kk + 1) * PEER_NK, lanes]
                a_ref[k * PEER_NK:(k + 1) * PEER_NK, lanes] = jax.nn.gelu(sk).astype(BF16) * g
    acc_ref[...] += jnp.dot(vt_ref[0], a_ref[...], preferred_element_type=F32)

    @pl.when(j == pl.num_programs(1) - 1)
    def _():
        hres = DEEPNORM_ALPHA * x_ref[...] + gate_ref[0] * acc_ref[...].T
        mu = jnp.mean(hres, -1, keepdims=True)
        hc = hres - mu
        var = jnp.mean(hc * hc, -1, keepdims=True)
        o_ref[...] = hc * lax.rsqrt(var + LN_EPS) * g_ref[...] + b_ref[...]


def peer_dense(h_bf16, u_bf16, vt_tiles, n1, c1, r2, e2, x, gate, ln_g, ln_b, rows_per_mod, tm, n_slab):
    T, D = h_bf16.shape
    NE = u_bf16.shape[0]
    e_tile = n_slab * PEER_NK
    n_e = NE // e_tile
    assert T % tm == 0 and NE % e_tile == 0 and vt_tiles.shape == (n_e, D, e_tile)
    slab_spec = pl.BlockSpec((PEER_HEADS, n_slab, tm), lambda i, j: (0, j, i))
    tok_spec = pl.BlockSpec((PEER_HEADS, PEER_NK, tm), lambda i, j: (0, 0, i))
    return pl.pallas_call(
        functools.partial(_peer_dense_body, n_slab=n_slab),
        out_shape=jax.ShapeDtypeStruct((T, D), F32),
        grid=(T // tm, n_e),
        in_specs=[pl.BlockSpec((tm, D), lambda i, j: (i, 0)),
                  pl.BlockSpec((e_tile, D), lambda i, j: (j, 0)),
                  pl.BlockSpec((1, D, e_tile), lambda i, j: (j, 0, 0)),
                  slab_spec, slab_spec, tok_spec, tok_spec,
                  pl.BlockSpec((tm, D), lambda i, j: (i, 0)),
                  pl.BlockSpec((1, 1, D), lambda i, j: (i // (rows_per_mod // tm), 0, 0)),
                  pl.BlockSpec((1, D), lambda i, j: (0, 0)), pl.BlockSpec((1, D), lambda i, j: (0, 0))],
        out_specs=pl.BlockSpec((tm, D), lambda i, j: (i, 0)),
        scratch_shapes=[pltpu.VMEM((D, tm), F32), pltpu.VMEM((e_tile, tm), BF16)],
        compiler_params=pltpu.CompilerParams(dimension_semantics=("arbitrary", "arbitrary"),
                                             vmem_limit_bytes=V7X_VMEM_LIMIT_BYTES),
        name="peer_dense",
    )(h_bf16, u_bf16, vt_tiles, n1, c1, r2, e2, x, gate, ln_g.reshape(1, D), ln_b.reshape(1, D))


PEER_SLABS = 8


def peer_v_tiles(v_tab):
    e_tile = PEER_SLABS * PEER_NK
    return v_tab.astype(BF16).reshape(v_tab.shape[0] // e_tile, e_tile, v_tab.shape[1]).transpose(0, 2, 1)


def peer_sublayer(x, shift, scale, gate, wq_bf16, keys_bf16, u_bf16, vt_tiles, ln_g, ln_b, rows_per_mod,
                  tm_route=256, tm_dense=512):
    h, n1, c1, r2, e2 = peer_route(x, shift, scale, wq_bf16, keys_bf16, rows_per_mod, tm_route)
    return peer_dense(h, u_bf16, vt_tiles, n1, c1, r2, e2, x, gate, ln_g, ln_b, rows_per_mod, tm_dense,
                      PEER_SLABS)


POOL_HALO = 8
POOL_ROWS = 256


def _pool_body(u_ref, w_ref, scale_ref, o_ref, pad_ref, *, seq_len):
    L = seq_len
    chunk = min(POOL_ROWS, L)
    zeros = jnp.zeros((POOL_HALO, W_GRP), F32)
    pad_ref[0:POOL_HALO, :] = zeros
    pad_ref[POOL_HALO + L:2 * POOL_HALO + L, :] = zeros
    pad_ref[POOL_HALO:POOL_HALO + L, :] = u_ref[0]
    for r0 in range(0, L, chunk):
        t = r0 + lax.broadcasted_iota(jnp.int32, (chunk, POOL_C), 0)
        for j, w in enumerate(POOL_WINDOWS):
            cols = slice(j * POOL_C, (j + 1) * POOL_C)
            acc = None
            for o in range(-(w // 2), w - w // 2):
                s = pad_ref[POOL_HALO + r0 + o:POOL_HALO + r0 + o + chunk, cols]
                acc = s if acc is None else acc + s
            lo = jnp.maximum(t - w // 2, 0)
            hi = jnp.minimum(t - w // 2 + w - 1, L - 1)
            cnt = (hi - lo + 1).astype(F32)
            pooled = acc / cnt - u_ref[0, r0:r0 + chunk, cols]
            y = jnp.dot(pooled.astype(BF16), w_ref[j], preferred_element_type=F32)
            o_ref[0, r0:r0 + chunk, cols] = y * scale_ref[:, cols]


def pool_mix_pallas(u, pool_w, pool_scale):
    B_, L, W = u.shape
    assert max(POOL_WINDOWS) // 2 <= POOL_HALO and L % min(POOL_ROWS, L) == 0
    blk = pl.BlockSpec((1, L, W), lambda b: (b, 0, 0))
    return pl.pallas_call(
        functools.partial(_pool_body, seq_len=L),
        out_shape=jax.ShapeDtypeStruct((B_, L, W), F32),
        grid=(B_,),
        in_specs=[blk, pl.BlockSpec(pool_w.shape, lambda b: (0, 0, 0)), pl.BlockSpec((1, W), lambda b: (0, 0))],
        out_specs=blk,
        scratch_shapes=[pltpu.VMEM((L + 2 * POOL_HALO, W), F32)],
        compiler_params=pltpu.CompilerParams(dimension_semantics=("arbitrary",),
                                             vmem_limit_bytes=V7X_VMEM_LIMIT_BYTES),
        name="pool_mix",
    )(u, pool_w.astype(BF16), pool_scale.reshape(1, W))


NA_TILE_R = 4
NA_TILE = NA_TILE_R * GRID_W
NA_SCALE = NA_HD ** -0.5


def _na_bias_table(rpb, n_rows):
    n_tiles = n_rows // NA_TILE_R
    KR = min(NA_WIN_R, n_rows)
    a = np.array([0, min(2, n_tiles - 1), n_tiles - 1]).reshape(3, 1, 1, 1, 1, 1)
    d = np.arange(3).reshape(1, 3, 1, 1, 1, 1)
    i = np.arange(NA_TILE_R).reshape(1, 1, NA_TILE_R, 1, 1, 1)
    qc = np.arange(GRID_W).reshape(1, 1, 1, GRID_W, 1, 1)
    j = np.arange(NA_TILE_R).reshape(1, 1, 1, 1, NA_TILE_R, 1)
    kc = np.arange(GRID_W).reshape(1, 1, 1, 1, 1, GRID_W)
    qr = NA_TILE_R * a + i
    kr = NA_TILE_R * (a + d - 1) + j
    rs = np.clip(qr - KR // 2, 0, n_rows - KR)
    c0 = np.clip(qc - NA_WIN_C // 2, 0, GRID_W - NA_WIN_C)
    ok = (kr >= rs) & (kr < rs + KR) & (kr >= 0) & (kr < n_rows) & (kc >= c0) & (kc < c0 + NA_WIN_C)
    row_rel = np.clip(kr - qr + NA_WIN_R - 1, 0, 2 * NA_WIN_R - 2)[:, :, :, 0, :, 0]
    col_rel = (np.clip(kc - qc, -(NA_WIN_C - 1), NA_WIN_C - 1) + NA_WIN_C - 1)[0, 0, 0, :, 0, :]
    onehot = (col_rel[None] == np.arange(2 * NA_WIN_C - 1)[:, None, None]).astype(np.float32)
    bias_rc = jnp.einsum('hrc,cqk->hrqk', rpb.astype(F32), onehot, precision=lax.Precision.HIGHEST)
    tab = jnp.stack([bias_rc[:, int(r)] for r in row_rel.reshape(-1)], axis=1)
    tab = tab.reshape(NA_H, 3, 3, NA_TILE_R, NA_TILE_R, GRID_W, GRID_W).transpose(1, 0, 2, 3, 5, 4, 6)
    tab = jnp.where(ok[:, None], tab, -jnp.inf)
    return tab.reshape(3, NA_H, 3, NA_TILE, NA_TILE)


def _na_body(q_ref, k_ref, v_ref, kc_ref, vc_ref, t_ref, o_ref, *, n_tiles, with_grid):
    a = pl.program_id(1)
    nt = (((1,), (1,)), ((), ()))
    for h in range(NA_H):
        hs = slice(h * NA_HD, (h + 1) * NA_HD)
        qh = q_ref[:, hs]
        scores = [lax.dot_general(qh, kc_ref[:, hs], nt, preferred_element_type=F32) * NA_SCALE]
        vals = [vc_ref[:, hs]]
        if with_grid:
            for d in range(3):
                ti = jnp.clip(a + d - 1, 0, n_tiles - 1)
                rows = pl.ds(pl.multiple_of(ti * NA_TILE, NA_TILE), NA_TILE)
                s = lax.dot_general(qh, k_ref[rows, hs], nt, preferred_element_type=F32)
                scores.append(s * NA_SCALE + t_ref[0, h, d])
                vals.append(v_ref[rows, hs])
        m = scores[0].max(axis=-1, keepdims=True)
        for s in scores[1:]:
            m = jnp.maximum(m, s.max(axis=-1, keepdims=True))
        den = None
        acc = None
        for s, vv in zip(scores, vals):
            p = jnp.exp(s - m)
            l = p.sum(axis=-1, keepdims=True)
            o = jnp.dot(p.astype(BF16), vv, preferred_element_type=F32)
            den = l if den is None else den + l
            acc = o if acc is None else acc + o
        o_ref[:, hs] = acc / den


def na_attention(q, k, v, kc, vc, table, seq_len, ctx_len, with_grid):
    W = q.shape[1]
    n_b = kc.shape[0] // ctx_len
    lq = q.shape[0] // n_b
    assert lq % NA_TILE == 0
    q_tiles = lq // NA_TILE
    n_tiles = seq_len // NA_TILE
    pat = lambda b, a: (jnp.where(a == 0, 0, jnp.where(a == n_tiles - 1, 2, 1)), 0, 0, 0, 0)
    qo_spec = pl.BlockSpec((NA_TILE, W), lambda b, a: (b * q_tiles + a, 0))
    return pl.pallas_call(
        functools.partial(_na_body, n_tiles=n_tiles, with_grid=with_grid),
        out_shape=jax.ShapeDtypeStruct(q.shape, F32),
        grid=(n_b, q_tiles),
        in_specs=[qo_spec,
                  pl.BlockSpec((seq_len, W), lambda b, a: (b, 0)), pl.BlockSpec((seq_len, W), lambda b, a: (b, 0)),
                  pl.BlockSpec((ctx_len, W), lambda b, a: (b, 0)), pl.BlockSpec((ctx_len, W), lambda b, a: (b, 0)),
                  pl.BlockSpec((1,) + table.shape[1:], pat)],
        out_specs=qo_spec,
        compiler_params=pltpu.CompilerParams(dimension_semantics=("arbitrary", "arbitrary"),
                                             vmem_limit_bytes=V7X_VMEM_LIMIT_BYTES),
        name="na_attention",
    )(q, k, v, kc, vc, table)


SSM_HALO = 8
SSM_BC = SSM_G * SSM_N


def _ssm_conv_body(prev_ref, cur_ref, next_ref, w_ref, b_ref, o_ref, pad_ref, *, tiles_per_seq):
    pos = pl.program_id(0) % tiles_per_seq
    tm = cur_ref.shape[0]
    pad_ref[0:SSM_HALO, :] = jnp.where(pos == 0, 0.0, prev_ref[...])
    pad_ref[SSM_HALO:SSM_HALO + tm, :] = cur_ref[...]
    pad_ref[SSM_HALO + tm:2 * SSM_HALO + tm, :] = jnp.where(pos == tiles_per_seq - 1, 0.0, next_ref[...])
    lead = (SSM_CONV - 1) // 2
    y = b_ref[...]
    for k in range(SSM_CONV):
        y = y + w_ref[k:k + 1, :] * pad_ref[SSM_HALO - lead + k:SSM_HALO - lead + k + tm, :]
    o_ref[...] = jax.nn.silu(y)


def ssm_conv(xbc, conv_w, conv_b, seq_len, tm):
    T, CH = xbc.shape
    assert seq_len % tm == 0 and tm % SSM_HALO == 0
    hb = tm // SSM_HALO
    n_hb = T // SSM_HALO
    return pl.pallas_call(
        functools.partial(_ssm_conv_body, tiles_per_seq=seq_len // tm),
        out_shape=jax.ShapeDtypeStruct((T, CH), F32),
        grid=(T // tm,),
        in_specs=[pl.BlockSpec((SSM_HALO, CH), lambda i: (jnp.maximum(i * hb - 1, 0), 0)),
                  pl.BlockSpec((tm, CH), lambda i: (i, 0)),
                  pl.BlockSpec((SSM_HALO, CH), lambda i: (jnp.minimum((i + 1) * hb, n_hb - 1), 0)),
                  pl.BlockSpec((SSM_CONV, CH), lambda i: (0, 0)), pl.BlockSpec((1, CH), lambda i: (0, 0))],
        out_specs=pl.BlockSpec((tm, CH), lambda i: (i, 0)),
        scratch_shapes=[pltpu.VMEM((tm + 2 * SSM_HALO, CH), F32)],
        compiler_params=pltpu.CompilerParams(dimension_semantics=("arbitrary",),
                                             vmem_limit_bytes=V7X_VMEM_LIMIT_BYTES),
        name="ssm_conv",
    )(xbc, xbc, xbc, conv_w, conv_b.reshape(1, CH))


def _ssd_body(xbc_ref, dt_ref, bias_ref, a_ref, h0_ref, y_ref, hfin_ref, h_scr, *, di, rev):
    ci = pl.program_id(1)
    Q = xbc_ref.shape[0]

    @pl.when(ci == 0)
    def _():
        h_scr[...] = h0_ref[0]

    dt_all = jax.nn.softplus(dt_ref[...] + bias_ref[...])
    a_cs = dt_all * a_ref[...]
    row_id = lax.broadcasted_iota(jnp.int32, a_cs.shape, 0)
    sh = 1
    while sh < Q:
        if rev:
            a_cs = a_cs + jnp.where(row_id < Q - sh, pltpu.roll(a_cs, Q - sh, 0), 0.0)
        else:
            a_cs = a_cs + jnp.where(row_id >= sh, pltpu.roll(a_cs, sh, 0), 0.0)
        sh *= 2
    a_cs_t = a_cs.T
    a_tot = a_cs[0:1, :] if rev else a_cs[Q - 1:Q, :]
    l_id = lax.broadcasted_iota(jnp.int32, (Q, Q), 0)
    s_id = lax.broadcasted_iota(jnp.int32, (Q, Q), 1)
    causal = (l_id <= s_id) if rev else (l_id >= s_id)
    nt = (((1,), (1,)), ((), ()))
    tn = (((0,), (0,)), ((), ()))
    cb = []
    for g in range(SSM_G):
        bg = xbc_ref[:, W_GRP + g * SSM_N:W_GRP + (g + 1) * SSM_N].astype(BF16)
        cg = xbc_ref[:, W_GRP + SSM_BC + g * SSM_N:W_GRP + SSM_BC + (g + 1) * SSM_N].astype(BF16)
        cb.append((bg, cg, lax.dot_general(cg, bg, nt, preferred_element_type=F32)))
    for h in range(SSM_H):
        c = di * SSM_H + h
        bg, cg, cbg = cb[h // (SSM_H // SSM_G)]
        col = a_cs[:, c:c + 1]
        lm = jnp.exp(jnp.where(causal, col - a_cs_t[c:c + 1, :], -jnp.inf))
        xh = xbc_ref[:, h * SSM_HD:(h + 1) * SSM_HD] * dt_all[:, c:c + 1]
        hp = h_scr[h]
        yd = jnp.dot((cbg * lm).astype(BF16), xh.astype(BF16), preferred_element_type=F32)
        yo = lax.dot_general(cg, hp.astype(BF16), nt, preferred_element_type=F32) * jnp.exp(col)
        y_ref[:, h * SSM_HD:(h + 1) * SSM_HD] = yd + yo
        tot = a_tot[:, c:c + 1]
        xd = (xh * jnp.exp(tot - col)).astype(BF16)
        h_scr[h] = jnp.exp(tot) * hp + lax.dot_general(xd, bg, tn, preferred_element_type=F32)

    @pl.when(ci == pl.num_programs(1) - 1)
    def _():
        hfin_ref[0] = h_scr[...]


def ssd_scan(xbc_act, dt_raw, bias128, a128, h0, seq_len, di, rev):
    T = xbc_act.shape[0]
    n_b = T // seq_len
    Q = min(SSM_CHUNK, seq_len)
    nc = seq_len // Q
    chunk = (lambda b, i: (b * nc + nc - 1 - i, 0)) if rev else (lambda b, i: (b * nc + i, 0))
    vec = pl.BlockSpec((1, V7X_LANES), lambda b, i: (0, 0))
    st = pl.BlockSpec((1, SSM_H, SSM_HD, SSM_N), lambda b, i: (b, 0, 0, 0))
    return pl.pallas_call(
        functools.partial(_ssd_body, di=di, rev=rev),
        out_shape=[jax.ShapeDtypeStruct((T, W_GRP), F32), jax.ShapeDtypeStruct(h0.shape, F32)],
        grid=(n_b, nc),
        in_specs=[pl.BlockSpec((Q, SSM_CONV_CH), chunk), pl.BlockSpec((Q, V7X_LANES), chunk), vec, vec, st],
        out_specs=[pl.BlockSpec((Q, W_GRP), chunk), st],
        scratch_shapes=[pltpu.VMEM((SSM_H, SSM_HD, SSM_N), F32)],
        compiler_params=pltpu.CompilerParams(dimension_semantics=("arbitrary", "arbitrary"),
                                             vmem_limit_bytes=V7X_VMEM_LIMIT_BYTES),
        name="ssd_scan",
    )(xbc_act, dt_raw, bias128, a128, h0)


def _ssm_out_body(xbc_ref, yf_ref, yb_ref, z_ref, d_ref, nw_ref, o_ref):
    y = d_ref[...] * xbc_ref[:, :W_GRP] + yf_ref[...] + yb_ref[...]
    g = y * jax.nn.silu(z_ref[...])
    gw = W_GRP // SSM_G
    for k in range(SSM_G):
        gk = g[:, k * gw:(k + 1) * gw]
        r = lax.rsqrt(jnp.mean(gk * gk, -1, keepdims=True) + LN_EPS)
        o_ref[:, k * gw:(k + 1) * gw] = gk * r * nw_ref[:, k * gw:(k + 1) * gw]


def ssm_out(xbc_act, yf, yb, z, d512, norm_w, tm):
    T = z.shape[0]
    assert T % tm == 0
    row = pl.BlockSpec((tm, W_GRP), lambda i: (i, 0))
    vec = pl.BlockSpec((1, W_GRP), lambda i: (0, 0))
    return pl.pallas_call(
        _ssm_out_body,
        out_shape=jax.ShapeDtypeStruct((T, W_GRP), F32),
        grid=(T // tm,),
        in_specs=[pl.BlockSpec((tm, SSM_CONV_CH), lambda i: (i, 0)), row, row, row, vec, vec],
        out_specs=row,
        compiler_params=pltpu.CompilerParams(dimension_semantics=("arbitrary",),
                                             vmem_limit_bytes=V7X_VMEM_LIMIT_BYTES),
        name="ssm_out",
    )(xbc_act, yf, yb, z, d512, norm_w.reshape(1, W_GRP))


def ssm_mixer_pallas(z_l, xbc_l, dt_l, z_c, xbc_c, dt_c, conv_w, conv_b, dt_bias, a_log, d, norm_w,
                     seq_len, ctx_len, with_ctx_out, tm=256):
    n_b = z_l.shape[0] // seq_len
    pad = V7X_LANES - 2 * SSM_H
    bias128 = jnp.pad(dt_bias.astype(F32).reshape(1, 2 * SSM_H), ((0, 0), (0, pad)))
    a128 = jnp.pad(-jnp.exp(a_log.astype(F32)).reshape(1, 2 * SSM_H), ((0, 0), (0, pad)))
    d512 = jnp.repeat(d.astype(F32), SSM_HD).reshape(1, W_GRP)
    act_c = ssm_conv(xbc_c, conv_w, conv_b, ctx_len, min(tm, ctx_len))
    act_l = ssm_conv(xbc_l, conv_w, conv_b, seq_len, tm)
    zero = jnp.zeros((n_b, SSM_H, SSM_HD, SSM_N), F32)
    y_c, y_l = [], []
    for di in range(2):
        yc, hc = ssd_scan(act_c, dt_c, bias128, a128, zero, ctx_len, di, di == 1)
        yl, _ = ssd_scan(act_l, dt_l, bias128, a128, hc, seq_len, di, di == 1)
        y_c.append(yc)
        y_l.append(yl)
    out_l = ssm_out(act_l, y_l[0], y_l[1], z_l, d512, norm_w, tm)
    out_c = ssm_out(act_c, y_c[0], y_c[1], z_c, d512, norm_w, min(tm, z_c.shape[0])) if with_ctx_out else None
    return out_l, out_c


def s5_discretize(a_re, a_im, log_dt, b_re, b_im):
    a_re, a_im = a_re.astype(F32), a_im.astype(F32)
    b_re, b_im = b_re.astype(F32), b_im.astype(F32)
    dt = jnp.exp(log_dt.astype(F32))[:, None]
    mag = jnp.exp(a_re * dt)
    ab_re = mag * jnp.cos(a_im * dt)
    ab_im = mag * jnp.sin(a_im * dt)
    den = a_re * a_re + a_im * a_im
    f_re = ((ab_re - 1) * a_re + ab_im * a_im) / den
    f_im = (ab_im * a_re - (ab_re - 1) * a_im) / den
    bb_re = f_re[..., None] * b_re - f_im[..., None] * b_im
    bb_im = f_re[..., None] * b_im + f_im[..., None] * b_re
    return ab_re, ab_im, bb_re, bb_im


def _cscan_combine(e1, e2):
    a1r, a1i, b1r, b1i = e1
    a2r, a2i, b2r, b2i = e2
    return (a2r * a1r - a2i * a1i, a2r * a1i + a2i * a1r,
            a2r * b1r - a2i * b1i + b2r, a2r * b1i + a2i * b1r + b2i)


def s5_states(u, ab_re, ab_im, bb_re, bb_im, h0_re, h0_im):
    bu_re = jnp.einsum('gnp,blgp->blgn', bb_re, u)
    bu_im = jnp.einsum('gnp,blgp->blgn', bb_im, u)
    bu_re = bu_re.at[:, 0].add(ab_re * h0_re - ab_im * h0_im)
    bu_im = bu_im.at[:, 0].add(ab_re * h0_im + ab_im * h0_re)
    a_re = jnp.broadcast_to(ab_re, bu_re.shape)
    a_im = jnp.broadcast_to(ab_im, bu_im.shape)
    _, _, h_re, h_im = lax.associative_scan(_cscan_combine, (a_re, a_im, bu_re, bu_im), axis=1)
    return h_re, h_im


def s5_readout(c_re, c_im, h_re, h_im):
    return (jnp.einsum('gpn,blgn->blgp', c_re.astype(F32), h_re)
            - jnp.einsum('gpn,blgn->blgp', c_im.astype(F32), h_im))


def s5_glu(y, w_glu, b_glu, dtype):
    B_, L = y.shape[:2]
    g = jax.nn.gelu(y.reshape(B_, L, W_GRP))
    return (g * jax.nn.sigmoid(g @ w_glu.astype(F32) + b_glu.astype(F32))).astype(dtype)


def s5_mixer(u_lat, u_ctx, a_re, a_im, log_dt, b_re, b_im, c_re, c_im, d, w_glu, b_glu, with_ctx_out):
    B_ = u_lat.shape[0]
    ul = u_lat.astype(F32).reshape(B_, -1, S5_G, S5_P)
    uc = u_ctx.astype(F32).reshape(B_, -1, S5_G, S5_P)
    dg = d.astype(F32).reshape(S5_G, S5_P)
    zero = jnp.zeros((B_, S5_G, S5_N), F32)
    y_lat = dg * ul
    y_ctx = dg * uc if with_ctx_out else None
    for di in range(2):
        rev = di == 1
        ab_re, ab_im, bb_re, bb_im = s5_discretize(a_re[di], a_im[di], log_dt[di], b_re[di], b_im[di])
        hc_re, hc_im = s5_states(_flip(uc, rev), ab_re, ab_im, bb_re, bb_im, zero, zero)
        hl_re, hl_im = s5_states(_flip(ul, rev), ab_re, ab_im, bb_re, bb_im, hc_re[:, -1], hc_im[:, -1])
        y_lat = y_lat + _flip(s5_readout(c_re[di], c_im[di], hl_re, hl_im), rev)
        if with_ctx_out:
            y_ctx = y_ctx + _flip(s5_readout(c_re[di], c_im[di], hc_re, hc_im), rev)
    out_lat = s5_glu(y_lat, w_glu, b_glu, u_lat.dtype)
    out_ctx = s5_glu(y_ctx, w_glu, b_glu, u_ctx.dtype) if with_ctx_out else None
    return out_lat, out_ctx


def pool_mix(u, pool_w, pool_scale):
    B_, L, _ = u.shape
    uf = u.astype(F32)
    csum = jnp.pad(jnp.cumsum(uf, axis=1), ((0, 0), (1, 0), (0, 0)))
    t = jnp.arange(L)
    outs = []
    for j, w in enumerate(POOL_WINDOWS):
        lo = jnp.clip(t - w // 2, 0, L - 1)
        hi = jnp.clip(t - w // 2 + w - 1, 0, L - 1)
        ch = slice(j * POOL_C, (j + 1) * POOL_C)
        cnt = (hi - lo + 1).astype(F32)[None, :, None]
        outs.append((csum[:, hi + 1, ch] - csum[:, lo, ch]) / cnt - uf[:, :, ch])
    pooled = jnp.stack(outs, axis=2)
    y = jnp.einsum('blgc,gcd->blgd', pooled, pool_w.astype(F32)).reshape(B_, L, W_GRP)
    return (y * pool_scale.astype(F32)).astype(u.dtype)


def na_latent(q, k, v, k_ctx, v_ctx, rpb):
    B_, L, _ = q.shape
    R = L // GRID_W
    KR = min(NA_WIN_R, R)
    KC = NA_WIN_C
    shp = (B_, R, GRID_W, NA_H, NA_HD)
    q, k, v = q.reshape(shp), k.reshape(shp), v.reshape(shp)
    r = jnp.arange(R)
    row_idx = jnp.clip(r - KR // 2, 0, R - KR)[:, None] + jnp.arange(KR)[None, :]
    k_rows = k[:, row_idx]
    v_rows = v[:, row_idx]
    col = jnp.arange(GRID_W)
    c0 = jnp.clip(col - KC // 2, 0, GRID_W - KC)
    in_win = (col[None, :] >= c0[:, None]) & (col[None, :] < c0[:, None] + KC)
    col_rel = jnp.clip(col[None, :] - col[:, None], -(KC - 1), KC - 1) + KC - 1
    row_rel = row_idx - r[:, None] + NA_WIN_R - 1
    bias = rpb.astype(F32)[:, row_rel][..., col_rel]
    bias = bias.transpose(1, 0, 3, 2, 4)
    scale = NA_HD ** -0.5
    s_nb = jnp.einsum('brqhd,brkwhd->brhqkw', q, k_rows).astype(F32) * scale + bias
    s_nb = jnp.where(in_win[:, None, :], s_nb, -jnp.inf).reshape(B_, R, NA_H, GRID_W, KR * GRID_W)
    s_cx = jnp.einsum('brqhd,bchd->brhqc', q, k_ctx).astype(F32) * scale
    p = jax.nn.softmax(jnp.concatenate([s_nb, s_cx], axis=-1), axis=-1).astype(v.dtype)
    p_nb = p[..., :KR * GRID_W].reshape(B_, R, NA_H, GRID_W, KR, GRID_W)
    p_cx = p[..., KR * GRID_W:]
    o = (jnp.einsum('brhqkw,brkwhd->brqhd', p_nb, v_rows)
         + jnp.einsum('brhqc,bchd->brqhd', p_cx, v_ctx))
    return o.reshape(B_, L, W_GRP)


def ctx_attn(q, k, v):
    s = jnp.einsum('bqhd,bkhd->bhqk', q, k).astype(F32) * NA_HD ** -0.5
    p = jax.nn.softmax(s, axis=-1).astype(v.dtype)
    return jnp.einsum('bhqk,bkhd->bqhd', p, v)


def na_mixer(q_l, k_l, v_l, q_c, k_c, v_c, rpb, with_ctx_out):
    B_, C_ = k_c.shape[:2]
    kc = k_c.reshape(B_, C_, NA_H, NA_HD)
    vc = v_c.reshape(B_, C_, NA_H, NA_HD)
    out_l = na_latent(q_l, k_l, v_l, kc, vc, rpb)
    out_c = (ctx_attn(q_c.reshape(B_, C_, NA_H, NA_HD), kc, vc).reshape(B_, C_, W_GRP)
             if with_ctx_out else None)
    return out_l, out_c


def dwconv(x, w, b):
    y = lax.conv_general_dilated(x, w[:, None, :].astype(x.dtype), window_strides=(1,),
                                 padding=[((SSM_CONV - 1) // 2, SSM_CONV // 2)],
                                 dimension_numbers=('NWC', 'WIO', 'NWC'),
                                 feature_group_count=x.shape[-1])
    return y + b


def segsum(x):
    T = x.shape[-1]
    cs = jnp.cumsum(x, axis=-1)
    d = cs[..., :, None] - cs[..., None, :]
    return jnp.where(jnp.tril(jnp.ones((T, T), bool)), d, -jnp.inf)


def ssd(X, A, Bm, Cm, h0):
    b, L, H, P = X.shape
    N = Bm.shape[-1]
    nc = L // SSM_CHUNK
    X = X.reshape(b, nc, SSM_CHUNK, H, P)
    Bm = Bm.reshape(b, nc, SSM_CHUNK, H, N)
    Cm = Cm.reshape(b, nc, SSM_CHUNK, H, N)
    A = A.reshape(b, nc, SSM_CHUNK, H).transpose(0, 3, 1, 2)
    A_cs = jnp.cumsum(A, axis=-1)
    cb = jnp.einsum('bclhn,bcshn->bhcls', Cm, Bm)
    y_diag = jnp.einsum('bhcls,bcshp->bclhp', cb * jnp.exp(segsum(A)), X)
    decay_states = jnp.exp(A_cs[..., -1:] - A_cs).transpose(0, 2, 3, 1)[..., None]
    states = jnp.einsum('bclhn,bclhp->bchpn', Bm, X * decay_states)
    states = jnp.concatenate([h0[:, None], states], axis=1)
    chunk_decay = jnp.exp(segsum(jnp.pad(A_cs[..., -1], ((0, 0), (0, 0), (1, 0)))))
    new_states = jnp.einsum('bhzc,bchpn->bzhpn', chunk_decay, states)
    prev, final = new_states[:, :-1], new_states[:, -1]
    y_off = (jnp.einsum('bclhn,bchpn->bclhp', Cm, prev)
             * jnp.exp(A_cs).transpose(0, 2, 3, 1)[..., None])
    return (y_diag + y_off).reshape(b, L, H, P), final


def ssm_prepare(pxbc, pdt, conv_w, conv_b, dt_bias):
    B_, L, _ = pxbc.shape
    xbc = jax.nn.silu(dwconv(pxbc, conv_w, conv_b).astype(F32))
    hpg = SSM_H // SSM_G
    xs = xbc[..., :W_GRP].reshape(B_, L, SSM_H, SSM_HD)
    bm = jnp.repeat(xbc[..., W_GRP:W_GRP + SSM_G * SSM_N].reshape(B_, L, SSM_G, SSM_N), hpg, axis=2)
    cm = jnp.repeat(xbc[..., W_GRP + SSM_G * SSM_N:].reshape(B_, L, SSM_G, SSM_N), hpg, axis=2)
    dt = jax.nn.softplus(pdt.astype(F32).reshape(B_, L, 2, SSM_H) + dt_bias.astype(F32))
    return xs, bm, cm, dt


def ssm_direction(xs, bm, cm, dt, a, h0, rev):
    y, h_fin = ssd(_flip(xs * dt[..., None], rev), _flip(dt * a, rev), _flip(bm, rev), _flip(cm, rev), h0)
    return _flip(y, rev), h_fin


def ssm_gate_norm(y, z, norm_w):
    B_, L = z.shape[:2]
    g = (y.reshape(B_, L, W_GRP) * jax.nn.silu(z.astype(F32))).reshape(B_, L, SSM_G, W_GRP // SSM_G)
    g = g * lax.rsqrt(jnp.mean(jnp.square(g), -1, keepdims=True) + LN_EPS)
    return (g.reshape(B_, L, W_GRP) * norm_w.astype(F32)).astype(z.dtype)


def ssm_mixer(z_l, xbc_l, dtr_l, z_c, xbc_c, dtr_c, conv_w, conv_b, dt_bias, a_log, d, norm_w, with_ctx_out):
    xs_l, bm_l, cm_l, dt_l = ssm_prepare(xbc_l, dtr_l, conv_w, conv_b, dt_bias)
    xs_c, bm_c, cm_c, dt_c = ssm_prepare(xbc_c, dtr_c, conv_w, conv_b, dt_bias)
    a = -jnp.exp(a_log.astype(F32))
    dh = d.astype(F32)[:, None]
    zero = jnp.zeros((xs_c.shape[0], SSM_H, SSM_HD, SSM_N), F32)
    y_l = dh * xs_l
    y_c = dh * xs_c if with_ctx_out else None
    for di in range(2):
        rev = di == 1
        yc, hc = ssm_direction(xs_c, bm_c, cm_c, dt_c[:, :, di], a[di], zero, rev)
        yl, _ = ssm_direction(xs_l, bm_l, cm_l, dt_l[:, :, di], a[di], hc, rev)
        y_l = y_l + yl
        if with_ctx_out:
            y_c = y_c + yc
    out_l = ssm_gate_norm(y_l, z_l, norm_w)
    out_c = ssm_gate_norm(y_c, z_c, norm_w) if with_ctx_out else None
    return out_l, out_c


def peer_ffn(t, w_q, sub_keys, u_tab, v_tab):
    n_tok = t.shape[0]
    q = (t @ w_q).reshape(n_tok, PEER_HEADS, 2, PEER_QDIM // 2)
    s = jnp.einsum('thsd,hskd->thsk', q, sub_keys).astype(F32)
    sv, si = lax.top_k(s, PEER_TOPK)
    cand_s = (sv[:, :, 0, :, None] + sv[:, :, 1, None, :]).reshape(n_tok, PEER_HEADS, PEER_TOPK * PEER_TOPK)
    cand_i = (si[:, :, 0, :, None] * PEER_NK + si[:, :, 1, None, :]).reshape(n_tok, PEER_HEADS, PEER_TOPK * PEER_TOPK)
    top_s, pos = lax.top_k(cand_s, PEER_TOPK)
    idx = jnp.take_along_axis(cand_i, pos, axis=-1).reshape(n_tok, PEER_HEADS * PEER_TOPK)
    gate = jax.nn.softmax(top_s, axis=-1).reshape(n_tok, PEER_HEADS * PEER_TOPK)
    n_blk = n_tok // PEER_BLOCK

    def expert_block(args):
        xb, ib, gb = args
        act = jnp.einsum('tkd,td->tk', u_tab[ib], xb).astype(F32)
        return jnp.einsum('tk,tkd->td', (jax.nn.gelu(act) * gb).astype(xb.dtype), v_tab[ib])

    out = lax.map(expert_block, (t.reshape(n_blk, PEER_BLOCK, -1),
                                 idx.reshape(n_blk, PEER_BLOCK, -1),
                                 gate.reshape(n_blk, PEER_BLOCK, -1)))
    return out.reshape(n_tok, -1)


def _mixers(p_l, p_c, prm, with_ctx_out):
    s5_l, pool_l, q_l, k_l, v_l, z_l, xbc_l, dt_l = p_l
    s5_c, pool_c, q_c, k_c, v_c, z_c, xbc_c, dt_c = p_c
    ya_l, ya_c = s5_mixer_pallas(s5_l, s5_c, prm["s5_a_re"], prm["s5_a_im"], prm["s5_log_dt"],
                                 prm["s5_b_re"], prm["s5_b_im"], prm["s5_c_re"], prm["s5_c_im"], prm["s5_d"],
                                 prm["s5_w_glu"], prm["s5_b_glu"], with_ctx_out)
    yb_l = pool_mix_pallas(pool_l, prm["pool_w"], prm["pool_scale"])
    yb_c = pool_mix_pallas(pool_c, prm["pool_w"], prm["pool_scale"]) if with_ctx_out else None
    B_, L, _ = s5_l.shape
    C_ = s5_c.shape[1]
    flat = lambda t: t.reshape(-1, t.shape[-1])
    table = _na_bias_table(prm["na_rpb"], L // GRID_W)
    yc_l = na_attention(flat(q_l), flat(k_l), flat(v_l), flat(k_c), flat(v_c), table, L, C_, True)
    yc_c = (na_attention(flat(q_c), flat(k_c), flat(v_c), flat(k_c), flat(v_c), table, C_, C_, False)
            if with_ctx_out else None)
    yd_l, yd_c = ssm_mixer_pallas(flat(z_l), flat(xbc_l), flat(dt_l), flat(z_c), flat(xbc_c), flat(dt_c),
                                  prm["ssm_conv_w"], prm["ssm_conv_b"], prm["ssm_dt_bias"], prm["ssm_a_log"],
                                  prm["ssm_d"], prm["ssm_norm_w"], L, C_, with_ctx_out)
    return (ya_l, yb_l, yc_l, yd_l), (ya_c, yb_c, yc_c, yd_c)


def kernel(x, c, ctx, c_ctx, w_ada, b_ada, w_in, w_out, s5_a_re, s5_a_im, s5_log_dt, s5_b_re, s5_b_im, s5_c_re, s5_c_im, s5_d, s5_w_glu, s5_b_glu, pool_w, pool_scale, na_rpb, ssm_conv_w, ssm_conv_b, ssm_dt_bias, ssm_a_log, ssm_d, ssm_norm_w, ln1_g, ln1_b, ln2_g, ln2_b, peer_w_q, peer_sub_keys, peer_u, peer_v):
    B_, L, D = x.shape
    C_ = ctx.shape[1]
    x_lat = x.reshape(B_ * L, D)
    x_ctx = ctx.reshape(B_ * C_, D)
    act_lat = jax.nn.silu(c)
    act_ctx = jax.nn.silu(c_ctx)
    dt_pad = V7X_LANES - IN_SPLITS[-1]
    splits = IN_SPLITS[:-1] + (V7X_LANES,)
    in_dtypes = (F32, F32, BF16, BF16, BF16, F32, F32, F32)
    TM = 256
    for l in range(DEPTH):
        last = l == DEPTH - 1
        prm = dict(s5_a_re=s5_a_re[l], s5_a_im=s5_a_im[l], s5_log_dt=s5_log_dt[l], s5_b_re=s5_b_re[l],
                   s5_b_im=s5_b_im[l], s5_c_re=s5_c_re[l], s5_c_im=s5_c_im[l], s5_d=s5_d[l],
                   s5_w_glu=s5_w_glu[l], s5_b_glu=s5_b_glu[l], pool_w=pool_w[l], pool_scale=pool_scale[l],
                   na_rpb=na_rpb[l], ssm_conv_w=ssm_conv_w[l], ssm_conv_b=ssm_conv_b[l],
                   ssm_dt_bias=ssm_dt_bias[l], ssm_a_log=ssm_a_log[l], ssm_d=ssm_d[l],
                   ssm_norm_w=ssm_norm_w[l])
        m_lat = (act_lat @ w_ada[l] + b_ada[l]).reshape(B_, 6, 1, D)
        m_ctx = (act_ctx @ w_ada[l] + b_ada[l]).reshape(1, 6, 1, D)
        w_in_b = jnp.pad(w_in[l], ((0, 0), (0, dt_pad))).astype(BF16)
        w_out_b = w_out[l].astype(BF16)

        p_l = modulated_matmul(x_lat, m_lat[:, 0], m_lat[:, 1], w_in_b, splits, in_dtypes, L, TM)
        p_c = modulated_matmul(x_ctx, m_ctx[:, 0], m_ctx[:, 1], w_in_b, splits, in_dtypes, B_ * C_, TM)
        p_l = [a.reshape(B_, L, -1) for a in p_l]
        p_c = [a.reshape(B_, C_, -1) for a in p_c]
        y_l, y_c = _mixers(p_l, p_c, prm, not last)

        x_lat = proj_residual_ln([a.reshape(B_ * L, -1) for a in y_l], x_lat, m_lat[:, 2], w_out_b,
                                 ln1_g[l], ln1_b[l], L, TM)
        wq_b = peer_w_q[l].astype(BF16)
        keys_b = peer_sub_keys[l].reshape(2 * PEER_HEADS, PEER_NK, PEER_HALF).astype(BF16)
        u_b = peer_u[l].astype(BF16)
        vt_b = peer_v_tiles(peer_v[l])
        x_lat = peer_sublayer(x_lat, m_lat[:, 3], m_lat[:, 4], m_lat[:, 5], wq_b, keys_b, u_b, vt_b,
                              ln2_g[l], ln2_b[l], L)
        if not last:
            x_ctx = proj_residual_ln([a.reshape(B_ * C_, -1) for a in y_c], x_ctx, m_ctx[:, 2], w_out_b,
                                     ln1_g[l], ln1_b[l], B_ * C_, TM)
            x_ctx = peer_sublayer(x_ctx, m_ctx[:, 3], m_ctx[:, 4], m_ctx[:, 5], wq_b, keys_b, u_b, vt_b,
                                  ln2_g[l], ln2_b[l], B_ * C_)
    return x_lat.reshape(B_, L, D)
```

```python
import functools
import math

import jax
import jax.numpy as jnp
import numpy as np
from jax import lax
from jax.experimental import pallas as pl
from jax.experimental.pallas import tpu as pltpu

D_MODEL = 2048
BATCH = 4
SEQ = 4096
DEPTH = 2

GRID_W = 64
CTX_LEN = 256
N_MIXERS = 4
D_MIX = D_MODEL
W_GRP = D_MIX // N_MIXERS

S5_P = 16
S5_G = W_GRP // S5_P
S5_N = 64

POOL_WINDOWS = (2, 4, 8, 16)
POOL_C = W_GRP // len(POOL_WINDOWS)

NA_HD = 64
NA_H = W_GRP // NA_HD
NA_WIN_R = 8
NA_WIN_C = 16

SSM_HD = 64
SSM_H = W_GRP // SSM_HD
SSM_G = 2
SSM_N = 128
SSM_CONV = 4
SSM_CHUNK = 128
SSM_CONV_CH = W_GRP + 2 * SSM_G * SSM_N

PEER_HEADS = 8
PEER_NK = 128
PEER_NE = PEER_NK * PEER_NK
PEER_QDIM = 256
PEER_TOPK = 16
PEER_BLOCK = 128

IN_SPLITS = (W_GRP, W_GRP, W_GRP, W_GRP, W_GRP, W_GRP, SSM_CONV_CH, 2 * SSM_H)
D_IN = sum(IN_SPLITS)

DEEPNORM_ALPHA = (2 * DEPTH) ** 0.25
DEEPNORM_BETA = (8 * DEPTH) ** -0.25
LN_EPS = 1e-5
F32 = jnp.float32
BF16 = jnp.bfloat16

V7X_LANES = 128
V7X_VMEM_LIMIT_BYTES = 56 * 1024 * 1024


def _flip(t, rev):
    return t[:, ::-1] if rev else t


def _modmm_body(x_ref, shift_ref, scale_ref, w_ref, *out_refs, col_splits):
    xm = (x_ref[...] * (1.0 + scale_ref[0]) + shift_ref[0]).astype(BF16)
    o = 0
    for j, (ref, n) in enumerate(zip(out_refs, col_splits)):
        val = jnp.dot(xm, w_ref[:, o:o + n], preferred_element_type=F32)
        ref[...] = val.astype(ref.dtype)
        if j == 0:
            tm = val.shape[0]
            r = lax.broadcasted_iota(jnp.int32, (tm, tm), 0)
            c = lax.broadcasted_iota(jnp.int32, (tm, tm), 1)
            flip = jnp.where(r + c == tm - 1, 1.0, 0.0).astype(BF16)
            out_refs[-1][...] = jnp.dot(flip, val.astype(BF16), preferred_element_type=F32).astype(BF16)
        o += n


def modulated_matmul(x, shift, scale, w_bf16, col_splits, out_dtypes, rows_per_mod, seq_len, tm):
    T, K = x.shape
    assert T % tm == 0 and rows_per_mod % tm == 0 and seq_len % tm == 0
    tiles_per_mod = rows_per_mod // tm
    tps = seq_len // tm
    n_tot = sum(col_splits)
    assert w_bf16.shape == (K, n_tot)
    mod_spec = pl.BlockSpec((1, 1, K), lambda i: (i // tiles_per_mod, 0, 0))
    shapes = [jax.ShapeDtypeStruct((T, n), dt) for n, dt in zip(col_splits, out_dtypes, strict=True)]
    specs = [pl.BlockSpec((tm, n), lambda i: (i, 0)) for n in col_splits]
    shapes.append(jax.ShapeDtypeStruct((T, col_splits[0]), BF16))
    specs.append(pl.BlockSpec((tm, col_splits[0]), lambda i: ((i // tps) * tps + tps - 1 - i % tps, 0)))
    return pl.pallas_call(
        functools.partial(_modmm_body, col_splits=tuple(col_splits)),
        out_shape=shapes,
        grid=(T // tm,),
        in_specs=[pl.BlockSpec((tm, K), lambda i: (i, 0)), mod_spec, mod_spec,
                  pl.BlockSpec((K, n_tot), lambda i: (0, 0))],
        out_specs=specs,
        compiler_params=pltpu.CompilerParams(dimension_semantics=("arbitrary",),
                                             vmem_limit_bytes=V7X_VMEM_LIMIT_BYTES),
        name="modulated_matmul",
    )(x, shift, scale, w_bf16)


def _proj_ln_body(*refs, n_parts, alpha):
    part_refs = refs[:n_parts]
    x_ref, gate_ref, w_ref, g_ref, b_ref, o_ref = refs[n_parts:]
    acc = None
    o = 0
    for pr in part_refs:
        n = pr.shape[-1]
        d = jnp.dot(pr[...].astype(BF16), w_ref[o:o + n, :], preferred_element_type=F32)
        acc = d if acc is None else acc + d
        o += n
    h = alpha * x_ref[...] + gate_ref[0] * acc
    mu = jnp.mean(h, -1, keepdims=True)
    hc = h - mu
    var = jnp.mean(hc * hc, -1, keepdims=True)
    o_ref[...] = hc * lax.rsqrt(var + LN_EPS) * g_ref[...] + b_ref[...]


def proj_residual_ln(parts, x, gate, w_bf16, g, b, rows_per_mod, tm):
    T, D = x.shape
    assert T % tm == 0 and rows_per_mod % tm == 0
    tiles_per_mod = rows_per_mod // tm
    k_tot = sum(p.shape[-1] for p in parts)
    assert w_bf16.shape == (k_tot, D)
    row = lambda n: pl.BlockSpec((tm, n), lambda i: (i, 0))
    vec = pl.BlockSpec((1, D), lambda i: (0, 0))
    return pl.pallas_call(
        functools.partial(_proj_ln_body, n_parts=len(parts), alpha=DEEPNORM_ALPHA),
        out_shape=jax.ShapeDtypeStruct((T, D), F32),
        grid=(T // tm,),
        in_specs=[row(p.shape[-1]) for p in parts] + [
            row(D), pl.BlockSpec((1, 1, D), lambda i: (i // tiles_per_mod, 0, 0)),
            pl.BlockSpec((k_tot, D), lambda i: (0, 0)), vec, vec],
        out_specs=row(D),
        compiler_params=pltpu.CompilerParams(dimension_semantics=("arbitrary",),
                                             vmem_limit_bytes=V7X_VMEM_LIMIT_BYTES),
        name="proj_residual_ln",
    )(*parts, x, gate, w_bf16, g.reshape(1, D), b.reshape(1, D))


def _res_ln_body(x_ref, f_ref, gate_ref, g_ref, b_ref, o_ref, *, alpha):
    h = alpha * x_ref[...] + gate_ref[0] * f_ref[...]
    mu = jnp.mean(h, -1, keepdims=True)
    hc = h - mu
    var = jnp.mean(hc * hc, -1, keepdims=True)
    o_ref[...] = hc * lax.rsqrt(var + LN_EPS) * g_ref[...] + b_ref[...]


def residual_ln(x, f, gate, g, b, rows_per_mod, tm):
    T, D = x.shape
    assert T % tm == 0 and rows_per_mod % tm == 0
    tiles_per_mod = rows_per_mod // tm
    row = pl.BlockSpec((tm, D), lambda i: (i, 0))
    vec = pl.BlockSpec((1, D), lambda i: (0, 0))
    return pl.pallas_call(
        functools.partial(_res_ln_body, alpha=DEEPNORM_ALPHA),
        out_shape=jax.ShapeDtypeStruct((T, D), F32),
        grid=(T // tm,),
        in_specs=[row, row, pl.BlockSpec((1, 1, D), lambda i: (i // tiles_per_mod, 0, 0)), vec, vec],
        out_specs=row,
        compiler_params=pltpu.CompilerParams(dimension_semantics=("arbitrary",),
                                             vmem_limit_bytes=V7X_VMEM_LIMIT_BYTES),
        name="residual_ln",
    )(x, f, gate, g.reshape(1, D), b.reshape(1, D))


S5_CHAINS = 2 * BATCH
S5_STATE = S5_G * S5_N
S5_SCAN_COLS = 512
S5_SB_IN = S5_SCAN_COLS // S5_N * S5_P


def _s5_scan_body(u_ref, wb_ref, wc_ref, are_ref, aim_ref, y_ref, bu_ref, h_ref, *, steps):
    rows = steps * S5_CHAINS

    @pl.when(pl.program_id(0) == 0)
    def _():
        h_ref[...] = jnp.zeros_like(h_ref)

    u = u_ref[...]
    chain = lax.broadcasted_iota(jnp.int32, u.shape, 0) % S5_CHAINS
    fwd = chain < BATCH
    zero = jnp.zeros_like(u)
    uf = jnp.where(fwd, u, zero)
    ub = jnp.where(fwd, zero, u)
    is_fwd_y = lax.broadcasted_iota(jnp.int32, (rows, S5_SB_IN), 0) % S5_CHAINS < BATCH

    for cb in range(S5_STATE // S5_SCAN_COLS):
        cin = slice(cb * S5_SB_IN, (cb + 1) * S5_SB_IN)
        re = pl.ds(cb * S5_SCAN_COLS, S5_SCAN_COLS)
        im = pl.ds(S5_STATE + cb * S5_SCAN_COLS, S5_SCAN_COLS)
        bu = jnp.dot(jnp.concatenate([uf[:, cin], ub[:, cin]], axis=1), wb_ref[cb], preferred_element_type=F32)
        bu_ref[:, re] = bu[:, :S5_SCAN_COLS]
        bu_ref[:, im] = bu[:, S5_SCAN_COLS:]
        a_re = are_ref[:, re]
        a_im = aim_ref[:, re]

        def step(s, carry):
            h_re, h_im = carry
            r = pl.ds(pl.multiple_of(s * S5_CHAINS, S5_CHAINS), S5_CHAINS)
            n_re = a_re * h_re - a_im * h_im + bu_ref[r, re]
            n_im = a_re * h_im + a_im * h_re + bu_ref[r, im]
            bu_ref[r, re] = n_re
            bu_ref[r, im] = n_im
            return n_re, n_im

        h_re, h_im = lax.fori_loop(0, steps, step, (h_ref[:, re], h_ref[:, im]), unroll=4)
        h_ref[:, re] = h_re
        h_ref[:, im] = h_im
        hb = jnp.concatenate([bu_ref[:, re], bu_ref[:, im]], axis=1).astype(BF16)
        y2 = jnp.dot(hb, wc_ref[cb], preferred_element_type=F32)
        y_ref[cb] = jnp.where(is_fwd_y, y2[:, :S5_SB_IN], y2[:, S5_SB_IN:])


def s5_scan(u8, wb, wc, a_re8, a_im8, steps):
    n_rows = u8.shape[0]
    rows = steps * S5_CHAINS
    assert n_rows % rows == 0
    full = lambda a: pl.BlockSpec(a.shape, lambda i: (0,) * a.ndim)
    return pl.pallas_call(
        functools.partial(_s5_scan_body, steps=steps),
        out_shape=jax.ShapeDtypeStruct((W_GRP // S5_SB_IN, n_rows, S5_SB_IN), F32),
        grid=(n_rows // rows,),
        in_specs=[pl.BlockSpec((rows, W_GRP), lambda i: (i, 0)), full(wb), full(wc), full(a_re8), full(a_im8)],
        out_specs=pl.BlockSpec((W_GRP // S5_SB_IN, rows, S5_SB_IN), lambda i: (0, i, 0)),
        scratch_shapes=[pltpu.VMEM((rows, 2 * S5_STATE), F32), pltpu.VMEM((S5_CHAINS, 2 * S5_STATE), F32)],
        compiler_params=pltpu.CompilerParams(dimension_semantics=("arbitrary",),
                                             vmem_limit_bytes=V7X_VMEM_LIMIT_BYTES),
        name="s5_scan",
    )(u8, wb, wc, a_re8, a_im8)


def _s5_glu_body(u_ref, yf_ref, yb_ref, d_ref, w_ref, b_ref, o_ref):
    tm = u_ref.shape[1]
    r = lax.broadcasted_iota(jnp.int32, (tm, tm), 0)
    c = lax.broadcasted_iota(jnp.int32, (tm, tm), 1)
    flip = jnp.where(r + c == tm - 1, 1.0, 0.0).astype(BF16)
    for b in range(BATCH):
        n_cb = yf_ref.shape[0]
        yf = jnp.concatenate([yf_ref[cb, pl.ds(b, tm, stride=S5_CHAINS), :] for cb in range(n_cb)], axis=1)
        yb = jnp.concatenate([yb_ref[cb, pl.ds(BATCH + b, tm, stride=S5_CHAINS), :] for cb in range(n_cb)],
                             axis=1)
        hi = yb.astype(BF16)
        lo = (yb - hi.astype(F32)).astype(BF16)
        yb = jnp.dot(flip, hi, preferred_element_type=F32) + jnp.dot(flip, lo, preferred_element_type=F32)
        y = d_ref[...] * u_ref[b] + yf + yb
        g = jax.nn.gelu(y)
        z = jnp.dot(g.astype(BF16), w_ref[...], preferred_element_type=F32) + b_ref[...]
        o_ref[b] = g * jax.nn.sigmoid(z)


def s5_glu_pallas(u, y8, first_step, d, w_bf16, b, tm):
    B_, n, W = u.shape
    assert B_ == BATCH and n % tm == 0 and first_step % tm == 0
    nt = n // tm
    off = first_step // tm
    row = pl.BlockSpec((B_, tm, W), lambda i: (0, i, 0))
    vec = pl.BlockSpec((1, W), lambda i: (0, 0))
    return pl.pallas_call(
        _s5_glu_body,
        out_shape=jax.ShapeDtypeStruct((B_, n, W), F32),
        grid=(nt,),
        in_specs=[row, pl.BlockSpec((y8.shape[0], tm * S5_CHAINS, y8.shape[2]), lambda i: (0, off + i, 0)),
                  pl.BlockSpec((y8.shape[0], tm * S5_CHAINS, y8.shape[2]), lambda i: (0, off + nt - 1 - i, 0)),
                  vec, pl.BlockSpec((W, W), lambda i: (0, 0)), vec],
        out_specs=row,
        compiler_params=pltpu.CompilerParams(dimension_semantics=("arbitrary",),
                                             vmem_limit_bytes=V7X_VMEM_LIMIT_BYTES),
        name="s5_glu",
    )(u, y8, y8, d.reshape(1, W), w_bf16, b.reshape(1, W))


def _s5_weights(a_re, a_im, log_dt, b_re, b_im, c_re, c_im):
    eye = jnp.eye(S5_G, dtype=F32)
    wb, a8 = [], []
    wc = []
    for di in range(2):
        ab_re, ab_im, bb_re, bb_im = s5_discretize(a_re[di], a_im[di], log_dt[di], b_re[di], b_im[di])
        blk = lambda m: jnp.einsum('gnp,gh->gphn', m, eye).reshape(W_GRP, S5_STATE)
        wb.append(jnp.concatenate([blk(bb_re), blk(bb_im)], axis=1))
        a8.append((jnp.broadcast_to(ab_re.reshape(1, S5_STATE), (BATCH, S5_STATE)),
                   jnp.broadcast_to(ab_im.reshape(1, S5_STATE), (BATCH, S5_STATE))))
        cblk = lambda m: jnp.einsum('gpn,gh->gnhp', m.astype(F32), eye).reshape(S5_STATE, W_GRP)
        wc.append(jnp.concatenate([cblk(c_re[di]), -cblk(c_im[di])], axis=0))
    wb_sb, wc_sb = [], []
    for sb in range(S5_STATE // S5_SCAN_COLS):
        cin = slice(sb * S5_SB_IN, (sb + 1) * S5_SB_IN)
        re = slice(sb * S5_SCAN_COLS, (sb + 1) * S5_SCAN_COLS)
        im = slice(S5_STATE + sb * S5_SCAN_COLS, S5_STATE + (sb + 1) * S5_SCAN_COLS)
        wb_sb.append(jnp.concatenate([jnp.concatenate([w[cin, re], w[cin, im]], axis=1) for w in wb], axis=0))
        wc_sb.append(jnp.concatenate([jnp.concatenate([w[re, cin], w[im, cin]], axis=0) for w in wc], axis=1))
    wb_sb = jnp.stack(wb_sb).astype(BF16)
    wc_sb = jnp.stack(wc_sb).astype(BF16)
    a_re8 = jnp.concatenate([a8[0][0], a8[1][0]], axis=0)
    a_im8 = jnp.concatenate([a8[0][1], a8[1][1]], axis=0)
    return wb_sb, wc_sb, a_re8, a_im8


def s5_mixer_pallas(u_lat, u_ctx, rev_lat, rev_ctx, a_re, a_im, log_dt, b_re, b_im, c_re, c_im, d, w_glu,
                    b_glu, with_ctx_out, steps=64, tm=512):
    B_, L, W = u_lat.shape
    C_ = u_ctx.shape[1]
    assert B_ == BATCH and W == W_GRP
    wb, wc, a_re8, a_im8 = _s5_weights(a_re, a_im, log_dt, b_re, b_im, c_re, c_im)
    seq_f = jnp.concatenate([u_ctx.astype(BF16), u_lat.astype(BF16)], axis=1)
    seq_b = jnp.concatenate([rev_ctx, rev_lat], axis=1)
    u8 = jnp.concatenate([seq_f, seq_b], axis=0).transpose(1, 0, 2).reshape((C_ + L) * S5_CHAINS, W)
    y8 = s5_scan(u8, wb, wc, a_re8, a_im8, steps)
    w_glu_b = w_glu.astype(BF16)
    tm = min(tm, C_)
    out_lat = s5_glu_pallas(u_lat, y8, C_, d, w_glu_b, b_glu, tm).reshape(B_ * L, W)
    out_ctx = s5_glu_pallas(u_ctx, y8, 0, d, w_glu_b, b_glu, tm).reshape(B_ * C_, W) if with_ctx_out else None
    return out_lat, out_ctx


PEER_HALF = PEER_QDIM // 2


def _argmax_rows(v, r):
    while v.shape[0] > 1:
        half = v.shape[0] // 2
        take_hi = v[half:] > v[:half]
        r = jnp.where(take_hi, r[half:], r[:half])
        v = jnp.maximum(v[:half], v[half:])
    return v, r


def _topk_rows(x, k):
    row = lax.broadcasted_iota(jnp.int32, x.shape, 0).astype(F32)
    krow = lax.broadcasted_iota(jnp.int32, (k, x.shape[1]), 0)

    def body(it, carry):
        x, rank, vals = carry
        m, first = _argmax_rows(x, row)
        hit = row == first
        return (jnp.where(hit, -jnp.inf, x), jnp.where(hit, it.astype(F32), rank), jnp.where(krow == it, m, vals))

    init = (x, jnp.full(x.shape, float(k), F32), jnp.zeros((k, x.shape[1]), F32))
    _, rank, vals = lax.fori_loop(0, k, body, init)
    return vals, rank


PEER_GRID_COLS = tuple(PEER_TOPK // (i + 1) for i in range(PEER_TOPK))
PEER_GRID_ROWS = 64


def _peer_route_body(x_ref, shift_ref, scale_ref, wq_ref, keys_ref, h_ref, n1_ref, c1_ref, r2_ref, e2_ref,
                     q_ref, *, lane_tiles):
    hm = (x_ref[...] * (1.0 + scale_ref[0]) + shift_ref[0]).astype(BF16)
    h_ref[...] = hm
    q_ref[...] = jnp.dot(hm, wq_ref[...], preferred_element_type=F32).astype(BF16)
    K = PEER_TOPK
    for hd in range(PEER_HEADS):
        for lt in range(lane_tiles):
            tok = pl.ds(lt * V7X_LANES, V7X_LANES)
            sc = []
            for side in range(2):
                col = (2 * hd + side) * PEER_HALF
                qs = q_ref[tok, col:col + PEER_HALF]
                sc.append(lax.dot_general(keys_ref[2 * hd + side], qs, (((1,), (1,)), ((), ())),
                                          preferred_element_type=F32))
            v1, r1 = _topk_rows(sc[0], K)
            v2, r2 = _topk_rows(sc[1], K)
            cells = [v1[i:i + 1] + v2[:PEER_GRID_COLS[i]] for i in range(K)]
            cells.append(jnp.full((PEER_GRID_ROWS - sum(PEER_GRID_COLS), V7X_LANES), -jnp.inf, F32))
            vc, rc = _topk_rows(jnp.concatenate(cells, axis=0), K)
            z = jnp.sum(jnp.exp(vc - vc[0:1]), axis=0, keepdims=True)
            chosen = jnp.where(rc < float(K), 1.0, 0.0)
            n1 = jnp.zeros_like(r1)
            off = 0
            for i in range(K):
                n_i = jnp.sum(chosen[off:off + PEER_GRID_COLS[i]], axis=0, keepdims=True)
                n1 = jnp.where(r1 == float(i), n_i, n1)
                off += PEER_GRID_COLS[i]
            n1_ref[hd, :, tok] = n1
            c1_ref[hd, :, tok] = jnp.exp(sc[0] - v1[0:1]) / z
            r2_ref[hd, :, tok] = r2.astype(BF16)
            e2_ref[hd, :, tok] = jnp.exp(sc[1] - v2[0:1]).astype(BF16)


def peer_route(x, shift, scale, wq_bf16, keys_bf16, rows_per_mod, tm):
    T, D = x.shape
    assert T % tm == 0 and rows_per_mod % tm == 0 and tm % V7X_LANES == 0
    tiles_per_mod = rows_per_mod // tm
    mod_spec = pl.BlockSpec((1, 1, D), lambda i: (i // tiles_per_mod, 0, 0))
    tab = lambda dt: jax.ShapeDtypeStruct((PEER_HEADS, PEER_NK, T), dt)
    tab_spec = pl.BlockSpec((PEER_HEADS, PEER_NK, tm), lambda i: (0, 0, i))
    return pl.pallas_call(
        functools.partial(_peer_route_body, lane_tiles=tm // V7X_LANES),
        out_shape=[jax.ShapeDtypeStruct((T, D), BF16), tab(F32), tab(F32), tab(BF16), tab(BF16)],
        grid=(T // tm,),
        in_specs=[pl.BlockSpec((tm, D), lambda i: (i, 0)), mod_spec, mod_spec,
                  pl.BlockSpec(wq_bf16.shape, lambda i: (0, 0)),
                  pl.BlockSpec(keys_bf16.shape, lambda i: (0, 0, 0))],
        out_specs=[pl.BlockSpec((tm, D), lambda i: (i, 0)), tab_spec, tab_spec, tab_spec, tab_spec],
        scratch_shapes=[pltpu.VMEM((tm, PEER_HEADS * PEER_QDIM), BF16)],
        compiler_params=pltpu.CompilerParams(dimension_semantics=("arbitrary",),
                                             vmem_limit_bytes=V7X_VMEM_LIMIT_BYTES),
        name="peer_route",
    )(x, shift, scale, wq_bf16, keys_bf16)


def _peer_dense_body(h_ref, u_ref, vt_ref, n1_ref, c1_ref, r2_ref, e2_ref, x_ref, gate_ref, g_ref, b_ref,
                     o_ref, acc_ref, a_ref, *, n_slab):
    j = pl.program_id(1)

    @pl.when(j == 0)
    def _():
        acc_ref[...] = jnp.zeros_like(acc_ref)

    pair = 2 * PEER_NK
    for p in range(n_slab // 2):
        s = lax.dot_general(u_ref[p * pair:(p + 1) * pair, :], h_ref[...], (((1,), (1,)), ((), ())),
                            preferred_element_type=F32)
        for kk in range(2):
            k = 2 * p + kk
            for lt in range(h_ref.shape[0] // V7X_LANES):
                lanes = slice(lt * V7X_LANES, (lt + 1) * V7X_LANES)
                g = None
                for hd in range(PEER_HEADS):
                    n1row = n1_ref[hd, k:k + 1, lanes].astype(BF16)
                    c1row = c1_ref[hd, k:k + 1, lanes].astype(BF16)
                    sel = lax.clamp(jnp.zeros((), BF16), n1row - r2_ref[hd, :, lanes], jnp.ones((), BF16))
                    gh = sel * e2_ref[hd, :, lanes] * c1row
                    g = gh if g is None else g + gh
                sk = s[kk * PEER_NK:(kk + 1) * PEER_NK, lanes]
                a_ref[k * PEER_NK:(k + 1) * PEER_NK, lanes] = jax.nn.gelu(sk).astype(BF16) * g
    acc_ref[...] += jnp.dot(vt_ref[0], a_ref[...], preferred_element_type=F32)

    @pl.when(j == pl.num_programs(1) - 1)
    def _():
        hres = DEEPNORM_ALPHA * x_ref[...] + gate_ref[0] * acc_ref[...].T
        mu = jnp.mean(hres, -1, keepdims=True)
        hc = hres - mu
        var = jnp.mean(hc * hc, -1, keepdims=True)
        o_ref[...] = hc * lax.rsqrt(var + LN_EPS) * g_ref[...] + b_ref[...]


def peer_dense(h_bf16, u_bf16, vt_tiles, n1, c1, r2, e2, x, gate, ln_g, ln_b, rows_per_mod, tm, n_slab):
    T, D = h_bf16.shape
    NE = u_bf16.shape[0]
    e_tile = n_slab * PEER_NK
    n_e = NE // e_tile
    assert T % tm == 0 and NE % e_tile == 0 and vt_tiles.shape == (n_e, D, e_tile)
    slab_spec = pl.BlockSpec((PEER_HEADS, n_slab, tm), lambda i, j: (0, j, i))
    tok_spec = pl.BlockSpec((PEER_HEADS, PEER_NK, tm), lambda i, j: (0, 0, i))
    return pl.pallas_call(
        functools.partial(_peer_dense_body, n_slab=n_slab),
        out_shape=jax.ShapeDtypeStruct((T, D), F32),
        grid=(T // tm, n_e),
        in_specs=[pl.BlockSpec((tm, D), lambda i, j: (i, 0)),
                  pl.BlockSpec((e_tile, D), lambda i, j: (j, 0)),
                  pl.BlockSpec((1, D, e_tile), lambda i, j: (j, 0, 0)),
                  slab_spec, slab_spec, tok_spec, tok_spec,
                  pl.BlockSpec((tm, D), lambda i, j: (i, 0)),
                  pl.BlockSpec((1, 1, D), lambda i, j: (i // (rows_per_mod // tm), 0, 0)),
                  pl.BlockSpec((1, D), lambda i, j: (0, 0)), pl.BlockSpec((1, D), lambda i, j: (0, 0))],
        out_specs=pl.BlockSpec((tm, D), lambda i, j: (i, 0)),
        scratch_shapes=[pltpu.VMEM((D, tm), F32), pltpu.VMEM((e_tile, tm), BF16)],
        compiler_params=pltpu.CompilerParams(dimension_semantics=("arbitrary", "arbitrary"),
                                             vmem_limit_bytes=V7X_VMEM_LIMIT_BYTES),
        name="peer_dense",
    )(h_bf16, u_bf16, vt_tiles, n1, c1, r2, e2, x, gate, ln_g.reshape(1, D), ln_b.reshape(1, D))


PEER_SLABS = 8


def peer_v_tiles(v_tab):
    e_tile = PEER_SLABS * PEER_NK
    return v_tab.astype(BF16).reshape(v_tab.shape[0] // e_tile, e_tile, v_tab.shape[1]).transpose(0, 2, 1)


def peer_sublayer(x, shift, scale, gate, wq_bf16, keys_bf16, u_bf16, vt_tiles, ln_g, ln_b, rows_per_mod,
                  tm_route=256, tm_dense=512):
    h, n1, c1, r2, e2 = peer_route(x, shift, scale, wq_bf16, keys_bf16, rows_per_mod, tm_route)
    return peer_dense(h, u_bf16, vt_tiles, n1, c1, r2, e2, x, gate, ln_g, ln_b, rows_per_mod, tm_dense,
                      PEER_SLABS)


POOL_HALO = 8
POOL_ROWS = 256


def _pool_body(u_ref, w_ref, scale_ref, o_ref, pad_ref, *, seq_len):
    L = seq_len
    chunk = min(POOL_ROWS, L)
    zeros = jnp.zeros((POOL_HALO, W_GRP), F32)
    pad_ref[0:POOL_HALO, :] = zeros
    pad_ref[POOL_HALO + L:2 * POOL_HALO + L, :] = zeros
    pad_ref[POOL_HALO:POOL_HALO + L, :] = u_ref[0]
    for r0 in range(0, L, chunk):
        t = r0 + lax.broadcasted_iota(jnp.int32, (chunk, POOL_C), 0)
        for j, w in enumerate(POOL_WINDOWS):
            cols = slice(j * POOL_C, (j + 1) * POOL_C)
            acc = None
            for o in range(-(w // 2), w - w // 2):
                s = pad_ref[POOL_HALO + r0 + o:POOL_HALO + r0 + o + chunk, cols]
                acc = s if acc is None else acc + s
            lo = jnp.maximum(t - w // 2, 0)
            hi = jnp.minimum(t - w // 2 + w - 1, L - 1)
            cnt = (hi - lo + 1).astype(F32)
            pooled = acc / cnt - u_ref[0, r0:r0 + chunk, cols]
            y = jnp.dot(pooled.astype(BF16), w_ref[j], preferred_element_type=F32)
            o_ref[0, r0:r0 + chunk, cols] = y * scale_ref[:, cols]


def pool_mix_pallas(u, pool_w, pool_scale):
    B_, L, W = u.shape
    assert max(POOL_WINDOWS) // 2 <= POOL_HALO and L % min(POOL_ROWS, L) == 0
    blk = pl.BlockSpec((1, L, W), lambda b: (b, 0, 0))
    return pl.pallas_call(
        functools.partial(_pool_body, seq_len=L),
        out_shape=jax.ShapeDtypeStruct((B_, L, W), F32),
        grid=(B_,),
        in_specs=[blk, pl.BlockSpec(pool_w.shape, lambda b: (0, 0, 0)), pl.BlockSpec((1, W), lambda b: (0, 0))],
        out_specs=blk,
        scratch_shapes=[pltpu.VMEM((L + 2 * POOL_HALO, W), F32)],
        compiler_params=pltpu.CompilerParams(dimension_semantics=("arbitrary",),
                                             vmem_limit_bytes=V7X_VMEM_LIMIT_BYTES),
        name="pool_mix",
    )(u, pool_w.astype(BF16), pool_scale.reshape(1, W))


NA_TILE_R = 4
NA_TILE = NA_TILE_R * GRID_W
NA_SCALE = NA_HD ** -0.5


def _na_bias_table(rpb, n_rows):
    n_tiles = n_rows // NA_TILE_R
    KR = min(NA_WIN_R, n_rows)
    a = np.array([0, min(2, n_tiles - 1), n_tiles - 1]).reshape(3, 1, 1, 1, 1, 1)
    d = np.arange(3).reshape(1, 3, 1, 1, 1, 1)
    i = np.arange(NA_TILE_R).reshape(1, 1, NA_TILE_R, 1, 1, 1)
    qc = np.arange(GRID_W).reshape(1, 1, 1, GRID_W, 1, 1)
    j = np.arange(NA_TILE_R).reshape(1, 1, 1, 1, NA_TILE_R, 1)
    kc = np.arange(GRID_W).reshape(1, 1, 1, 1, 1, GRID_W)
    qr = NA_TILE_R * a + i
    kr = NA_TILE_R * (a + d - 1) + j
    rs = np.clip(qr - KR // 2, 0, n_rows - KR)
    c0 = np.clip(qc - NA_WIN_C // 2, 0, GRID_W - NA_WIN_C)
    ok = (kr >= rs) & (kr < rs + KR) & (kr >= 0) & (kr < n_rows) & (kc >= c0) & (kc < c0 + NA_WIN_C)
    row_rel = np.clip(kr - qr + NA_WIN_R - 1, 0, 2 * NA_WIN_R - 2)[:, :, :, 0, :, 0]
    col_rel = (np.clip(kc - qc, -(NA_WIN_C - 1), NA_WIN_C - 1) + NA_WIN_C - 1)[0, 0, 0, :, 0, :]
    onehot = (col_rel[None] == np.arange(2 * NA_WIN_C - 1)[:, None, None]).astype(np.float32)
    bias_rc = jnp.einsum('hrc,cqk->hrqk', rpb.astype(F32), onehot, precision=lax.Precision.HIGHEST)
    tab = jnp.stack([bias_rc[:, int(r)] for r in row_rel.reshape(-1)], axis=1)
    tab = tab.reshape(NA_H, 3, 3, NA_TILE_R, NA_TILE_R, GRID_W, GRID_W).transpose(1, 0, 2, 3, 5, 4, 6)
    tab = jnp.where(ok[:, None], tab, -jnp.inf)
    return tab.reshape(3, NA_H, 3, NA_TILE, NA_TILE)


def _na_body(q_ref, k_ref, v_ref, kc_ref, vc_ref, t_ref, o_ref, *, n_tiles, with_grid):
    a = pl.program_id(1)
    nt = (((1,), (1,)), ((), ()))
    for h in range(NA_H):
        hs = slice(h * NA_HD, (h + 1) * NA_HD)
        qh = q_ref[:, hs]
        scores = [lax.dot_general(qh, kc_ref[:, hs], nt, preferred_element_type=F32) * NA_SCALE]
        vals = [vc_ref[:, hs]]
        if with_grid:
            for d in range(3):
                ti = jnp.clip(a + d - 1, 0, n_tiles - 1)
                rows = pl.ds(pl.multiple_of(ti * NA_TILE, NA_TILE), NA_TILE)
                s = lax.dot_general(qh, k_ref[rows, hs], nt, preferred_element_type=F32)
                scores.append(s * NA_SCALE + t_ref[0, h, d])
                vals.append(v_ref[rows, hs])
        m = scores[0].max(axis=-1, keepdims=True)
        for s in scores[1:]:
            m = jnp.maximum(m, s.max(axis=-1, keepdims=True))
        den = None
        acc = None
        for s, vv in zip(scores, vals):
            p = jnp.exp(s - m)
            l = p.sum(axis=-1, keepdims=True)
            o = jnp.dot(p.astype(BF16), vv, preferred_element_type=F32)
            den = l if den is None else den + l
            acc = o if acc is None else acc + o
        o_ref[:, hs] = acc / den


def na_attention(q, k, v, kc, vc, table, seq_len, ctx_len, with_grid):
    W = q.shape[1]
    n_b = kc.shape[0] // ctx_len
    lq = q.shape[0] // n_b
    assert lq % NA_TILE == 0
    q_tiles = lq // NA_TILE
    n_tiles = seq_len // NA_TILE
    pat = lambda b, a: (jnp.where(a == 0, 0, jnp.where(a == n_tiles - 1, 2, 1)), 0, 0, 0, 0)
    qo_spec = pl.BlockSpec((NA_TILE, W), lambda b, a: (b * q_tiles + a, 0))
    return pl.pallas_call(
        functools.partial(_na_body, n_tiles=n_tiles, with_grid=with_grid),
        out_shape=jax.ShapeDtypeStruct(q.shape, F32),
        grid=(n_b, q_tiles),
        in_specs=[qo_spec,
                  pl.BlockSpec((seq_len, W), lambda b, a: (b, 0)), pl.BlockSpec((seq_len, W), lambda b, a: (b, 0)),
                  pl.BlockSpec((ctx_len, W), lambda b, a: (b, 0)), pl.BlockSpec((ctx_len, W), lambda b, a: (b, 0)),
                  pl.BlockSpec((1,) + table.shape[1:], pat)],
        out_specs=qo_spec,
        compiler_params=pltpu.CompilerParams(dimension_semantics=("arbitrary", "arbitrary"),
                                             vmem_limit_bytes=V7X_VMEM_LIMIT_BYTES),
        name="na_attention",
    )(q, k, v, kc, vc, table)


SSM_HALO = 8
SSM_BC = SSM_G * SSM_N


def _ssm_conv_body(prev_ref, cur_ref, next_ref, w_ref, b_ref, o_ref, pad_ref, *, tiles_per_seq):
    pos = pl.program_id(0) % tiles_per_seq
    tm = cur_ref.shape[0]
    pad_ref[0:SSM_HALO, :] = jnp.where(pos == 0, 0.0, prev_ref[...])
    pad_ref[SSM_HALO:SSM_HALO + tm, :] = cur_ref[...]
    pad_ref[SSM_HALO + tm:2 * SSM_HALO + tm, :] = jnp.where(pos == tiles_per_seq - 1, 0.0, next_ref[...])
    lead = (SSM_CONV - 1) // 2
    y = b_ref[...]
    for k in range(SSM_CONV):
        y = y + w_ref[k:k + 1, :] * pad_ref[SSM_HALO - lead + k:SSM_HALO - lead + k + tm, :]
    o_ref[...] = jax.nn.silu(y)


def ssm_conv(xbc, conv_w, conv_b, seq_len, tm):
    T, CH = xbc.shape
    assert seq_len % tm == 0 and tm % SSM_HALO == 0
    hb = tm // SSM_HALO
    n_hb = T // SSM_HALO
    return pl.pallas_call(
        functools.partial(_ssm_conv_body, tiles_per_seq=seq_len // tm),
        out_shape=jax.ShapeDtypeStruct((T, CH), F32),
        grid=(T // tm,),
        in_specs=[pl.BlockSpec((SSM_HALO, CH), lambda i: (jnp.maximum(i * hb - 1, 0), 0)),
                  pl.BlockSpec((tm, CH), lambda i: (i, 0)),
                  pl.BlockSpec((SSM_HALO, CH), lambda i: (jnp.minimum((i + 1) * hb, n_hb - 1), 0)),
                  pl.BlockSpec((SSM_CONV, CH), lambda i: (0, 0)), pl.BlockSpec((1, CH), lambda i: (0, 0))],
        out_specs=pl.BlockSpec((tm, CH), lambda i: (i, 0)),
        scratch_shapes=[pltpu.VMEM((tm + 2 * SSM_HALO, CH), F32)],
        compiler_params=pltpu.CompilerParams(dimension_semantics=("arbitrary",),
                                             vmem_limit_bytes=V7X_VMEM_LIMIT_BYTES),
        name="ssm_conv",
    )(xbc, xbc, xbc, conv_w, conv_b.reshape(1, CH))


def _ssd_body(xbc_ref, dt_ref, bias_ref, a_ref, h0_ref, y_ref, hfin_ref, h_scr, *, di, rev):
    ci = pl.program_id(1)
    Q = xbc_ref.shape[0]

    @pl.when(ci == 0)
    def _():
        h_scr[...] = h0_ref[0]

    dt_all = jax.nn.softplus(dt_ref[...] + bias_ref[...])
    a_cs = dt_all * a_ref[...]
    row_id = lax.broadcasted_iota(jnp.int32, a_cs.shape, 0)
    sh = 1
    while sh < Q:
        if rev:
            a_cs = a_cs + jnp.where(row_id < Q - sh, pltpu.roll(a_cs, Q - sh, 0), 0.0)
        else:
            a_cs = a_cs + jnp.where(row_id >= sh, pltpu.roll(a_cs, sh, 0), 0.0)
        sh *= 2
    a_cs_t = a_cs.T
    a_tot = a_cs[0:1, :] if rev else a_cs[Q - 1:Q, :]
    l_id = lax.broadcasted_iota(jnp.int32, (Q, Q), 0)
    s_id = lax.broadcasted_iota(jnp.int32, (Q, Q), 1)
    causal = (l_id <= s_id) if rev else (l_id >= s_id)
    nt = (((1,), (1,)), ((), ()))
    tn = (((0,), (0,)), ((), ()))
    cb = []
    for g in range(SSM_G):
        bg = xbc_ref[:, W_GRP + g * SSM_N:W_GRP + (g + 1) * SSM_N].astype(BF16)
        cg = xbc_ref[:, W_GRP + SSM_BC + g * SSM_N:W_GRP + SSM_BC + (g + 1) * SSM_N].astype(BF16)
        cb.append((bg, cg, lax.dot_general(cg, bg, nt, preferred_element_type=F32)))
    for h in range(SSM_H):
        c = di * SSM_H + h
        bg, cg, cbg = cb[h // (SSM_H // SSM_G)]
        col = a_cs[:, c:c + 1]
        lm = jnp.exp(jnp.where(causal, col - a_cs_t[c:c + 1, :], -jnp.inf))
        xh = xbc_ref[:, h * SSM_HD:(h + 1) * SSM_HD] * dt_all[:, c:c + 1]
        hp = h_scr[h]
        yd = jnp.dot((cbg * lm).astype(BF16), xh.astype(BF16), preferred_element_type=F32)
        yo = lax.dot_general(cg, hp.astype(BF16), nt, preferred_element_type=F32) * jnp.exp(col)
        y_ref[:, h * SSM_HD:(h + 1) * SSM_HD] = yd + yo
        tot = a_tot[:, c:c + 1]
        xd = (xh * jnp.exp(tot - col)).astype(BF16)
        h_scr[h] = jnp.exp(tot) * hp + lax.dot_general(xd, bg, tn, preferred_element_type=F32)

    @pl.when(ci == pl.num_programs(1) - 1)
    def _():
        hfin_ref[0] = h_scr[...]


def ssd_scan(xbc_act, dt_raw, bias128, a128, h0, seq_len, di, rev):
    T = xbc_act.shape[0]
    n_b = T // seq_len
    Q = min(SSM_CHUNK, seq_len)
    nc = seq_len // Q
    chunk = (lambda b, i: (b * nc + nc - 1 - i, 0)) if rev else (lambda b, i: (b * nc + i, 0))
    vec = pl.BlockSpec((1, V7X_LANES), lambda b, i: (0, 0))
    st = pl.BlockSpec((1, SSM_H, SSM_HD, SSM_N), lambda b, i: (b, 0, 0, 0))
    return pl.pallas_call(
        functools.partial(_ssd_body, di=di, rev=rev),
        out_shape=[jax.ShapeDtypeStruct((T, W_GRP), F32), jax.ShapeDtypeStruct(h0.shape, F32)],
        grid=(n_b, nc),
        in_specs=[pl.BlockSpec((Q, SSM_CONV_CH), chunk), pl.BlockSpec((Q, V7X_LANES), chunk), vec, vec, st],
        out_specs=[pl.BlockSpec((Q, W_GRP), chunk), st],
        scratch_shapes=[pltpu.VMEM((SSM_H, SSM_HD, SSM_N), F32)],
        compiler_params=pltpu.CompilerParams(dimension_semantics=("arbitrary", "arbitrary"),
                                             vmem_limit_bytes=V7X_VMEM_LIMIT_BYTES),
        name="ssd_scan",
    )(xbc_act, dt_raw, bias128, a128, h0)


def _ssm_out_body(xbc_ref, yf_ref, yb_ref, z_ref, d_ref, nw_ref, o_ref):
    y = d_ref[...] * xbc_ref[:, :W_GRP] + yf_ref[...] + yb_ref[...]
    g = y * jax.nn.silu(z_ref[...])
    gw = W_GRP // SSM_G
    for k in range(SSM_G):
        gk = g[:, k * gw:(k + 1) * gw]
        r = lax.rsqrt(jnp.mean(gk * gk, -1, keepdims=True) + LN_EPS)
        o_ref[:, k * gw:(k + 1) * gw] = gk * r * nw_ref[:, k * gw:(k + 1) * gw]


def ssm_out(xbc_act, yf, yb, z, d512, norm_w, tm):
    T = z.shape[0]
    assert T % tm == 0
    row = pl.BlockSpec((tm, W_GRP), lambda i: (i, 0))
    vec = pl.BlockSpec((1, W_GRP), lambda i: (0, 0))
    return pl.pallas_call(
        _ssm_out_body,
        out_shape=jax.ShapeDtypeStruct((T, W_GRP), F32),
        grid=(T // tm,),
        in_specs=[pl.BlockSpec((tm, SSM_CONV_CH), lambda i: (i, 0)), row, row, row, vec, vec],
        out_specs=row,
        compiler_params=pltpu.CompilerParams(dimension_semantics=("arbitrary",),
                                             vmem_limit_bytes=V7X_VMEM_LIMIT_BYTES),
        name="ssm_out",
    )(xbc_act, yf, yb, z, d512, norm_w.reshape(1, W_GRP))


def ssm_mixer_pallas(z_l, xbc_l, dt_l, z_c, xbc_c, dt_c, conv_w, conv_b, dt_bias, a_log, d, norm_w,
                     seq_len, ctx_len, with_ctx_out, tm=256):
    n_b = z_l.shape[0] // seq_len
    pad = V7X_LANES - 2 * SSM_H
    bias128 = jnp.pad(dt_bias.astype(F32).reshape(1, 2 * SSM_H), ((0, 0), (0, pad)))
    a128 = jnp.pad(-jnp.exp(a_log.astype(F32)).reshape(1, 2 * SSM_H), ((0, 0), (0, pad)))
    d512 = jnp.repeat(d.astype(F32), SSM_HD).reshape(1, W_GRP)
    act_c = ssm_conv(xbc_c, conv_w, conv_b, ctx_len, min(tm, ctx_len))
    act_l = ssm_conv(xbc_l, conv_w, conv_b, seq_len, tm)
    zero = jnp.zeros((n_b, SSM_H, SSM_HD, SSM_N), F32)
    y_c, y_l = [], []
    for di in range(2):
        yc, hc = ssd_scan(act_c, dt_c, bias128, a128, zero, ctx_len, di, di == 1)
        yl, _ = ssd_scan(act_l, dt_l, bias128, a128, hc, seq_len, di, di == 1)
        y_c.append(yc)
        y_l.append(yl)
    out_l = ssm_out(act_l, y_l[0], y_l[1], z_l, d512, norm_w, tm)
    out_c = ssm_out(act_c, y_c[0], y_c[1], z_c, d512, norm_w, min(tm, z_c.shape[0])) if with_ctx_out else None
    return out_l, out_c


def s5_discretize(a_re, a_im, log_dt, b_re, b_im):
    a_re, a_im = a_re.astype(F32), a_im.astype(F32)
    b_re, b_im = b_re.astype(F32), b_im.astype(F32)
    dt = jnp.exp(log_dt.astype(F32))[:, None]
    mag = jnp.exp(a_re * dt)
    ab_re = mag * jnp.cos(a_im * dt)
    ab_im = mag * jnp.sin(a_im * dt)
    den = a_re * a_re + a_im * a_im
    f_re = ((ab_re - 1) * a_re + ab_im * a_im) / den
    f_im = (ab_im * a_re - (ab_re - 1) * a_im) / den
    bb_re = f_re[..., None] * b_re - f_im[..., None] * b_im
    bb_im = f_re[..., None] * b_im + f_im[..., None] * b_re
    return ab_re, ab_im, bb_re, bb_im


def _cscan_combine(e1, e2):
    a1r, a1i, b1r, b1i = e1
    a2r, a2i, b2r, b2i = e2
    return (a2r * a1r - a2i * a1i, a2r * a1i + a2i * a1r,
            a2r * b1r - a2i * b1i + b2r, a2r * b1i + a2i * b1r + b2i)


def s5_states(u, ab_re, ab_im, bb_re, bb_im, h0_re, h0_im):
    bu_re = jnp.einsum('gnp,blgp->blgn', bb_re, u)
    bu_im = jnp.einsum('gnp,blgp->blgn', bb_im, u)
    bu_re = bu_re.at[:, 0].add(ab_re * h0_re - ab_im * h0_im)
    bu_im = bu_im.at[:, 0].add(ab_re * h0_im + ab_im * h0_re)
    a_re = jnp.broadcast_to(ab_re, bu_re.shape)
    a_im = jnp.broadcast_to(ab_im, bu_im.shape)
    _, _, h_re, h_im = lax.associative_scan(_cscan_combine, (a_re, a_im, bu_re, bu_im), axis=1)
    return h_re, h_im


def s5_readout(c_re, c_im, h_re, h_im):
    return (jnp.einsum('gpn,blgn->blgp', c_re.astype(F32), h_re)
            - jnp.einsum('gpn,blgn->blgp', c_im.astype(F32), h_im))


def s5_glu(y, w_glu, b_glu, dtype):
    B_, L = y.shape[:2]
    g = jax.nn.gelu(y.reshape(B_, L, W_GRP))
    return (g * jax.nn.sigmoid(g @ w_glu.astype(F32) + b_glu.astype(F32))).astype(dtype)


def s5_mixer(u_lat, u_ctx, a_re, a_im, log_dt, b_re, b_im, c_re, c_im, d, w_glu, b_glu, with_ctx_out):
    B_ = u_lat.shape[0]
    ul = u_lat.astype(F32).reshape(B_, -1, S5_G, S5_P)
    uc = u_ctx.astype(F32).reshape(B_, -1, S5_G, S5_P)
    dg = d.astype(F32).reshape(S5_G, S5_P)
    zero = jnp.zeros((B_, S5_G, S5_N), F32)
    y_lat = dg * ul
    y_ctx = dg * uc if with_ctx_out else None
    for di in range(2):
        rev = di == 1
        ab_re, ab_im, bb_re, bb_im = s5_discretize(a_re[di], a_im[di], log_dt[di], b_re[di], b_im[di])
        hc_re, hc_im = s5_states(_flip(uc, rev), ab_re, ab_im, bb_re, bb_im, zero, zero)
        hl_re, hl_im = s5_states(_flip(ul, rev), ab_re, ab_im, bb_re, bb_im, hc_re[:, -1], hc_im[:, -1])
        y_lat = y_lat + _flip(s5_readout(c_re[di], c_im[di], hl_re, hl_im), rev)
        if with_ctx_out:
            y_ctx = y_ctx + _flip(s5_readout(c_re[di], c_im[di], hc_re, hc_im), rev)
    out_lat = s5_glu(y_lat, w_glu, b_glu, u_lat.dtype)
    out_ctx = s5_glu(y_ctx, w_glu, b_glu, u_ctx.dtype) if with_ctx_out else None
    return out_lat, out_ctx


def pool_mix(u, pool_w, pool_scale):
    B_, L, _ = u.shape
    uf = u.astype(F32)
    csum = jnp.pad(jnp.cumsum(uf, axis=1), ((0, 0), (1, 0), (0, 0)))
    t = jnp.arange(L)
    outs = []
    for j, w in enumerate(POOL_WINDOWS):
        lo = jnp.clip(t - w // 2, 0, L - 1)
        hi = jnp.clip(t - w // 2 + w - 1, 0, L - 1)
        ch = slice(j * POOL_C, (j + 1) * POOL_C)
        cnt = (hi - lo + 1).astype(F32)[None, :, None]
        outs.append((csum[:, hi + 1, ch] - csum[:, lo, ch]) / cnt - uf[:, :, ch])
    pooled = jnp.stack(outs, axis=2)
    y = jnp.einsum('blgc,gcd->blgd', pooled, pool_w.astype(F32)).reshape(B_, L, W_GRP)
    return (y * pool_scale.astype(F32)).astype(u.dtype)


def na_latent(q, k, v, k_ctx, v_ctx, rpb):
    B_, L, _ = q.shape
    R = L // GRID_W
    KR = min(NA_WIN_R, R)
    KC = NA_WIN_C
    shp = (B_, R, GRID_W, NA_H, NA_HD)
    q, k, v = q.reshape(shp), k.reshape(shp), v.reshape(shp)
    r = jnp.arange(R)
    row_idx = jnp.clip(r - KR // 2, 0, R - KR)[:, None] + jnp.arange(KR)[None, :]
    k_rows = k[:, row_idx]
    v_rows = v[:, row_idx]
    col = jnp.arange(GRID_W)
    c0 = jnp.clip(col - KC // 2, 0, GRID_W - KC)
    in_win = (col[None, :] >= c0[:, None]) & (col[None, :] < c0[:, None] + KC)
    col_rel = jnp.clip(col[None, :] - col[:, None], -(KC - 1), KC - 1) + KC - 1
    row_rel = row_idx - r[:, None] + NA_WIN_R - 1
    bias = rpb.astype(F32)[:, row_rel][..., col_rel]
    bias = bias.transpose(1, 0, 3, 2, 4)
    scale = NA_HD ** -0.5
    s_nb = jnp.einsum('brqhd,brkwhd->brhqkw', q, k_rows).astype(F32) * scale + bias
    s_nb = jnp.where(in_win[:, None, :], s_nb, -jnp.inf).reshape(B_, R, NA_H, GRID_W, KR * GRID_W)
    s_cx = jnp.einsum('brqhd,bchd->brhqc', q, k_ctx).astype(F32) * scale
    p = jax.nn.softmax(jnp.concatenate([s_nb, s_cx], axis=-1), axis=-1).astype(v.dtype)
    p_nb = p[..., :KR * GRID_W].reshape(B_, R, NA_H, GRID_W, KR, GRID_W)
    p_cx = p[..., KR * GRID_W:]
    o = (jnp.einsum('brhqkw,brkwhd->brqhd', p_nb, v_rows)
         + jnp.einsum('brhqc,bchd->brqhd', p_cx, v_ctx))
    return o.reshape(B_, L, W_GRP)


def ctx_attn(q, k, v):
    s = jnp.einsum('bqhd,bkhd->bhqk', q, k).astype(F32) * NA_HD ** -0.5
    p = jax.nn.softmax(s, axis=-1).astype(v.dtype)
    return jnp.einsum('bhqk,bkhd->bqhd', p, v)


def na_mixer(q_l, k_l, v_l, q_c, k_c, v_c, rpb, with_ctx_out):
    B_, C_ = k_c.shape[:2]
    kc = k_c.reshape(B_, C_, NA_H, NA_HD)
    vc = v_c.reshape(B_, C_, NA_H, NA_HD)
    out_l = na_latent(q_l, k_l, v_l, kc, vc, rpb)
    out_c = (ctx_attn(q_c.reshape(B_, C_, NA_H, NA_HD), kc, vc).reshape(B_, C_, W_GRP)
             if with_ctx_out else None)
    return out_l, out_c


def dwconv(x, w, b):
    y = lax.conv_general_dilated(x, w[:, None, :].astype(x.dtype), window_strides=(1,),
                                 padding=[((SSM_CONV - 1) // 2, SSM_CONV // 2)],
                                 dimension_numbers=('NWC', 'WIO', 'NWC'),
                                 feature_group_count=x.shape[-1])
    return y + b


def segsum(x):
    T = x.shape[-1]
    cs = jnp.cumsum(x, axis=-1)
    d = cs[..., :, None] - cs[..., None, :]
    return jnp.where(jnp.tril(jnp.ones((T, T), bool)), d, -jnp.inf)


def ssd(X, A, Bm, Cm, h0):
    b, L, H, P = X.shape
    N = Bm.shape[-1]
    nc = L // SSM_CHUNK
    X = X.reshape(b, nc, SSM_CHUNK, H, P)
    Bm = Bm.reshape(b, nc, SSM_CHUNK, H, N)
    Cm = Cm.reshape(b, nc, SSM_CHUNK, H, N)
    A = A.reshape(b, nc, SSM_CHUNK, H).transpose(0, 3, 1, 2)
    A_cs = jnp.cumsum(A, axis=-1)
    cb = jnp.einsum('bclhn,bcshn->bhcls', Cm, Bm)
    y_diag = jnp.einsum('bhcls,bcshp->bclhp', cb * jnp.exp(segsum(A)), X)
    decay_states = jnp.exp(A_cs[..., -1:] - A_cs).transpose(0, 2, 3, 1)[..., None]
    states = jnp.einsum('bclhn,bclhp->bchpn', Bm, X * decay_states)
    states = jnp.concatenate([h0[:, None], states], axis=1)
    chunk_decay = jnp.exp(segsum(jnp.pad(A_cs[..., -1], ((0, 0), (0, 0), (1, 0)))))
    new_states = jnp.einsum('bhzc,bchpn->bzhpn', chunk_decay, states)
    prev, final = new_states[:, :-1], new_states[:, -1]
    y_off = (jnp.einsum('bclhn,bchpn->bclhp', Cm, prev)
             * jnp.exp(A_cs).transpose(0, 2, 3, 1)[..., None])
    return (y_diag + y_off).reshape(b, L, H, P), final


def ssm_prepare(pxbc, pdt, conv_w, conv_b, dt_bias):
    B_, L, _ = pxbc.shape
    xbc = jax.nn.silu(dwconv(pxbc, conv_w, conv_b).astype(F32))
    hpg = SSM_H // SSM_G
    xs = xbc[..., :W_GRP].reshape(B_, L, SSM_H, SSM_HD)
    bm = jnp.repeat(xbc[..., W_GRP:W_GRP + SSM_G * SSM_N].reshape(B_, L, SSM_G, SSM_N), hpg, axis=2)
    cm = jnp.repeat(xbc[..., W_GRP + SSM_G * SSM_N:].reshape(B_, L, SSM_G, SSM_N), hpg, axis=2)
    dt = jax.nn.softplus(pdt.astype(F32).reshape(B_, L, 2, SSM_H) + dt_bias.astype(F32))
    return xs, bm, cm, dt


def ssm_direction(xs, bm, cm, dt, a, h0, rev):
    y, h_fin = ssd(_flip(xs * dt[..., None], rev), _flip(dt * a, rev), _flip(bm, rev), _flip(cm, rev), h0)
    return _flip(y, rev), h_fin


def ssm_gate_norm(y, z, norm_w):
    B_, L = z.shape[:2]
    g = (y.reshape(B_, L, W_GRP) * jax.nn.silu(z.astype(F32))).reshape(B_, L, SSM_G, W_GRP // SSM_G)
    g = g * lax.rsqrt(jnp.mean(jnp.square(g), -1, keepdims=True) + LN_EPS)
    return (g.reshape(B_, L, W_GRP) * norm_w.astype(F32)).astype(z.dtype)


def ssm_mixer(z_l, xbc_l, dtr_l, z_c, xbc_c, dtr_c, conv_w, conv_b, dt_bias, a_log, d, norm_w, with_ctx_out):
    xs_l, bm_l, cm_l, dt_l = ssm_prepare(xbc_l, dtr_l, conv_w, conv_b, dt_bias)
    xs_c, bm_c, cm_c, dt_c = ssm_prepare(xbc_c, dtr_c, conv_w, conv_b, dt_bias)
    a = -jnp.exp(a_log.astype(F32))
    dh = d.astype(F32)[:, None]
    zero = jnp.zeros((xs_c.shape[0], SSM_H, SSM_HD, SSM_N), F32)
    y_l = dh * xs_l
    y_c = dh * xs_c if with_ctx_out else None
    for di in range(2):
        rev = di == 1
        yc, hc = ssm_direction(xs_c, bm_c, cm_c, dt_c[:, :, di], a[di], zero, rev)
        yl, _ = ssm_direction(xs_l, bm_l, cm_l, dt_l[:, :, di], a[di], hc, rev)
        y_l = y_l + yl
        if with_ctx_out:
            y_c = y_c + yc
    out_l = ssm_gate_norm(y_l, z_l, norm_w)
    out_c = ssm_gate_norm(y_c, z_c, norm_w) if with_ctx_out else None
    return out_l, out_c


def peer_ffn(t, w_q, sub_keys, u_tab, v_tab):
    n_tok = t.shape[0]
    q = (t @ w_q).reshape(n_tok, PEER_HEADS, 2, PEER_QDIM // 2)
    s = jnp.einsum('thsd,hskd->thsk', q, sub_keys).astype(F32)
    sv, si = lax.top_k(s, PEER_TOPK)
    cand_s = (sv[:, :, 0, :, None] + sv[:, :, 1, None, :]).reshape(n_tok, PEER_HEADS, PEER_TOPK * PEER_TOPK)
    cand_i = (si[:, :, 0, :, None] * PEER_NK + si[:, :, 1, None, :]).reshape(n_tok, PEER_HEADS, PEER_TOPK * PEER_TOPK)
    top_s, pos = lax.top_k(cand_s, PEER_TOPK)
    idx = jnp.take_along_axis(cand_i, pos, axis=-1).reshape(n_tok, PEER_HEADS * PEER_TOPK)
    gate = jax.nn.softmax(top_s, axis=-1).reshape(n_tok, PEER_HEADS * PEER_TOPK)
    n_blk = n_tok // PEER_BLOCK

    def expert_block(args):
        xb, ib, gb = args
        act = jnp.einsum('tkd,td->tk', u_tab[ib], xb).astype(F32)
        return jnp.einsum('tk,tkd->td', (jax.nn.gelu(act) * gb).astype(xb.dtype), v_tab[ib])

    out = lax.map(expert_block, (t.reshape(n_blk, PEER_BLOCK, -1),
                                 idx.reshape(n_blk, PEER_BLOCK, -1),
                                 gate.reshape(n_blk, PEER_BLOCK, -1)))
    return out.reshape(n_tok, -1)


def _mixers(p_l, p_c, prm, with_ctx_out):
    s5_l, pool_l, q_l, k_l, v_l, z_l, xbc_l, dt_l, s5r_l = p_l
    s5_c, pool_c, q_c, k_c, v_c, z_c, xbc_c, dt_c, s5r_c = p_c
    ya_l, ya_c = s5_mixer_pallas(s5_l, s5_c, s5r_l, s5r_c, prm["s5_a_re"], prm["s5_a_im"], prm["s5_log_dt"],
                                 prm["s5_b_re"], prm["s5_b_im"], prm["s5_c_re"], prm["s5_c_im"], prm["s5_d"],
                                 prm["s5_w_glu"], prm["s5_b_glu"], with_ctx_out)
    yb_l = pool_mix_pallas(pool_l, prm["pool_w"], prm["pool_scale"])
    yb_c = pool_mix_pallas(pool_c, prm["pool_w"], prm["pool_scale"]) if with_ctx_out else None
    B_, L, _ = s5_l.shape
    C_ = s5_c.shape[1]
    flat = lambda t: t.reshape(-1, t.shape[-1])
    table = _na_bias_table(prm["na_rpb"], L // GRID_W)
    yc_l = na_attention(flat(q_l), flat(k_l), flat(v_l), flat(k_c), flat(v_c), table, L, C_, True)
    yc_c = (na_attention(flat(q_c), flat(k_c), flat(v_c), flat(k_c), flat(v_c), table, C_, C_, False)
            if with_ctx_out else None)
    yd_l, yd_c = ssm_mixer_pallas(flat(z_l), flat(xbc_l), flat(dt_l), flat(z_c), flat(xbc_c), flat(dt_c),
                                  prm["ssm_conv_w"], prm["ssm_conv_b"], prm["ssm_dt_bias"], prm["ssm_a_log"],
                                  prm["ssm_d"], prm["ssm_norm_w"], L, C_, with_ctx_out)
    return (ya_l, yb_l, yc_l, yd_l), (ya_c, yb_c, yc_c, yd_c)


def kernel(x, c, ctx, c_ctx, w_ada, b_ada, w_in, w_out, s5_a_re, s5_a_im, s5_log_dt, s5_b_re, s5_b_im, s5_c_re, s5_c_im, s5_d, s5_w_glu, s5_b_glu, pool_w, pool_scale, na_rpb, ssm_conv_w, ssm_conv_b, ssm_dt_bias, ssm_a_log, ssm_d, ssm_norm_w, ln1_g, ln1_b, ln2_g, ln2_b, peer_w_q, peer_sub_keys, peer_u, peer_v):
    B_, L, D = x.shape
    C_ = ctx.shape[1]
    x_lat = x.reshape(B_ * L, D)
    x_ctx = ctx.reshape(B_ * C_, D)
    act_lat = jax.nn.silu(c)
    act_ctx = jax.nn.silu(c_ctx)
    dt_pad = V7X_LANES - IN_SPLITS[-1]
    splits = IN_SPLITS[:-1] + (V7X_LANES,)
    in_dtypes = (F32, F32, BF16, BF16, BF16, F32, F32, F32)
    TM = 256
    for l in range(DEPTH):
        last = l == DEPTH - 1
        prm = dict(s5_a_re=s5_a_re[l], s5_a_im=s5_a_im[l], s5_log_dt=s5_log_dt[l], s5_b_re=s5_b_re[l],
                   s5_b_im=s5_b_im[l], s5_c_re=s5_c_re[l], s5_c_im=s5_c_im[l], s5_d=s5_d[l],
                   s5_w_glu=s5_w_glu[l], s5_b_glu=s5_b_glu[l], pool_w=pool_w[l], pool_scale=pool_scale[l],
                   na_rpb=na_rpb[l], ssm_conv_w=ssm_conv_w[l], ssm_conv_b=ssm_conv_b[l],
                   ssm_dt_bias=ssm_dt_bias[l], ssm_a_log=ssm_a_log[l], ssm_d=ssm_d[l],
                   ssm_norm_w=ssm_norm_w[l])
        m_lat = (act_lat @ w_ada[l] + b_ada[l]).reshape(B_, 6, 1, D)
        m_ctx = (act_ctx @ w_ada[l] + b_ada[l]).reshape(1, 6, 1, D)
        w_in_b = jnp.pad(w_in[l], ((0, 0), (0, dt_pad))).astype(BF16)
        w_out_b = w_out[l].astype(BF16)

        p_l = modulated_matmul(x_lat, m_lat[:, 0], m_lat[:, 1], w_in_b, splits, in_dtypes, L, L, TM)
        p_c = modulated_matmul(x_ctx, m_ctx[:, 0], m_ctx[:, 1], w_in_b, splits, in_dtypes, B_ * C_, C_, TM)
        p_l = [a.reshape(B_, L, -1) for a in p_l]
        p_c = [a.reshape(B_, C_, -1) for a in p_c]
        y_l, y_c = _mixers(p_l, p_c, prm, not last)

        x_lat = proj_residual_ln([a.reshape(B_ * L, -1) for a in y_l], x_lat, m_lat[:, 2], w_out_b,
                                 ln1_g[l], ln1_b[l], L, TM)
        wq_b = peer_w_q[l].astype(BF16)
        keys_b = peer_sub_keys[l].reshape(2 * PEER_HEADS, PEER_NK, PEER_HALF).astype(BF16)
        u_b = peer_u[l].astype(BF16)
        vt_b = peer_v_tiles(peer_v[l])
        x_lat = peer_sublayer(x_lat, m_lat[:, 3], m_lat[:, 4], m_lat[:, 5], wq_b, keys_b, u_b, vt_b,
                              ln2_g[l], ln2_b[l], L)
        if not last:
            x_ctx = proj_residual_ln([a.reshape(B_ * C_, -1) for a in y_c], x_ctx, m_ctx[:, 2], w_out_b,
                                     ln1_g[l], ln1_b[l], B_ * C_, TM)
            x_ctx = peer_sublayer(x_ctx, m_ctx[:, 3], m_ctx[:, 4], m_ctx[:, 5], wq_b, keys_b, u_b, vt_b,
                                  ln2_g[l], ln2_b[l], B_ * C_)
    return x_lat.reshape(B_, L, D)
```

```python
import functools

import jax
import jax.numpy as jnp
import numpy as np
from jax import lax
from jax.experimental import pallas as pl
from jax.experimental.pallas import tpu as pltpu

D_MODEL = 2048
BATCH = 4
SEQ = 4096
DEPTH = 2

GRID_W = 64
CTX_LEN = 256
N_MIXERS = 4
D_MIX = D_MODEL
W_GRP = D_MIX // N_MIXERS

S5_P = 16
S5_G = W_GRP // S5_P
S5_N = 64

POOL_WINDOWS = (2, 4, 8, 16)
POOL_C = W_GRP // len(POOL_WINDOWS)

NA_HD = 64
NA_H = W_GRP // NA_HD
NA_WIN_R = 8
NA_WIN_C = 16

SSM_HD = 64
SSM_H = W_GRP // SSM_HD
SSM_G = 2
SSM_N = 128
SSM_CONV = 4
SSM_CHUNK = 128
SSM_CONV_CH = W_GRP + 2 * SSM_G * SSM_N

PEER_HEADS = 8
PEER_NK = 128
PEER_NE = PEER_NK * PEER_NK
PEER_QDIM = 256
PEER_TOPK = 16
PEER_BLOCK = 128

IN_SPLITS = (W_GRP, W_GRP, W_GRP, W_GRP, W_GRP, W_GRP, SSM_CONV_CH, 2 * SSM_H)
D_IN = sum(IN_SPLITS)

DEEPNORM_ALPHA = (2 * DEPTH) ** 0.25
DEEPNORM_BETA = (8 * DEPTH) ** -0.25
LN_EPS = 1e-5
F32 = jnp.float32
BF16 = jnp.bfloat16

V7X_LANES = 128
V7X_VMEM_BYTES = 64 * 1024 * 1024
V7X_VMEM_LIMIT_BYTES = V7X_VMEM_BYTES * 7 // 8

ROW_TILE = 256
S5_STEPS = 64
S5_GLU_TILE = 256
PEER_ROUTE_TILE = 256
PEER_DENSE_TILE = 512


ADA_ROWS = 8
ADA_TN = 1536


def _ada_body(c_ref, w_ref, b_ref, o_ref):
    act = jax.nn.silu(c_ref[...]).astype(BF16)
    o_ref[...] = jnp.dot(act, w_ref[...].astype(BF16), preferred_element_type=F32) + b_ref[...]


def ada_modulation(c8, w, b):
    R, D = c8.shape
    N = w.shape[1]
    assert N % ADA_TN == 0
    return pl.pallas_call(
        _ada_body,
        out_shape=jax.ShapeDtypeStruct((R, N), F32),
        grid=(N // ADA_TN,),
        in_specs=[pl.BlockSpec((R, D), lambda j: (0, 0)), pl.BlockSpec((D, ADA_TN), lambda j: (0, j)),
                  pl.BlockSpec((1, ADA_TN), lambda j: (0, j))],
        out_specs=pl.BlockSpec((R, ADA_TN), lambda j: (0, j)),
        compiler_params=pltpu.CompilerParams(dimension_semantics=("arbitrary",),
                                             vmem_limit_bytes=V7X_VMEM_LIMIT_BYTES),
        name="ada_modulation",
    )(c8, w, b.reshape(1, N))
def _modmm_body(x_ref, shift_ref, scale_ref, w_ref, *out_refs, col_splits):
    xm = (x_ref[...] * (1.0 + scale_ref[0]) + shift_ref[0]).astype(BF16)
    o = 0
    for j, (ref, n) in enumerate(zip(out_refs, col_splits)):
        val = jnp.dot(xm, w_ref[:, o:o + n], preferred_element_type=F32)
        ref[...] = val.astype(ref.dtype)
        if j == 0:
            tm = val.shape[0]
            r = lax.broadcasted_iota(jnp.int32, (tm, tm), 0)
            c = lax.broadcasted_iota(jnp.int32, (tm, tm), 1)
            flip = jnp.where(r + c == tm - 1, 1.0, 0.0).astype(BF16)
            out_refs[-1][...] = jnp.dot(flip, val.astype(BF16), preferred_element_type=F32).astype(BF16)
        o += n


def modulated_matmul(x, shift, scale, w_bf16, col_splits, out_dtypes, rows_per_mod, seq_len, tm):
    T, K = x.shape
    assert T % tm == 0 and rows_per_mod % tm == 0 and seq_len % tm == 0
    tiles_per_mod = rows_per_mod // tm
    tps = seq_len // tm
    n_tot = sum(col_splits)
    assert w_bf16.shape == (K, n_tot)
    mod_spec = pl.BlockSpec((1, 1, K), lambda i: (i // tiles_per_mod, 0, 0))
    shapes = [jax.ShapeDtypeStruct((T, n), dt) for n, dt in zip(col_splits, out_dtypes, strict=True)]
    specs = [pl.BlockSpec((tm, n), lambda i: (i, 0)) for n in col_splits]
    shapes.append(jax.ShapeDtypeStruct((T, col_splits[0]), BF16))
    specs.append(pl.BlockSpec((tm, col_splits[0]), lambda i: ((i // tps) * tps + tps - 1 - i % tps, 0)))
    return pl.pallas_call(
        functools.partial(_modmm_body, col_splits=tuple(col_splits)),
        out_shape=shapes,
        grid=(T // tm,),
        in_specs=[pl.BlockSpec((tm, K), lambda i: (i, 0)), mod_spec, mod_spec,
                  pl.BlockSpec((K, n_tot), lambda i: (0, 0))],
        out_specs=specs,
        compiler_params=pltpu.CompilerParams(dimension_semantics=("arbitrary",),
                                             vmem_limit_bytes=V7X_VMEM_LIMIT_BYTES),
        name="modulated_matmul",
    )(x, shift, scale, w_bf16)


def _proj_ln_body(*refs, n_parts, alpha):
    part_refs = refs[:n_parts]
    x_ref, gate_ref, w_ref, g_ref, b_ref, o_ref = refs[n_parts:]
    acc = None
    o = 0
    for pr in part_refs:
        n = pr.shape[-1]
        d = jnp.dot(pr[...].astype(BF16), w_ref[o:o + n, :], preferred_element_type=F32)
        acc = d if acc is None else acc + d
        o += n
    h = alpha * x_ref[...] + gate_ref[0] * acc
    mu = jnp.mean(h, -1, keepdims=True)
    hc = h - mu
    var = jnp.mean(hc * hc, -1, keepdims=True)
    o_ref[...] = hc * lax.rsqrt(var + LN_EPS) * g_ref[...] + b_ref[...]


def proj_residual_ln(parts, x, gate, w_bf16, g, b, rows_per_mod, tm):
    T, D = x.shape
    assert T % tm == 0 and rows_per_mod % tm == 0
    tiles_per_mod = rows_per_mod // tm
    k_tot = sum(p.shape[-1] for p in parts)
    assert w_bf16.shape == (k_tot, D)
    row = lambda n: pl.BlockSpec((tm, n), lambda i: (i, 0))
    vec = pl.BlockSpec((1, D), lambda i: (0, 0))
    return pl.pallas_call(
        functools.partial(_proj_ln_body, n_parts=len(parts), alpha=DEEPNORM_ALPHA),
        out_shape=jax.ShapeDtypeStruct((T, D), F32),
        grid=(T // tm,),
        in_specs=[row(p.shape[-1]) for p in parts] + [
            row(D), pl.BlockSpec((1, 1, D), lambda i: (i // tiles_per_mod, 0, 0)),
            pl.BlockSpec((k_tot, D), lambda i: (0, 0)), vec, vec],
        out_specs=row(D),
        compiler_params=pltpu.CompilerParams(dimension_semantics=("arbitrary",),
                                             vmem_limit_bytes=V7X_VMEM_LIMIT_BYTES),
        name="proj_residual_ln",
    )(*parts, x, gate, w_bf16, g.reshape(1, D), b.reshape(1, D))


S5_CHAINS = 2 * BATCH
S5_STATE = S5_G * S5_N
S5_SCAN_COLS = 512
S5_SB_IN = S5_SCAN_COLS // S5_N * S5_P


def _s5_scan_body(u_ref, wb_ref, wc_ref, are_ref, aim_ref, y_ref, bu_ref, h_ref, *, steps):
    rows = steps * S5_CHAINS

    @pl.when(pl.program_id(0) == 0)
    def _():
        h_ref[...] = jnp.zeros_like(h_ref)

    u = u_ref[...]
    chain = lax.broadcasted_iota(jnp.int32, u.shape, 0) % S5_CHAINS
    fwd = chain < BATCH
    zero = jnp.zeros_like(u)
    uf = jnp.where(fwd, u, zero)
    ub = jnp.where(fwd, zero, u)
    is_fwd_y = lax.broadcasted_iota(jnp.int32, (rows, S5_SB_IN), 0) % S5_CHAINS < BATCH

    for cb in range(S5_STATE // S5_SCAN_COLS):
        cin = slice(cb * S5_SB_IN, (cb + 1) * S5_SB_IN)
        re = pl.ds(cb * S5_SCAN_COLS, S5_SCAN_COLS)
        im = pl.ds(S5_STATE + cb * S5_SCAN_COLS, S5_SCAN_COLS)
        bu = jnp.dot(jnp.concatenate([uf[:, cin], ub[:, cin]], axis=1), wb_ref[cb], preferred_element_type=F32)
        bu_ref[:, re] = bu[:, :S5_SCAN_COLS]
        bu_ref[:, im] = bu[:, S5_SCAN_COLS:]
        a_re = are_ref[:, re]
        a_im = aim_ref[:, re]

        def step(s, carry):
            h_re, h_im = carry
            r = pl.ds(pl.multiple_of(s * S5_CHAINS, S5_CHAINS), S5_CHAINS)
            n_re = a_re * h_re - a_im * h_im + bu_ref[r, re]
            n_im = a_re * h_im + a_im * h_re + bu_ref[r, im]
            bu_ref[r, re] = n_re
            bu_ref[r, im] = n_im
            return n_re, n_im

        h_re, h_im = lax.fori_loop(0, steps, step, (h_ref[:, re], h_ref[:, im]), unroll=4)
        h_ref[:, re] = h_re
        h_ref[:, im] = h_im
        hb = jnp.concatenate([bu_ref[:, re], bu_ref[:, im]], axis=1).astype(BF16)
        y2 = jnp.dot(hb, wc_ref[cb], preferred_element_type=F32)
        y_ref[cb] = jnp.where(is_fwd_y, y2[:, :S5_SB_IN], y2[:, S5_SB_IN:])


def s5_scan(u8, wb, wc, a_re8, a_im8, steps):
    n_rows = u8.shape[0]
    rows = steps * S5_CHAINS
    assert n_rows % rows == 0
    full = lambda a: pl.BlockSpec(a.shape, lambda i: (0,) * a.ndim)
    return pl.pallas_call(
        functools.partial(_s5_scan_body, steps=steps),
        out_shape=jax.ShapeDtypeStruct((W_GRP // S5_SB_IN, n_rows, S5_SB_IN), F32),
        grid=(n_rows // rows,),
        in_specs=[pl.BlockSpec((rows, W_GRP), lambda i: (i, 0)), full(wb), full(wc), full(a_re8), full(a_im8)],
        out_specs=pl.BlockSpec((W_GRP // S5_SB_IN, rows, S5_SB_IN), lambda i: (0, i, 0)),
        scratch_shapes=[pltpu.VMEM((rows, 2 * S5_STATE), F32), pltpu.VMEM((S5_CHAINS, 2 * S5_STATE), F32)],
        compiler_params=pltpu.CompilerParams(dimension_semantics=("arbitrary",),
                                             vmem_limit_bytes=V7X_VMEM_LIMIT_BYTES),
        name="s5_scan",
    )(u8, wb, wc, a_re8, a_im8)


def _s5_glu_body(u_ref, yf_ref, yb_ref, d_ref, w_ref, b_ref, o_ref):
    tm = u_ref.shape[1]
    r = lax.broadcasted_iota(jnp.int32, (tm, tm), 0)
    c = lax.broadcasted_iota(jnp.int32, (tm, tm), 1)
    flip = jnp.where(r + c == tm - 1, 1.0, 0.0).astype(BF16)
    for b in range(BATCH):
        n_cb = yf_ref.shape[0]
        yf = jnp.concatenate([yf_ref[cb, pl.ds(b, tm, stride=S5_CHAINS), :] for cb in range(n_cb)], axis=1)
        yb = jnp.concatenate([yb_ref[cb, pl.ds(BATCH + b, tm, stride=S5_CHAINS), :] for cb in range(n_cb)],
                             axis=1)
        hi = yb.astype(BF16)
        lo = (yb - hi.astype(F32)).astype(BF16)
        yb = jnp.dot(flip, hi, preferred_element_type=F32) + jnp.dot(flip, lo, preferred_element_type=F32)
        y = d_ref[...] * u_ref[b] + yf + yb
        g = jax.nn.gelu(y)
        z = jnp.dot(g.astype(BF16), w_ref[...], preferred_element_type=F32) + b_ref[...]
        o_ref[b] = g * jax.nn.sigmoid(z)


def s5_glu_pallas(u, y8, first_step, d, w_bf16, b, tm):
    B_, n, W = u.shape
    assert B_ == BATCH and n % tm == 0 and first_step % tm == 0
    nt = n // tm
    off = first_step // tm
    row = pl.BlockSpec((B_, tm, W), lambda i: (0, i, 0))
    vec = pl.BlockSpec((1, W), lambda i: (0, 0))
    return pl.pallas_call(
        _s5_glu_body,
        out_shape=jax.ShapeDtypeStruct((B_, n, W), F32),
        grid=(nt,),
        in_specs=[row, pl.BlockSpec((y8.shape[0], tm * S5_CHAINS, y8.shape[2]), lambda i: (0, off + i, 0)),
                  pl.BlockSpec((y8.shape[0], tm * S5_CHAINS, y8.shape[2]), lambda i: (0, off + nt - 1 - i, 0)),
                  vec, pl.BlockSpec((W, W), lambda i: (0, 0)), vec],
        out_specs=row,
        compiler_params=pltpu.CompilerParams(dimension_semantics=("arbitrary",),
                                             vmem_limit_bytes=V7X_VMEM_LIMIT_BYTES),
        name="s5_glu",
    )(u, y8, y8, d.reshape(1, W), w_bf16, b.reshape(1, W))


def _s5_weights(a_re, a_im, log_dt, b_re, b_im, c_re, c_im):
    eye = jnp.eye(S5_G, dtype=F32)
    wb, a8 = [], []
    wc = []
    for di in range(2):
        ab_re, ab_im, bb_re, bb_im = s5_discretize(a_re[di], a_im[di], log_dt[di], b_re[di], b_im[di])
        blk = lambda m: jnp.einsum('gnp,gh->gphn', m, eye).reshape(W_GRP, S5_STATE)
        wb.append(jnp.concatenate([blk(bb_re), blk(bb_im)], axis=1))
        a8.append((jnp.broadcast_to(ab_re.reshape(1, S5_STATE), (BATCH, S5_STATE)),
                   jnp.broadcast_to(ab_im.reshape(1, S5_STATE), (BATCH, S5_STATE))))
        cblk = lambda m: jnp.einsum('gpn,gh->gnhp', m.astype(F32), eye).reshape(S5_STATE, W_GRP)
        wc.append(jnp.concatenate([cblk(c_re[di]), -cblk(c_im[di])], axis=0))
    wb_sb, wc_sb = [], []
    for sb in range(S5_STATE // S5_SCAN_COLS):
        cin = slice(sb * S5_SB_IN, (sb + 1) * S5_SB_IN)
        re = slice(sb * S5_SCAN_COLS, (sb + 1) * S5_SCAN_COLS)
        im = slice(S5_STATE + sb * S5_SCAN_COLS, S5_STATE + (sb + 1) * S5_SCAN_COLS)
        wb_sb.append(jnp.concatenate([jnp.concatenate([w[cin, re], w[cin, im]], axis=1) for w in wb], axis=0))
        wc_sb.append(jnp.concatenate([jnp.concatenate([w[re, cin], w[im, cin]], axis=0) for w in wc], axis=1))
    wb_sb = jnp.stack(wb_sb).astype(BF16)
    wc_sb = jnp.stack(wc_sb).astype(BF16)
    a_re8 = jnp.concatenate([a8[0][0], a8[1][0]], axis=0)
    a_im8 = jnp.concatenate([a8[0][1], a8[1][1]], axis=0)
    return wb_sb, wc_sb, a_re8, a_im8


def s5_mixer_pallas(u_lat, u_ctx, rev_lat, rev_ctx, a_re, a_im, log_dt, b_re, b_im, c_re, c_im, d, w_glu,
                    b_glu, with_ctx_out, steps=S5_STEPS, tm=S5_GLU_TILE):
    B_, L, W = u_lat.shape
    C_ = u_ctx.shape[1]
    assert B_ == BATCH and W == W_GRP
    wb, wc, a_re8, a_im8 = _s5_weights(a_re, a_im, log_dt, b_re, b_im, c_re, c_im)
    seq_f = jnp.concatenate([u_ctx.astype(BF16), u_lat.astype(BF16)], axis=1)
    seq_b = jnp.concatenate([rev_ctx, rev_lat], axis=1)
    u8 = jnp.concatenate([seq_f, seq_b], axis=0).transpose(1, 0, 2).reshape((C_ + L) * S5_CHAINS, W)
    y8 = s5_scan(u8, wb, wc, a_re8, a_im8, steps)
    w_glu_b = w_glu.astype(BF16)
    tm = min(tm, C_)
    out_lat = s5_glu_pallas(u_lat, y8, C_, d, w_glu_b, b_glu, tm).reshape(B_ * L, W)
    out_ctx = s5_glu_pallas(u_ctx, y8, 0, d, w_glu_b, b_glu, tm).reshape(B_ * C_, W) if with_ctx_out else None
    return out_lat, out_ctx


PEER_HALF = PEER_QDIM // 2


def _argmax_rows(v, r):
    while v.shape[0] > 1:
        half = v.shape[0] // 2
        take_hi = v[half:] > v[:half]
        r = jnp.where(take_hi, r[half:], r[:half])
        v = jnp.maximum(v[:half], v[half:])
    return v, r


def _topk_rows(x, k):
    row = lax.broadcasted_iota(jnp.int32, x.shape, 0).astype(F32)
    krow = lax.broadcasted_iota(jnp.int32, (k, x.shape[1]), 0)

    def body(it, carry):
        x, rank, vals = carry
        m, first = _argmax_rows(x, row)
        hit = row == first
        return (jnp.where(hit, -jnp.inf, x), jnp.where(hit, it.astype(F32), rank), jnp.where(krow == it, m, vals))

    init = (x, jnp.full(x.shape, float(k), F32), jnp.zeros((k, x.shape[1]), F32))
    _, rank, vals = lax.fori_loop(0, k, body, init)
    return vals, rank


PEER_GRID_COLS = tuple(PEER_TOPK // (i + 1) for i in range(PEER_TOPK))
PEER_GRID_ROWS = 64


def _peer_route_body(x_ref, shift_ref, scale_ref, wq_ref, keys_ref, h_ref, n1_ref, c1_ref, r2_ref, e2_ref,
                     q_ref, *, lane_tiles):
    hm = (x_ref[...] * (1.0 + scale_ref[0]) + shift_ref[0]).astype(BF16)
    h_ref[...] = hm
    q_ref[...] = jnp.dot(hm, wq_ref[...], preferred_element_type=F32).astype(BF16)
    K = PEER_TOPK
    for hd in range(PEER_HEADS):
        for lt in range(lane_tiles):
            tok = pl.ds(lt * V7X_LANES, V7X_LANES)
            sc = []
            for side in range(2):
                col = (2 * hd + side) * PEER_HALF
                qs = q_ref[tok, col:col + PEER_HALF]
                sc.append(lax.dot_general(keys_ref[2 * hd + side], qs, (((1,), (1,)), ((), ())),
                                          preferred_element_type=F32))
            v1, r1 = _topk_rows(sc[0], K)
            v2, r2 = _topk_rows(sc[1], K)
            cells = [v1[i:i + 1] + v2[:PEER_GRID_COLS[i]] for i in range(K)]
            cells.append(jnp.full((PEER_GRID_ROWS - sum(PEER_GRID_COLS), V7X_LANES), -jnp.inf, F32))
            vc, rc = _topk_rows(jnp.concatenate(cells, axis=0), K)
            z = jnp.sum(jnp.exp(vc - vc[0:1]), axis=0, keepdims=True)
            chosen = jnp.where(rc < float(K), 1.0, 0.0)
            n1 = jnp.zeros_like(r1)
            off = 0
            for i in range(K):
                n_i = jnp.sum(chosen[off:off + PEER_GRID_COLS[i]], axis=0, keepdims=True)
                n1 = jnp.where(r1 == float(i), n_i, n1)
                off += PEER_GRID_COLS[i]
            n1_ref[hd, :, tok] = n1
            c1_ref[hd, :, tok] = jnp.exp(sc[0] - v1[0:1]) / z
            r2_ref[hd, :, tok] = r2.astype(BF16)
            e2_ref[hd, :, tok] = jnp.exp(sc[1] - v2[0:1]).astype(BF16)


def peer_route(x, shift, scale, wq_bf16, keys_bf16, rows_per_mod, tm):
    T, D = x.shape
    assert T % tm == 0 and rows_per_mod % tm == 0 and tm % V7X_LANES == 0
    tiles_per_mod = rows_per_mod // tm
    mod_spec = pl.BlockSpec((1, 1, D), lambda i: (i // tiles_per_mod, 0, 0))
    tab = lambda dt: jax.ShapeDtypeStruct((PEER_HEADS, PEER_NK, T), dt)
    tab_spec = pl.BlockSpec((PEER_HEADS, PEER_NK, tm), lambda i: (0, 0, i))
    return pl.pallas_call(
        functools.partial(_peer_route_body, lane_tiles=tm // V7X_LANES),
        out_shape=[jax.ShapeDtypeStruct((T, D), BF16), tab(F32), tab(F32), tab(BF16), tab(BF16)],
        grid=(T // tm,),
        in_specs=[pl.BlockSpec((tm, D), lambda i: (i, 0)), mod_spec, mod_spec,
                  pl.BlockSpec(wq_bf16.shape, lambda i: (0, 0)),
                  pl.BlockSpec(keys_bf16.shape, lambda i: (0, 0, 0))],
        out_specs=[pl.BlockSpec((tm, D), lambda i: (i, 0)), tab_spec, tab_spec, tab_spec, tab_spec],
        scratch_shapes=[pltpu.VMEM((tm, PEER_HEADS * PEER_QDIM), BF16)],
        compiler_params=pltpu.CompilerParams(dimension_semantics=("arbitrary",),
                                             vmem_limit_bytes=V7X_VMEM_LIMIT_BYTES),
        name="peer_route",
    )(x, shift, scale, wq_bf16, keys_bf16)


def _peer_dense_body(h_ref, u_ref, vt_ref, n1_ref, c1_ref, r2_ref, e2_ref, x_ref, gate_ref, g_ref, b_ref,
                     o_ref, acc_ref, a_ref, *, n_slab):
    j = pl.program_id(1)

    @pl.when(j == 0)
    def _():
        acc_ref[...] = jnp.zeros_like(acc_ref)

    pair = 2 * PEER_NK
    for p in range(n_slab // 2):
        s = lax.dot_general(u_ref[p * pair:(p + 1) * pair, :], h_ref[...], (((1,), (1,)), ((), ())),
                            preferred_element_type=F32)
        for kk in range(2):
            k = 2 * p + kk
            for lt in range(h_ref.shape[0] // V7X_LANES):
                lanes = slice(lt * V7X_LANES, (lt + 1) * V7X_LANES)
                g = None
                for hd in range(PEER_HEADS):
                    n1row = n1_ref[hd, k:k + 1, lanes].astype(BF16)
                    c1row = c1_ref[hd, k:k + 1, lanes].astype(BF16)
                    sel = lax.clamp(jnp.zeros((), BF16), n1row - r2_ref[hd, :, lanes], jnp.ones((), BF16))
                    gh = sel * e2_ref[hd, :, lanes] * c1row
                    g = gh if g is None else g + gh
                sk = s[kk * PEER_NK:(kk + 1) * PEER_NK, lanes]
                a_ref[k * PEER_NK:(k + 1) * PEER_NK, lanes] = jax.nn.gelu(sk).astype(BF16) * g
    acc_ref[...] += jnp.dot(vt_ref[0], a_ref[...], preferred_element_type=F32)

    @pl.when(j == pl.num_programs(1) - 1)
    def _():
        hres = DEEPNORM_ALPHA * x_ref[...] + gate_ref[0] * acc_ref[...].T
        mu = jnp.mean(hres, -1, keepdims=True)
        hc = hres - mu
        var = jnp.mean(hc * hc, -1, keepdims=True)
        o_ref[...] = hc * lax.rsqrt(var + LN_EPS) * g_ref[...] + b_ref[...]


def peer_dense(h_bf16, u_bf16, vt_tiles, n1, c1, r2, e2, x, gate, ln_g, ln_b, rows_per_mod, tm, n_slab):
    T, D = h_bf16.shape
    NE = u_bf16.shape[0]
    e_tile = n_slab * PEER_NK
    n_e = NE // e_tile
    assert T % tm == 0 and NE % e_tile == 0 and vt_tiles.shape == (n_e, D, e_tile)
    slab_spec = pl.BlockSpec((PEER_HEADS, n_slab, tm), lambda i, j: (0, j, i))
    tok_spec = pl.BlockSpec((PEER_HEADS, PEER_NK, tm), lambda i, j: (0, 0, i))
    return pl.pallas_call(
        functools.partial(_peer_dense_body, n_slab=n_slab),
        out_shape=jax.ShapeDtypeStruct((T, D), F32),
        grid=(T // tm, n_e),
        in_specs=[pl.BlockSpec((tm, D), lambda i, j: (i, 0)),
                  pl.BlockSpec((e_tile, D), lambda i, j: (j, 0)),
                  pl.BlockSpec((1, D, e_tile), lambda i, j: (j, 0, 0)),
                  slab_spec, slab_spec, tok_spec, tok_spec,
                  pl.BlockSpec((tm, D), lambda i, j: (i, 0)),
                  pl.BlockSpec((1, 1, D), lambda i, j: (i // (rows_per_mod // tm), 0, 0)),
                  pl.BlockSpec((1, D), lambda i, j: (0, 0)), pl.BlockSpec((1, D), lambda i, j: (0, 0))],
        out_specs=pl.BlockSpec((tm, D), lambda i, j: (i, 0)),
        scratch_shapes=[pltpu.VMEM((D, tm), F32), pltpu.VMEM((e_tile, tm), BF16)],
        compiler_params=pltpu.CompilerParams(dimension_semantics=("arbitrary", "arbitrary"),
                                             vmem_limit_bytes=V7X_VMEM_LIMIT_BYTES),
        name="peer_dense",
    )(h_bf16, u_bf16, vt_tiles, n1, c1, r2, e2, x, gate, ln_g.reshape(1, D), ln_b.reshape(1, D))


PEER_SLABS = 8


def peer_v_tiles(v_tab):
    e_tile = PEER_SLABS * PEER_NK
    return v_tab.astype(BF16).reshape(v_tab.shape[0] // e_tile, e_tile, v_tab.shape[1]).transpose(0, 2, 1)


def peer_sublayer(x, shift, scale, gate, wq_bf16, keys_bf16, u_bf16, vt_tiles, ln_g, ln_b, rows_per_mod,
                  tm_route=PEER_ROUTE_TILE, tm_dense=PEER_DENSE_TILE):
    h, n1, c1, r2, e2 = peer_route(x, shift, scale, wq_bf16, keys_bf16, rows_per_mod, tm_route)
    return peer_dense(h, u_bf16, vt_tiles, n1, c1, r2, e2, x, gate, ln_g, ln_b, rows_per_mod, tm_dense,
                      PEER_SLABS)


POOL_HALO = 8
POOL_ROWS = 256


def _pool_body(u_ref, w_ref, scale_ref, o_ref, pad_ref, *, seq_len):
    L = seq_len
    chunk = min(POOL_ROWS, L)
    zeros = jnp.zeros((POOL_HALO, W_GRP), F32)
    pad_ref[0:POOL_HALO, :] = zeros
    pad_ref[POOL_HALO + L:2 * POOL_HALO + L, :] = zeros
    pad_ref[POOL_HALO:POOL_HALO + L, :] = u_ref[0]
    for r0 in range(0, L, chunk):
        t = r0 + lax.broadcasted_iota(jnp.int32, (chunk, POOL_C), 0)
        for j, w in enumerate(POOL_WINDOWS):
            cols = slice(j * POOL_C, (j + 1) * POOL_C)
            acc = None
            for o in range(-(w // 2), w - w // 2):
                s = pad_ref[POOL_HALO + r0 + o:POOL_HALO + r0 + o + chunk, cols]
                acc = s if acc is None else acc + s
            lo = jnp.maximum(t - w // 2, 0)
            hi = jnp.minimum(t - w // 2 + w - 1, L - 1)
            cnt = (hi - lo + 1).astype(F32)
            pooled = acc / cnt - u_ref[0, r0:r0 + chunk, cols]
            y = jnp.dot(pooled.astype(BF16), w_ref[j], preferred_element_type=F32)
            o_ref[0, r0:r0 + chunk, cols] = y * scale_ref[:, cols]


def pool_mix_pallas(u, pool_w, pool_scale):
    B_, L, W = u.shape
    assert max(POOL_WINDOWS) // 2 <= POOL_HALO and L % min(POOL_ROWS, L) == 0
    blk = pl.BlockSpec((1, L, W), lambda b: (b, 0, 0))
    return pl.pallas_call(
        functools.partial(_pool_body, seq_len=L),
        out_shape=jax.ShapeDtypeStruct((B_, L, W), F32),
        grid=(B_,),
        in_specs=[blk, pl.BlockSpec(pool_w.shape, lambda b: (0, 0, 0)), pl.BlockSpec((1, W), lambda b: (0, 0))],
        out_specs=blk,
        scratch_shapes=[pltpu.VMEM((L + 2 * POOL_HALO, W), F32)],
        compiler_params=pltpu.CompilerParams(dimension_semantics=("arbitrary",),
                                             vmem_limit_bytes=V7X_VMEM_LIMIT_BYTES),
        name="pool_mix",
    )(u, pool_w.astype(BF16), pool_scale.reshape(1, W))


NA_TILE_R = 4
NA_TILE = NA_TILE_R * GRID_W
NA_SCALE = NA_HD ** -0.5


def _na_bias_table(rpb, n_rows):
    n_tiles = n_rows // NA_TILE_R
    KR = min(NA_WIN_R, n_rows)
    a = np.array([0, min(2, n_tiles - 1), n_tiles - 1]).reshape(3, 1, 1, 1, 1, 1)
    d = np.arange(3).reshape(1, 3, 1, 1, 1, 1)
    i = np.arange(NA_TILE_R).reshape(1, 1, NA_TILE_R, 1, 1, 1)
    qc = np.arange(GRID_W).reshape(1, 1, 1, GRID_W, 1, 1)
    j = np.arange(NA_TILE_R).reshape(1, 1, 1, 1, NA_TILE_R, 1)
    kc = np.arange(GRID_W).reshape(1, 1, 1, 1, 1, GRID_W)
    qr = NA_TILE_R * a + i
    kr = NA_TILE_R * (a + d - 1) + j
    rs = np.clip(qr - KR // 2, 0, n_rows - KR)
    c0 = np.clip(qc - NA_WIN_C // 2, 0, GRID_W - NA_WIN_C)
    ok = (kr >= rs) & (kr < rs + KR) & (kr >= 0) & (kr < n_rows) & (kc >= c0) & (kc < c0 + NA_WIN_C)
    row_rel = np.clip(kr - qr + NA_WIN_R - 1, 0, 2 * NA_WIN_R - 2)[:, :, :, 0, :, 0]
    col_rel = (np.clip(kc - qc, -(NA_WIN_C - 1), NA_WIN_C - 1) + NA_WIN_C - 1)[0, 0, 0, :, 0, :]
    onehot = (col_rel[None] == np.arange(2 * NA_WIN_C - 1)[:, None, None]).astype(np.float32)
    bias_rc = jnp.einsum('hrc,cqk->hrqk', rpb.astype(F32), onehot, precision=lax.Precision.HIGHEST)
    tab = jnp.stack([bias_rc[:, int(r)] for r in row_rel.reshape(-1)], axis=1)
    tab = tab.reshape(NA_H, 3, 3, NA_TILE_R, NA_TILE_R, GRID_W, GRID_W).transpose(1, 0, 2, 3, 5, 4, 6)
    tab = jnp.where(ok[:, None], tab, -jnp.inf)
    return tab.reshape(3, NA_H, 3, NA_TILE, NA_TILE)


def _na_body(q_ref, k_ref, v_ref, kc_ref, vc_ref, t_ref, o_ref, *, n_tiles, with_grid):
    a = pl.program_id(1)
    nt = (((1,), (1,)), ((), ()))
    for h in range(NA_H):
        hs = slice(h * NA_HD, (h + 1) * NA_HD)
        qh = q_ref[:, hs]
        scores = [lax.dot_general(qh, kc_ref[:, hs], nt, preferred_element_type=F32) * NA_SCALE]
        vals = [vc_ref[:, hs]]
        if with_grid:
            for d in range(3):
                ti = jnp.clip(a + d - 1, 0, n_tiles - 1)
                rows = pl.ds(pl.multiple_of(ti * NA_TILE, NA_TILE), NA_TILE)
                s = lax.dot_general(qh, k_ref[rows, hs], nt, preferred_element_type=F32)
                scores.append(s * NA_SCALE + t_ref[0, h, d])
                vals.append(v_ref[rows, hs])
        m = scores[0].max(axis=-1, keepdims=True)
        for s in scores[1:]:
            m = jnp.maximum(m, s.max(axis=-1, keepdims=True))
        den = None
        acc = None
        for s, vv in zip(scores, vals):
            p = jnp.exp(s - m)
            l = p.sum(axis=-1, keepdims=True)
            o = jnp.dot(p.astype(BF16), vv, preferred_element_type=F32)
            den = l if den is None else den + l
            acc = o if acc is None else acc + o
        o_ref[:, hs] = acc / den


def na_attention(q, k, v, kc, vc, table, seq_len, ctx_len, with_grid):
    W = q.shape[1]
    n_b = kc.shape[0] // ctx_len
    lq = q.shape[0] // n_b
    assert lq % NA_TILE == 0
    q_tiles = lq // NA_TILE
    n_tiles = seq_len // NA_TILE
    pat = lambda b, a: (jnp.where(a == 0, 0, jnp.where(a == n_tiles - 1, 2, 1)), 0, 0, 0, 0)
    qo_spec = pl.BlockSpec((NA_TILE, W), lambda b, a: (b * q_tiles + a, 0))
    return pl.pallas_call(
        functools.partial(_na_body, n_tiles=n_tiles, with_grid=with_grid),
        out_shape=jax.ShapeDtypeStruct(q.shape, F32),
        grid=(n_b, q_tiles),
        in_specs=[qo_spec,
                  pl.BlockSpec((seq_len, W), lambda b, a: (b, 0)), pl.BlockSpec((seq_len, W), lambda b, a: (b, 0)),
                  pl.BlockSpec((ctx_len, W), lambda b, a: (b, 0)), pl.BlockSpec((ctx_len, W), lambda b, a: (b, 0)),
                  pl.BlockSpec((1,) + table.shape[1:], pat)],
        out_specs=qo_spec,
        compiler_params=pltpu.CompilerParams(dimension_semantics=("arbitrary", "arbitrary"),
                                             vmem_limit_bytes=V7X_VMEM_LIMIT_BYTES),
        name="na_attention",
    )(q, k, v, kc, vc, table)


SSM_HALO = 8
SSM_BC = SSM_G * SSM_N


def _ssm_conv_body(prev_ref, cur_ref, next_ref, w_ref, b_ref, o_ref, pad_ref, *, tiles_per_seq):
    pos = pl.program_id(0) % tiles_per_seq
    tm = cur_ref.shape[0]
    pad_ref[0:SSM_HALO, :] = jnp.where(pos == 0, 0.0, prev_ref[...])
    pad_ref[SSM_HALO:SSM_HALO + tm, :] = cur_ref[...]
    pad_ref[SSM_HALO + tm:2 * SSM_HALO + tm, :] = jnp.where(pos == tiles_per_seq - 1, 0.0, next_ref[...])
    lead = (SSM_CONV - 1) // 2
    y = b_ref[...]
    for k in range(SSM_CONV):
        y = y + w_ref[k:k + 1, :] * pad_ref[SSM_HALO - lead + k:SSM_HALO - lead + k + tm, :]
    o_ref[...] = jax.nn.silu(y)


def ssm_conv(xbc, conv_w, conv_b, seq_len, tm):
    T, CH = xbc.shape
    assert seq_len % tm == 0 and tm % SSM_HALO == 0
    hb = tm // SSM_HALO
    n_hb = T // SSM_HALO
    return pl.pallas_call(
        functools.partial(_ssm_conv_body, tiles_per_seq=seq_len // tm),
        out_shape=jax.ShapeDtypeStruct((T, CH), F32),
        grid=(T // tm,),
        in_specs=[pl.BlockSpec((SSM_HALO, CH), lambda i: (jnp.maximum(i * hb - 1, 0), 0)),
                  pl.BlockSpec((tm, CH), lambda i: (i, 0)),
                  pl.BlockSpec((SSM_HALO, CH), lambda i: (jnp.minimum((i + 1) * hb, n_hb - 1), 0)),
                  pl.BlockSpec((SSM_CONV, CH), lambda i: (0, 0)), pl.BlockSpec((1, CH), lambda i: (0, 0))],
        out_specs=pl.BlockSpec((tm, CH), lambda i: (i, 0)),
        scratch_shapes=[pltpu.VMEM((tm + 2 * SSM_HALO, CH), F32)],
        compiler_params=pltpu.CompilerParams(dimension_semantics=("arbitrary",),
                                             vmem_limit_bytes=V7X_VMEM_LIMIT_BYTES),
        name="ssm_conv",
    )(xbc, xbc, xbc, conv_w, conv_b.reshape(1, CH))


def _ssd_direction(xbc_ref, dt_ref, bias_ref, a_ref, h_scr, y_ref, *, di):
    rev = di == 1
    Q = xbc_ref.shape[0]
    dt_all = jax.nn.softplus(dt_ref[...] + bias_ref[...])
    a_cs = dt_all * a_ref[...]
    row_id = lax.broadcasted_iota(jnp.int32, a_cs.shape, 0)
    sh = 1
    while sh < Q:
        if rev:
            a_cs = a_cs + jnp.where(row_id < Q - sh, pltpu.roll(a_cs, Q - sh, 0), 0.0)
        else:
            a_cs = a_cs + jnp.where(row_id >= sh, pltpu.roll(a_cs, sh, 0), 0.0)
        sh *= 2
    a_cs_t = a_cs.T
    a_tot = a_cs[0:1, :] if rev else a_cs[Q - 1:Q, :]
    l_id = lax.broadcasted_iota(jnp.int32, (Q, Q), 0)
    s_id = lax.broadcasted_iota(jnp.int32, (Q, Q), 1)
    causal = (l_id <= s_id) if rev else (l_id >= s_id)
    nt = (((1,), (1,)), ((), ()))
    tn = (((0,), (0,)), ((), ()))
    cb = []
    for g in range(SSM_G):
        bg = xbc_ref[:, W_GRP + g * SSM_N:W_GRP + (g + 1) * SSM_N].astype(BF16)
        cg = xbc_ref[:, W_GRP + SSM_BC + g * SSM_N:W_GRP + SSM_BC + (g + 1) * SSM_N].astype(BF16)
        cb.append((bg, cg, lax.dot_general(cg, bg, nt, preferred_element_type=F32)))
    for h in range(SSM_H):
        c = di * SSM_H + h
        bg, cg, cbg = cb[h // (SSM_H // SSM_G)]
        col = a_cs[:, c:c + 1]
        lm = jnp.exp(jnp.where(causal, col - a_cs_t[c:c + 1, :], -jnp.inf))
        xh = xbc_ref[:, h * SSM_HD:(h + 1) * SSM_HD] * dt_all[:, c:c + 1]
        hp = h_scr[di, h]
        yd = jnp.dot((cbg * lm).astype(BF16), xh.astype(BF16), preferred_element_type=F32)
        yo = lax.dot_general(cg, hp.astype(BF16), nt, preferred_element_type=F32) * jnp.exp(col)
        y_ref[:, h * SSM_HD:(h + 1) * SSM_HD] = yd + yo
        tot = a_tot[:, c:c + 1]
        xd = (xh * jnp.exp(tot - col)).astype(BF16)
        h_scr[di, h] = jnp.exp(tot) * hp + lax.dot_general(xd, bg, tn, preferred_element_type=F32)


def _ssd_body(xf_ref, dtf_ref, xb_ref, dtb_ref, bias_ref, a_ref, h0_ref, yf_ref, yb_ref, hfin_ref, h_scr):
    ci = pl.program_id(1)

    @pl.when(ci == 0)
    def _():
        h_scr[...] = h0_ref[:, 0]

    _ssd_direction(xf_ref, dtf_ref, bias_ref, a_ref, h_scr, yf_ref, di=0)
    _ssd_direction(xb_ref, dtb_ref, bias_ref, a_ref, h_scr, yb_ref, di=1)

    @pl.when(ci == pl.num_programs(1) - 1)
    def _():
        hfin_ref[:, 0] = h_scr[...]


def ssd_scan(xbc_act, dt_raw, bias128, a128, h0, seq_len):
    T = xbc_act.shape[0]
    n_b = T // seq_len
    Q = min(SSM_CHUNK, seq_len)
    nc = seq_len // Q
    fwd = lambda b, i: (b * nc + i, 0)
    bwd = lambda b, i: (b * nc + nc - 1 - i, 0)
    vec = pl.BlockSpec((1, V7X_LANES), lambda b, i: (0, 0))
    st = pl.BlockSpec((2, 1, SSM_H, SSM_HD, SSM_N), lambda b, i: (0, b, 0, 0, 0))
    y_shape = jax.ShapeDtypeStruct((T, W_GRP), F32)
    return pl.pallas_call(
        _ssd_body,
        out_shape=[y_shape, y_shape, jax.ShapeDtypeStruct(h0.shape, F32)],
        grid=(n_b, nc),
        in_specs=[pl.BlockSpec((Q, SSM_CONV_CH), fwd), pl.BlockSpec((Q, V7X_LANES), fwd),
                  pl.BlockSpec((Q, SSM_CONV_CH), bwd), pl.BlockSpec((Q, V7X_LANES), bwd), vec, vec, st],
        out_specs=[pl.BlockSpec((Q, W_GRP), fwd), pl.BlockSpec((Q, W_GRP), bwd), st],
        scratch_shapes=[pltpu.VMEM((2, SSM_H, SSM_HD, SSM_N), F32)],
        compiler_params=pltpu.CompilerParams(dimension_semantics=("arbitrary", "arbitrary"),
                                             vmem_limit_bytes=V7X_VMEM_LIMIT_BYTES),
        name="ssd_scan",
    )(xbc_act, dt_raw, xbc_act, dt_raw, bias128, a128, h0)


def _ssm_out_body(xbc_ref, yf_ref, yb_ref, z_ref, d_ref, nw_ref, o_ref):
    y = d_ref[...] * xbc_ref[:, :W_GRP] + yf_ref[...] + yb_ref[...]
    g = y * jax.nn.silu(z_ref[...])
    gw = W_GRP // SSM_G
    for k in range(SSM_G):
        gk = g[:, k * gw:(k + 1) * gw]
        r = lax.rsqrt(jnp.mean(gk * gk, -1, keepdims=True) + LN_EPS)
        o_ref[:, k * gw:(k + 1) * gw] = gk * r * nw_ref[:, k * gw:(k + 1) * gw]


def ssm_out(xbc_act, yf, yb, z, d512, norm_w, tm):
    T = z.shape[0]
    assert T % tm == 0
    row = pl.BlockSpec((tm, W_GRP), lambda i: (i, 0))
    vec = pl.BlockSpec((1, W_GRP), lambda i: (0, 0))
    return pl.pallas_call(
        _ssm_out_body,
        out_shape=jax.ShapeDtypeStruct((T, W_GRP), F32),
        grid=(T // tm,),
        in_specs=[pl.BlockSpec((tm, SSM_CONV_CH), lambda i: (i, 0)), row, row, row, vec, vec],
        out_specs=row,
        compiler_params=pltpu.CompilerParams(dimension_semantics=("arbitrary",),
                                             vmem_limit_bytes=V7X_VMEM_LIMIT_BYTES),
        name="ssm_out",
    )(xbc_act, yf, yb, z, d512, norm_w.reshape(1, W_GRP))


def ssm_mixer_pallas(z_l, xbc_l, dt_l, z_c, xbc_c, dt_c, conv_w, conv_b, dt_bias, a_log, d, norm_w,
                     seq_len, ctx_len, with_ctx_out, tm=ROW_TILE):
    n_b = z_l.shape[0] // seq_len
    pad = V7X_LANES - 2 * SSM_H
    bias128 = jnp.pad(dt_bias.astype(F32).reshape(1, 2 * SSM_H), ((0, 0), (0, pad)))
    a128 = jnp.pad(-jnp.exp(a_log.astype(F32)).reshape(1, 2 * SSM_H), ((0, 0), (0, pad)))
    d512 = jnp.repeat(d.astype(F32), SSM_HD).reshape(1, W_GRP)
    act_c = ssm_conv(xbc_c, conv_w, conv_b, ctx_len, min(tm, ctx_len))
    act_l = ssm_conv(xbc_l, conv_w, conv_b, seq_len, tm)
    zero = jnp.zeros((2, n_b, SSM_H, SSM_HD, SSM_N), F32)
    ycf, ycb, hc = ssd_scan(act_c, dt_c, bias128, a128, zero, ctx_len)
    ylf, ylb, _ = ssd_scan(act_l, dt_l, bias128, a128, hc, seq_len)
    out_l = ssm_out(act_l, ylf, ylb, z_l, d512, norm_w, tm)
    out_c = ssm_out(act_c, ycf, ycb, z_c, d512, norm_w, min(tm, z_c.shape[0])) if with_ctx_out else None
    return out_l, out_c


def s5_discretize(a_re, a_im, log_dt, b_re, b_im):
    a_re, a_im = a_re.astype(F32), a_im.astype(F32)
    b_re, b_im = b_re.astype(F32), b_im.astype(F32)
    dt = jnp.exp(log_dt.astype(F32))[:, None]
    mag = jnp.exp(a_re * dt)
    ab_re = mag * jnp.cos(a_im * dt)
    ab_im = mag * jnp.sin(a_im * dt)
    den = a_re * a_re + a_im * a_im
    f_re = ((ab_re - 1) * a_re + ab_im * a_im) / den
    f_im = (ab_im * a_re - (ab_re - 1) * a_im) / den
    bb_re = f_re[..., None] * b_re - f_im[..., None] * b_im
    bb_im = f_re[..., None] * b_im + f_im[..., None] * b_re
    return ab_re, ab_im, bb_re, bb_im


def _mixers(p_l, p_c, prm, with_ctx_out):
    s5_l, pool_l, q_l, k_l, v_l, z_l, xbc_l, dt_l, s5r_l = p_l
    s5_c, pool_c, q_c, k_c, v_c, z_c, xbc_c, dt_c, s5r_c = p_c
    ya_l, ya_c = s5_mixer_pallas(s5_l, s5_c, s5r_l, s5r_c, prm["s5_a_re"], prm["s5_a_im"], prm["s5_log_dt"],
                                 prm["s5_b_re"], prm["s5_b_im"], prm["s5_c_re"], prm["s5_c_im"], prm["s5_d"],
                                 prm["s5_w_glu"], prm["s5_b_glu"], with_ctx_out)
    yb_l = pool_mix_pallas(pool_l, prm["pool_w"], prm["pool_scale"])
    yb_c = pool_mix_pallas(pool_c, prm["pool_w"], prm["pool_scale"]) if with_ctx_out else None
    B_, L, _ = s5_l.shape
    C_ = s5_c.shape[1]
    flat = lambda t: t.reshape(-1, t.shape[-1])
    table = _na_bias_table(prm["na_rpb"], L // GRID_W)
    yc_l = na_attention(flat(q_l), flat(k_l), flat(v_l), flat(k_c), flat(v_c), table, L, C_, True)
    yc_c = (na_attention(flat(q_c), flat(k_c), flat(v_c), flat(k_c), flat(v_c), table, C_, C_, False)
            if with_ctx_out else None)
    yd_l, yd_c = ssm_mixer_pallas(flat(z_l), flat(xbc_l), flat(dt_l), flat(z_c), flat(xbc_c), flat(dt_c),
                                  prm["ssm_conv_w"], prm["ssm_conv_b"], prm["ssm_dt_bias"], prm["ssm_a_log"],
                                  prm["ssm_d"], prm["ssm_norm_w"], L, C_, with_ctx_out)
    return (ya_l, yb_l, yc_l, yd_l), (ya_c, yb_c, yc_c, yd_c)


def kernel(x, c, ctx, c_ctx, w_ada, b_ada, w_in, w_out, s5_a_re, s5_a_im, s5_log_dt, s5_b_re, s5_b_im, s5_c_re, s5_c_im, s5_d, s5_w_glu, s5_b_glu, pool_w, pool_scale, na_rpb, ssm_conv_w, ssm_conv_b, ssm_dt_bias, ssm_a_log, ssm_d, ssm_norm_w, ln1_g, ln1_b, ln2_g, ln2_b, peer_w_q, peer_sub_keys, peer_u, peer_v):
    B_, L, D = x.shape
    C_ = ctx.shape[1]
    x_lat = x.reshape(B_ * L, D)
    x_ctx = ctx.reshape(B_ * C_, D)
    assert B_ + 1 <= ADA_ROWS
    cond = jnp.concatenate([c, c_ctx[None], jnp.zeros((ADA_ROWS - B_ - 1, D), F32)], axis=0)
    dt_pad = V7X_LANES - IN_SPLITS[-1]
    splits = IN_SPLITS[:-1] + (V7X_LANES,)
    in_dtypes = (F32, F32, BF16, BF16, BF16, F32, F32, F32)
    TM = ROW_TILE
    for l in range(DEPTH):
        last = l == DEPTH - 1
        prm = dict(s5_a_re=s5_a_re[l], s5_a_im=s5_a_im[l], s5_log_dt=s5_log_dt[l], s5_b_re=s5_b_re[l],
                   s5_b_im=s5_b_im[l], s5_c_re=s5_c_re[l], s5_c_im=s5_c_im[l], s5_d=s5_d[l],
                   s5_w_glu=s5_w_glu[l], s5_b_glu=s5_b_glu[l], pool_w=pool_w[l], pool_scale=pool_scale[l],
                   na_rpb=na_rpb[l], ssm_conv_w=ssm_conv_w[l], ssm_conv_b=ssm_conv_b[l],
                   ssm_dt_bias=ssm_dt_bias[l], ssm_a_log=ssm_a_log[l], ssm_d=ssm_d[l],
                   ssm_norm_w=ssm_norm_w[l])
        mod = ada_modulation(cond, w_ada[l], b_ada[l])
        m_lat = mod[:B_].reshape(B_, 6, 1, D)
        m_ctx = mod[B_:B_ + 1].reshape(1, 6, 1, D)
        w_in_b = jnp.pad(w_in[l], ((0, 0), (0, dt_pad))).astype(BF16)
        w_out_b = w_out[l].astype(BF16)

        p_l = modulated_matmul(x_lat, m_lat[:, 0], m_lat[:, 1], w_in_b, splits, in_dtypes, L, L, TM)
        p_c = modulated_matmul(x_ctx, m_ctx[:, 0], m_ctx[:, 1], w_in_b, splits, in_dtypes, B_ * C_, C_, TM)
        p_l = [a.reshape(B_, L, -1) for a in p_l]
        p_c = [a.reshape(B_, C_, -1) for a in p_c]
        y_l, y_c = _mixers(p_l, p_c, prm, not last)

        x_lat = proj_residual_ln([a.reshape(B_ * L, -1) for a in y_l], x_lat, m_lat[:, 2], w_out_b,
                                 ln1_g[l], ln1_b[l], L, TM)
        wq_b = peer_w_q[l].astype(BF16)
        keys_b = peer_sub_keys[l].reshape(2 * PEER_HEADS, PEER_NK, PEER_HALF).astype(BF16)
        u_b = peer_u[l].astype(BF16)
        vt_b = peer_v_tiles(peer_v[l])
        x_lat = peer_sublayer(x_lat, m_lat[:, 3], m_lat[:, 4], m_lat[:, 5], wq_b, keys_b, u_b, vt_b,
                              ln2_g[l], ln2_b[l], L)
        if not last:
            x_ctx = proj_residual_ln([a.reshape(B_ * C_, -1) for a in y_c], x_ctx, m_ctx[:, 2], w_out_b,
                                     ln1_g[l], ln1_b[l], B_ * C_, TM)
            x_ctx = peer_sublayer(x_ctx, m_ctx[:, 3], m_ctx[:, 4], m_ctx[:, 5], wq_b, keys_b, u_b, vt_b,
                                  ln2_g[l], ln2_b[l], B_ * C_)
    return x_lat.reshape(B_, L, D)
```

```python
import functools

import jax
import jax.numpy as jnp
import numpy as np
from jax import lax
from jax.experimental import pallas as pl
from jax.experimental.pallas import tpu as pltpu

D_MODEL = 2048
BATCH = 4
SEQ = 4096
DEPTH = 2

GRID_W = 64
CTX_LEN = 256
N_MIXERS = 4
D_MIX = D_MODEL
W_GRP = D_MIX // N_MIXERS

S5_P = 16
S5_G = W_GRP // S5_P
S5_N = 64

POOL_WINDOWS = (2, 4, 8, 16)
POOL_C = W_GRP // len(POOL_WINDOWS)

NA_HD = 64
NA_H = W_GRP // NA_HD
NA_WIN_R = 8
NA_WIN_C = 16

SSM_HD = 64
SSM_H = W_GRP // SSM_HD
SSM_G = 2
SSM_N = 128
SSM_CONV = 4
SSM_CHUNK = 128
SSM_CONV_CH = W_GRP + 2 * SSM_G * SSM_N

PEER_HEADS = 8
PEER_NK = 128
PEER_NE = PEER_NK * PEER_NK
PEER_QDIM = 256
PEER_TOPK = 16
PEER_BLOCK = 128

IN_SPLITS = (W_GRP, W_GRP, W_GRP, W_GRP, W_GRP, W_GRP, SSM_CONV_CH, 2 * SSM_H)
D_IN = sum(IN_SPLITS)

DEEPNORM_ALPHA = (2 * DEPTH) ** 0.25
DEEPNORM_BETA = (8 * DEPTH) ** -0.25
LN_EPS = 1e-5
F32 = jnp.float32
BF16 = jnp.bfloat16

V7X_LANES = 128
V7X_VMEM_BYTES = 64 * 1024 * 1024
V7X_VMEM_LIMIT_BYTES = V7X_VMEM_BYTES * 7 // 8

ROW_TILE = 256
S5_STEPS = 64
S5_GLU_TILE = 256
PEER_ROUTE_TILE = 256
PEER_DENSE_TILE = 512


ADA_ROWS = 8
ADA_TN = 1536


def _ada_body(c_ref, w_ref, b_ref, o_ref):
    act = jax.nn.silu(c_ref[...]).astype(BF16)
    o_ref[...] = jnp.dot(act, w_ref[...].astype(BF16), preferred_element_type=F32) + b_ref[...]


def ada_modulation(c8, w, b):
    R, D = c8.shape
    N = w.shape[1]
    assert N % ADA_TN == 0
    return pl.pallas_call(
        _ada_body,
        out_shape=jax.ShapeDtypeStruct((R, N), F32),
        grid=(N // ADA_TN,),
        in_specs=[pl.BlockSpec((R, D), lambda j: (0, 0)), pl.BlockSpec((D, ADA_TN), lambda j: (0, j)),
                  pl.BlockSpec((1, ADA_TN), lambda j: (0, j))],
        out_specs=pl.BlockSpec((R, ADA_TN), lambda j: (0, j)),
        compiler_params=pltpu.CompilerParams(dimension_semantics=("arbitrary",),
                                             vmem_limit_bytes=V7X_VMEM_LIMIT_BYTES),
        name="ada_modulation",
    )(c8, w, b.reshape(1, N))
def _modmm_body(x_ref, shift_ref, scale_ref, w_ref, *out_refs, col_splits):
    xm = (x_ref[...] * (1.0 + scale_ref[0]) + shift_ref[0]).astype(BF16)
    o = 0
    for j, (ref, n) in enumerate(zip(out_refs, col_splits)):
        val = jnp.dot(xm, w_ref[:, o:o + n], preferred_element_type=F32)
        ref[...] = val.astype(ref.dtype)
        if j == 0:
            tm = val.shape[0]
            r = lax.broadcasted_iota(jnp.int32, (tm, tm), 0)
            c = lax.broadcasted_iota(jnp.int32, (tm, tm), 1)
            flip = jnp.where(r + c == tm - 1, 1.0, 0.0).astype(BF16)
            out_refs[-1][...] = jnp.dot(flip, val.astype(BF16), preferred_element_type=F32).astype(BF16)
        o += n


def modulated_matmul(x, shift, scale, w_bf16, col_splits, out_dtypes, rows_per_mod, seq_len, tm):
    T, K = x.shape
    assert T % tm == 0 and rows_per_mod % tm == 0 and seq_len % tm == 0
    tiles_per_mod = rows_per_mod // tm
    tps = seq_len // tm
    n_tot = sum(col_splits)
    assert w_bf16.shape == (K, n_tot)
    mod_spec = pl.BlockSpec((1, 1, K), lambda i: (i // tiles_per_mod, 0, 0))
    shapes = [jax.ShapeDtypeStruct((T, n), dt) for n, dt in zip(col_splits, out_dtypes, strict=True)]
    specs = [pl.BlockSpec((tm, n), lambda i: (i, 0)) for n in col_splits]
    shapes.append(jax.ShapeDtypeStruct((T, col_splits[0]), BF16))
    specs.append(pl.BlockSpec((tm, col_splits[0]), lambda i: ((i // tps) * tps + tps - 1 - i % tps, 0)))
    return pl.pallas_call(
        functools.partial(_modmm_body, col_splits=tuple(col_splits)),
        out_shape=shapes,
        grid=(T // tm,),
        in_specs=[pl.BlockSpec((tm, K), lambda i: (i, 0)), mod_spec, mod_spec,
                  pl.BlockSpec((K, n_tot), lambda i: (0, 0))],
        out_specs=specs,
        compiler_params=pltpu.CompilerParams(dimension_semantics=("arbitrary",),
                                             vmem_limit_bytes=V7X_VMEM_LIMIT_BYTES),
        name="modulated_matmul",
    )(x, shift, scale, w_bf16)


def _proj_ln_body(*refs, n_parts, alpha):
    part_refs = refs[:n_parts]
    x_ref, gate_ref, w_ref, g_ref, b_ref, o_ref = refs[n_parts:]
    acc = None
    o = 0
    for pr in part_refs:
        n = pr.shape[-1]
        d = jnp.dot(pr[...].astype(BF16), w_ref[o:o + n, :], preferred_element_type=F32)
        acc = d if acc is None else acc + d
        o += n
    h = alpha * x_ref[...] + gate_ref[0] * acc
    mu = jnp.mean(h, -1, keepdims=True)
    hc = h - mu
    var = jnp.mean(hc * hc, -1, keepdims=True)
    o_ref[...] = hc * lax.rsqrt(var + LN_EPS) * g_ref[...] + b_ref[...]


def proj_residual_ln(parts, x, gate, w_bf16, g, b, rows_per_mod, tm):
    T, D = x.shape
    assert T % tm == 0 and rows_per_mod % tm == 0
    tiles_per_mod = rows_per_mod // tm
    k_tot = sum(p.shape[-1] for p in parts)
    assert w_bf16.shape == (k_tot, D)
    row = lambda n: pl.BlockSpec((tm, n), lambda i: (i, 0))
    vec = pl.BlockSpec((1, D), lambda i: (0, 0))
    return pl.pallas_call(
        functools.partial(_proj_ln_body, n_parts=len(parts), alpha=DEEPNORM_ALPHA),
        out_shape=jax.ShapeDtypeStruct((T, D), F32),
        grid=(T // tm,),
        in_specs=[row(p.shape[-1]) for p in parts] + [
            row(D), pl.BlockSpec((1, 1, D), lambda i: (i // tiles_per_mod, 0, 0)),
            pl.BlockSpec((k_tot, D), lambda i: (0, 0)), vec, vec],
        out_specs=row(D),
        compiler_params=pltpu.CompilerParams(dimension_semantics=("arbitrary",),
                                             vmem_limit_bytes=V7X_VMEM_LIMIT_BYTES),
        name="proj_residual_ln",
    )(*parts, x, gate, w_bf16, g.reshape(1, D), b.reshape(1, D))


S5_CHAINS = 2 * BATCH
S5_STATE = S5_G * S5_N
S5_SCAN_COLS = 512
S5_SB_IN = S5_SCAN_COLS // S5_N * S5_P


def _s5_scan_body(u_ref, wb_ref, wc_ref, are_ref, aim_ref, y_ref, bu_ref, h_ref, *, steps):
    rows = steps * S5_CHAINS

    @pl.when(pl.program_id(0) == 0)
    def _():
        h_ref[...] = jnp.zeros_like(h_ref)

    u = u_ref[...]
    chain = lax.broadcasted_iota(jnp.int32, u.shape, 0) % S5_CHAINS
    fwd = chain < BATCH
    zero = jnp.zeros_like(u)
    uf = jnp.where(fwd, u, zero)
    ub = jnp.where(fwd, zero, u)
    is_fwd_y = lax.broadcasted_iota(jnp.int32, (rows, S5_SB_IN), 0) % S5_CHAINS < BATCH

    for cb in range(S5_STATE // S5_SCAN_COLS):
        cin = slice(cb * S5_SB_IN, (cb + 1) * S5_SB_IN)
        re = pl.ds(cb * S5_SCAN_COLS, S5_SCAN_COLS)
        im = pl.ds(S5_STATE + cb * S5_SCAN_COLS, S5_SCAN_COLS)
        bu = jnp.dot(jnp.concatenate([uf[:, cin], ub[:, cin]], axis=1), wb_ref[cb], preferred_element_type=F32)
        bu_ref[:, re] = bu[:, :S5_SCAN_COLS]
        bu_ref[:, im] = bu[:, S5_SCAN_COLS:]
        a_re = are_ref[:, re]
        a_im = aim_ref[:, re]

        def step(s, carry):
            h_re, h_im = carry
            r = pl.ds(pl.multiple_of(s * S5_CHAINS, S5_CHAINS), S5_CHAINS)
            n_re = a_re * h_re - a_im * h_im + bu_ref[r, re]
            n_im = a_re * h_im + a_im * h_re + bu_ref[r, im]
            bu_ref[r, re] = n_re
            bu_ref[r, im] = n_im
            return n_re, n_im

        h_re, h_im = lax.fori_loop(0, steps, step, (h_ref[:, re], h_ref[:, im]), unroll=4)
        h_ref[:, re] = h_re
        h_ref[:, im] = h_im
        hb = jnp.concatenate([bu_ref[:, re], bu_ref[:, im]], axis=1).astype(BF16)
        y2 = jnp.dot(hb, wc_ref[cb], preferred_element_type=F32)
        y_ref[cb] = jnp.where(is_fwd_y, y2[:, :S5_SB_IN], y2[:, S5_SB_IN:])


def s5_scan(u8, wb, wc, a_re8, a_im8, steps):
    n_rows = u8.shape[0]
    rows = steps * S5_CHAINS
    assert n_rows % rows == 0
    full = lambda a: pl.BlockSpec(a.shape, lambda i: (0,) * a.ndim)
    return pl.pallas_call(
        functools.partial(_s5_scan_body, steps=steps),
        out_shape=jax.ShapeDtypeStruct((W_GRP // S5_SB_IN, n_rows, S5_SB_IN), F32),
        grid=(n_rows // rows,),
        in_specs=[pl.BlockSpec((rows, W_GRP), lambda i: (i, 0)), full(wb), full(wc), full(a_re8), full(a_im8)],
        out_specs=pl.BlockSpec((W_GRP // S5_SB_IN, rows, S5_SB_IN), lambda i: (0, i, 0)),
        scratch_shapes=[pltpu.VMEM((rows, 2 * S5_STATE), F32), pltpu.VMEM((S5_CHAINS, 2 * S5_STATE), F32)],
        compiler_params=pltpu.CompilerParams(dimension_semantics=("arbitrary",),
                                             vmem_limit_bytes=V7X_VMEM_LIMIT_BYTES),
        name="s5_scan",
    )(u8, wb, wc, a_re8, a_im8)


def _s5_glu_body(u_ref, yf_ref, yb_ref, d_ref, w_ref, b_ref, o_ref):
    tm = u_ref.shape[1]
    r = lax.broadcasted_iota(jnp.int32, (tm, tm), 0)
    c = lax.broadcasted_iota(jnp.int32, (tm, tm), 1)
    flip = jnp.where(r + c == tm - 1, 1.0, 0.0).astype(BF16)
    for b in range(BATCH):
        n_cb = yf_ref.shape[0]
        yf = jnp.concatenate([yf_ref[cb, pl.ds(b, tm, stride=S5_CHAINS), :] for cb in range(n_cb)], axis=1)
        yb = jnp.concatenate([yb_ref[cb, pl.ds(BATCH + b, tm, stride=S5_CHAINS), :] for cb in range(n_cb)],
                             axis=1)
        hi = yb.astype(BF16)
        lo = (yb - hi.astype(F32)).astype(BF16)
        yb = jnp.dot(flip, hi, preferred_element_type=F32) + jnp.dot(flip, lo, preferred_element_type=F32)
        y = d_ref[...] * u_ref[b] + yf + yb
        g = jax.nn.gelu(y)
        z = jnp.dot(g.astype(BF16), w_ref[...], preferred_element_type=F32) + b_ref[...]
        o_ref[b] = g * jax.nn.sigmoid(z)


def s5_glu_pallas(u, y8, first_step, d, w_bf16, b, tm):
    B_, n, W = u.shape
    assert B_ == BATCH and n % tm == 0 and first_step % tm == 0
    nt = n // tm
    off = first_step // tm
    row = pl.BlockSpec((B_, tm, W), lambda i: (0, i, 0))
    vec = pl.BlockSpec((1, W), lambda i: (0, 0))
    return pl.pallas_call(
        _s5_glu_body,
        out_shape=jax.ShapeDtypeStruct((B_, n, W), F32),
        grid=(nt,),
        in_specs=[row, pl.BlockSpec((y8.shape[0], tm * S5_CHAINS, y8.shape[2]), lambda i: (0, off + i, 0)),
                  pl.BlockSpec((y8.shape[0], tm * S5_CHAINS, y8.shape[2]), lambda i: (0, off + nt - 1 - i, 0)),
                  vec, pl.BlockSpec((W, W), lambda i: (0, 0)), vec],
        out_specs=row,
        compiler_params=pltpu.CompilerParams(dimension_semantics=("arbitrary",),
                                             vmem_limit_bytes=V7X_VMEM_LIMIT_BYTES),
        name="s5_glu",
    )(u, y8, y8, d.reshape(1, W), w_bf16, b.reshape(1, W))


def _s5_weights(a_re, a_im, log_dt, b_re, b_im, c_re, c_im):
    eye = jnp.eye(S5_G, dtype=F32)
    wb, a8 = [], []
    wc = []
    for di in range(2):
        ab_re, ab_im, bb_re, bb_im = s5_discretize(a_re[di], a_im[di], log_dt[di], b_re[di], b_im[di])
        blk = lambda m: jnp.einsum('gnp,gh->gphn', m, eye).reshape(W_GRP, S5_STATE)
        wb.append(jnp.concatenate([blk(bb_re), blk(bb_im)], axis=1))
        a8.append((jnp.broadcast_to(ab_re.reshape(1, S5_STATE), (BATCH, S5_STATE)),
                   jnp.broadcast_to(ab_im.reshape(1, S5_STATE), (BATCH, S5_STATE))))
        cblk = lambda m: jnp.einsum('gpn,gh->gnhp', m.astype(F32), eye).reshape(S5_STATE, W_GRP)
        wc.append(jnp.concatenate([cblk(c_re[di]), -cblk(c_im[di])], axis=0))
    wb_sb, wc_sb = [], []
    for sb in range(S5_STATE // S5_SCAN_COLS):
        cin = slice(sb * S5_SB_IN, (sb + 1) * S5_SB_IN)
        re = slice(sb * S5_SCAN_COLS, (sb + 1) * S5_SCAN_COLS)
        im = slice(S5_STATE + sb * S5_SCAN_COLS, S5_STATE + (sb + 1) * S5_SCAN_COLS)
        wb_sb.append(jnp.concatenate([jnp.concatenate([w[cin, re], w[cin, im]], axis=1) for w in wb], axis=0))
        wc_sb.append(jnp.concatenate([jnp.concatenate([w[re, cin], w[im, cin]], axis=0) for w in wc], axis=1))
    wb_sb = jnp.stack(wb_sb).astype(BF16)
    wc_sb = jnp.stack(wc_sb).astype(BF16)
    a_re8 = jnp.concatenate([a8[0][0], a8[1][0]], axis=0)
    a_im8 = jnp.concatenate([a8[0][1], a8[1][1]], axis=0)
    return wb_sb, wc_sb, a_re8, a_im8


def s5_mixer_pallas(u_lat, u_ctx, rev_lat, rev_ctx, a_re, a_im, log_dt, b_re, b_im, c_re, c_im, d, w_glu,
                    b_glu, with_ctx_out, steps=S5_STEPS, tm=S5_GLU_TILE):
    B_, L, W = u_lat.shape
    C_ = u_ctx.shape[1]
    assert B_ == BATCH and W == W_GRP
    wb, wc, a_re8, a_im8 = _s5_weights(a_re, a_im, log_dt, b_re, b_im, c_re, c_im)
    seq_f = jnp.concatenate([u_ctx.astype(BF16), u_lat.astype(BF16)], axis=1)
    seq_b = jnp.concatenate([rev_ctx, rev_lat], axis=1)
    u8 = jnp.concatenate([seq_f, seq_b], axis=0).transpose(1, 0, 2).reshape((C_ + L) * S5_CHAINS, W)
    y8 = s5_scan(u8, wb, wc, a_re8, a_im8, steps)
    w_glu_b = w_glu.astype(BF16)
    tm = min(tm, C_)
    out_lat = s5_glu_pallas(u_lat, y8, C_, d, w_glu_b, b_glu, tm).reshape(B_ * L, W)
    out_ctx = s5_glu_pallas(u_ctx, y8, 0, d, w_glu_b, b_glu, tm).reshape(B_ * C_, W) if with_ctx_out else None
    return out_lat, out_ctx


PEER_HALF = PEER_QDIM // 2


def _argmax_rows(v, r):
    while v.shape[0] > 1:
        half = v.shape[0] // 2
        take_hi = v[half:] > v[:half]
        r = jnp.where(take_hi, r[half:], r[:half])
        v = jnp.maximum(v[:half], v[half:])
    return v, r


def _topk_rows(xs, k, with_rank):
    n, lanes = xs[0].shape
    row = lax.broadcasted_iota(jnp.int32, (n, lanes), 0).astype(F32)
    krow = lax.broadcasted_iota(jnp.int32, (k, lanes), 0)

    def body(it, carry):
        out = []
        for x, vals, aux in carry:
            m, first = _argmax_rows(x, row)
            hit = row == first
            sel = krow == it
            aux = jnp.where(hit, it.astype(F32), aux) if with_rank else jnp.where(sel, first, aux)
            out.append((jnp.where(hit, -jnp.inf, x), jnp.where(sel, m, vals), aux))
        return tuple(out)

    zeros = jnp.zeros((k, lanes), F32)
    aux0 = jnp.full((n, lanes), float(k), F32) if with_rank else zeros
    res = lax.fori_loop(0, k, body, tuple((x, zeros, aux0) for x in xs))
    return [(vals, aux) for _, vals, aux in res]


PEER_GRID_COLS = tuple(PEER_TOPK // (i + 1) for i in range(PEER_TOPK))
PEER_GRID_ROWS = 64


def _peer_route_body(x_ref, shift_ref, scale_ref, wq_ref, keys_ref, h_ref, n1_ref, c1_ref, r2_ref, e2_ref,
                     q_ref, *, lane_tiles):
    hm = (x_ref[...] * (1.0 + scale_ref[0]) + shift_ref[0]).astype(BF16)
    h_ref[...] = hm
    q_ref[...] = jnp.dot(hm, wq_ref[...], preferred_element_type=F32).astype(BF16)
    K = PEER_TOPK
    key = lax.broadcasted_iota(jnp.int32, (PEER_NK, V7X_LANES), 0).astype(F32)
    for hd in range(PEER_HEADS):
        for lt in range(lane_tiles):
            tok = pl.ds(lt * V7X_LANES, V7X_LANES)
            sc = []
            for side in range(2):
                col = (2 * hd + side) * PEER_HALF
                qs = q_ref[tok, col:col + PEER_HALF]
                sc.append(lax.dot_general(keys_ref[2 * hd + side], qs, (((1,), (1,)), ((), ())),
                                          preferred_element_type=F32))
            (v1, idx1), (v2, idx2) = _topk_rows(sc, K, with_rank=False)
            cells = [v1[i:i + 1] + v2[:PEER_GRID_COLS[i]] for i in range(K)]
            cells.append(jnp.full((PEER_GRID_ROWS - sum(PEER_GRID_COLS), V7X_LANES), -jnp.inf, F32))
            (vc, rc), = _topk_rows([jnp.concatenate(cells, axis=0)], K, with_rank=True)
            z = jnp.sum(jnp.exp(vc - vc[0:1]), axis=0, keepdims=True)
            chosen = jnp.where(rc < float(K), 1.0, 0.0)
            n1 = jnp.zeros_like(key)
            r2 = jnp.full(key.shape, float(K), F32)
            off = 0
            for i in range(K):
                n_i = jnp.sum(chosen[off:off + PEER_GRID_COLS[i]], axis=0, keepdims=True)
                n1 = jnp.where(key == idx1[i:i + 1], n_i, n1)
                r2 = jnp.where(key == idx2[i:i + 1], float(i), r2)
                off += PEER_GRID_COLS[i]
            n1_ref[hd, :, tok] = n1
            c1_ref[hd, :, tok] = jnp.exp(sc[0] - v1[0:1]) / z
            r2_ref[hd, :, tok] = r2.astype(BF16)
            e2_ref[hd, :, tok] = jnp.exp(sc[1] - v2[0:1]).astype(BF16)


def peer_route(x, shift, scale, wq_bf16, keys_bf16, rows_per_mod, tm):
    T, D = x.shape
    assert T % tm == 0 and rows_per_mod % tm == 0 and tm % V7X_LANES == 0
    tiles_per_mod = rows_per_mod // tm
    mod_spec = pl.BlockSpec((1, 1, D), lambda i: (i // tiles_per_mod, 0, 0))
    tab = lambda dt: jax.ShapeDtypeStruct((PEER_HEADS, PEER_NK, T), dt)
    tab_spec = pl.BlockSpec((PEER_HEADS, PEER_NK, tm), lambda i: (0, 0, i))
    return pl.pallas_call(
        functools.partial(_peer_route_body, lane_tiles=tm // V7X_LANES),
        out_shape=[jax.ShapeDtypeStruct((T, D), BF16), tab(F32), tab(F32), tab(BF16), tab(BF16)],
        grid=(T // tm,),
        in_specs=[pl.BlockSpec((tm, D), lambda i: (i, 0)), mod_spec, mod_spec,
                  pl.BlockSpec(wq_bf16.shape, lambda i: (0, 0)),
                  pl.BlockSpec(keys_bf16.shape, lambda i: (0, 0, 0))],
        out_specs=[pl.BlockSpec((tm, D), lambda i: (i, 0)), tab_spec, tab_spec, tab_spec, tab_spec],
        scratch_shapes=[pltpu.VMEM((tm, PEER_HEADS * PEER_QDIM), BF16)],
        compiler_params=pltpu.CompilerParams(dimension_semantics=("arbitrary",),
                                             vmem_limit_bytes=V7X_VMEM_LIMIT_BYTES),
        name="peer_route",
    )(x, shift, scale, wq_bf16, keys_bf16)


def _peer_dense_body(h_ref, u_ref, vt_ref, n1_ref, c1_ref, r2_ref, e2_ref, x_ref, gate_ref, g_ref, b_ref,
                     o_ref, acc_ref, a_ref, *, n_slab):
    j = pl.program_id(1)

    @pl.when(j == 0)
    def _():
        acc_ref[...] = jnp.zeros_like(acc_ref)

    pair = 2 * PEER_NK
    for p in range(n_slab // 2):
        s = lax.dot_general(u_ref[p * pair:(p + 1) * pair, :], h_ref[...], (((1,), (1,)), ((), ())),
                            preferred_element_type=F32)
        for kk in range(2):
            k = 2 * p + kk
            for lt in range(h_ref.shape[0] // V7X_LANES):
                lanes = slice(lt * V7X_LANES, (lt + 1) * V7X_LANES)
                g = None
                for hd in range(PEER_HEADS):
                    n1row = n1_ref[hd, k:k + 1, lanes].astype(BF16)
                    c1row = c1_ref[hd, k:k + 1, lanes].astype(BF16)
                    sel = lax.clamp(jnp.zeros((), BF16), n1row - r2_ref[hd, :, lanes], jnp.ones((), BF16))
                    gh = sel * e2_ref[hd, :, lanes] * c1row
                    g = gh if g is None else g + gh
                sk = s[kk * PEER_NK:(kk + 1) * PEER_NK, lanes]
                a_ref[k * PEER_NK:(k + 1) * PEER_NK, lanes] = jax.nn.gelu(sk).astype(BF16) * g
    acc_ref[...] += jnp.dot(vt_ref[0], a_ref[...], preferred_element_type=F32)

    @pl.when(j == pl.num_programs(1) - 1)
    def _():
        hres = DEEPNORM_ALPHA * x_ref[...] + gate_ref[0] * acc_ref[...].T
        mu = jnp.mean(hres, -1, keepdims=True)
        hc = hres - mu
        var = jnp.mean(hc * hc, -1, keepdims=True)
        o_ref[...] = hc * lax.rsqrt(var + LN_EPS) * g_ref[...] + b_ref[...]


def peer_dense(h_bf16, u_bf16, vt_tiles, n1, c1, r2, e2, x, gate, ln_g, ln_b, rows_per_mod, tm, n_slab):
    T, D = h_bf16.shape
    NE = u_bf16.shape[0]
    e_tile = n_slab * PEER_NK
    n_e = NE // e_tile
    assert T % tm == 0 and NE % e_tile == 0 and vt_tiles.shape == (n_e, D, e_tile)
    slab_spec = pl.BlockSpec((PEER_HEADS, n_slab, tm), lambda i, j: (0, j, i))
    tok_spec = pl.BlockSpec((PEER_HEADS, PEER_NK, tm), lambda i, j: (0, 0, i))
    return pl.pallas_call(
        functools.partial(_peer_dense_body, n_slab=n_slab),
        out_shape=jax.ShapeDtypeStruct((T, D), F32),
        grid=(T // tm, n_e),
        in_specs=[pl.BlockSpec((tm, D), lambda i, j: (i, 0)),
                  pl.BlockSpec((e_tile, D), lambda i, j: (j, 0)),
                  pl.BlockSpec((1, D, e_tile), lambda i, j: (j, 0, 0)),
                  slab_spec, slab_spec, tok_spec, tok_spec,
                  pl.BlockSpec((tm, D), lambda i, j: (i, 0)),
                  pl.BlockSpec((1, 1, D), lambda i, j: (i // (rows_per_mod // tm), 0, 0)),
                  pl.BlockSpec((1, D), lambda i, j: (0, 0)), pl.BlockSpec((1, D), lambda i, j: (0, 0))],
        out_specs=pl.BlockSpec((tm, D), lambda i, j: (i, 0)),
        scratch_shapes=[pltpu.VMEM((D, tm), F32), pltpu.VMEM((e_tile, tm), BF16)],
        compiler_params=pltpu.CompilerParams(dimension_semantics=("arbitrary", "arbitrary"),
                                             vmem_limit_bytes=V7X_VMEM_LIMIT_BYTES),
        name="peer_dense",
    )(h_bf16, u_bf16, vt_tiles, n1, c1, r2, e2, x, gate, ln_g.reshape(1, D), ln_b.reshape(1, D))


PEER_SLABS = 8


def peer_v_tiles(v_tab):
    e_tile = PEER_SLABS * PEER_NK
    return v_tab.astype(BF16).reshape(v_tab.shape[0] // e_tile, e_tile, v_tab.shape[1]).transpose(0, 2, 1)


def peer_sublayer(x, shift, scale, gate, wq_bf16, keys_bf16, u_bf16, vt_tiles, ln_g, ln_b, rows_per_mod,
                  tm_route=PEER_ROUTE_TILE, tm_dense=PEER_DENSE_TILE):
    h, n1, c1, r2, e2 = peer_route(x, shift, scale, wq_bf16, keys_bf16, rows_per_mod, tm_route)
    return peer_dense(h, u_bf16, vt_tiles, n1, c1, r2, e2, x, gate, ln_g, ln_b, rows_per_mod, tm_dense,
                      PEER_SLABS)


POOL_HALO = 8
POOL_ROWS = 256


def _pool_body(u_ref, w_ref, scale_ref, o_ref, pad_ref, *, seq_len):
    L = seq_len
    chunk = min(POOL_ROWS, L)
    zeros = jnp.zeros((POOL_HALO, W_GRP), F32)
    pad_ref[0:POOL_HALO, :] = zeros
    pad_ref[POOL_HALO + L:2 * POOL_HALO + L, :] = zeros
    pad_ref[POOL_HALO:POOL_HALO + L, :] = u_ref[0]
    for r0 in range(0, L, chunk):
        t = r0 + lax.broadcasted_iota(jnp.int32, (chunk, POOL_C), 0)
        for j, w in enumerate(POOL_WINDOWS):
            cols = slice(j * POOL_C, (j + 1) * POOL_C)
            acc = None
            for o in range(-(w // 2), w - w // 2):
                s = pad_ref[POOL_HALO + r0 + o:POOL_HALO + r0 + o + chunk, cols]
                acc = s if acc is None else acc + s
            lo = jnp.maximum(t - w // 2, 0)
            hi = jnp.minimum(t - w // 2 + w - 1, L - 1)
            cnt = (hi - lo + 1).astype(F32)
            pooled = acc / cnt - u_ref[0, r0:r0 + chunk, cols]
            y = jnp.dot(pooled.astype(BF16), w_ref[j], preferred_element_type=F32)
            o_ref[0, r0:r0 + chunk, cols] = y * scale_ref[:, cols]


def pool_mix_pallas(u, pool_w, pool_scale):
    B_, L, W = u.shape
    assert max(POOL_WINDOWS) // 2 <= POOL_HALO and L % min(POOL_ROWS, L) == 0
    blk = pl.BlockSpec((1, L, W), lambda b: (b, 0, 0))
    return pl.pallas_call(
        functools.partial(_pool_body, seq_len=L),
        out_shape=jax.ShapeDtypeStruct((B_, L, W), F32),
        grid=(B_,),
        in_specs=[blk, pl.BlockSpec(pool_w.shape, lambda b: (0, 0, 0)), pl.BlockSpec((1, W), lambda b: (0, 0))],
        out_specs=blk,
        scratch_shapes=[pltpu.VMEM((L + 2 * POOL_HALO, W), F32)],
        compiler_params=pltpu.CompilerParams(dimension_semantics=("arbitrary",),
                                             vmem_limit_bytes=V7X_VMEM_LIMIT_BYTES),
        name="pool_mix",
    )(u, pool_w.astype(BF16), pool_scale.reshape(1, W))


NA_TILE_R = 4
NA_TILE = NA_TILE_R * GRID_W
NA_SCALE = NA_HD ** -0.5


def _na_bias_table(rpb, n_rows):
    n_tiles = n_rows // NA_TILE_R
    KR = min(NA_WIN_R, n_rows)
    a = np.array([0, min(2, n_tiles - 1), n_tiles - 1]).reshape(3, 1, 1, 1, 1, 1)
    d = np.arange(3).reshape(1, 3, 1, 1, 1, 1)
    i = np.arange(NA_TILE_R).reshape(1, 1, NA_TILE_R, 1, 1, 1)
    qc = np.arange(GRID_W).reshape(1, 1, 1, GRID_W, 1, 1)
    j = np.arange(NA_TILE_R).reshape(1, 1, 1, 1, NA_TILE_R, 1)
    kc = np.arange(GRID_W).reshape(1, 1, 1, 1, 1, GRID_W)
    qr = NA_TILE_R * a + i
    kr = NA_TILE_R * (a + d - 1) + j
    rs = np.clip(qr - KR // 2, 0, n_rows - KR)
    c0 = np.clip(qc - NA_WIN_C // 2, 0, GRID_W - NA_WIN_C)
    ok = (kr >= rs) & (kr < rs + KR) & (kr >= 0) & (kr < n_rows) & (kc >= c0) & (kc < c0 + NA_WIN_C)
    row_rel = np.clip(kr - qr + NA_WIN_R - 1, 0, 2 * NA_WIN_R - 2)[:, :, :, 0, :, 0]
    col_rel = (np.clip(kc - qc, -(NA_WIN_C - 1), NA_WIN_C - 1) + NA_WIN_C - 1)[0, 0, 0, :, 0, :]
    onehot = (col_rel[None] == np.arange(2 * NA_WIN_C - 1)[:, None, None]).astype(np.float32)
    bias_rc = jnp.einsum('hrc,cqk->hrqk', rpb.astype(F32), onehot, precision=lax.Precision.HIGHEST)
    tab = jnp.stack([bias_rc[:, int(r)] for r in row_rel.reshape(-1)], axis=1)
    tab = tab.reshape(NA_H, 3, 3, NA_TILE_R, NA_TILE_R, GRID_W, GRID_W).transpose(1, 0, 2, 3, 5, 4, 6)
    tab = jnp.where(ok[:, None], tab, -jnp.inf)
    return tab.reshape(3, NA_H, 3, NA_TILE, NA_TILE)


def _na_body(q_ref, k_ref, v_ref, kc_ref, vc_ref, t_ref, o_ref, *, n_tiles, with_grid):
    a = pl.program_id(1)
    nt = (((1,), (1,)), ((), ()))
    for h in range(NA_H):
        hs = slice(h * NA_HD, (h + 1) * NA_HD)
        qh = q_ref[:, hs]
        scores = [lax.dot_general(qh, kc_ref[:, hs], nt, preferred_element_type=F32) * NA_SCALE]
        vals = [vc_ref[:, hs]]
        if with_grid:
            for d in range(3):
                ti = jnp.clip(a + d - 1, 0, n_tiles - 1)
                rows = pl.ds(pl.multiple_of(ti * NA_TILE, NA_TILE), NA_TILE)
                s = lax.dot_general(qh, k_ref[rows, hs], nt, preferred_element_type=F32)
                scores.append(s * NA_SCALE + t_ref[0, h, d])
                vals.append(v_ref[rows, hs])
        m = scores[0].max(axis=-1, keepdims=True)
        for s in scores[1:]:
            m = jnp.maximum(m, s.max(axis=-1, keepdims=True))
        den = None
        acc = None
        for s, vv in zip(scores, vals):
            p = jnp.exp(s - m)
            l = p.sum(axis=-1, keepdims=True)
            o = jnp.dot(p.astype(BF16), vv, preferred_element_type=F32)
            den = l if den is None else den + l
            acc = o if acc is None else acc + o
        o_ref[:, hs] = acc / den


def na_attention(q, k, v, kc, vc, table, seq_len, ctx_len, with_grid):
    W = q.shape[1]
    n_b = kc.shape[0] // ctx_len
    lq = q.shape[0] // n_b
    assert lq % NA_TILE == 0
    q_tiles = lq // NA_TILE
    n_tiles = seq_len // NA_TILE
    pat = lambda b, a: (jnp.where(a == 0, 0, jnp.where(a == n_tiles - 1, 2, 1)), 0, 0, 0, 0)
    qo_spec = pl.BlockSpec((NA_TILE, W), lambda b, a: (b * q_tiles + a, 0))
    return pl.pallas_call(
        functools.partial(_na_body, n_tiles=n_tiles, with_grid=with_grid),
        out_shape=jax.ShapeDtypeStruct(q.shape, F32),
        grid=(n_b, q_tiles),
        in_specs=[qo_spec,
                  pl.BlockSpec((seq_len, W), lambda b, a: (b, 0)), pl.BlockSpec((seq_len, W), lambda b, a: (b, 0)),
                  pl.BlockSpec((ctx_len, W), lambda b, a: (b, 0)), pl.BlockSpec((ctx_len, W), lambda b, a: (b, 0)),
                  pl.BlockSpec((1,) + table.shape[1:], pat)],
        out_specs=qo_spec,
        compiler_params=pltpu.CompilerParams(dimension_semantics=("arbitrary", "arbitrary"),
                                             vmem_limit_bytes=V7X_VMEM_LIMIT_BYTES),
        name="na_attention",
    )(q, k, v, kc, vc, table)


SSM_HALO = 8
SSM_BC = SSM_G * SSM_N


def _ssm_conv_body(prev_ref, cur_ref, next_ref, w_ref, b_ref, o_ref, pad_ref, *, tiles_per_seq):
    pos = pl.program_id(0) % tiles_per_seq
    tm = cur_ref.shape[0]
    pad_ref[0:SSM_HALO, :] = jnp.where(pos == 0, 0.0, prev_ref[...])
    pad_ref[SSM_HALO:SSM_HALO + tm, :] = cur_ref[...]
    pad_ref[SSM_HALO + tm:2 * SSM_HALO + tm, :] = jnp.where(pos == tiles_per_seq - 1, 0.0, next_ref[...])
    lead = (SSM_CONV - 1) // 2
    y = b_ref[...]
    for k in range(SSM_CONV):
        y = y + w_ref[k:k + 1, :] * pad_ref[SSM_HALO - lead + k:SSM_HALO - lead + k + tm, :]
    o_ref[...] = jax.nn.silu(y)


def ssm_conv(xbc, conv_w, conv_b, seq_len, tm):
    T, CH = xbc.shape
    assert seq_len % tm == 0 and tm % SSM_HALO == 0
    hb = tm // SSM_HALO
    n_hb = T // SSM_HALO
    return pl.pallas_call(
        functools.partial(_ssm_conv_body, tiles_per_seq=seq_len // tm),
        out_shape=jax.ShapeDtypeStruct((T, CH), F32),
        grid=(T // tm,),
        in_specs=[pl.BlockSpec((SSM_HALO, CH), lambda i: (jnp.maximum(i * hb - 1, 0), 0)),
                  pl.BlockSpec((tm, CH), lambda i: (i, 0)),
                  pl.BlockSpec((SSM_HALO, CH), lambda i: (jnp.minimum((i + 1) * hb, n_hb - 1), 0)),
                  pl.BlockSpec((SSM_CONV, CH), lambda i: (0, 0)), pl.BlockSpec((1, CH), lambda i: (0, 0))],
        out_specs=pl.BlockSpec((tm, CH), lambda i: (i, 0)),
        scratch_shapes=[pltpu.VMEM((tm + 2 * SSM_HALO, CH), F32)],
        compiler_params=pltpu.CompilerParams(dimension_semantics=("arbitrary",),
                                             vmem_limit_bytes=V7X_VMEM_LIMIT_BYTES),
        name="ssm_conv",
    )(xbc, xbc, xbc, conv_w, conv_b.reshape(1, CH))


def _ssd_chunk_prep(xbc_ref, dt_ref, bias_ref, a_ref, *, di):
    rev = di == 1
    Q = xbc_ref.shape[0]
    dt_all = jax.nn.softplus(dt_ref[...] + bias_ref[...])
    a_cs = dt_all * a_ref[...]
    row_id = lax.broadcasted_iota(jnp.int32, a_cs.shape, 0)
    sh = 1
    while sh < Q:
        if rev:
            a_cs = a_cs + jnp.where(row_id < Q - sh, pltpu.roll(a_cs, Q - sh, 0), 0.0)
        else:
            a_cs = a_cs + jnp.where(row_id >= sh, pltpu.roll(a_cs, sh, 0), 0.0)
        sh *= 2
    a_cs_t = a_cs.T
    a_tot = a_cs[0:1, :] if rev else a_cs[Q - 1:Q, :]
    l_id = lax.broadcasted_iota(jnp.int32, (Q, Q), 0)
    s_id = lax.broadcasted_iota(jnp.int32, (Q, Q), 1)
    causal = (l_id <= s_id) if rev else (l_id >= s_id)
    cb = []
    for g in range(SSM_G):
        bg = xbc_ref[:, W_GRP + g * SSM_N:W_GRP + (g + 1) * SSM_N].astype(BF16)
        cg = xbc_ref[:, W_GRP + SSM_BC + g * SSM_N:W_GRP + SSM_BC + (g + 1) * SSM_N].astype(BF16)
        cb.append((bg, cg, lax.dot_general(cg, bg, (((1,), (1,)), ((), ())), preferred_element_type=F32)))
    return dt_all, a_cs, a_cs_t, a_tot, causal, cb


def _ssd_head(prep, xbc_ref, h_scr, y_ref, *, di, h):
    dt_all, a_cs, a_cs_t, a_tot, causal, cb = prep
    nt = (((1,), (1,)), ((), ()))
    tn = (((0,), (0,)), ((), ()))
    c = di * SSM_H + h
    bg, cg, cbg = cb[h // (SSM_H // SSM_G)]
    col = a_cs[:, c:c + 1]
    lm = jnp.exp(jnp.where(causal, col - a_cs_t[c:c + 1, :], -jnp.inf))
    xh = xbc_ref[:, h * SSM_HD:(h + 1) * SSM_HD] * dt_all[:, c:c + 1]
    hp = h_scr[di, h]
    yd = jnp.dot((cbg * lm).astype(BF16), xh.astype(BF16), preferred_element_type=F32)
    yo = lax.dot_general(cg, hp.astype(BF16), nt, preferred_element_type=F32) * jnp.exp(col)
    y_ref[:, h * SSM_HD:(h + 1) * SSM_HD] = yd + yo
    tot = a_tot[:, c:c + 1]
    xd = (xh * jnp.exp(tot - col)).astype(BF16)
    h_scr[di, h] = jnp.exp(tot) * hp + lax.dot_general(xd, bg, tn, preferred_element_type=F32)


def _ssd_body(xf_ref, dtf_ref, xb_ref, dtb_ref, bias_ref, a_ref, h0_ref, yf_ref, yb_ref, hfin_ref, h_scr):
    ci = pl.program_id(1)

    @pl.when(ci == 0)
    def _():
        h_scr[...] = h0_ref[:, 0]

    prep_f = _ssd_chunk_prep(xf_ref, dtf_ref, bias_ref, a_ref, di=0)
    prep_b = _ssd_chunk_prep(xb_ref, dtb_ref, bias_ref, a_ref, di=1)
    for h in range(SSM_H):
        _ssd_head(prep_f, xf_ref, h_scr, yf_ref, di=0, h=h)
        _ssd_head(prep_b, xb_ref, h_scr, yb_ref, di=1, h=h)

    @pl.when(ci == pl.num_programs(1) - 1)
    def _():
        hfin_ref[:, 0] = h_scr[...]


def ssd_scan(xbc_act, dt_raw, bias128, a128, h0, seq_len):
    T = xbc_act.shape[0]
    n_b = T // seq_len
    Q = min(SSM_CHUNK, seq_len)
    nc = seq_len // Q
    fwd = lambda b, i: (b * nc + i, 0)
    bwd = lambda b, i: (b * nc + nc - 1 - i, 0)
    vec = pl.BlockSpec((1, V7X_LANES), lambda b, i: (0, 0))
    st = pl.BlockSpec((2, 1, SSM_H, SSM_HD, SSM_N), lambda b, i: (0, b, 0, 0, 0))
    y_shape = jax.ShapeDtypeStruct((T, W_GRP), F32)
    return pl.pallas_call(
        _ssd_body,
        out_shape=[y_shape, y_shape, jax.ShapeDtypeStruct(h0.shape, F32)],
        grid=(n_b, nc),
        in_specs=[pl.BlockSpec((Q, SSM_CONV_CH), fwd), pl.BlockSpec((Q, V7X_LANES), fwd),
                  pl.BlockSpec((Q, SSM_CONV_CH), bwd), pl.BlockSpec((Q, V7X_LANES), bwd), vec, vec, st],
        out_specs=[pl.BlockSpec((Q, W_GRP), fwd), pl.BlockSpec((Q, W_GRP), bwd), st],
        scratch_shapes=[pltpu.VMEM((2, SSM_H, SSM_HD, SSM_N), F32)],
        compiler_params=pltpu.CompilerParams(dimension_semantics=("arbitrary", "arbitrary"),
                                             vmem_limit_bytes=V7X_VMEM_LIMIT_BYTES),
        name="ssd_scan",
    )(xbc_act, dt_raw, xbc_act, dt_raw, bias128, a128, h0)


def _ssm_out_body(xbc_ref, yf_ref, yb_ref, z_ref, d_ref, nw_ref, o_ref):
    y = d_ref[...] * xbc_ref[:, :W_GRP] + yf_ref[...] + yb_ref[...]
    g = y * jax.nn.silu(z_ref[...])
    gw = W_GRP // SSM_G
    for k in range(SSM_G):
        gk = g[:, k * gw:(k + 1) * gw]
        r = lax.rsqrt(jnp.mean(gk * gk, -1, keepdims=True) + LN_EPS)
        o_ref[:, k * gw:(k + 1) * gw] = gk * r * nw_ref[:, k * gw:(k + 1) * gw]


def ssm_out(xbc_act, yf, yb, z, d512, norm_w, tm):
    T = z.shape[0]
    assert T % tm == 0
    row = pl.BlockSpec((tm, W_GRP), lambda i: (i, 0))
    vec = pl.BlockSpec((1, W_GRP), lambda i: (0, 0))
    return pl.pallas_call(
        _ssm_out_body,
        out_shape=jax.ShapeDtypeStruct((T, W_GRP), F32),
        grid=(T // tm,),
        in_specs=[pl.BlockSpec((tm, SSM_CONV_CH), lambda i: (i, 0)), row, row, row, vec, vec],
        out_specs=row,
        compiler_params=pltpu.CompilerParams(dimension_semantics=("arbitrary",),
                                             vmem_limit_bytes=V7X_VMEM_LIMIT_BYTES),
        name="ssm_out",
    )(xbc_act, yf, yb, z, d512, norm_w.reshape(1, W_GRP))


def ssm_mixer_pallas(z_l, xbc_l, dt_l, z_c, xbc_c, dt_c, conv_w, conv_b, dt_bias, a_log, d, norm_w,
                     seq_len, ctx_len, with_ctx_out, tm=ROW_TILE):
    n_b = z_l.shape[0] // seq_len
    pad = V7X_LANES - 2 * SSM_H
    bias128 = jnp.pad(dt_bias.astype(F32).reshape(1, 2 * SSM_H), ((0, 0), (0, pad)))
    a128 = jnp.pad(-jnp.exp(a_log.astype(F32)).reshape(1, 2 * SSM_H), ((0, 0), (0, pad)))
    d512 = jnp.repeat(d.astype(F32), SSM_HD).reshape(1, W_GRP)
    act_c = ssm_conv(xbc_c, conv_w, conv_b, ctx_len, min(tm, ctx_len))
    act_l = ssm_conv(xbc_l, conv_w, conv_b, seq_len, tm)
    zero = jnp.zeros((2, n_b, SSM_H, SSM_HD, SSM_N), F32)
    ycf, ycb, hc = ssd_scan(act_c, dt_c, bias128, a128, zero, ctx_len)
    ylf, ylb, _ = ssd_scan(act_l, dt_l, bias128, a128, hc, seq_len)
    out_l = ssm_out(act_l, ylf, ylb, z_l, d512, norm_w, tm)
    out_c = ssm_out(act_c, ycf, ycb, z_c, d512, norm_w, min(tm, z_c.shape[0])) if with_ctx_out else None
    return out_l, out_c


def s5_discretize(a_re, a_im, log_dt, b_re, b_im):
    a_re, a_im = a_re.astype(F32), a_im.astype(F32)
    b_re, b_im = b_re.astype(F32), b_im.astype(F32)
    dt = jnp.exp(log_dt.astype(F32))[:, None]
    mag = jnp.exp(a_re * dt)
    ab_re = mag * jnp.cos(a_im * dt)
    ab_im = mag * jnp.sin(a_im * dt)
    den = a_re * a_re + a_im * a_im
    f_re = ((ab_re - 1) * a_re + ab_im * a_im) / den
    f_im = (ab_im * a_re - (ab_re - 1) * a_im) / den
    bb_re = f_re[..., None] * b_re - f_im[..., None] * b_im
    bb_im = f_re[..., None] * b_im + f_im[..., None] * b_re
    return ab_re, ab_im, bb_re, bb_im


def _mixers(p_l, p_c, prm, with_ctx_out):
    s5_l, pool_l, q_l, k_l, v_l, z_l, xbc_l, dt_l, s5r_l = p_l
    s5_c, pool_c, q_c, k_c, v_c, z_c, xbc_c, dt_c, s5r_c = p_c
    ya_l, ya_c = s5_mixer_pallas(s5_l, s5_c, s5r_l, s5r_c, prm["s5_a_re"], prm["s5_a_im"], prm["s5_log_dt"],
                                 prm["s5_b_re"], prm["s5_b_im"], prm["s5_c_re"], prm["s5_c_im"], prm["s5_d"],
                                 prm["s5_w_glu"], prm["s5_b_glu"], with_ctx_out)
    yb_l = pool_mix_pallas(pool_l, prm["pool_w"], prm["pool_scale"])
    yb_c = pool_mix_pallas(pool_c, prm["pool_w"], prm["pool_scale"]) if with_ctx_out else None
    B_, L, _ = s5_l.shape
    C_ = s5_c.shape[1]
    flat = lambda t: t.reshape(-1, t.shape[-1])
    table = _na_bias_table(prm["na_rpb"], L // GRID_W)
    yc_l = na_attention(flat(q_l), flat(k_l), flat(v_l), flat(k_c), flat(v_c), table, L, C_, True)
    yc_c = (na_attention(flat(q_c), flat(k_c), flat(v_c), flat(k_c), flat(v_c), table, C_, C_, False)
            if with_ctx_out else None)
    yd_l, yd_c = ssm_mixer_pallas(flat(z_l), flat(xbc_l), flat(dt_l), flat(z_c), flat(xbc_c), flat(dt_c),
                                  prm["ssm_conv_w"], prm["ssm_conv_b"], prm["ssm_dt_bias"], prm["ssm_a_log"],
                                  prm["ssm_d"], prm["ssm_norm_w"], L, C_, with_ctx_out)
    return (ya_l, yb_l, yc_l, yd_l), (ya_c, yb_c, yc_c, yd_c)


def kernel(x, c, ctx, c_ctx, w_ada, b_ada, w_in, w_out, s5_a_re, s5_a_im, s5_log_dt, s5_b_re, s5_b_im, s5_c_re, s5_c_im, s5_d, s5_w_glu, s5_b_glu, pool_w, pool_scale, na_rpb, ssm_conv_w, ssm_conv_b, ssm_dt_bias, ssm_a_log, ssm_d, ssm_norm_w, ln1_g, ln1_b, ln2_g, ln2_b, peer_w_q, peer_sub_keys, peer_u, peer_v):
    B_, L, D = x.shape
    C_ = ctx.shape[1]
    x_lat = x.reshape(B_ * L, D)
    x_ctx = ctx.reshape(B_ * C_, D)
    assert B_ + 1 <= ADA_ROWS
    cond = jnp.concatenate([c, c_ctx[None], jnp.zeros((ADA_ROWS - B_ - 1, D), F32)], axis=0)
    dt_pad = V7X_LANES - IN_SPLITS[-1]
    splits = IN_SPLITS[:-1] + (V7X_LANES,)
    in_dtypes = (F32, F32, BF16, BF16, BF16, F32, F32, F32)
    TM = ROW_TILE
    for l in range(DEPTH):
        last = l == DEPTH - 1
        prm = dict(s5_a_re=s5_a_re[l], s5_a_im=s5_a_im[l], s5_log_dt=s5_log_dt[l], s5_b_re=s5_b_re[l],
                   s5_b_im=s5_b_im[l], s5_c_re=s5_c_re[l], s5_c_im=s5_c_im[l], s5_d=s5_d[l],
                   s5_w_glu=s5_w_glu[l], s5_b_glu=s5_b_glu[l], pool_w=pool_w[l], pool_scale=pool_scale[l],
                   na_rpb=na_rpb[l], ssm_conv_w=ssm_conv_w[l], ssm_conv_b=ssm_conv_b[l],
                   ssm_dt_bias=ssm_dt_bias[l], ssm_a_log=ssm_a_log[l], ssm_d=ssm_d[l],
                   ssm_norm_w=ssm_norm_w[l])
        mod = ada_modulation(cond, w_ada[l], b_ada[l])
        m_lat = mod[:B_].reshape(B_, 6, 1, D)
        m_ctx = mod[B_:B_ + 1].reshape(1, 6, 1, D)
        w_in_b = jnp.pad(w_in[l], ((0, 0), (0, dt_pad))).astype(BF16)
        w_out_b = w_out[l].astype(BF16)

        p_l = modulated_matmul(x_lat, m_lat[:, 0], m_lat[:, 1], w_in_b, splits, in_dtypes, L, L, TM)
        p_c = modulated_matmul(x_ctx, m_ctx[:, 0], m_ctx[:, 1], w_in_b, splits, in_dtypes, B_ * C_, C_, TM)
        p_l = [a.reshape(B_, L, -1) for a in p_l]
        p_c = [a.reshape(B_, C_, -1) for a in p_c]
        y_l, y_c = _mixers(p_l, p_c, prm, not last)

        x_lat = proj_residual_ln([a.reshape(B_ * L, -1) for a in y_l], x_lat, m_lat[:, 2], w_out_b,
                                 ln1_g[l], ln1_b[l], L, TM)
        wq_b = peer_w_q[l].astype(BF16)
        keys_b = peer_sub_keys[l].reshape(2 * PEER_HEADS, PEER_NK, PEER_HALF).astype(BF16)
        u_b = peer_u[l].astype(BF16)
        vt_b = peer_v_tiles(peer_v[l])
        x_lat = peer_sublayer(x_lat, m_lat[:, 3], m_lat[:, 4], m_lat[:, 5], wq_b, keys_b, u_b, vt_b,
                              ln2_g[l], ln2_b[l], L)
        if not last:
            x_ctx = proj_residual_ln([a.reshape(B_ * C_, -1) for a in y_c], x_ctx, m_ctx[:, 2], w_out_b,
                                     ln1_g[l], ln1_b[l], B_ * C_, TM)
            x_ctx = peer_sublayer(x_ctx, m_ctx[:, 3], m_ctx[:, 4], m_ctx[:, 5], wq_b, keys_b, u_b, vt_b,
                                  ln2_g[l], ln2_b[l], B_ * C_)
    return x_lat.reshape(B_, L, D)
```

```python
import functools

import jax
import jax.numpy as jnp
import numpy as np
from jax import lax
from jax.experimental import pallas as pl
from jax.experimental.pallas import tpu as pltpu

D_MODEL = 2048
BATCH = 4
SEQ = 4096
DEPTH = 2

GRID_W = 64
CTX_LEN = 256
N_MIXERS = 4
D_MIX = D_MODEL
W_GRP = D_MIX // N_MIXERS

S5_P = 16
S5_G = W_GRP // S5_P
S5_N = 64

POOL_WINDOWS = (2, 4, 8, 16)
POOL_C = W_GRP // len(POOL_WINDOWS)

NA_HD = 64
NA_H = W_GRP // NA_HD
NA_WIN_R = 8
NA_WIN_C = 16

SSM_HD = 64
SSM_H = W_GRP // SSM_HD
SSM_G = 2
SSM_N = 128
SSM_CONV = 4
SSM_CHUNK = 128
SSM_CONV_CH = W_GRP + 2 * SSM_G * SSM_N

PEER_HEADS = 8
PEER_NK = 128
PEER_NE = PEER_NK * PEER_NK
PEER_QDIM = 256
PEER_TOPK = 16
PEER_BLOCK = 128

IN_SPLITS = (W_GRP, W_GRP, W_GRP, W_GRP, W_GRP, W_GRP, SSM_CONV_CH, 2 * SSM_H)
D_IN = sum(IN_SPLITS)

DEEPNORM_ALPHA = (2 * DEPTH) ** 0.25
DEEPNORM_BETA = (8 * DEPTH) ** -0.25
LN_EPS = 1e-5
F32 = jnp.float32
BF16 = jnp.bfloat16

V7X_LANES = 128
V7X_VMEM_BYTES = 64 * 1024 * 1024
V7X_VMEM_LIMIT_BYTES = V7X_VMEM_BYTES * 7 // 8

ROW_TILE = 256
S5_STEPS = 64
S5_GLU_TILE = 256
PEER_ROUTE_TILE = 256
PEER_DENSE_TILE = 512


ADA_ROWS = 8
ADA_TN = 1536


def _ada_body(c_ref, w_ref, b_ref, o_ref):
    act = jax.nn.silu(c_ref[...]).astype(BF16)
    o_ref[...] = jnp.dot(act, w_ref[...].astype(BF16), preferred_element_type=F32) + b_ref[...]


def ada_modulation(c8, w, b):
    R, D = c8.shape
    N = w.shape[1]
    assert N % ADA_TN == 0
    return pl.pallas_call(
        _ada_body,
        out_shape=jax.ShapeDtypeStruct((R, N), F32),
        grid=(N // ADA_TN,),
        in_specs=[pl.BlockSpec((R, D), lambda j: (0, 0)), pl.BlockSpec((D, ADA_TN), lambda j: (0, j)),
                  pl.BlockSpec((1, ADA_TN), lambda j: (0, j))],
        out_specs=pl.BlockSpec((R, ADA_TN), lambda j: (0, j)),
        compiler_params=pltpu.CompilerParams(dimension_semantics=("arbitrary",),
                                             vmem_limit_bytes=V7X_VMEM_LIMIT_BYTES),
        name="ada_modulation",
    )(c8, w, b.reshape(1, N))
def _modmm_body(x_ref, shift_ref, scale_ref, w_ref, *out_refs, col_splits):
    xm = (x_ref[...] * (1.0 + scale_ref[0]) + shift_ref[0]).astype(BF16)
    o = 0
    for j, (ref, n) in enumerate(zip(out_refs, col_splits)):
        val = jnp.dot(xm, w_ref[:, o:o + n], preferred_element_type=F32)
        ref[...] = val.astype(ref.dtype)
        if j == 0:
            tm = val.shape[0]
            r = lax.broadcasted_iota(jnp.int32, (tm, tm), 0)
            c = lax.broadcasted_iota(jnp.int32, (tm, tm), 1)
            flip = jnp.where(r + c == tm - 1, 1.0, 0.0).astype(BF16)
            out_refs[-1][...] = jnp.dot(flip, val.astype(BF16), preferred_element_type=F32).astype(BF16)
        o += n


def modulated_matmul(x, shift, scale, w_bf16, col_splits, out_dtypes, rows_per_mod, seq_len, tm):
    T, K = x.shape
    assert T % tm == 0 and rows_per_mod % tm == 0 and seq_len % tm == 0
    tiles_per_mod = rows_per_mod // tm
    tps = seq_len // tm
    n_tot = sum(col_splits)
    assert w_bf16.shape == (K, n_tot)
    mod_spec = pl.BlockSpec((1, 1, K), lambda i: (i // tiles_per_mod, 0, 0))
    shapes = [jax.ShapeDtypeStruct((T, n), dt) for n, dt in zip(col_splits, out_dtypes, strict=True)]
    specs = [pl.BlockSpec((tm, n), lambda i: (i, 0)) for n in col_splits]
    shapes.append(jax.ShapeDtypeStruct((T, col_splits[0]), BF16))
    specs.append(pl.BlockSpec((tm, col_splits[0]), lambda i: ((i // tps) * tps + tps - 1 - i % tps, 0)))
    return pl.pallas_call(
        functools.partial(_modmm_body, col_splits=tuple(col_splits)),
        out_shape=shapes,
        grid=(T // tm,),
        in_specs=[pl.BlockSpec((tm, K), lambda i: (i, 0)), mod_spec, mod_spec,
                  pl.BlockSpec((K, n_tot), lambda i: (0, 0))],
        out_specs=specs,
        compiler_params=pltpu.CompilerParams(dimension_semantics=("arbitrary",),
                                             vmem_limit_bytes=V7X_VMEM_LIMIT_BYTES),
        name="modulated_matmul",
    )(x, shift, scale, w_bf16)


def _proj_ln_body(*refs, n_parts, alpha):
    part_refs = refs[:n_parts]
    x_ref, gate_ref, w_ref, g_ref, b_ref, o_ref = refs[n_parts:]
    acc = None
    o = 0
    for pr in part_refs:
        n = pr.shape[-1]
        d = jnp.dot(pr[...].astype(BF16), w_ref[o:o + n, :], preferred_element_type=F32)
        acc = d if acc is None else acc + d
        o += n
    h = alpha * x_ref[...] + gate_ref[0] * acc
    mu = jnp.mean(h, -1, keepdims=True)
    hc = h - mu
    var = jnp.mean(hc * hc, -1, keepdims=True)
    o_ref[...] = hc * lax.rsqrt(var + LN_EPS) * g_ref[...] + b_ref[...]


def proj_residual_ln(parts, x, gate, w_bf16, g, b, rows_per_mod, tm):
    T, D = x.shape
    assert T % tm == 0 and rows_per_mod % tm == 0
    tiles_per_mod = rows_per_mod // tm
    k_tot = sum(p.shape[-1] for p in parts)
    assert w_bf16.shape == (k_tot, D)
    row = lambda n: pl.BlockSpec((tm, n), lambda i: (i, 0))
    vec = pl.BlockSpec((1, D), lambda i: (0, 0))
    return pl.pallas_call(
        functools.partial(_proj_ln_body, n_parts=len(parts), alpha=DEEPNORM_ALPHA),
        out_shape=jax.ShapeDtypeStruct((T, D), F32),
        grid=(T // tm,),
        in_specs=[row(p.shape[-1]) for p in parts] + [
            row(D), pl.BlockSpec((1, 1, D), lambda i: (i // tiles_per_mod, 0, 0)),
            pl.BlockSpec((k_tot, D), lambda i: (0, 0)), vec, vec],
        out_specs=row(D),
        compiler_params=pltpu.CompilerParams(dimension_semantics=("arbitrary",),
                                             vmem_limit_bytes=V7X_VMEM_LIMIT_BYTES),
        name="proj_residual_ln",
    )(*parts, x, gate, w_bf16, g.reshape(1, D), b.reshape(1, D))


S5_CHAINS = 2 * BATCH
S5_STATE = S5_G * S5_N
S5_SCAN_COLS = 512
S5_SB_IN = S5_SCAN_COLS // S5_N * S5_P


def _s5_scan_body(u_ref, wb_ref, wc_ref, are_ref, aim_ref, y_ref, bu_ref, h_ref, *, steps):
    rows = steps * S5_CHAINS

    @pl.when(pl.program_id(0) == 0)
    def _():
        h_ref[...] = jnp.zeros_like(h_ref)

    u = u_ref[...]
    chain = lax.broadcasted_iota(jnp.int32, u.shape, 0) % S5_CHAINS
    fwd = chain < BATCH
    zero = jnp.zeros_like(u)
    uf = jnp.where(fwd, u, zero)
    ub = jnp.where(fwd, zero, u)
    is_fwd_y = lax.broadcasted_iota(jnp.int32, (rows, S5_SB_IN), 0) % S5_CHAINS < BATCH

    for cb in range(S5_STATE // S5_SCAN_COLS):
        cin = slice(cb * S5_SB_IN, (cb + 1) * S5_SB_IN)
        re = pl.ds(cb * S5_SCAN_COLS, S5_SCAN_COLS)
        im = pl.ds(S5_STATE + cb * S5_SCAN_COLS, S5_SCAN_COLS)
        bu = jnp.dot(jnp.concatenate([uf[:, cin], ub[:, cin]], axis=1), wb_ref[cb], preferred_element_type=F32)
        bu_ref[:, re] = bu[:, :S5_SCAN_COLS]
        bu_ref[:, im] = bu[:, S5_SCAN_COLS:]
        a_re = are_ref[:, re]
        a_im = aim_ref[:, re]

        def step(s, carry):
            h_re, h_im = carry
            r = pl.ds(pl.multiple_of(s * S5_CHAINS, S5_CHAINS), S5_CHAINS)
            n_re = a_re * h_re - a_im * h_im + bu_ref[r, re]
            n_im = a_re * h_im + a_im * h_re + bu_ref[r, im]
            bu_ref[r, re] = n_re
            bu_ref[r, im] = n_im
            return n_re, n_im

        h_re, h_im = lax.fori_loop(0, steps, step, (h_ref[:, re], h_ref[:, im]), unroll=4)
        h_ref[:, re] = h_re
        h_ref[:, im] = h_im
        hb = jnp.concatenate([bu_ref[:, re], bu_ref[:, im]], axis=1).astype(BF16)
        y2 = jnp.dot(hb, wc_ref[cb], preferred_element_type=F32)
        y_ref[cb] = jnp.where(is_fwd_y, y2[:, :S5_SB_IN], y2[:, S5_SB_IN:])


def s5_scan(u8, wb, wc, a_re8, a_im8, steps):
    n_rows = u8.shape[0]
    rows = steps * S5_CHAINS
    assert n_rows % rows == 0
    full = lambda a: pl.BlockSpec(a.shape, lambda i: (0,) * a.ndim)
    return pl.pallas_call(
        functools.partial(_s5_scan_body, steps=steps),
        out_shape=jax.ShapeDtypeStruct((W_GRP // S5_SB_IN, n_rows, S5_SB_IN), F32),
        grid=(n_rows // rows,),
        in_specs=[pl.BlockSpec((rows, W_GRP), lambda i: (i, 0)), full(wb), full(wc), full(a_re8), full(a_im8)],
        out_specs=pl.BlockSpec((W_GRP // S5_SB_IN, rows, S5_SB_IN), lambda i: (0, i, 0)),
        scratch_shapes=[pltpu.VMEM((rows, 2 * S5_STATE), F32), pltpu.VMEM((S5_CHAINS, 2 * S5_STATE), F32)],
        compiler_params=pltpu.CompilerParams(dimension_semantics=("arbitrary",),
                                             vmem_limit_bytes=V7X_VMEM_LIMIT_BYTES),
        name="s5_scan",
    )(u8, wb, wc, a_re8, a_im8)


def _s5_glu_body(u_ref, yf_ref, yb_ref, d_ref, w_ref, b_ref, o_ref):
    tm = u_ref.shape[1]
    r = lax.broadcasted_iota(jnp.int32, (tm, tm), 0)
    c = lax.broadcasted_iota(jnp.int32, (tm, tm), 1)
    flip = jnp.where(r + c == tm - 1, 1.0, 0.0).astype(BF16)
    for b in range(BATCH):
        n_cb = yf_ref.shape[0]
        yf = jnp.concatenate([yf_ref[cb, pl.ds(b, tm, stride=S5_CHAINS), :] for cb in range(n_cb)], axis=1)
        yb = jnp.concatenate([yb_ref[cb, pl.ds(BATCH + b, tm, stride=S5_CHAINS), :] for cb in range(n_cb)],
                             axis=1)
        hi = yb.astype(BF16)
        lo = (yb - hi.astype(F32)).astype(BF16)
        yb = jnp.dot(flip, hi, preferred_element_type=F32) + jnp.dot(flip, lo, preferred_element_type=F32)
        y = d_ref[...] * u_ref[b] + yf + yb
        g = jax.nn.gelu(y)
        z = jnp.dot(g.astype(BF16), w_ref[...], preferred_element_type=F32) + b_ref[...]
        o_ref[b] = g * jax.nn.sigmoid(z)


def s5_glu_pallas(u, y8, first_step, d, w_bf16, b, tm):
    B_, n, W = u.shape
    assert B_ == BATCH and n % tm == 0 and first_step % tm == 0
    nt = n // tm
    off = first_step // tm
    row = pl.BlockSpec((B_, tm, W), lambda i: (0, i, 0))
    vec = pl.BlockSpec((1, W), lambda i: (0, 0))
    return pl.pallas_call(
        _s5_glu_body,
        out_shape=jax.ShapeDtypeStruct((B_, n, W), F32),
        grid=(nt,),
        in_specs=[row, pl.BlockSpec((y8.shape[0], tm * S5_CHAINS, y8.shape[2]), lambda i: (0, off + i, 0)),
                  pl.BlockSpec((y8.shape[0], tm * S5_CHAINS, y8.shape[2]), lambda i: (0, off + nt - 1 - i, 0)),
                  vec, pl.BlockSpec((W, W), lambda i: (0, 0)), vec],
        out_specs=row,
        compiler_params=pltpu.CompilerParams(dimension_semantics=("arbitrary",),
                                             vmem_limit_bytes=V7X_VMEM_LIMIT_BYTES),
        name="s5_glu",
    )(u, y8, y8, d.reshape(1, W), w_bf16, b.reshape(1, W))


def _s5_weights(a_re, a_im, log_dt, b_re, b_im, c_re, c_im):
    eye = jnp.eye(S5_G, dtype=F32)
    wb, a8 = [], []
    wc = []
    for di in range(2):
        ab_re, ab_im, bb_re, bb_im = s5_discretize(a_re[di], a_im[di], log_dt[di], b_re[di], b_im[di])
        blk = lambda m: jnp.einsum('gnp,gh->gphn', m, eye).reshape(W_GRP, S5_STATE)
        wb.append(jnp.concatenate([blk(bb_re), blk(bb_im)], axis=1))
        a8.append((jnp.broadcast_to(ab_re.reshape(1, S5_STATE), (BATCH, S5_STATE)),
                   jnp.broadcast_to(ab_im.reshape(1, S5_STATE), (BATCH, S5_STATE))))
        cblk = lambda m: jnp.einsum('gpn,gh->gnhp', m.astype(F32), eye).reshape(S5_STATE, W_GRP)
        wc.append(jnp.concatenate([cblk(c_re[di]), -cblk(c_im[di])], axis=0))
    wb_sb, wc_sb = [], []
    for sb in range(S5_STATE // S5_SCAN_COLS):
        cin = slice(sb * S5_SB_IN, (sb + 1) * S5_SB_IN)
        re = slice(sb * S5_SCAN_COLS, (sb + 1) * S5_SCAN_COLS)
        im = slice(S5_STATE + sb * S5_SCAN_COLS, S5_STATE + (sb + 1) * S5_SCAN_COLS)
        wb_sb.append(jnp.concatenate([jnp.concatenate([w[cin, re], w[cin, im]], axis=1) for w in wb], axis=0))
        wc_sb.append(jnp.concatenate([jnp.concatenate([w[re, cin], w[im, cin]], axis=0) for w in wc], axis=1))
    wb_sb = jnp.stack(wb_sb).astype(BF16)
    wc_sb = jnp.stack(wc_sb).astype(BF16)
    a_re8 = jnp.concatenate([a8[0][0], a8[1][0]], axis=0)
    a_im8 = jnp.concatenate([a8[0][1], a8[1][1]], axis=0)
    return wb_sb, wc_sb, a_re8, a_im8


def s5_mixer_pallas(u_lat, u_ctx, rev_lat, rev_ctx, a_re, a_im, log_dt, b_re, b_im, c_re, c_im, d, w_glu,
                    b_glu, with_ctx_out, steps=S5_STEPS, tm=S5_GLU_TILE):
    B_, L, W = u_lat.shape
    C_ = u_ctx.shape[1]
    assert B_ == BATCH and W == W_GRP
    wb, wc, a_re8, a_im8 = _s5_weights(a_re, a_im, log_dt, b_re, b_im, c_re, c_im)
    seq_f = jnp.concatenate([u_ctx.astype(BF16), u_lat.astype(BF16)], axis=1)
    seq_b = jnp.concatenate([rev_ctx, rev_lat], axis=1)
    u8 = jnp.concatenate([seq_f, seq_b], axis=0).transpose(1, 0, 2).reshape((C_ + L) * S5_CHAINS, W)
    y8 = s5_scan(u8, wb, wc, a_re8, a_im8, steps)
    w_glu_b = w_glu.astype(BF16)
    tm = min(tm, C_)
    out_lat = s5_glu_pallas(u_lat, y8, C_, d, w_glu_b, b_glu, tm).reshape(B_ * L, W)
    out_ctx = s5_glu_pallas(u_ctx, y8, 0, d, w_glu_b, b_glu, tm).reshape(B_ * C_, W) if with_ctx_out else None
    return out_lat, out_ctx


PEER_HALF = PEER_QDIM // 2


def _argmax_rows(v, r):
    while v.shape[0] > 1:
        half = v.shape[0] // 2
        take_hi = v[half:] > v[:half]
        r = jnp.where(take_hi, r[half:], r[:half])
        v = jnp.maximum(v[:half], v[half:])
    return v, r


def _topk_rows(xs, k, with_rank):
    n, lanes = xs[0].shape
    row = lax.broadcasted_iota(jnp.int32, (n, lanes), 0).astype(F32)
    krow = lax.broadcasted_iota(jnp.int32, (k, lanes), 0)

    def body(it, carry):
        out = []
        for x, vals, aux in carry:
            m, first = _argmax_rows(x, row)
            hit = row == first
            sel = krow == it
            aux = jnp.where(hit, it.astype(F32), aux) if with_rank else jnp.where(sel, first, aux)
            out.append((jnp.where(hit, -jnp.inf, x), jnp.where(sel, m, vals), aux))
        return tuple(out)

    zeros = jnp.zeros((k, lanes), F32)
    aux0 = jnp.full((n, lanes), float(k), F32) if with_rank else zeros
    res = lax.fori_loop(0, k, body, tuple((x, zeros, aux0) for x in xs))
    return [(vals, aux) for _, vals, aux in res]


PEER_GRID_COLS = tuple(PEER_TOPK // (i + 1) for i in range(PEER_TOPK))
PEER_GRID_ROWS = 64


def _peer_route_body(x_ref, shift_ref, scale_ref, wq_ref, keys_ref, h_ref, n1_ref, c1_ref, r2_ref, e2_ref,
                     q_ref, *, lane_tiles):
    hm = (x_ref[...] * (1.0 + scale_ref[0]) + shift_ref[0]).astype(BF16)
    h_ref[...] = hm
    q_ref[...] = jnp.dot(hm, wq_ref[...], preferred_element_type=F32).astype(BF16)
    K = PEER_TOPK
    key = lax.broadcasted_iota(jnp.int32, (PEER_NK, V7X_LANES), 0).astype(F32)
    for hd in range(PEER_HEADS):
        for lt in range(lane_tiles):
            tok = pl.ds(lt * V7X_LANES, V7X_LANES)
            sc = []
            for side in range(2):
                col = (2 * hd + side) * PEER_HALF
                qs = q_ref[tok, col:col + PEER_HALF]
                sc.append(lax.dot_general(keys_ref[2 * hd + side], qs, (((1,), (1,)), ((), ())),
                                          preferred_element_type=F32))
            (v1, idx1), (v2, idx2) = _topk_rows(sc, K, with_rank=False)
            cells = [v1[i:i + 1] + v2[:PEER_GRID_COLS[i]] for i in range(K)]
            cells.append(jnp.full((PEER_GRID_ROWS - sum(PEER_GRID_COLS), V7X_LANES), -jnp.inf, F32))
            (vc, rc), = _topk_rows([jnp.concatenate(cells, axis=0)], K, with_rank=True)
            z = jnp.sum(jnp.exp(vc - vc[0:1]), axis=0, keepdims=True)
            chosen = jnp.where(rc < float(K), 1.0, 0.0)
            n1 = jnp.zeros_like(key)
            r2 = jnp.full(key.shape, float(K), F32)
            off = 0
            for i in range(K):
                n_i = jnp.sum(chosen[off:off + PEER_GRID_COLS[i]], axis=0, keepdims=True)
                n1 = jnp.where(key == idx1[i:i + 1], n_i, n1)
                r2 = jnp.where(key == idx2[i:i + 1], float(i), r2)
                off += PEER_GRID_COLS[i]
            n1_ref[hd, :, tok] = n1
            c1_ref[hd, :, tok] = jnp.exp(sc[0] - v1[0:1]) / z
            r2_ref[hd, :, tok] = r2.astype(BF16)
            e2_ref[hd, :, tok] = jnp.exp(sc[1] - v2[0:1]).astype(BF16)


def peer_route(x, shift, scale, wq_bf16, keys_bf16, rows_per_mod, tm):
    T, D = x.shape
    assert T % tm == 0 and rows_per_mod % tm == 0 and tm % V7X_LANES == 0
    tiles_per_mod = rows_per_mod // tm
    mod_spec = pl.BlockSpec((1, 1, D), lambda i: (i // tiles_per_mod, 0, 0))
    tab = lambda dt: jax.ShapeDtypeStruct((PEER_HEADS, PEER_NK, T), dt)
    tab_spec = pl.BlockSpec((PEER_HEADS, PEER_NK, tm), lambda i: (0, 0, i))
    return pl.pallas_call(
        functools.partial(_peer_route_body, lane_tiles=tm // V7X_LANES),
        out_shape=[jax.ShapeDtypeStruct((T, D), BF16), tab(F32), tab(F32), tab(BF16), tab(BF16)],
        grid=(T // tm,),
        in_specs=[pl.BlockSpec((tm, D), lambda i: (i, 0)), mod_spec, mod_spec,
                  pl.BlockSpec(wq_bf16.shape, lambda i: (0, 0)),
                  pl.BlockSpec(keys_bf16.shape, lambda i: (0, 0, 0))],
        out_specs=[pl.BlockSpec((tm, D), lambda i: (i, 0)), tab_spec, tab_spec, tab_spec, tab_spec],
        scratch_shapes=[pltpu.VMEM((tm, PEER_HEADS * PEER_QDIM), BF16)],
        compiler_params=pltpu.CompilerParams(dimension_semantics=("arbitrary",),
                                             vmem_limit_bytes=V7X_VMEM_LIMIT_BYTES),
        name="peer_route",
    )(x, shift, scale, wq_bf16, keys_bf16)


def _peer_dense_body(h_ref, u_ref, vt_ref, n1_ref, c1_ref, r2_ref, e2_ref, x_ref, gate_ref, g_ref, b_ref,
                     o_ref, acc_ref, a_ref, *, n_slab):
    j = pl.program_id(1)

    @pl.when(j == 0)
    def _():
        acc_ref[...] = jnp.zeros_like(acc_ref)

    rows = PEER_SCORE_SLABS * PEER_NK
    for p in range(n_slab // PEER_SCORE_SLABS):
        s = lax.dot_general(u_ref[p * rows:(p + 1) * rows, :], h_ref[...], (((1,), (1,)), ((), ())),
                            preferred_element_type=F32)
        for kk in range(PEER_SCORE_SLABS):
            k = PEER_SCORE_SLABS * p + kk
            for lt in range(h_ref.shape[0] // V7X_LANES):
                lanes = slice(lt * V7X_LANES, (lt + 1) * V7X_LANES)
                g = None
                for hd in range(PEER_HEADS):
                    n1row = n1_ref[hd, k:k + 1, lanes].astype(BF16)
                    c1row = c1_ref[hd, k:k + 1, lanes].astype(BF16)
                    sel = lax.clamp(jnp.zeros((), BF16), n1row - r2_ref[hd, :, lanes], jnp.ones((), BF16))
                    gh = sel * e2_ref[hd, :, lanes] * c1row
                    g = gh if g is None else g + gh
                sk = s[kk * PEER_NK:(kk + 1) * PEER_NK, lanes]
                a_ref[k * PEER_NK:(k + 1) * PEER_NK, lanes] = jax.nn.gelu(sk).astype(BF16) * g
    acc_ref[...] += jnp.dot(vt_ref[0], a_ref[...], preferred_element_type=F32)

    @pl.when(j == pl.num_programs(1) - 1)
    def _():
        hres = DEEPNORM_ALPHA * x_ref[...] + gate_ref[0] * acc_ref[...].T
        mu = jnp.mean(hres, -1, keepdims=True)
        hc = hres - mu
        var = jnp.mean(hc * hc, -1, keepdims=True)
        o_ref[...] = hc * lax.rsqrt(var + LN_EPS) * g_ref[...] + b_ref[...]


def peer_dense(h_bf16, u_bf16, vt_tiles, n1, c1, r2, e2, x, gate, ln_g, ln_b, rows_per_mod, tm, n_slab):
    T, D = h_bf16.shape
    NE = u_bf16.shape[0]
    e_tile = n_slab * PEER_NK
    n_e = NE // e_tile
    assert T % tm == 0 and NE % e_tile == 0 and vt_tiles.shape == (n_e, D, e_tile)
    slab_spec = pl.BlockSpec((PEER_HEADS, n_slab, tm), lambda i, j: (0, j, i))
    tok_spec = pl.BlockSpec((PEER_HEADS, PEER_NK, tm), lambda i, j: (0, 0, i))
    return pl.pallas_call(
        functools.partial(_peer_dense_body, n_slab=n_slab),
        out_shape=jax.ShapeDtypeStruct((T, D), F32),
        grid=(T // tm, n_e),
        in_specs=[pl.BlockSpec((tm, D), lambda i, j: (i, 0)),
                  pl.BlockSpec((e_tile, D), lambda i, j: (j, 0)),
                  pl.BlockSpec((1, D, e_tile), lambda i, j: (j, 0, 0)),
                  slab_spec, slab_spec, tok_spec, tok_spec,
                  pl.BlockSpec((tm, D), lambda i, j: (i, 0)),
                  pl.BlockSpec((1, 1, D), lambda i, j: (i // (rows_per_mod // tm), 0, 0)),
                  pl.BlockSpec((1, D), lambda i, j: (0, 0)), pl.BlockSpec((1, D), lambda i, j: (0, 0))],
        out_specs=pl.BlockSpec((tm, D), lambda i, j: (i, 0)),
        scratch_shapes=[pltpu.VMEM((D, tm), F32), pltpu.VMEM((e_tile, tm), BF16)],
        compiler_params=pltpu.CompilerParams(dimension_semantics=("arbitrary", "arbitrary"),
                                             vmem_limit_bytes=V7X_VMEM_LIMIT_BYTES),
        name="peer_dense",
    )(h_bf16, u_bf16, vt_tiles, n1, c1, r2, e2, x, gate, ln_g.reshape(1, D), ln_b.reshape(1, D))


PEER_SLABS = 8
PEER_SCORE_SLABS = 4


def peer_v_tiles(v_tab):
    e_tile = PEER_SLABS * PEER_NK
    return v_tab.astype(BF16).reshape(v_tab.shape[0] // e_tile, e_tile, v_tab.shape[1]).transpose(0, 2, 1)


def peer_sublayer(x, shift, scale, gate, wq_bf16, keys_bf16, u_bf16, vt_tiles, ln_g, ln_b, rows_per_mod,
                  tm_route=PEER_ROUTE_TILE, tm_dense=PEER_DENSE_TILE):
    h, n1, c1, r2, e2 = peer_route(x, shift, scale, wq_bf16, keys_bf16, rows_per_mod, tm_route)
    return peer_dense(h, u_bf16, vt_tiles, n1, c1, r2, e2, x, gate, ln_g, ln_b, rows_per_mod, tm_dense,
                      PEER_SLABS)


POOL_HALO = 8
POOL_ROWS = 256


def _pool_body(u_ref, w_ref, scale_ref, o_ref, pad_ref, *, seq_len):
    L = seq_len
    chunk = min(POOL_ROWS, L)
    zeros = jnp.zeros((POOL_HALO, W_GRP), F32)
    pad_ref[0:POOL_HALO, :] = zeros
    pad_ref[POOL_HALO + L:2 * POOL_HALO + L, :] = zeros
    pad_ref[POOL_HALO:POOL_HALO + L, :] = u_ref[0]
    for r0 in range(0, L, chunk):
        t = r0 + lax.broadcasted_iota(jnp.int32, (chunk, POOL_C), 0)
        for j, w in enumerate(POOL_WINDOWS):
            cols = slice(j * POOL_C, (j + 1) * POOL_C)
            acc = None
            for o in range(-(w // 2), w - w // 2):
                s = pad_ref[POOL_HALO + r0 + o:POOL_HALO + r0 + o + chunk, cols]
                acc = s if acc is None else acc + s
            lo = jnp.maximum(t - w // 2, 0)
            hi = jnp.minimum(t - w // 2 + w - 1, L - 1)
            cnt = (hi - lo + 1).astype(F32)
            pooled = acc / cnt - u_ref[0, r0:r0 + chunk, cols]
            y = jnp.dot(pooled.astype(BF16), w_ref[j], preferred_element_type=F32)
            o_ref[0, r0:r0 + chunk, cols] = y * scale_ref[:, cols]


def pool_mix_pallas(u, pool_w, pool_scale):
    B_, L, W = u.shape
    assert max(POOL_WINDOWS) // 2 <= POOL_HALO and L % min(POOL_ROWS, L) == 0
    blk = pl.BlockSpec((1, L, W), lambda b: (b, 0, 0))
    return pl.pallas_call(
        functools.partial(_pool_body, seq_len=L),
        out_shape=jax.ShapeDtypeStruct((B_, L, W), F32),
        grid=(B_,),
        in_specs=[blk, pl.BlockSpec(pool_w.shape, lambda b: (0, 0, 0)), pl.BlockSpec((1, W), lambda b: (0, 0))],
        out_specs=blk,
        scratch_shapes=[pltpu.VMEM((L + 2 * POOL_HALO, W), F32)],
        compiler_params=pltpu.CompilerParams(dimension_semantics=("arbitrary",),
                                             vmem_limit_bytes=V7X_VMEM_LIMIT_BYTES),
        name="pool_mix",
    )(u, pool_w.astype(BF16), pool_scale.reshape(1, W))


NA_TILE_R = 4
NA_TILE = NA_TILE_R * GRID_W
NA_SCALE = NA_HD ** -0.5


def _na_bias_table(rpb, n_rows):
    n_tiles = n_rows // NA_TILE_R
    KR = min(NA_WIN_R, n_rows)
    a = np.array([0, min(2, n_tiles - 1), n_tiles - 1]).reshape(3, 1, 1, 1, 1, 1)
    d = np.arange(3).reshape(1, 3, 1, 1, 1, 1)
    i = np.arange(NA_TILE_R).reshape(1, 1, NA_TILE_R, 1, 1, 1)
    qc = np.arange(GRID_W).reshape(1, 1, 1, GRID_W, 1, 1)
    j = np.arange(NA_TILE_R).reshape(1, 1, 1, 1, NA_TILE_R, 1)
    kc = np.arange(GRID_W).reshape(1, 1, 1, 1, 1, GRID_W)
    qr = NA_TILE_R * a + i
    kr = NA_TILE_R * (a + d - 1) + j
    rs = np.clip(qr - KR // 2, 0, n_rows - KR)
    c0 = np.clip(qc - NA_WIN_C // 2, 0, GRID_W - NA_WIN_C)
    ok = (kr >= rs) & (kr < rs + KR) & (kr >= 0) & (kr < n_rows) & (kc >= c0) & (kc < c0 + NA_WIN_C)
    row_rel = np.clip(kr - qr + NA_WIN_R - 1, 0, 2 * NA_WIN_R - 2)[0, :, :, 0, :, 0]
    col_rel = (np.clip(kc - qc, -(NA_WIN_C - 1), NA_WIN_C - 1) + NA_WIN_C - 1)[0, 0, 0, :, 0, :]
    onehot = (col_rel[None] == np.arange(2 * NA_WIN_C - 1)[:, None, None]).astype(np.float32)
    bias_rc = jnp.einsum('hrc,cqk->hrqk', rpb.astype(F32), onehot, precision=lax.Precision.HIGHEST)
    tab = jnp.stack([bias_rc[:, int(r)] for r in row_rel.reshape(-1)], axis=1)
    tab = tab.reshape(NA_H, 3, NA_TILE_R, NA_TILE_R, GRID_W, GRID_W).transpose(0, 1, 2, 4, 3, 5)
    tab = jnp.where(ok[:, None], tab[None], -jnp.inf)
    return tab.reshape(3, NA_H, 3, NA_TILE, NA_TILE)


def _na_body(q_ref, k_ref, v_ref, kc_ref, vc_ref, t_ref, o_ref, *, n_tiles, with_grid):
    a = pl.program_id(1)
    nt = (((1,), (1,)), ((), ()))
    for h in range(NA_H):
        hs = slice(h * NA_HD, (h + 1) * NA_HD)
        qh = q_ref[:, hs]
        scores = [lax.dot_general(qh, kc_ref[:, hs], nt, preferred_element_type=F32) * NA_SCALE]
        vals = [vc_ref[:, hs]]
        if with_grid:
            for d in range(3):
                ti = jnp.clip(a + d - 1, 0, n_tiles - 1)
                rows = pl.ds(pl.multiple_of(ti * NA_TILE, NA_TILE), NA_TILE)
                s = lax.dot_general(qh, k_ref[rows, hs], nt, preferred_element_type=F32)
                scores.append(s * NA_SCALE + t_ref[0, h, d])
                vals.append(v_ref[rows, hs])
        m = scores[0].max(axis=-1, keepdims=True)
        for s in scores[1:]:
            m = jnp.maximum(m, s.max(axis=-1, keepdims=True))
        den = None
        acc = None
        for s, vv in zip(scores, vals):
            p = jnp.exp(s - m)
            l = p.sum(axis=-1, keepdims=True)
            o = jnp.dot(p.astype(BF16), vv, preferred_element_type=F32)
            den = l if den is None else den + l
            acc = o if acc is None else acc + o
        o_ref[:, hs] = acc / den


def na_attention(q, k, v, kc, vc, table, seq_len, ctx_len, with_grid):
    W = q.shape[1]
    n_b = kc.shape[0] // ctx_len
    lq = q.shape[0] // n_b
    assert lq % NA_TILE == 0
    q_tiles = lq // NA_TILE
    n_tiles = seq_len // NA_TILE
    pat = lambda b, a: (jnp.where(a == 0, 0, jnp.where(a == n_tiles - 1, 2, 1)), 0, 0, 0, 0)
    qo_spec = pl.BlockSpec((NA_TILE, W), lambda b, a: (b * q_tiles + a, 0))
    return pl.pallas_call(
        functools.partial(_na_body, n_tiles=n_tiles, with_grid=with_grid),
        out_shape=jax.ShapeDtypeStruct(q.shape, F32),
        grid=(n_b, q_tiles),
        in_specs=[qo_spec,
                  pl.BlockSpec((seq_len, W), lambda b, a: (b, 0)), pl.BlockSpec((seq_len, W), lambda b, a: (b, 0)),
                  pl.BlockSpec((ctx_len, W), lambda b, a: (b, 0)), pl.BlockSpec((ctx_len, W), lambda b, a: (b, 0)),
                  pl.BlockSpec((1,) + table.shape[1:], pat)],
        out_specs=qo_spec,
        compiler_params=pltpu.CompilerParams(dimension_semantics=("arbitrary", "arbitrary"),
                                             vmem_limit_bytes=V7X_VMEM_LIMIT_BYTES),
        name="na_attention",
    )(q, k, v, kc, vc, table)


SSM_HALO = 8
SSM_BC = SSM_G * SSM_N


def _ssm_conv_body(prev_ref, cur_ref, next_ref, w_ref, b_ref, o_ref, pad_ref, *, tiles_per_seq):
    pos = pl.program_id(0) % tiles_per_seq
    tm = cur_ref.shape[0]
    pad_ref[0:SSM_HALO, :] = jnp.where(pos == 0, 0.0, prev_ref[...])
    pad_ref[SSM_HALO:SSM_HALO + tm, :] = cur_ref[...]
    pad_ref[SSM_HALO + tm:2 * SSM_HALO + tm, :] = jnp.where(pos == tiles_per_seq - 1, 0.0, next_ref[...])
    lead = (SSM_CONV - 1) // 2
    y = b_ref[...]
    for k in range(SSM_CONV):
        y = y + w_ref[k:k + 1, :] * pad_ref[SSM_HALO - lead + k:SSM_HALO - lead + k + tm, :]
    o_ref[...] = jax.nn.silu(y)


def ssm_conv(xbc, conv_w, conv_b, seq_len, tm):
    T, CH = xbc.shape
    assert seq_len % tm == 0 and tm % SSM_HALO == 0
    hb = tm // SSM_HALO
    n_hb = T // SSM_HALO
    return pl.pallas_call(
        functools.partial(_ssm_conv_body, tiles_per_seq=seq_len // tm),
        out_shape=jax.ShapeDtypeStruct((T, CH), F32),
        grid=(T // tm,),
        in_specs=[pl.BlockSpec((SSM_HALO, CH), lambda i: (jnp.maximum(i * hb - 1, 0), 0)),
                  pl.BlockSpec((tm, CH), lambda i: (i, 0)),
                  pl.BlockSpec((SSM_HALO, CH), lambda i: (jnp.minimum((i + 1) * hb, n_hb - 1), 0)),
                  pl.BlockSpec((SSM_CONV, CH), lambda i: (0, 0)), pl.BlockSpec((1, CH), lambda i: (0, 0))],
        out_specs=pl.BlockSpec((tm, CH), lambda i: (i, 0)),
        scratch_shapes=[pltpu.VMEM((tm + 2 * SSM_HALO, CH), F32)],
        compiler_params=pltpu.CompilerParams(dimension_semantics=("arbitrary",),
                                             vmem_limit_bytes=V7X_VMEM_LIMIT_BYTES),
        name="ssm_conv",
    )(xbc, xbc, xbc, conv_w, conv_b.reshape(1, CH))


def _ssd_chunk_prep(xbc_ref, dt_ref, bias_ref, a_ref, *, di):
    rev = di == 1
    Q = xbc_ref.shape[0]
    dt_all = jax.nn.softplus(dt_ref[...] + bias_ref[...])
    a_cs = dt_all * a_ref[...]
    row_id = lax.broadcasted_iota(jnp.int32, a_cs.shape, 0)
    sh = 1
    while sh < Q:
        if rev:
            a_cs = a_cs + jnp.where(row_id < Q - sh, pltpu.roll(a_cs, Q - sh, 0), 0.0)
        else:
            a_cs = a_cs + jnp.where(row_id >= sh, pltpu.roll(a_cs, sh, 0), 0.0)
        sh *= 2
    a_cs_t = a_cs.T
    a_tot = a_cs[0:1, :] if rev else a_cs[Q - 1:Q, :]
    l_id = lax.broadcasted_iota(jnp.int32, (Q, Q), 0)
    s_id = lax.broadcasted_iota(jnp.int32, (Q, Q), 1)
    causal = (l_id <= s_id) if rev else (l_id >= s_id)
    cb = []
    for g in range(SSM_G):
        bg = xbc_ref[:, W_GRP + g * SSM_N:W_GRP + (g + 1) * SSM_N].astype(BF16)
        cg = xbc_ref[:, W_GRP + SSM_BC + g * SSM_N:W_GRP + SSM_BC + (g + 1) * SSM_N].astype(BF16)
        cb.append((bg, cg, lax.dot_general(cg, bg, (((1,), (1,)), ((), ())), preferred_element_type=F32)))
    return dt_all, a_cs, a_cs_t, a_tot, causal, cb


def _ssd_head(prep, xbc_ref, h_scr, y_ref, *, di, h):
    dt_all, a_cs, a_cs_t, a_tot, causal, cb = prep
    nt = (((1,), (1,)), ((), ()))
    tn = (((0,), (0,)), ((), ()))
    c = di * SSM_H + h
    bg, cg, cbg = cb[h // (SSM_H // SSM_G)]
    col = a_cs[:, c:c + 1]
    lm = jnp.exp(jnp.where(causal, col - a_cs_t[c:c + 1, :], -jnp.inf))
    xh = xbc_ref[:, h * SSM_HD:(h + 1) * SSM_HD] * dt_all[:, c:c + 1]
    hp = h_scr[di, h]
    yd = jnp.dot((cbg * lm).astype(BF16), xh.astype(BF16), preferred_element_type=F32)
    yo = lax.dot_general(cg, hp.astype(BF16), nt, preferred_element_type=F32) * jnp.exp(col)
    y_ref[:, h * SSM_HD:(h + 1) * SSM_HD] = yd + yo
    tot = a_tot[:, c:c + 1]
    xd = (xh * jnp.exp(tot - col)).astype(BF16)
    h_scr[di, h] = jnp.exp(tot) * hp + lax.dot_general(xd, bg, tn, preferred_element_type=F32)


def _ssd_body(xf_ref, dtf_ref, xb_ref, dtb_ref, bias_ref, a_ref, h0_ref, yf_ref, yb_ref, hfin_ref, h_scr):
    ci = pl.program_id(1)

    @pl.when(ci == 0)
    def _():
        h_scr[...] = h0_ref[:, 0]

    prep_f = _ssd_chunk_prep(xf_ref, dtf_ref, bias_ref, a_ref, di=0)
    prep_b = _ssd_chunk_prep(xb_ref, dtb_ref, bias_ref, a_ref, di=1)
    for h in range(SSM_H):
        _ssd_head(prep_f, xf_ref, h_scr, yf_ref, di=0, h=h)
        _ssd_head(prep_b, xb_ref, h_scr, yb_ref, di=1, h=h)

    @pl.when(ci == pl.num_programs(1) - 1)
    def _():
        hfin_ref[:, 0] = h_scr[...]


def ssd_scan(xbc_act, dt_raw, bias128, a128, h0, seq_len):
    T = xbc_act.shape[0]
    n_b = T // seq_len
    Q = min(SSM_CHUNK, seq_len)
    nc = seq_len // Q
    fwd = lambda b, i: (b * nc + i, 0)
    bwd = lambda b, i: (b * nc + nc - 1 - i, 0)
    vec = pl.BlockSpec((1, V7X_LANES), lambda b, i: (0, 0))
    st = pl.BlockSpec((2, 1, SSM_H, SSM_HD, SSM_N), lambda b, i: (0, b, 0, 0, 0))
    y_shape = jax.ShapeDtypeStruct((T, W_GRP), F32)
    return pl.pallas_call(
        _ssd_body,
        out_shape=[y_shape, y_shape, jax.ShapeDtypeStruct(h0.shape, F32)],
        grid=(n_b, nc),
        in_specs=[pl.BlockSpec((Q, SSM_CONV_CH), fwd), pl.BlockSpec((Q, V7X_LANES), fwd),
                  pl.BlockSpec((Q, SSM_CONV_CH), bwd), pl.BlockSpec((Q, V7X_LANES), bwd), vec, vec, st],
        out_specs=[pl.BlockSpec((Q, W_GRP), fwd), pl.BlockSpec((Q, W_GRP), bwd), st],
        scratch_shapes=[pltpu.VMEM((2, SSM_H, SSM_HD, SSM_N), F32)],
        compiler_params=pltpu.CompilerParams(dimension_semantics=("arbitrary", "arbitrary"),
                                             vmem_limit_bytes=V7X_VMEM_LIMIT_BYTES),
        name="ssd_scan",
    )(xbc_act, dt_raw, xbc_act, dt_raw, bias128, a128, h0)


def _ssm_out_body(xbc_ref, yf_ref, yb_ref, z_ref, d_ref, nw_ref, o_ref):
    y = d_ref[...] * xbc_ref[:, :W_GRP] + yf_ref[...] + yb_ref[...]
    g = y * jax.nn.silu(z_ref[...])
    gw = W_GRP // SSM_G
    for k in range(SSM_G):
        gk = g[:, k * gw:(k + 1) * gw]
        r = lax.rsqrt(jnp.mean(gk * gk, -1, keepdims=True) + LN_EPS)
        o_ref[:, k * gw:(k + 1) * gw] = gk * r * nw_ref[:, k * gw:(k + 1) * gw]


def ssm_out(xbc_act, yf, yb, z, d512, norm_w, tm):
    T = z.shape[0]
    assert T % tm == 0
    row = pl.BlockSpec((tm, W_GRP), lambda i: (i, 0))
    vec = pl.BlockSpec((1, W_GRP), lambda i: (0, 0))
    return pl.pallas_call(
        _ssm_out_body,
        out_shape=jax.ShapeDtypeStruct((T, W_GRP), F32),
        grid=(T // tm,),
        in_specs=[pl.BlockSpec((tm, SSM_CONV_CH), lambda i: (i, 0)), row, row, row, vec, vec],
        out_specs=row,
        compiler_params=pltpu.CompilerParams(dimension_semantics=("arbitrary",),
                                             vmem_limit_bytes=V7X_VMEM_LIMIT_BYTES),
        name="ssm_out",
    )(xbc_act, yf, yb, z, d512, norm_w.reshape(1, W_GRP))


def ssm_mixer_pallas(z_l, xbc_l, dt_l, z_c, xbc_c, dt_c, conv_w, conv_b, dt_bias, a_log, d, norm_w,
                     seq_len, ctx_len, with_ctx_out, tm=ROW_TILE):
    n_b = z_l.shape[0] // seq_len
    pad = V7X_LANES - 2 * SSM_H
    bias128 = jnp.pad(dt_bias.astype(F32).reshape(1, 2 * SSM_H), ((0, 0), (0, pad)))
    a128 = jnp.pad(-jnp.exp(a_log.astype(F32)).reshape(1, 2 * SSM_H), ((0, 0), (0, pad)))
    d512 = jnp.repeat(d.astype(F32), SSM_HD).reshape(1, W_GRP)
    act_c = ssm_conv(xbc_c, conv_w, conv_b, ctx_len, min(tm, ctx_len))
    act_l = ssm_conv(xbc_l, conv_w, conv_b, seq_len, tm)
    zero = jnp.zeros((2, n_b, SSM_H, SSM_HD, SSM_N), F32)
    ycf, ycb, hc = ssd_scan(act_c, dt_c, bias128, a128, zero, ctx_len)
    ylf, ylb, _ = ssd_scan(act_l, dt_l, bias128, a128, hc, seq_len)
    out_l = ssm_out(act_l, ylf, ylb, z_l, d512, norm_w, tm)
    out_c = ssm_out(act_c, ycf, ycb, z_c, d512, norm_w, min(tm, z_c.shape[0])) if with_ctx_out else None
    return out_l, out_c


def s5_discretize(a_re, a_im, log_dt, b_re, b_im):
    a_re, a_im = a_re.astype(F32), a_im.astype(F32)
    b_re, b_im = b_re.astype(F32), b_im.astype(F32)
    dt = jnp.exp(log_dt.astype(F32))[:, None]
    mag = jnp.exp(a_re * dt)
    ab_re = mag * jnp.cos(a_im * dt)
    ab_im = mag * jnp.sin(a_im * dt)
    den = a_re * a_re + a_im * a_im
    f_re = ((ab_re - 1) * a_re + ab_im * a_im) / den
    f_im = (ab_im * a_re - (ab_re - 1) * a_im) / den
    bb_re = f_re[..., None] * b_re - f_im[..., None] * b_im
    bb_im = f_re[..., None] * b_im + f_im[..., None] * b_re
    return ab_re, ab_im, bb_re, bb_im


def _mixers(p_l, p_c, prm, with_ctx_out):
    s5_l, pool_l, q_l, k_l, v_l, z_l, xbc_l, dt_l, s5r_l = p_l
    s5_c, pool_c, q_c, k_c, v_c, z_c, xbc_c, dt_c, s5r_c = p_c
    ya_l, ya_c = s5_mixer_pallas(s5_l, s5_c, s5r_l, s5r_c, prm["s5_a_re"], prm["s5_a_im"], prm["s5_log_dt"],
                                 prm["s5_b_re"], prm["s5_b_im"], prm["s5_c_re"], prm["s5_c_im"], prm["s5_d"],
                                 prm["s5_w_glu"], prm["s5_b_glu"], with_ctx_out)
    yb_l = pool_mix_pallas(pool_l, prm["pool_w"], prm["pool_scale"])
    yb_c = pool_mix_pallas(pool_c, prm["pool_w"], prm["pool_scale"]) if with_ctx_out else None
    B_, L, _ = s5_l.shape
    C_ = s5_c.shape[1]
    flat = lambda t: t.reshape(-1, t.shape[-1])
    table = _na_bias_table(prm["na_rpb"], L // GRID_W)
    yc_l = na_attention(flat(q_l), flat(k_l), flat(v_l), flat(k_c), flat(v_c), table, L, C_, True)
    yc_c = (na_attention(flat(q_c), flat(k_c), flat(v_c), flat(k_c), flat(v_c), table, C_, C_, False)
            if with_ctx_out else None)
    yd_l, yd_c = ssm_mixer_pallas(flat(z_l), flat(xbc_l), flat(dt_l), flat(z_c), flat(xbc_c), flat(dt_c),
                                  prm["ssm_conv_w"], prm["ssm_conv_b"], prm["ssm_dt_bias"], prm["ssm_a_log"],
                                  prm["ssm_d"], prm["ssm_norm_w"], L, C_, with_ctx_out)
    return (ya_l, yb_l, yc_l, yd_l), (ya_c, yb_c, yc_c, yd_c)


def kernel(x, c, ctx, c_ctx, w_ada, b_ada, w_in, w_out, s5_a_re, s5_a_im, s5_log_dt, s5_b_re, s5_b_im, s5_c_re, s5_c_im, s5_d, s5_w_glu, s5_b_glu, pool_w, pool_scale, na_rpb, ssm_conv_w, ssm_conv_b, ssm_dt_bias, ssm_a_log, ssm_d, ssm_norm_w, ln1_g, ln1_b, ln2_g, ln2_b, peer_w_q, peer_sub_keys, peer_u, peer_v):
    B_, L, D = x.shape
    C_ = ctx.shape[1]
    x_lat = x.reshape(B_ * L, D)
    x_ctx = ctx.reshape(B_ * C_, D)
    assert B_ + 1 <= ADA_ROWS
    cond = jnp.concatenate([c, c_ctx[None], jnp.zeros((ADA_ROWS - B_ - 1, D), F32)], axis=0)
    dt_pad = V7X_LANES - IN_SPLITS[-1]
    splits = IN_SPLITS[:-1] + (V7X_LANES,)
    in_dtypes = (F32, F32, BF16, BF16, BF16, F32, F32, F32)
    TM = ROW_TILE
    for l in range(DEPTH):
        last = l == DEPTH - 1
        prm = dict(s5_a_re=s5_a_re[l], s5_a_im=s5_a_im[l], s5_log_dt=s5_log_dt[l], s5_b_re=s5_b_re[l],
                   s5_b_im=s5_b_im[l], s5_c_re=s5_c_re[l], s5_c_im=s5_c_im[l], s5_d=s5_d[l],
                   s5_w_glu=s5_w_glu[l], s5_b_glu=s5_b_glu[l], pool_w=pool_w[l], pool_scale=pool_scale[l],
                   na_rpb=na_rpb[l], ssm_conv_w=ssm_conv_w[l], ssm_conv_b=ssm_conv_b[l],
                   ssm_dt_bias=ssm_dt_bias[l], ssm_a_log=ssm_a_log[l], ssm_d=ssm_d[l],
                   ssm_norm_w=ssm_norm_w[l])
        mod = ada_modulation(cond, w_ada[l], b_ada[l])
        m_lat = mod[:B_].reshape(B_, 6, 1, D)
        m_ctx = mod[B_:B_ + 1].reshape(1, 6, 1, D)
        w_in_b = jnp.pad(w_in[l], ((0, 0), (0, dt_pad))).astype(BF16)
        w_out_b = w_out[l].astype(BF16)

        p_l = modulated_matmul(x_lat, m_lat[:, 0], m_lat[:, 1], w_in_b, splits, in_dtypes, L, L, TM)
        p_c = modulated_matmul(x_ctx, m_ctx[:, 0], m_ctx[:, 1], w_in_b, splits, in_dtypes, B_ * C_, C_, TM)
        p_l = [a.reshape(B_, L, -1) for a in p_l]
        p_c = [a.reshape(B_, C_, -1) for a in p_c]
        y_l, y_c = _mixers(p_l, p_c, prm, not last)

        x_lat = proj_residual_ln([a.reshape(B_ * L, -1) for a in y_l], x_lat, m_lat[:, 2], w_out_b,
                                 ln1_g[l], ln1_b[l], L, TM)
        wq_b = peer_w_q[l].astype(BF16)
        keys_b = peer_sub_keys[l].reshape(2 * PEER_HEADS, PEER_NK, PEER_HALF).astype(BF16)
        u_b = peer_u[l].astype(BF16)
        vt_b = peer_v_tiles(peer_v[l])
        x_lat = peer_sublayer(x_lat, m_lat[:, 3], m_lat[:, 4], m_lat[:, 5], wq_b, keys_b, u_b, vt_b,
                              ln2_g[l], ln2_b[l], L)
        if not last:
            x_ctx = proj_residual_ln([a.reshape(B_ * C_, -1) for a in y_c], x_ctx, m_ctx[:, 2], w_out_b,
                                     ln1_g[l], ln1_b[l], B_ * C_, TM)
            x_ctx = peer_sublayer(x_ctx, m_ctx[:, 3], m_ctx[:, 4], m_ctx[:, 5], wq_b, keys_b, u_b, vt_b,
                                  ln2_g[l], ln2_b[l], B_ * C_)
    return x_lat.reshape(B_, L, D)
```

```python
import functools

import jax
import jax.numpy as jnp
import numpy as np
from jax import lax
from jax.experimental import pallas as pl
from jax.experimental.pallas import tpu as pltpu

D_MODEL = 2048
BATCH = 4
SEQ = 4096
DEPTH = 2

GRID_W = 64
CTX_LEN = 256
N_MIXERS = 4
D_MIX = D_MODEL
W_GRP = D_MIX // N_MIXERS

S5_P = 16
S5_G = W_GRP // S5_P
S5_N = 64

POOL_WINDOWS = (2, 4, 8, 16)
POOL_C = W_GRP // len(POOL_WINDOWS)

NA_HD = 64
NA_H = W_GRP // NA_HD
NA_WIN_R = 8
NA_WIN_C = 16

SSM_HD = 64
SSM_H = W_GRP // SSM_HD
SSM_G = 2
SSM_N = 128
SSM_CONV = 4
SSM_CHUNK = 128
SSM_CONV_CH = W_GRP + 2 * SSM_G * SSM_N

PEER_HEADS = 8
PEER_NK = 128
PEER_NE = PEER_NK * PEER_NK
PEER_QDIM = 256
PEER_TOPK = 16
PEER_BLOCK = 128

IN_SPLITS = (W_GRP, W_GRP, W_GRP, W_GRP, W_GRP, W_GRP, SSM_CONV_CH, 2 * SSM_H)
D_IN = sum(IN_SPLITS)

DEEPNORM_ALPHA = (2 * DEPTH) ** 0.25
DEEPNORM_BETA = (8 * DEPTH) ** -0.25
LN_EPS = 1e-5
F32 = jnp.float32
BF16 = jnp.bfloat16

V7X_LANES = 128
V7X_VMEM_BYTES = 64 * 1024 * 1024
V7X_VMEM_LIMIT_BYTES = V7X_VMEM_BYTES * 7 // 8

ROW_TILE = 256
S5_STEPS = 64
S5_GLU_TILE = 256
PEER_ROUTE_TILE = 256
PEER_DENSE_TILE = 512


ADA_ROWS = 8
ADA_TN = 1536


def _ada_body(c_ref, w_ref, b_ref, o_ref):
    act = jax.nn.silu(c_ref[...]).astype(BF16)
    o_ref[...] = jnp.dot(act, w_ref[...].astype(BF16), preferred_element_type=F32) + b_ref[...]


def ada_modulation(c8, w, b):
    R, D = c8.shape
    N = w.shape[1]
    assert N % ADA_TN == 0
    return pl.pallas_call(
        _ada_body,
        out_shape=jax.ShapeDtypeStruct((R, N), F32),
        grid=(N // ADA_TN,),
        in_specs=[pl.BlockSpec((R, D), lambda j: (0, 0)), pl.BlockSpec((D, ADA_TN), lambda j: (0, j)),
                  pl.BlockSpec((1, ADA_TN), lambda j: (0, j))],
        out_specs=pl.BlockSpec((R, ADA_TN), lambda j: (0, j)),
        compiler_params=pltpu.CompilerParams(dimension_semantics=("arbitrary",),
                                             vmem_limit_bytes=V7X_VMEM_LIMIT_BYTES),
        name="ada_modulation",
    )(c8, w, b.reshape(1, N))
def _modmm_body(x_ref, shift_ref, scale_ref, w_ref, *out_refs, col_splits):
    xm = (x_ref[...] * (1.0 + scale_ref[0]) + shift_ref[0]).astype(BF16)
    o = 0
    for j, (ref, n) in enumerate(zip(out_refs, col_splits)):
        val = jnp.dot(xm, w_ref[:, o:o + n], preferred_element_type=F32)
        ref[...] = val.astype(ref.dtype)
        if j == 0:
            tm = val.shape[0]
            r = lax.broadcasted_iota(jnp.int32, (tm, tm), 0)
            c = lax.broadcasted_iota(jnp.int32, (tm, tm), 1)
            flip = jnp.where(r + c == tm - 1, 1.0, 0.0).astype(BF16)
            out_refs[-1][...] = jnp.dot(flip, val.astype(BF16), preferred_element_type=F32).astype(BF16)
        o += n


def modulated_matmul(x, shift, scale, w_bf16, col_splits, out_dtypes, rows_per_mod, seq_len, tm):
    T, K = x.shape
    assert T % tm == 0 and rows_per_mod % tm == 0 and seq_len % tm == 0
    tiles_per_mod = rows_per_mod // tm
    tps = seq_len // tm
    n_tot = sum(col_splits)
    assert w_bf16.shape == (K, n_tot)
    mod_spec = pl.BlockSpec((1, 1, K), lambda i: (i // tiles_per_mod, 0, 0))
    shapes = [jax.ShapeDtypeStruct((T, n), dt) for n, dt in zip(col_splits, out_dtypes, strict=True)]
    specs = [pl.BlockSpec((tm, n), lambda i: (i, 0)) for n in col_splits]
    shapes.append(jax.ShapeDtypeStruct((T, col_splits[0]), BF16))
    specs.append(pl.BlockSpec((tm, col_splits[0]), lambda i: ((i // tps) * tps + tps - 1 - i % tps, 0)))
    return pl.pallas_call(
        functools.partial(_modmm_body, col_splits=tuple(col_splits)),
        out_shape=shapes,
        grid=(T // tm,),
        in_specs=[pl.BlockSpec((tm, K), lambda i: (i, 0)), mod_spec, mod_spec,
                  pl.BlockSpec((K, n_tot), lambda i: (0, 0))],
        out_specs=specs,
        compiler_params=pltpu.CompilerParams(dimension_semantics=("arbitrary",),
                                             vmem_limit_bytes=V7X_VMEM_LIMIT_BYTES),
        name="modulated_matmul",
    )(x, shift, scale, w_bf16)


def _proj_ln_body(*refs, n_parts, alpha):
    part_refs = refs[:n_parts]
    x_ref, gate_ref, w_ref, g_ref, b_ref, o_ref = refs[n_parts:]
    acc = None
    o = 0
    for pr in part_refs:
        n = pr.shape[-1]
        d = jnp.dot(pr[...].astype(BF16), w_ref[o:o + n, :], preferred_element_type=F32)
        acc = d if acc is None else acc + d
        o += n
    h = alpha * x_ref[...] + gate_ref[0] * acc
    mu = jnp.mean(h, -1, keepdims=True)
    hc = h - mu
    var = jnp.mean(hc * hc, -1, keepdims=True)
    o_ref[...] = hc * lax.rsqrt(var + LN_EPS) * g_ref[...] + b_ref[...]


def proj_residual_ln(parts, x, gate, w_bf16, g, b, rows_per_mod, tm):
    T, D = x.shape
    assert T % tm == 0 and rows_per_mod % tm == 0
    tiles_per_mod = rows_per_mod // tm
    k_tot = sum(p.shape[-1] for p in parts)
    assert w_bf16.shape == (k_tot, D)
    row = lambda n: pl.BlockSpec((tm, n), lambda i: (i, 0))
    vec = pl.BlockSpec((1, D), lambda i: (0, 0))
    return pl.pallas_call(
        functools.partial(_proj_ln_body, n_parts=len(parts), alpha=DEEPNORM_ALPHA),
        out_shape=jax.ShapeDtypeStruct((T, D), F32),
        grid=(T // tm,),
        in_specs=[row(p.shape[-1]) for p in parts] + [
            row(D), pl.BlockSpec((1, 1, D), lambda i: (i // tiles_per_mod, 0, 0)),
            pl.BlockSpec((k_tot, D), lambda i: (0, 0)), vec, vec],
        out_specs=row(D),
        compiler_params=pltpu.CompilerParams(dimension_semantics=("arbitrary",),
                                             vmem_limit_bytes=V7X_VMEM_LIMIT_BYTES),
        name="proj_residual_ln",
    )(*parts, x, gate, w_bf16, g.reshape(1, D), b.reshape(1, D))


S5_CHAINS = 2 * BATCH
S5_STATE = S5_G * S5_N
S5_SCAN_COLS = 512
S5_SB_IN = S5_SCAN_COLS // S5_N * S5_P


def _s5_scan_body(u_ref, wb_ref, wc_ref, are_ref, aim_ref, y_ref, bu_ref, h_ref, *, steps):
    rows = steps * S5_CHAINS

    @pl.when(pl.program_id(0) == 0)
    def _():
        h_ref[...] = jnp.zeros_like(h_ref)

    u = u_ref[...]
    chain = lax.broadcasted_iota(jnp.int32, u.shape, 0) % S5_CHAINS
    fwd = chain < BATCH
    zero = jnp.zeros_like(u)
    uf = jnp.where(fwd, u, zero)
    ub = jnp.where(fwd, zero, u)
    is_fwd_y = lax.broadcasted_iota(jnp.int32, (rows, S5_SB_IN), 0) % S5_CHAINS < BATCH

    for cb in range(S5_STATE // S5_SCAN_COLS):
        cin = slice(cb * S5_SB_IN, (cb + 1) * S5_SB_IN)
        re = pl.ds(cb * S5_SCAN_COLS, S5_SCAN_COLS)
        im = pl.ds(S5_STATE + cb * S5_SCAN_COLS, S5_SCAN_COLS)
        bu = jnp.dot(jnp.concatenate([uf[:, cin], ub[:, cin]], axis=1), wb_ref[cb], preferred_element_type=F32)
        bu_ref[:, re] = bu[:, :S5_SCAN_COLS]
        bu_ref[:, im] = bu[:, S5_SCAN_COLS:]
        a_re = are_ref[:, re]
        a_im = aim_ref[:, re]

        def step(s, carry):
            h_re, h_im = carry
            r = pl.ds(pl.multiple_of(s * S5_CHAINS, S5_CHAINS), S5_CHAINS)
            n_re = a_re * h_re - a_im * h_im + bu_ref[r, re]
            n_im = a_re * h_im + a_im * h_re + bu_ref[r, im]
            bu_ref[r, re] = n_re
            bu_ref[r, im] = n_im
            return n_re, n_im

        h_re, h_im = lax.fori_loop(0, steps, step, (h_ref[:, re], h_ref[:, im]), unroll=8)
        h_ref[:, re] = h_re
        h_ref[:, im] = h_im
        hb = jnp.concatenate([bu_ref[:, re], bu_ref[:, im]], axis=1).astype(BF16)
        y2 = jnp.dot(hb, wc_ref[cb], preferred_element_type=F32)
        y_ref[cb] = jnp.where(is_fwd_y, y2[:, :S5_SB_IN], y2[:, S5_SB_IN:])


def s5_scan(u8, wb, wc, a_re8, a_im8, steps):
    n_rows = u8.shape[0]
    rows = steps * S5_CHAINS
    assert n_rows % rows == 0
    full = lambda a: pl.BlockSpec(a.shape, lambda i: (0,) * a.ndim)
    return pl.pallas_call(
        functools.partial(_s5_scan_body, steps=steps),
        out_shape=jax.ShapeDtypeStruct((W_GRP // S5_SB_IN, n_rows, S5_SB_IN), F32),
        grid=(n_rows // rows,),
        in_specs=[pl.BlockSpec((rows, W_GRP), lambda i: (i, 0)), full(wb), full(wc), full(a_re8), full(a_im8)],
        out_specs=pl.BlockSpec((W_GRP // S5_SB_IN, rows, S5_SB_IN), lambda i: (0, i, 0)),
        scratch_shapes=[pltpu.VMEM((rows, 2 * S5_STATE), F32), pltpu.VMEM((S5_CHAINS, 2 * S5_STATE), F32)],
        compiler_params=pltpu.CompilerParams(dimension_semantics=("arbitrary",),
                                             vmem_limit_bytes=V7X_VMEM_LIMIT_BYTES),
        name="s5_scan",
    )(u8, wb, wc, a_re8, a_im8)


def _s5_glu_body(u_ref, yf_ref, yb_ref, d_ref, w_ref, b_ref, o_ref):
    tm = u_ref.shape[1]
    r = lax.broadcasted_iota(jnp.int32, (tm, tm), 0)
    c = lax.broadcasted_iota(jnp.int32, (tm, tm), 1)
    flip = jnp.where(r + c == tm - 1, 1.0, 0.0).astype(BF16)
    for b in range(BATCH):
        n_cb = yf_ref.shape[0]
        yf = jnp.concatenate([yf_ref[cb, pl.ds(b, tm, stride=S5_CHAINS), :] for cb in range(n_cb)], axis=1)
        yb = jnp.concatenate([yb_ref[cb, pl.ds(BATCH + b, tm, stride=S5_CHAINS), :] for cb in range(n_cb)],
                             axis=1)
        hi = yb.astype(BF16)
        lo = (yb - hi.astype(F32)).astype(BF16)
        yb = jnp.dot(flip, hi, preferred_element_type=F32) + jnp.dot(flip, lo, preferred_element_type=F32)
        y = d_ref[...] * u_ref[b] + yf + yb
        g = jax.nn.gelu(y)
        z = jnp.dot(g.astype(BF16), w_ref[...], preferred_element_type=F32) + b_ref[...]
        o_ref[b] = g * jax.nn.sigmoid(z)


def s5_glu_pallas(u, y8, first_step, d, w_bf16, b, tm):
    B_, n, W = u.shape
    assert B_ == BATCH and n % tm == 0 and first_step % tm == 0
    nt = n // tm
    off = first_step // tm
    row = pl.BlockSpec((B_, tm, W), lambda i: (0, i, 0))
    vec = pl.BlockSpec((1, W), lambda i: (0, 0))
    return pl.pallas_call(
        _s5_glu_body,
        out_shape=jax.ShapeDtypeStruct((B_, n, W), F32),
        grid=(nt,),
        in_specs=[row, pl.BlockSpec((y8.shape[0], tm * S5_CHAINS, y8.shape[2]), lambda i: (0, off + i, 0)),
                  pl.BlockSpec((y8.shape[0], tm * S5_CHAINS, y8.shape[2]), lambda i: (0, off + nt - 1 - i, 0)),
                  vec, pl.BlockSpec((W, W), lambda i: (0, 0)), vec],
        out_specs=row,
        compiler_params=pltpu.CompilerParams(dimension_semantics=("arbitrary",),
                                             vmem_limit_bytes=V7X_VMEM_LIMIT_BYTES),
        name="s5_glu",
    )(u, y8, y8, d.reshape(1, W), w_bf16, b.reshape(1, W))


def _s5_weights(a_re, a_im, log_dt, b_re, b_im, c_re, c_im):
    eye = jnp.eye(S5_G, dtype=F32)
    wb, a8 = [], []
    wc = []
    for di in range(2):
        ab_re, ab_im, bb_re, bb_im = s5_discretize(a_re[di], a_im[di], log_dt[di], b_re[di], b_im[di])
        blk = lambda m: jnp.einsum('gnp,gh->gphn', m, eye).reshape(W_GRP, S5_STATE)
        wb.append(jnp.concatenate([blk(bb_re), blk(bb_im)], axis=1))
        a8.append((jnp.broadcast_to(ab_re.reshape(1, S5_STATE), (BATCH, S5_STATE)),
                   jnp.broadcast_to(ab_im.reshape(1, S5_STATE), (BATCH, S5_STATE))))
        cblk = lambda m: jnp.einsum('gpn,gh->gnhp', m.astype(F32), eye).reshape(S5_STATE, W_GRP)
        wc.append(jnp.concatenate([cblk(c_re[di]), -cblk(c_im[di])], axis=0))
    wb_sb, wc_sb = [], []
    for sb in range(S5_STATE // S5_SCAN_COLS):
        cin = slice(sb * S5_SB_IN, (sb + 1) * S5_SB_IN)
        re = slice(sb * S5_SCAN_COLS, (sb + 1) * S5_SCAN_COLS)
        im = slice(S5_STATE + sb * S5_SCAN_COLS, S5_STATE + (sb + 1) * S5_SCAN_COLS)
        wb_sb.append(jnp.concatenate([jnp.concatenate([w[cin, re], w[cin, im]], axis=1) for w in wb], axis=0))
        wc_sb.append(jnp.concatenate([jnp.concatenate([w[re, cin], w[im, cin]], axis=0) for w in wc], axis=1))
    wb_sb = jnp.stack(wb_sb).astype(BF16)
    wc_sb = jnp.stack(wc_sb).astype(BF16)
    a_re8 = jnp.concatenate([a8[0][0], a8[1][0]], axis=0)
    a_im8 = jnp.concatenate([a8[0][1], a8[1][1]], axis=0)
    return wb_sb, wc_sb, a_re8, a_im8


def s5_mixer_pallas(u_lat, u_ctx, rev_lat, rev_ctx, a_re, a_im, log_dt, b_re, b_im, c_re, c_im, d, w_glu,
                    b_glu, with_ctx_out, steps=S5_STEPS, tm=S5_GLU_TILE):
    B_, L, W = u_lat.shape
    C_ = u_ctx.shape[1]
    assert B_ == BATCH and W == W_GRP
    wb, wc, a_re8, a_im8 = _s5_weights(a_re, a_im, log_dt, b_re, b_im, c_re, c_im)
    seq_f = jnp.concatenate([u_ctx.astype(BF16), u_lat.astype(BF16)], axis=1)
    seq_b = jnp.concatenate([rev_ctx, rev_lat], axis=1)
    u8 = jnp.concatenate([seq_f, seq_b], axis=0).transpose(1, 0, 2).reshape((C_ + L) * S5_CHAINS, W)
    y8 = s5_scan(u8, wb, wc, a_re8, a_im8, steps)
    w_glu_b = w_glu.astype(BF16)
    tm = min(tm, C_)
    out_lat = s5_glu_pallas(u_lat, y8, C_, d, w_glu_b, b_glu, tm).reshape(B_ * L, W)
    out_ctx = s5_glu_pallas(u_ctx, y8, 0, d, w_glu_b, b_glu, tm).reshape(B_ * C_, W) if with_ctx_out else None
    return out_lat, out_ctx


PEER_HALF = PEER_QDIM // 2


def _argmax_rows(v, r):
    while v.shape[0] > 1:
        half = v.shape[0] // 2
        take_hi = v[half:] > v[:half]
        r = jnp.where(take_hi, r[half:], r[:half])
        v = jnp.maximum(v[:half], v[half:])
    return v, r


def _topk_rows(xs, k, with_rank):
    n, lanes = xs[0].shape
    row = lax.broadcasted_iota(jnp.int32, (n, lanes), 0).astype(F32)
    krow = lax.broadcasted_iota(jnp.int32, (k, lanes), 0)

    def body(it, carry):
        out = []
        for x, vals, aux in carry:
            m, first = _argmax_rows(x, row)
            hit = row == first
            sel = krow == it
            aux = jnp.where(hit, it.astype(F32), aux) if with_rank else jnp.where(sel, first, aux)
            out.append((jnp.where(hit, -jnp.inf, x), jnp.where(sel, m, vals), aux))
        return tuple(out)

    zeros = jnp.zeros((k, lanes), F32)
    aux0 = jnp.full((n, lanes), float(k), F32) if with_rank else zeros
    res = lax.fori_loop(0, k, body, tuple((x, zeros, aux0) for x in xs), unroll=1 if with_rank else 2)
    return [(vals, aux) for _, vals, aux in res]


PEER_GRID_COLS = tuple(PEER_TOPK // (i + 1) for i in range(PEER_TOPK))
PEER_GRID_ROWS = 64


def _peer_route_body(x_ref, shift_ref, scale_ref, wq_ref, keys_ref, h_ref, n1_ref, c1_ref, r2_ref, e2_ref,
                     q_ref, *, lane_tiles):
    hm = (x_ref[...] * (1.0 + scale_ref[0]) + shift_ref[0]).astype(BF16)
    h_ref[...] = hm
    q_ref[...] = jnp.dot(hm, wq_ref[...], preferred_element_type=F32).astype(BF16)
    K = PEER_TOPK
    key = lax.broadcasted_iota(jnp.int32, (PEER_NK, V7X_LANES), 0).astype(F32)
    for hd in range(PEER_HEADS):
        for lt in range(lane_tiles):
            tok = pl.ds(lt * V7X_LANES, V7X_LANES)
            sc = []
            for side in range(2):
                col = (2 * hd + side) * PEER_HALF
                qs = q_ref[tok, col:col + PEER_HALF]
                sc.append(lax.dot_general(keys_ref[2 * hd + side], qs, (((1,), (1,)), ((), ())),
                                          preferred_element_type=F32))
            (v1, idx1), (v2, idx2) = _topk_rows(sc, K, with_rank=False)
            cells = [v1[i:i + 1] + v2[:PEER_GRID_COLS[i]] for i in range(K)]
            cells.append(jnp.full((PEER_GRID_ROWS - sum(PEER_GRID_COLS), V7X_LANES), -jnp.inf, F32))
            (vc, rc), = _topk_rows([jnp.concatenate(cells, axis=0)], K, with_rank=True)
            z = jnp.sum(jnp.exp(vc - vc[0:1]), axis=0, keepdims=True)
            chosen = jnp.where(rc < float(K), 1.0, 0.0)
            n1 = jnp.zeros_like(key)
            r2 = jnp.full(key.shape, float(K), F32)
            off = 0
            for i in range(K):
                n_i = jnp.sum(chosen[off:off + PEER_GRID_COLS[i]], axis=0, keepdims=True)
                n1 = jnp.where(key == idx1[i:i + 1], n_i, n1)
                r2 = jnp.where(key == idx2[i:i + 1], float(i), r2)
                off += PEER_GRID_COLS[i]
            n1_ref[hd, :, tok] = n1
            c1_ref[hd, :, tok] = jnp.exp(sc[0] - v1[0:1]) / z
            r2_ref[hd, :, tok] = r2.astype(BF16)
            e2_ref[hd, :, tok] = jnp.exp(sc[1] - v2[0:1]).astype(BF16)


def peer_route(x, shift, scale, wq_bf16, keys_bf16, rows_per_mod, tm):
    T, D = x.shape
    assert T % tm == 0 and rows_per_mod % tm == 0 and tm % V7X_LANES == 0
    tiles_per_mod = rows_per_mod // tm
    mod_spec = pl.BlockSpec((1, 1, D), lambda i: (i // tiles_per_mod, 0, 0))
    tab = lambda dt: jax.ShapeDtypeStruct((PEER_HEADS, PEER_NK, T), dt)
    tab_spec = pl.BlockSpec((PEER_HEADS, PEER_NK, tm), lambda i: (0, 0, i))
    return pl.pallas_call(
        functools.partial(_peer_route_body, lane_tiles=tm // V7X_LANES),
        out_shape=[jax.ShapeDtypeStruct((T, D), BF16), tab(F32), tab(F32), tab(BF16), tab(BF16)],
        grid=(T // tm,),
        in_specs=[pl.BlockSpec((tm, D), lambda i: (i, 0)), mod_spec, mod_spec,
                  pl.BlockSpec(wq_bf16.shape, lambda i: (0, 0)),
                  pl.BlockSpec(keys_bf16.shape, lambda i: (0, 0, 0))],
        out_specs=[pl.BlockSpec((tm, D), lambda i: (i, 0)), tab_spec, tab_spec, tab_spec, tab_spec],
        scratch_shapes=[pltpu.VMEM((tm, PEER_HEADS * PEER_QDIM), BF16)],
        compiler_params=pltpu.CompilerParams(dimension_semantics=("arbitrary",),
                                             vmem_limit_bytes=V7X_VMEM_LIMIT_BYTES),
        name="peer_route",
    )(x, shift, scale, wq_bf16, keys_bf16)


def _peer_dense_body(h_ref, u_ref, vt_ref, n1_ref, c1_ref, r2_ref, e2_ref, x_ref, gate_ref, g_ref, b_ref,
                     o_ref, acc_ref, a_ref, *, n_slab):
    j = pl.program_id(1)

    @pl.when(j == 0)
    def _():
        acc_ref[...] = jnp.zeros_like(acc_ref)

    rows = PEER_SCORE_SLABS * PEER_NK
    for p in range(n_slab // PEER_SCORE_SLABS):
        s = lax.dot_general(u_ref[p * rows:(p + 1) * rows, :], h_ref[...], (((1,), (1,)), ((), ())),
                            preferred_element_type=F32)
        for kk in range(PEER_SCORE_SLABS):
            k = PEER_SCORE_SLABS * p + kk
            for lt in range(h_ref.shape[0] // V7X_LANES):
                lanes = slice(lt * V7X_LANES, (lt + 1) * V7X_LANES)
                g = None
                for hd in range(PEER_HEADS):
                    n1row = n1_ref[hd, k:k + 1, lanes].astype(BF16)
                    c1row = c1_ref[hd, k:k + 1, lanes].astype(BF16)
                    sel = lax.clamp(jnp.zeros((), BF16), n1row - r2_ref[hd, :, lanes], jnp.ones((), BF16))
                    gh = sel * e2_ref[hd, :, lanes] * c1row
                    g = gh if g is None else g + gh
                sk = s[kk * PEER_NK:(kk + 1) * PEER_NK, lanes]
                a_ref[k * PEER_NK:(k + 1) * PEER_NK, lanes] = jax.nn.gelu(sk).astype(BF16) * g
    acc_ref[...] += jnp.dot(vt_ref[0], a_ref[...], preferred_element_type=F32)

    @pl.when(j == pl.num_programs(1) - 1)
    def _():
        hres = DEEPNORM_ALPHA * x_ref[...] + gate_ref[0] * acc_ref[...].T
        mu = jnp.mean(hres, -1, keepdims=True)
        hc = hres - mu
        var = jnp.mean(hc * hc, -1, keepdims=True)
        o_ref[...] = hc * lax.rsqrt(var + LN_EPS) * g_ref[...] + b_ref[...]


def peer_dense(h_bf16, u_bf16, vt_tiles, n1, c1, r2, e2, x, gate, ln_g, ln_b, rows_per_mod, tm, n_slab):
    T, D = h_bf16.shape
    NE = u_bf16.shape[0]
    e_tile = n_slab * PEER_NK
    n_e = NE // e_tile
    assert T % tm == 0 and NE % e_tile == 0 and vt_tiles.shape == (n_e, D, e_tile)
    slab_spec = pl.BlockSpec((PEER_HEADS, n_slab, tm), lambda i, j: (0, j, i))
    tok_spec = pl.BlockSpec((PEER_HEADS, PEER_NK, tm), lambda i, j: (0, 0, i))
    return pl.pallas_call(
        functools.partial(_peer_dense_body, n_slab=n_slab),
        out_shape=jax.ShapeDtypeStruct((T, D), F32),
        grid=(T // tm, n_e),
        in_specs=[pl.BlockSpec((tm, D), lambda i, j: (i, 0)),
                  pl.BlockSpec((e_tile, D), lambda i, j: (j, 0)),
                  pl.BlockSpec((1, D, e_tile), lambda i, j: (j, 0, 0)),
                  slab_spec, slab_spec, tok_spec, tok_spec,
                  pl.BlockSpec((tm, D), lambda i, j: (i, 0)),
                  pl.BlockSpec((1, 1, D), lambda i, j: (i // (rows_per_mod // tm), 0, 0)),
                  pl.BlockSpec((1, D), lambda i, j: (0, 0)), pl.BlockSpec((1, D), lambda i, j: (0, 0))],
        out_specs=pl.BlockSpec((tm, D), lambda i, j: (i, 0)),
        scratch_shapes=[pltpu.VMEM((D, tm), F32), pltpu.VMEM((e_tile, tm), BF16)],
        compiler_params=pltpu.CompilerParams(dimension_semantics=("arbitrary", "arbitrary"),
                                             vmem_limit_bytes=V7X_VMEM_LIMIT_BYTES),
        name="peer_dense",
    )(h_bf16, u_bf16, vt_tiles, n1, c1, r2, e2, x, gate, ln_g.reshape(1, D), ln_b.reshape(1, D))


PEER_SLABS = 8
PEER_SCORE_SLABS = 4


def peer_v_tiles(v_tab):
    e_tile = PEER_SLABS * PEER_NK
    return v_tab.astype(BF16).reshape(v_tab.shape[0] // e_tile, e_tile, v_tab.shape[1]).transpose(0, 2, 1)


def peer_sublayer(x, shift, scale, gate, wq_bf16, keys_bf16, u_bf16, vt_tiles, ln_g, ln_b, rows_per_mod,
                  tm_route=PEER_ROUTE_TILE, tm_dense=PEER_DENSE_TILE):
    h, n1, c1, r2, e2 = peer_route(x, shift, scale, wq_bf16, keys_bf16, rows_per_mod, tm_route)
    return peer_dense(h, u_bf16, vt_tiles, n1, c1, r2, e2, x, gate, ln_g, ln_b, rows_per_mod, tm_dense,
                      PEER_SLABS)


POOL_HALO = 8
POOL_ROWS = 256


def _pool_body(u_ref, w_ref, scale_ref, o_ref, pad_ref, *, seq_len):
    L = seq_len
    chunk = min(POOL_ROWS, L)
    zeros = jnp.zeros((POOL_HALO, W_GRP), F32)
    pad_ref[0:POOL_HALO, :] = zeros
    pad_ref[POOL_HALO + L:2 * POOL_HALO + L, :] = zeros
    pad_ref[POOL_HALO:POOL_HALO + L, :] = u_ref[0]
    for r0 in range(0, L, chunk):
        t = r0 + lax.broadcasted_iota(jnp.int32, (chunk, POOL_C), 0)
        for j, w in enumerate(POOL_WINDOWS):
            cols = slice(j * POOL_C, (j + 1) * POOL_C)
            acc = None
            for o in range(-(w // 2), w - w // 2):
                s = pad_ref[POOL_HALO + r0 + o:POOL_HALO + r0 + o + chunk, cols]
                acc = s if acc is None else acc + s
            lo = jnp.maximum(t - w // 2, 0)
            hi = jnp.minimum(t - w // 2 + w - 1, L - 1)
            cnt = (hi - lo + 1).astype(F32)
            pooled = acc / cnt - u_ref[0, r0:r0 + chunk, cols]
            y = jnp.dot(pooled.astype(BF16), w_ref[j], preferred_element_type=F32)
            o_ref[0, r0:r0 + chunk, cols] = y * scale_ref[:, cols]


def pool_mix_pallas(u, pool_w, pool_scale):
    B_, L, W = u.shape
    assert max(POOL_WINDOWS) // 2 <= POOL_HALO and L % min(POOL_ROWS, L) == 0
    blk = pl.BlockSpec((1, L, W), lambda b: (b, 0, 0))
    return pl.pallas_call(
        functools.partial(_pool_body, seq_len=L),
        out_shape=jax.ShapeDtypeStruct((B_, L, W), F32),
        grid=(B_,),
        in_specs=[blk, pl.BlockSpec(pool_w.shape, lambda b: (0, 0, 0)), pl.BlockSpec((1, W), lambda b: (0, 0))],
        out_specs=blk,
        scratch_shapes=[pltpu.VMEM((L + 2 * POOL_HALO, W), F32)],
        compiler_params=pltpu.CompilerParams(dimension_semantics=("arbitrary",),
                                             vmem_limit_bytes=V7X_VMEM_LIMIT_BYTES),
        name="pool_mix",
    )(u, pool_w.astype(BF16), pool_scale.reshape(1, W))


NA_TILE_R = 4
NA_TILE = NA_TILE_R * GRID_W
NA_SCALE = NA_HD ** -0.5


def _na_bias_table(rpb, n_rows):
    n_tiles = n_rows // NA_TILE_R
    KR = min(NA_WIN_R, n_rows)
    a = np.array([0, min(2, n_tiles - 1), n_tiles - 1]).reshape(3, 1, 1, 1, 1, 1)
    d = np.arange(3).reshape(1, 3, 1, 1, 1, 1)
    i = np.arange(NA_TILE_R).reshape(1, 1, NA_TILE_R, 1, 1, 1)
    qc = np.arange(GRID_W).reshape(1, 1, 1, GRID_W, 1, 1)
    j = np.arange(NA_TILE_R).reshape(1, 1, 1, 1, NA_TILE_R, 1)
    kc = np.arange(GRID_W).reshape(1, 1, 1, 1, 1, GRID_W)
    qr = NA_TILE_R * a + i
    kr = NA_TILE_R * (a + d - 1) + j
    rs = np.clip(qr - KR // 2, 0, n_rows - KR)
    c0 = np.clip(qc - NA_WIN_C // 2, 0, GRID_W - NA_WIN_C)
    ok = (kr >= rs) & (kr < rs + KR) & (kr >= 0) & (kr < n_rows) & (kc >= c0) & (kc < c0 + NA_WIN_C)
    row_rel = np.clip(kr - qr + NA_WIN_R - 1, 0, 2 * NA_WIN_R - 2)[0, :, :, 0, :, 0]
    col_rel = (np.clip(kc - qc, -(NA_WIN_C - 1), NA_WIN_C - 1) + NA_WIN_C - 1)[0, 0, 0, :, 0, :]
    onehot = (col_rel[None] == np.arange(2 * NA_WIN_C - 1)[:, None, None]).astype(np.float32)
    bias_rc = jnp.einsum('hrc,cqk->hrqk', rpb.astype(F32), onehot, precision=lax.Precision.HIGHEST)
    tab = jnp.stack([bias_rc[:, int(r)] for r in row_rel.reshape(-1)], axis=1)
    tab = tab.reshape(NA_H, 3, NA_TILE_R, NA_TILE_R, GRID_W, GRID_W).transpose(0, 1, 2, 4, 3, 5)
    tab = jnp.where(ok[:, None], tab[None], -jnp.inf)
    return tab.reshape(3, NA_H, 3, NA_TILE, NA_TILE)


def _na_body(q_ref, k_ref, v_ref, kc_ref, vc_ref, t_ref, o_ref, *, n_tiles, with_grid):
    a = pl.program_id(1)
    nt = (((1,), (1,)), ((), ()))
    for h in range(NA_H):
        hs = slice(h * NA_HD, (h + 1) * NA_HD)
        qh = q_ref[:, hs]
        scores = [lax.dot_general(qh, kc_ref[:, hs], nt, preferred_element_type=F32) * NA_SCALE]
        vals = [vc_ref[:, hs]]
        if with_grid:
            for d in range(3):
                ti = jnp.clip(a + d - 1, 0, n_tiles - 1)
                rows = pl.ds(pl.multiple_of(ti * NA_TILE, NA_TILE), NA_TILE)
                s = lax.dot_general(qh, k_ref[rows, hs], nt, preferred_element_type=F32)
                scores.append(s * NA_SCALE + t_ref[0, h, d])
                vals.append(v_ref[rows, hs])
        m = scores[0].max(axis=-1, keepdims=True)
        for s in scores[1:]:
            m = jnp.maximum(m, s.max(axis=-1, keepdims=True))
        den = None
        acc = None
        for s, vv in zip(scores, vals):
            p = jnp.exp(s - m)
            l = p.sum(axis=-1, keepdims=True)
            o = jnp.dot(p.astype(BF16), vv, preferred_element_type=F32)
            den = l if den is None else den + l
            acc = o if acc is None else acc + o
        o_ref[:, hs] = acc / den


def na_attention(q, k, v, kc, vc, table, seq_len, ctx_len, with_grid):
    W = q.shape[1]
    n_b = kc.shape[0] // ctx_len
    lq = q.shape[0] // n_b
    assert lq % NA_TILE == 0
    q_tiles = lq // NA_TILE
    n_tiles = seq_len // NA_TILE
    pat = lambda b, a: (jnp.where(a == 0, 0, jnp.where(a == n_tiles - 1, 2, 1)), 0, 0, 0, 0)
    qo_spec = pl.BlockSpec((NA_TILE, W), lambda b, a: (b * q_tiles + a, 0))
    return pl.pallas_call(
        functools.partial(_na_body, n_tiles=n_tiles, with_grid=with_grid),
        out_shape=jax.ShapeDtypeStruct(q.shape, F32),
        grid=(n_b, q_tiles),
        in_specs=[qo_spec,
                  pl.BlockSpec((seq_len, W), lambda b, a: (b, 0)), pl.BlockSpec((seq_len, W), lambda b, a: (b, 0)),
                  pl.BlockSpec((ctx_len, W), lambda b, a: (b, 0)), pl.BlockSpec((ctx_len, W), lambda b, a: (b, 0)),
                  pl.BlockSpec((1,) + table.shape[1:], pat)],
        out_specs=qo_spec,
        compiler_params=pltpu.CompilerParams(dimension_semantics=("arbitrary", "arbitrary"),
                                             vmem_limit_bytes=V7X_VMEM_LIMIT_BYTES),
        name="na_attention",
    )(q, k, v, kc, vc, table)


SSM_HALO = 8
SSM_BC = SSM_G * SSM_N


def _ssm_conv_body(prev_ref, cur_ref, next_ref, w_ref, b_ref, o_ref, pad_ref, *, tiles_per_seq):
    pos = pl.program_id(0) % tiles_per_seq
    tm = cur_ref.shape[0]
    pad_ref[0:SSM_HALO, :] = jnp.where(pos == 0, 0.0, prev_ref[...])
    pad_ref[SSM_HALO:SSM_HALO + tm, :] = cur_ref[...]
    pad_ref[SSM_HALO + tm:2 * SSM_HALO + tm, :] = jnp.where(pos == tiles_per_seq - 1, 0.0, next_ref[...])
    lead = (SSM_CONV - 1) // 2
    y = b_ref[...]
    for k in range(SSM_CONV):
        y = y + w_ref[k:k + 1, :] * pad_ref[SSM_HALO - lead + k:SSM_HALO - lead + k + tm, :]
    o_ref[...] = jax.nn.silu(y)


def ssm_conv(xbc, conv_w, conv_b, seq_len, tm):
    T, CH = xbc.shape
    assert seq_len % tm == 0 and tm % SSM_HALO == 0
    hb = tm // SSM_HALO
    n_hb = T // SSM_HALO
    return pl.pallas_call(
        functools.partial(_ssm_conv_body, tiles_per_seq=seq_len // tm),
        out_shape=jax.ShapeDtypeStruct((T, CH), F32),
        grid=(T // tm,),
        in_specs=[pl.BlockSpec((SSM_HALO, CH), lambda i: (jnp.maximum(i * hb - 1, 0), 0)),
                  pl.BlockSpec((tm, CH), lambda i: (i, 0)),
                  pl.BlockSpec((SSM_HALO, CH), lambda i: (jnp.minimum((i + 1) * hb, n_hb - 1), 0)),
                  pl.BlockSpec((SSM_CONV, CH), lambda i: (0, 0)), pl.BlockSpec((1, CH), lambda i: (0, 0))],
        out_specs=pl.BlockSpec((tm, CH), lambda i: (i, 0)),
        scratch_shapes=[pltpu.VMEM((tm + 2 * SSM_HALO, CH), F32)],
        compiler_params=pltpu.CompilerParams(dimension_semantics=("arbitrary",),
                                             vmem_limit_bytes=V7X_VMEM_LIMIT_BYTES),
        name="ssm_conv",
    )(xbc, xbc, xbc, conv_w, conv_b.reshape(1, CH))


def _ssd_chunk_prep(xbc_ref, dt_ref, bias_ref, a_ref, *, di):
    rev = di == 1
    Q = xbc_ref.shape[0]
    dt_all = jax.nn.softplus(dt_ref[...] + bias_ref[...])
    a_cs = dt_all * a_ref[...]
    row_id = lax.broadcasted_iota(jnp.int32, a_cs.shape, 0)
    sh = 1
    while sh < Q:
        if rev:
            a_cs = a_cs + jnp.where(row_id < Q - sh, pltpu.roll(a_cs, Q - sh, 0), 0.0)
        else:
            a_cs = a_cs + jnp.where(row_id >= sh, pltpu.roll(a_cs, sh, 0), 0.0)
        sh *= 2
    a_cs_t = a_cs.T
    a_tot = a_cs[0:1, :] if rev else a_cs[Q - 1:Q, :]
    l_id = lax.broadcasted_iota(jnp.int32, (Q, Q), 0)
    s_id = lax.broadcasted_iota(jnp.int32, (Q, Q), 1)
    causal = (l_id <= s_id) if rev else (l_id >= s_id)
    cb = []
    for g in range(SSM_G):
        bg = xbc_ref[:, W_GRP + g * SSM_N:W_GRP + (g + 1) * SSM_N].astype(BF16)
        cg = xbc_ref[:, W_GRP + SSM_BC + g * SSM_N:W_GRP + SSM_BC + (g + 1) * SSM_N].astype(BF16)
        cb.append((bg, cg, lax.dot_general(cg, bg, (((1,), (1,)), ((), ())), preferred_element_type=F32)))
    return dt_all, a_cs, a_cs_t, a_tot, causal, cb


def _ssd_head(prep, xbc_ref, h_scr, y_ref, *, di, h):
    dt_all, a_cs, a_cs_t, a_tot, causal, cb = prep
    nt = (((1,), (1,)), ((), ()))
    tn = (((0,), (0,)), ((), ()))
    c = di * SSM_H + h
    bg, cg, cbg = cb[h // (SSM_H // SSM_G)]
    col = a_cs[:, c:c + 1]
    lm = jnp.exp(jnp.where(causal, col - a_cs_t[c:c + 1, :], -jnp.inf))
    xh = xbc_ref[:, h * SSM_HD:(h + 1) * SSM_HD] * dt_all[:, c:c + 1]
    hp = h_scr[di, h]
    yd = jnp.dot((cbg * lm).astype(BF16), xh.astype(BF16), preferred_element_type=F32)
    yo = lax.dot_general(cg, hp.astype(BF16), nt, preferred_element_type=F32) * jnp.exp(col)
    y_ref[:, h * SSM_HD:(h + 1) * SSM_HD] = yd + yo
    tot = a_tot[:, c:c + 1]
    xd = (xh * jnp.exp(tot - col)).astype(BF16)
    h_scr[di, h] = jnp.exp(tot) * hp + lax.dot_general(xd, bg, tn, preferred_element_type=F32)


def _ssd_body(xf_ref, dtf_ref, xb_ref, dtb_ref, bias_ref, a_ref, h0_ref, yf_ref, yb_ref, hfin_ref, h_scr):
    ci = pl.program_id(1)

    @pl.when(ci == 0)
    def _():
        h_scr[...] = h0_ref[:, 0]

    prep_f = _ssd_chunk_prep(xf_ref, dtf_ref, bias_ref, a_ref, di=0)
    prep_b = _ssd_chunk_prep(xb_ref, dtb_ref, bias_ref, a_ref, di=1)
    for h in range(SSM_H):
        _ssd_head(prep_f, xf_ref, h_scr, yf_ref, di=0, h=h)
        _ssd_head(prep_b, xb_ref, h_scr, yb_ref, di=1, h=h)

    @pl.when(ci == pl.num_programs(1) - 1)
    def _():
        hfin_ref[:, 0] = h_scr[...]


def ssd_scan(xbc_act, dt_raw, bias128, a128, h0, seq_len):
    T = xbc_act.shape[0]
    n_b = T // seq_len
    Q = min(SSM_CHUNK, seq_len)
    nc = seq_len // Q
    fwd = lambda b, i: (b * nc + i, 0)
    bwd = lambda b, i: (b * nc + nc - 1 - i, 0)
    vec = pl.BlockSpec((1, V7X_LANES), lambda b, i: (0, 0))
    st = pl.BlockSpec((2, 1, SSM_H, SSM_HD, SSM_N), lambda b, i: (0, b, 0, 0, 0))
    y_shape = jax.ShapeDtypeStruct((T, W_GRP), F32)
    return pl.pallas_call(
        _ssd_body,
        out_shape=[y_shape, y_shape, jax.ShapeDtypeStruct(h0.shape, F32)],
        grid=(n_b, nc),
        in_specs=[pl.BlockSpec((Q, SSM_CONV_CH), fwd), pl.BlockSpec((Q, V7X_LANES), fwd),
                  pl.BlockSpec((Q, SSM_CONV_CH), bwd), pl.BlockSpec((Q, V7X_LANES), bwd), vec, vec, st],
        out_specs=[pl.BlockSpec((Q, W_GRP), fwd), pl.BlockSpec((Q, W_GRP), bwd), st],
        scratch_shapes=[pltpu.VMEM((2, SSM_H, SSM_HD, SSM_N), F32)],
        compiler_params=pltpu.CompilerParams(dimension_semantics=("arbitrary", "arbitrary"),
                                             vmem_limit_bytes=V7X_VMEM_LIMIT_BYTES),
        name="ssd_scan",
    )(xbc_act, dt_raw, xbc_act, dt_raw, bias128, a128, h0)


def _ssm_out_body(xbc_ref, yf_ref, yb_ref, z_ref, d_ref, nw_ref, o_ref):
    y = d_ref[...] * xbc_ref[:, :W_GRP] + yf_ref[...] + yb_ref[...]
    g = y * jax.nn.silu(z_ref[...])
    gw = W_GRP // SSM_G
    for k in range(SSM_G):
        gk = g[:, k * gw:(k + 1) * gw]
        r = lax.rsqrt(jnp.mean(gk * gk, -1, keepdims=True) + LN_EPS)
        o_ref[:, k * gw:(k + 1) * gw] = gk * r * nw_ref[:, k * gw:(k + 1) * gw]


def ssm_out(xbc_act, yf, yb, z, d512, norm_w, tm):
    T = z.shape[0]
    assert T % tm == 0
    row = pl.BlockSpec((tm, W_GRP), lambda i: (i, 0))
    vec = pl.BlockSpec((1, W_GRP), lambda i: (0, 0))
    return pl.pallas_call(
        _ssm_out_body,
        out_shape=jax.ShapeDtypeStruct((T, W_GRP), F32),
        grid=(T // tm,),
        in_specs=[pl.BlockSpec((tm, SSM_CONV_CH), lambda i: (i, 0)), row, row, row, vec, vec],
        out_specs=row,
        compiler_params=pltpu.CompilerParams(dimension_semantics=("arbitrary",),
                                             vmem_limit_bytes=V7X_VMEM_LIMIT_BYTES),
        name="ssm_out",
    )(xbc_act, yf, yb, z, d512, norm_w.reshape(1, W_GRP))


def ssm_mixer_pallas(z_l, xbc_l, dt_l, z_c, xbc_c, dt_c, conv_w, conv_b, dt_bias, a_log, d, norm_w,
                     seq_len, ctx_len, with_ctx_out, tm=ROW_TILE):
    n_b = z_l.shape[0] // seq_len
    pad = V7X_LANES - 2 * SSM_H
    bias128 = jnp.pad(dt_bias.astype(F32).reshape(1, 2 * SSM_H), ((0, 0), (0, pad)))
    a128 = jnp.pad(-jnp.exp(a_log.astype(F32)).reshape(1, 2 * SSM_H), ((0, 0), (0, pad)))
    d512 = jnp.repeat(d.astype(F32), SSM_HD).reshape(1, W_GRP)
    act_c = ssm_conv(xbc_c, conv_w, conv_b, ctx_len, min(tm, ctx_len))
    act_l = ssm_conv(xbc_l, conv_w, conv_b, seq_len, tm)
    zero = jnp.zeros((2, n_b, SSM_H, SSM_HD, SSM_N), F32)
    ycf, ycb, hc = ssd_scan(act_c, dt_c, bias128, a128, zero, ctx_len)
    ylf, ylb, _ = ssd_scan(act_l, dt_l, bias128, a128, hc, seq_len)
    out_l = ssm_out(act_l, ylf, ylb, z_l, d512, norm_w, tm)
    out_c = ssm_out(act_c, ycf, ycb, z_c, d512, norm_w, min(tm, z_c.shape[0])) if with_ctx_out else None
    return out_l, out_c


def s5_discretize(a_re, a_im, log_dt, b_re, b_im):
    a_re, a_im = a_re.astype(F32), a_im.astype(F32)
    b_re, b_im = b_re.astype(F32), b_im.astype(F32)
    dt = jnp.exp(log_dt.astype(F32))[:, None]
    mag = jnp.exp(a_re * dt)
    ab_re = mag * jnp.cos(a_im * dt)
    ab_im = mag * jnp.sin(a_im * dt)
    den = a_re * a_re + a_im * a_im
    f_re = ((ab_re - 1) * a_re + ab_im * a_im) / den
    f_im = (ab_im * a_re - (ab_re - 1) * a_im) / den
    bb_re = f_re[..., None] * b_re - f_im[..., None] * b_im
    bb_im = f_re[..., None] * b_im + f_im[..., None] * b_re
    return ab_re, ab_im, bb_re, bb_im


def _mixers(p_l, p_c, prm, with_ctx_out):
    s5_l, pool_l, q_l, k_l, v_l, z_l, xbc_l, dt_l, s5r_l = p_l
    s5_c, pool_c, q_c, k_c, v_c, z_c, xbc_c, dt_c, s5r_c = p_c
    ya_l, ya_c = s5_mixer_pallas(s5_l, s5_c, s5r_l, s5r_c, prm["s5_a_re"], prm["s5_a_im"], prm["s5_log_dt"],
                                 prm["s5_b_re"], prm["s5_b_im"], prm["s5_c_re"], prm["s5_c_im"], prm["s5_d"],
                                 prm["s5_w_glu"], prm["s5_b_glu"], with_ctx_out)
    yb_l = pool_mix_pallas(pool_l, prm["pool_w"], prm["pool_scale"])
    yb_c = pool_mix_pallas(pool_c, prm["pool_w"], prm["pool_scale"]) if with_ctx_out else None
    B_, L, _ = s5_l.shape
    C_ = s5_c.shape[1]
    flat = lambda t: t.reshape(-1, t.shape[-1])
    table = _na_bias_table(prm["na_rpb"], L // GRID_W)
    yc_l = na_attention(flat(q_l), flat(k_l), flat(v_l), flat(k_c), flat(v_c), table, L, C_, True)
    yc_c = (na_attention(flat(q_c), flat(k_c), flat(v_c), flat(k_c), flat(v_c), table, C_, C_, False)
            if with_ctx_out else None)
    yd_l, yd_c = ssm_mixer_pallas(flat(z_l), flat(xbc_l), flat(dt_l), flat(z_c), flat(xbc_c), flat(dt_c),
                                  prm["ssm_conv_w"], prm["ssm_conv_b"], prm["ssm_dt_bias"], prm["ssm_a_log"],
                                  prm["ssm_d"], prm["ssm_norm_w"], L, C_, with_ctx_out)
    return (ya_l, yb_l, yc_l, yd_l), (ya_c, yb_c, yc_c, yd_c)


def kernel(x, c, ctx, c_ctx, w_ada, b_ada, w_in, w_out, s5_a_re, s5_a_im, s5_log_dt, s5_b_re, s5_b_im, s5_c_re, s5_c_im, s5_d, s5_w_glu, s5_b_glu, pool_w, pool_scale, na_rpb, ssm_conv_w, ssm_conv_b, ssm_dt_bias, ssm_a_log, ssm_d, ssm_norm_w, ln1_g, ln1_b, ln2_g, ln2_b, peer_w_q, peer_sub_keys, peer_u, peer_v):
    B_, L, D = x.shape
    C_ = ctx.shape[1]
    x_lat = x.reshape(B_ * L, D)
    x_ctx = ctx.reshape(B_ * C_, D)
    assert B_ + 1 <= ADA_ROWS
    cond = jnp.concatenate([c, c_ctx[None], jnp.zeros((ADA_ROWS - B_ - 1, D), F32)], axis=0)
    dt_pad = V7X_LANES - IN_SPLITS[-1]
    splits = IN_SPLITS[:-1] + (V7X_LANES,)
    in_dtypes = (F32, F32, BF16, BF16, BF16, F32, F32, F32)
    TM = ROW_TILE
    for l in range(DEPTH):
        last = l == DEPTH - 1
        prm = dict(s5_a_re=s5_a_re[l], s5_a_im=s5_a_im[l], s5_log_dt=s5_log_dt[l], s5_b_re=s5_b_re[l],
                   s5_b_im=s5_b_im[l], s5_c_re=s5_c_re[l], s5_c_im=s5_c_im[l], s5_d=s5_d[l],
                   s5_w_glu=s5_w_glu[l], s5_b_glu=s5_b_glu[l], pool_w=pool_w[l], pool_scale=pool_scale[l],
                   na_rpb=na_rpb[l], ssm_conv_w=ssm_conv_w[l], ssm_conv_b=ssm_conv_b[l],
                   ssm_dt_bias=ssm_dt_bias[l], ssm_a_log=ssm_a_log[l], ssm_d=ssm_d[l],
                   ssm_norm_w=ssm_norm_w[l])
        mod = ada_modulation(cond, w_ada[l], b_ada[l])
        m_lat = mod[:B_].reshape(B_, 6, 1, D)
        m_ctx = mod[B_:B_ + 1].reshape(1, 6, 1, D)
        w_in_b = jnp.pad(w_in[l], ((0, 0), (0, dt_pad))).astype(BF16)
        w_out_b = w_out[l].astype(BF16)

        p_l = modulated_matmul(x_lat, m_lat[:, 0], m_lat[:, 1], w_in_b, splits, in_dtypes, L, L, TM)
        p_c = modulated_matmul(x_ctx, m_ctx[:, 0], m_ctx[:, 1], w_in_b, splits, in_dtypes, B_ * C_, C_, TM)
        p_l = [a.reshape(B_, L, -1) for a in p_l]
        p_c = [a.reshape(B_, C_, -1) for a in p_c]
        y_l, y_c = _mixers(p_l, p_c, prm, not last)

        x_lat = proj_residual_ln([a.reshape(B_ * L, -1) for a in y_l], x_lat, m_lat[:, 2], w_out_b,
                                 ln1_g[l], ln1_b[l], L, TM)
        wq_b = peer_w_q[l].astype(BF16)
        keys_b = peer_sub_keys[l].reshape(2 * PEER_HEADS, PEER_NK, PEER_HALF).astype(BF16)
        u_b = peer_u[l].astype(BF16)
        vt_b = peer_v_tiles(peer_v[l])
        x_lat = peer_sublayer(x_lat, m_lat[:, 3], m_lat[:, 4], m_lat[:, 5], wq_b, keys_b, u_b, vt_b,
                              ln2_g[l], ln2_b[l], L)
        if not last:
            x_ctx = proj_residual_ln([a.reshape(B_ * C_, -1) for a in y_c], x_ctx, m_ctx[:, 2], w_out_b,
                                     ln1_g[l], ln1_b[l], B_ * C_, TM)
            x_ctx = peer_sublayer(x_ctx, m_ctx[:, 3], m_ctx[:, 4], m_ctx[:, 5], wq_b, keys_b, u_b, vt_b,
                                  ln2_g[l], ln2_b[l], B_ * C_)
    return x_lat.reshape(B_, L, D)
```

```python
import functools

import jax
import jax.numpy as jnp
import numpy as np
from jax import lax
from jax.experimental import pallas as pl
from jax.experimental.pallas import tpu as pltpu

D_MODEL = 2048
BATCH = 4
SEQ = 4096
DEPTH = 2

GRID_W = 64
CTX_LEN = 256
N_MIXERS = 4
D_MIX = D_MODEL
W_GRP = D_MIX // N_MIXERS

S5_P = 16
S5_G = W_GRP // S5_P
S5_N = 64

POOL_WINDOWS = (2, 4, 8, 16)
POOL_C = W_GRP // len(POOL_WINDOWS)

NA_HD = 64
NA_H = W_GRP // NA_HD
NA_WIN_R = 8
NA_WIN_C = 16

SSM_HD = 64
SSM_H = W_GRP // SSM_HD
SSM_G = 2
SSM_N = 128
SSM_CONV = 4
SSM_CHUNK = 128
SSM_CONV_CH = W_GRP + 2 * SSM_G * SSM_N

PEER_HEADS = 8
PEER_NK = 128
PEER_NE = PEER_NK * PEER_NK
PEER_QDIM = 256
PEER_TOPK = 16
PEER_BLOCK = 128

IN_SPLITS = (W_GRP, W_GRP, W_GRP, W_GRP, W_GRP, W_GRP, SSM_CONV_CH, 2 * SSM_H)
D_IN = sum(IN_SPLITS)

DEEPNORM_ALPHA = (2 * DEPTH) ** 0.25
DEEPNORM_BETA = (8 * DEPTH) ** -0.25
LN_EPS = 1e-5
F32 = jnp.float32
BF16 = jnp.bfloat16

V7X_LANES = 128
V7X_VMEM_BYTES = 64 * 1024 * 1024
V7X_VMEM_LIMIT_BYTES = V7X_VMEM_BYTES * 7 // 8

ROW_TILE = 256
S5_STEPS = 64
S5_GLU_TILE = 256
PEER_ROUTE_TILE = 256
PEER_DENSE_TILE = 512


ADA_ROWS = 8
ADA_TN = 1536


def _ada_body(c_ref, w_ref, b_ref, o_ref):
    act = jax.nn.silu(c_ref[...]).astype(BF16)
    o_ref[...] = jnp.dot(act, w_ref[...].astype(BF16), preferred_element_type=F32) + b_ref[...]


def ada_modulation(c8, w, b):
    R, D = c8.shape
    N = w.shape[1]
    assert N % ADA_TN == 0
    return pl.pallas_call(
        _ada_body,
        out_shape=jax.ShapeDtypeStruct((R, N), F32),
        grid=(N // ADA_TN,),
        in_specs=[pl.BlockSpec((R, D), lambda j: (0, 0)), pl.BlockSpec((D, ADA_TN), lambda j: (0, j)),
                  pl.BlockSpec((1, ADA_TN), lambda j: (0, j))],
        out_specs=pl.BlockSpec((R, ADA_TN), lambda j: (0, j)),
        compiler_params=pltpu.CompilerParams(dimension_semantics=("arbitrary",),
                                             vmem_limit_bytes=V7X_VMEM_LIMIT_BYTES),
        name="ada_modulation",
    )(c8, w, b.reshape(1, N))
def _modmm_body(x_ref, shift_ref, scale_ref, w_ref, *out_refs, col_splits):
    xm = (x_ref[...] * (1.0 + scale_ref[0]) + shift_ref[0]).astype(BF16)
    o = 0
    for j, (ref, n) in enumerate(zip(out_refs, col_splits)):
        val = jnp.dot(xm, w_ref[:, o:o + n], preferred_element_type=F32)
        ref[...] = val.astype(ref.dtype)
        if j == 0:
            tm = val.shape[0]
            r = lax.broadcasted_iota(jnp.int32, (tm, tm), 0)
            c = lax.broadcasted_iota(jnp.int32, (tm, tm), 1)
            flip = jnp.where(r + c == tm - 1, 1.0, 0.0).astype(BF16)
            out_refs[-1][...] = jnp.dot(flip, val.astype(BF16), preferred_element_type=F32).astype(BF16)
        o += n


def modulated_matmul(x, shift, scale, w_bf16, col_splits, out_dtypes, rows_per_mod, seq_len, tm):
    T, K = x.shape
    assert T % tm == 0 and rows_per_mod % tm == 0 and seq_len % tm == 0
    tiles_per_mod = rows_per_mod // tm
    tps = seq_len // tm
    n_tot = sum(col_splits)
    assert w_bf16.shape == (K, n_tot)
    mod_spec = pl.BlockSpec((1, 1, K), lambda i: (i // tiles_per_mod, 0, 0))
    shapes = [jax.ShapeDtypeStruct((T, n), dt) for n, dt in zip(col_splits, out_dtypes, strict=True)]
    specs = [pl.BlockSpec((tm, n), lambda i: (i, 0)) for n in col_splits]
    shapes.append(jax.ShapeDtypeStruct((T, col_splits[0]), BF16))
    specs.append(pl.BlockSpec((tm, col_splits[0]), lambda i: ((i // tps) * tps + tps - 1 - i % tps, 0)))
    return pl.pallas_call(
        functools.partial(_modmm_body, col_splits=tuple(col_splits)),
        out_shape=shapes,
        grid=(T // tm,),
        in_specs=[pl.BlockSpec((tm, K), lambda i: (i, 0)), mod_spec, mod_spec,
                  pl.BlockSpec((K, n_tot), lambda i: (0, 0))],
        out_specs=specs,
        compiler_params=pltpu.CompilerParams(dimension_semantics=("arbitrary",),
                                             vmem_limit_bytes=V7X_VMEM_LIMIT_BYTES),
        name="modulated_matmul",
    )(x, shift, scale, w_bf16)


def _proj_ln_body(*refs, n_parts, alpha):
    part_refs = refs[:n_parts]
    x_ref, gate_ref, w_ref, g_ref, b_ref, o_ref = refs[n_parts:]
    acc = None
    o = 0
    for pr in part_refs:
        n = pr.shape[-1]
        d = jnp.dot(pr[...].astype(BF16), w_ref[o:o + n, :], preferred_element_type=F32)
        acc = d if acc is None else acc + d
        o += n
    h = alpha * x_ref[...] + gate_ref[0] * acc
    mu = jnp.mean(h, -1, keepdims=True)
    hc = h - mu
    var = jnp.mean(hc * hc, -1, keepdims=True)
    o_ref[...] = hc * lax.rsqrt(var + LN_EPS) * g_ref[...] + b_ref[...]


def proj_residual_ln(parts, x, gate, w_bf16, g, b, rows_per_mod, tm):
    T, D = x.shape
    assert T % tm == 0 and rows_per_mod % tm == 0
    tiles_per_mod = rows_per_mod // tm
    k_tot = sum(p.shape[-1] for p in parts)
    assert w_bf16.shape == (k_tot, D)
    row = lambda n: pl.BlockSpec((tm, n), lambda i: (i, 0))
    vec = pl.BlockSpec((1, D), lambda i: (0, 0))
    return pl.pallas_call(
        functools.partial(_proj_ln_body, n_parts=len(parts), alpha=DEEPNORM_ALPHA),
        out_shape=jax.ShapeDtypeStruct((T, D), F32),
        grid=(T // tm,),
        in_specs=[row(p.shape[-1]) for p in parts] + [
            row(D), pl.BlockSpec((1, 1, D), lambda i: (i // tiles_per_mod, 0, 0)),
            pl.BlockSpec((k_tot, D), lambda i: (0, 0)), vec, vec],
        out_specs=row(D),
        compiler_params=pltpu.CompilerParams(dimension_semantics=("arbitrary",),
                                             vmem_limit_bytes=V7X_VMEM_LIMIT_BYTES),
        name="proj_residual_ln",
    )(*parts, x, gate, w_bf16, g.reshape(1, D), b.reshape(1, D))


S5_CHAINS = 2 * BATCH
S5_STATE = S5_G * S5_N
S5_SCAN_COLS = 512
S5_SB_IN = S5_SCAN_COLS // S5_N * S5_P


def _s5_scan_body(u_ref, wb_ref, wc_ref, are_ref, aim_ref, y_ref, bu_ref, h_ref, *, steps):
    rows = steps * S5_CHAINS

    @pl.when(pl.program_id(0) == 0)
    def _():
        h_ref[...] = jnp.zeros_like(h_ref)

    u = u_ref[...]
    chain = lax.broadcasted_iota(jnp.int32, u.shape, 0) % S5_CHAINS
    fwd = chain < BATCH
    zero = jnp.zeros_like(u)
    uf = jnp.where(fwd, u, zero)
    ub = jnp.where(fwd, zero, u)
    is_fwd_y = lax.broadcasted_iota(jnp.int32, (rows, S5_SB_IN), 0) % S5_CHAINS < BATCH

    for cb in range(S5_STATE // S5_SCAN_COLS):
        cin = slice(cb * S5_SB_IN, (cb + 1) * S5_SB_IN)
        re = pl.ds(cb * S5_SCAN_COLS, S5_SCAN_COLS)
        im = pl.ds(S5_STATE + cb * S5_SCAN_COLS, S5_SCAN_COLS)
        bu = jnp.dot(jnp.concatenate([uf[:, cin], ub[:, cin]], axis=1), wb_ref[cb], preferred_element_type=F32)
        bu_ref[:, re] = bu[:, :S5_SCAN_COLS]
        bu_ref[:, im] = bu[:, S5_SCAN_COLS:]
        a_re = are_ref[:, re]
        a_im = aim_ref[:, re]

        def step(s, carry):
            h_re, h_im = carry
            r = pl.ds(pl.multiple_of(s * S5_CHAINS, S5_CHAINS), S5_CHAINS)
            n_re = a_re * h_re - a_im * h_im + bu_ref[r, re]
            n_im = a_re * h_im + a_im * h_re + bu_ref[r, im]
            bu_ref[r, re] = n_re
            bu_ref[r, im] = n_im
            return n_re, n_im

        h_re, h_im = lax.fori_loop(0, steps, step, (h_ref[:, re], h_ref[:, im]), unroll=8)
        h_ref[:, re] = h_re
        h_ref[:, im] = h_im
        hb = jnp.concatenate([bu_ref[:, re], bu_ref[:, im]], axis=1).astype(BF16)
        y2 = jnp.dot(hb, wc_ref[cb], preferred_element_type=F32)
        y_ref[cb] = jnp.where(is_fwd_y, y2[:, :S5_SB_IN], y2[:, S5_SB_IN:])


def s5_scan(u8, wb, wc, a_re8, a_im8, steps):
    n_rows = u8.shape[0]
    rows = steps * S5_CHAINS
    assert n_rows % rows == 0
    full = lambda a: pl.BlockSpec(a.shape, lambda i: (0,) * a.ndim)
    return pl.pallas_call(
        functools.partial(_s5_scan_body, steps=steps),
        out_shape=jax.ShapeDtypeStruct((W_GRP // S5_SB_IN, n_rows, S5_SB_IN), F32),
        grid=(n_rows // rows,),
        in_specs=[pl.BlockSpec((rows, W_GRP), lambda i: (i, 0)), full(wb), full(wc), full(a_re8), full(a_im8)],
        out_specs=pl.BlockSpec((W_GRP // S5_SB_IN, rows, S5_SB_IN), lambda i: (0, i, 0)),
        scratch_shapes=[pltpu.VMEM((rows, 2 * S5_STATE), F32), pltpu.VMEM((S5_CHAINS, 2 * S5_STATE), F32)],
        compiler_params=pltpu.CompilerParams(dimension_semantics=("arbitrary",),
                                             vmem_limit_bytes=V7X_VMEM_LIMIT_BYTES),
        name="s5_scan",
    )(u8, wb, wc, a_re8, a_im8)


def _s5_glu_body(u_ref, yf_ref, yb_ref, d_ref, w_ref, b_ref, o_ref):
    tm = u_ref.shape[1]
    r = lax.broadcasted_iota(jnp.int32, (tm, tm), 0)
    c = lax.broadcasted_iota(jnp.int32, (tm, tm), 1)
    flip = jnp.where(r + c == tm - 1, 1.0, 0.0).astype(BF16)
    for b in range(BATCH):
        n_cb = yf_ref.shape[0]
        yf = jnp.concatenate([yf_ref[cb, pl.ds(b, tm, stride=S5_CHAINS), :] for cb in range(n_cb)], axis=1)
        yb = jnp.concatenate([yb_ref[cb, pl.ds(BATCH + b, tm, stride=S5_CHAINS), :] for cb in range(n_cb)],
                             axis=1)
        hi = yb.astype(BF16)
        lo = (yb - hi.astype(F32)).astype(BF16)
        yb = jnp.dot(flip, hi, preferred_element_type=F32) + jnp.dot(flip, lo, preferred_element_type=F32)
        y = d_ref[...] * u_ref[b] + yf + yb
        g = jax.nn.gelu(y)
        z = jnp.dot(g.astype(BF16), w_ref[...], preferred_element_type=F32) + b_ref[...]
        o_ref[b] = g * jax.nn.sigmoid(z)


def s5_glu_pallas(u, y8, first_step, d, w_bf16, b, tm):
    B_, n, W = u.shape
    assert B_ == BATCH and n % tm == 0 and first_step % tm == 0
    nt = n // tm
    off = first_step // tm
    row = pl.BlockSpec((B_, tm, W), lambda i: (0, i, 0))
    vec = pl.BlockSpec((1, W), lambda i: (0, 0))
    return pl.pallas_call(
        _s5_glu_body,
        out_shape=jax.ShapeDtypeStruct((B_, n, W), F32),
        grid=(nt,),
        in_specs=[row, pl.BlockSpec((y8.shape[0], tm * S5_CHAINS, y8.shape[2]), lambda i: (0, off + i, 0)),
                  pl.BlockSpec((y8.shape[0], tm * S5_CHAINS, y8.shape[2]), lambda i: (0, off + nt - 1 - i, 0)),
                  vec, pl.BlockSpec((W, W), lambda i: (0, 0)), vec],
        out_specs=row,
        compiler_params=pltpu.CompilerParams(dimension_semantics=("arbitrary",),
                                             vmem_limit_bytes=V7X_VMEM_LIMIT_BYTES),
        name="s5_glu",
    )(u, y8, y8, d.reshape(1, W), w_bf16, b.reshape(1, W))


def _s5_weights(a_re, a_im, log_dt, b_re, b_im, c_re, c_im):
    eye = jnp.eye(S5_G, dtype=F32)
    wb, a8 = [], []
    wc = []
    for di in range(2):
        ab_re, ab_im, bb_re, bb_im = s5_discretize(a_re[di], a_im[di], log_dt[di], b_re[di], b_im[di])
        blk = lambda m: jnp.einsum('gnp,gh->gphn', m, eye).reshape(W_GRP, S5_STATE)
        wb.append(jnp.concatenate([blk(bb_re), blk(bb_im)], axis=1))
        a8.append((jnp.broadcast_to(ab_re.reshape(1, S5_STATE), (BATCH, S5_STATE)),
                   jnp.broadcast_to(ab_im.reshape(1, S5_STATE), (BATCH, S5_STATE))))
        cblk = lambda m: jnp.einsum('gpn,gh->gnhp', m.astype(F32), eye).reshape(S5_STATE, W_GRP)
        wc.append(jnp.concatenate([cblk(c_re[di]), -cblk(c_im[di])], axis=0))
    wb_sb, wc_sb = [], []
    for sb in range(S5_STATE // S5_SCAN_COLS):
        cin = slice(sb * S5_SB_IN, (sb + 1) * S5_SB_IN)
        re = slice(sb * S5_SCAN_COLS, (sb + 1) * S5_SCAN_COLS)
        im = slice(S5_STATE + sb * S5_SCAN_COLS, S5_STATE + (sb + 1) * S5_SCAN_COLS)
        wb_sb.append(jnp.concatenate([jnp.concatenate([w[cin, re], w[cin, im]], axis=1) for w in wb], axis=0))
        wc_sb.append(jnp.concatenate([jnp.concatenate([w[re, cin], w[im, cin]], axis=0) for w in wc], axis=1))
    wb_sb = jnp.stack(wb_sb).astype(BF16)
    wc_sb = jnp.stack(wc_sb).astype(BF16)
    a_re8 = jnp.concatenate([a8[0][0], a8[1][0]], axis=0)
    a_im8 = jnp.concatenate([a8[0][1], a8[1][1]], axis=0)
    return wb_sb, wc_sb, a_re8, a_im8


def s5_mixer_pallas(u_lat, u_ctx, rev_lat, rev_ctx, a_re, a_im, log_dt, b_re, b_im, c_re, c_im, d, w_glu,
                    b_glu, with_ctx_out, steps=S5_STEPS, tm=S5_GLU_TILE):
    B_, L, W = u_lat.shape
    C_ = u_ctx.shape[1]
    assert B_ == BATCH and W == W_GRP
    wb, wc, a_re8, a_im8 = _s5_weights(a_re, a_im, log_dt, b_re, b_im, c_re, c_im)
    seq_f = jnp.concatenate([u_ctx.astype(BF16), u_lat.astype(BF16)], axis=1)
    seq_b = jnp.concatenate([rev_ctx, rev_lat], axis=1)
    u8 = jnp.concatenate([seq_f, seq_b], axis=0).transpose(1, 0, 2).reshape((C_ + L) * S5_CHAINS, W)
    y8 = s5_scan(u8, wb, wc, a_re8, a_im8, steps)
    w_glu_b = w_glu.astype(BF16)
    tm = min(tm, C_)
    out_lat = s5_glu_pallas(u_lat, y8, C_, d, w_glu_b, b_glu, tm).reshape(B_ * L, W)
    out_ctx = s5_glu_pallas(u_ctx, y8, 0, d, w_glu_b, b_glu, tm).reshape(B_ * C_, W) if with_ctx_out else None
    return out_lat, out_ctx


PEER_HALF = PEER_QDIM // 2


def _argmax_rows(v, r):
    while v.shape[0] > 1:
        half = v.shape[0] // 2
        take_hi = v[half:] > v[:half]
        r = jnp.where(take_hi, r[half:], r[:half])
        v = jnp.maximum(v[:half], v[half:])
    return v, r


def _topk_rows(xs, k, with_rank):
    n, lanes = xs[0].shape
    row = lax.broadcasted_iota(jnp.int32, (n, lanes), 0).astype(F32)
    krow = lax.broadcasted_iota(jnp.int32, (k, lanes), 0)

    def body(it, carry):
        out = []
        for x, vals, aux in carry:
            m, first = _argmax_rows(x, row)
            hit = row == first
            sel = krow == it
            aux = jnp.where(hit, it.astype(F32), aux) if with_rank else jnp.where(sel, first, aux)
            out.append((jnp.where(hit, -jnp.inf, x), jnp.where(sel, m, vals), aux))
        return tuple(out)

    zeros = jnp.zeros((k, lanes), F32)
    aux0 = jnp.full((n, lanes), float(k), F32) if with_rank else zeros
    res = lax.fori_loop(0, k, body, tuple((x, zeros, aux0) for x in xs), unroll=1 if with_rank else 2)
    return [(vals, aux) for _, vals, aux in res]


PEER_GRID_COLS = tuple(PEER_TOPK // (i + 1) for i in range(PEER_TOPK))
PEER_GRID_ROWS = 64


def _peer_route_body(x_ref, shift_ref, scale_ref, wq_ref, keys_ref, h_ref, n1_ref, c1_ref, r2_ref, e2_ref,
                     q_ref, *, lane_tiles):
    hm = (x_ref[...] * (1.0 + scale_ref[0]) + shift_ref[0]).astype(BF16)
    h_ref[...] = hm
    q_ref[...] = jnp.dot(hm, wq_ref[...], preferred_element_type=F32).astype(BF16)
    K = PEER_TOPK
    key = lax.broadcasted_iota(jnp.int32, (PEER_NK, V7X_LANES), 0).astype(F32)
    for hd in range(PEER_HEADS):
        for lt in range(lane_tiles):
            tok = pl.ds(lt * V7X_LANES, V7X_LANES)
            sc = []
            for side in range(2):
                col = (2 * hd + side) * PEER_HALF
                qs = q_ref[tok, col:col + PEER_HALF]
                sc.append(lax.dot_general(keys_ref[2 * hd + side], qs, (((1,), (1,)), ((), ())),
                                          preferred_element_type=F32))
            (v1, idx1), (v2, idx2) = _topk_rows(sc, K, with_rank=False)
            cells = [v1[i:i + 1] + v2[:PEER_GRID_COLS[i]] for i in range(K)]
            cells.append(jnp.full((PEER_GRID_ROWS - sum(PEER_GRID_COLS), V7X_LANES), -jnp.inf, F32))
            (vc, rc), = _topk_rows([jnp.concatenate(cells, axis=0)], K, with_rank=True)
            z = jnp.sum(jnp.exp(vc - vc[0:1]), axis=0, keepdims=True)
            chosen = jnp.where(rc < float(K), 1.0, 0.0)
            n1 = jnp.zeros_like(key)
            r2 = jnp.full(key.shape, float(K), F32)
            off = 0
            for i in range(K):
                n_i = jnp.sum(chosen[off:off + PEER_GRID_COLS[i]], axis=0, keepdims=True)
                n1 = jnp.where(key == idx1[i:i + 1], n_i, n1)
                r2 = jnp.where(key == idx2[i:i + 1], float(i), r2)
                off += PEER_GRID_COLS[i]
            n1_ref[hd, :, tok] = n1
            c1_ref[hd, :, tok] = jnp.exp(sc[0] - v1[0:1]) / z
            r2_ref[hd, :, tok] = r2.astype(BF16)
            e2_ref[hd, :, tok] = jnp.exp(sc[1] - v2[0:1]).astype(BF16)


def peer_route(x, shift, scale, wq_bf16, keys_bf16, rows_per_mod, tm):
    T, D = x.shape
    assert T % tm == 0 and rows_per_mod % tm == 0 and tm % V7X_LANES == 0
    tiles_per_mod = rows_per_mod // tm
    mod_spec = pl.BlockSpec((1, 1, D), lambda i: (i // tiles_per_mod, 0, 0))
    tab = lambda dt: jax.ShapeDtypeStruct((PEER_HEADS, PEER_NK, T), dt)
    tab_spec = pl.BlockSpec((PEER_HEADS, PEER_NK, tm), lambda i: (0, 0, i))
    return pl.pallas_call(
        functools.partial(_peer_route_body, lane_tiles=tm // V7X_LANES),
        out_shape=[jax.ShapeDtypeStruct((T, D), BF16), tab(F32), tab(F32), tab(BF16), tab(BF16)],
        grid=(T // tm,),
        in_specs=[pl.BlockSpec((tm, D), lambda i: (i, 0)), mod_spec, mod_spec,
                  pl.BlockSpec(wq_bf16.shape, lambda i: (0, 0)),
                  pl.BlockSpec(keys_bf16.shape, lambda i: (0, 0, 0))],
        out_specs=[pl.BlockSpec((tm, D), lambda i: (i, 0)), tab_spec, tab_spec, tab_spec, tab_spec],
        scratch_shapes=[pltpu.VMEM((tm, PEER_HEADS * PEER_QDIM), BF16)],
        compiler_params=pltpu.CompilerParams(dimension_semantics=("arbitrary",),
                                             vmem_limit_bytes=V7X_VMEM_LIMIT_BYTES),
        name="peer_route",
    )(x, shift, scale, wq_bf16, keys_bf16)


def _peer_dense_body(h_ref, u_ref, vt_ref, n1_ref, c1_ref, r2_ref, e2_ref, x_ref, gate_ref, g_ref, b_ref,
                     o_ref, acc_ref, a_ref, *, n_slab):
    j = pl.program_id(1)

    @pl.when(j == 0)
    def _():
        acc_ref[...] = jnp.zeros_like(acc_ref)

    rows = PEER_SCORE_SLABS * PEER_NK
    for p in range(n_slab // PEER_SCORE_SLABS):
        s = lax.dot_general(u_ref[p * rows:(p + 1) * rows, :], h_ref[...], (((1,), (1,)), ((), ())),
                            preferred_element_type=F32)
        for kk in range(PEER_SCORE_SLABS):
            k = PEER_SCORE_SLABS * p + kk
            for lt in range(h_ref.shape[0] // V7X_LANES):
                lanes = slice(lt * V7X_LANES, (lt + 1) * V7X_LANES)
                g = None
                for hd in range(PEER_HEADS):
                    n1row = n1_ref[hd, k:k + 1, lanes].astype(BF16)
                    c1row = c1_ref[hd, k:k + 1, lanes].astype(BF16)
                    sel = lax.clamp(jnp.zeros((), BF16), n1row - r2_ref[hd, :, lanes], jnp.ones((), BF16))
                    gh = sel * e2_ref[hd, :, lanes] * c1row
                    g = gh if g is None else g + gh
                sk = s[kk * PEER_NK:(kk + 1) * PEER_NK, lanes]
                a_ref[k * PEER_NK:(k + 1) * PEER_NK, lanes] = jax.nn.gelu(sk).astype(BF16) * g
    acc_ref[...] += jnp.dot(vt_ref[0], a_ref[...], preferred_element_type=F32)

    @pl.when(j == pl.num_programs(1) - 1)
    def _():
        hres = DEEPNORM_ALPHA * x_ref[...] + gate_ref[0] * acc_ref[...].T
        mu = jnp.mean(hres, -1, keepdims=True)
        hc = hres - mu
        var = jnp.mean(hc * hc, -1, keepdims=True)
        o_ref[...] = hc * lax.rsqrt(var + LN_EPS) * g_ref[...] + b_ref[...]


def peer_dense(h_bf16, u_bf16, vt_tiles, n1, c1, r2, e2, x, gate, ln_g, ln_b, rows_per_mod, tm, n_slab):
    T, D = h_bf16.shape
    NE = u_bf16.shape[0]
    e_tile = n_slab * PEER_NK
    n_e = NE // e_tile
    assert T % tm == 0 and NE % e_tile == 0 and vt_tiles.shape == (n_e, D, e_tile)
    slab_spec = pl.BlockSpec((PEER_HEADS, n_slab, tm), lambda i, j: (0, j, i))
    tok_spec = pl.BlockSpec((PEER_HEADS, PEER_NK, tm), lambda i, j: (0, 0, i))
    return pl.pallas_call(
        functools.partial(_peer_dense_body, n_slab=n_slab),
        out_shape=jax.ShapeDtypeStruct((T, D), F32),
        grid=(T // tm, n_e),
        in_specs=[pl.BlockSpec((tm, D), lambda i, j: (i, 0)),
                  pl.BlockSpec((e_tile, D), lambda i, j: (j, 0)),
                  pl.BlockSpec((1, D, e_tile), lambda i, j: (j, 0, 0)),
                  slab_spec, slab_spec, tok_spec, tok_spec,
                  pl.BlockSpec((tm, D), lambda i, j: (i, 0)),
                  pl.BlockSpec((1, 1, D), lambda i, j: (i // (rows_per_mod // tm), 0, 0)),
                  pl.BlockSpec((1, D), lambda i, j: (0, 0)), pl.BlockSpec((1, D), lambda i, j: (0, 0))],
        out_specs=pl.BlockSpec((tm, D), lambda i, j: (i, 0)),
        scratch_shapes=[pltpu.VMEM((D, tm), F32), pltpu.VMEM((e_tile, tm), BF16)],
        compiler_params=pltpu.CompilerParams(dimension_semantics=("arbitrary", "arbitrary"),
                                             vmem_limit_bytes=V7X_VMEM_LIMIT_BYTES),
        name="peer_dense",
    )(h_bf16, u_bf16, vt_tiles, n1, c1, r2, e2, x, gate, ln_g.reshape(1, D), ln_b.reshape(1, D))


PEER_SLABS = 8
PEER_SCORE_SLABS = 4


def peer_v_tiles(v_tab):
    e_tile = PEER_SLABS * PEER_NK
    return v_tab.astype(BF16).reshape(v_tab.shape[0] // e_tile, e_tile, v_tab.shape[1]).transpose(0, 2, 1)


def peer_sublayer(x, shift, scale, gate, wq_bf16, keys_bf16, u_bf16, vt_tiles, ln_g, ln_b, rows_per_mod,
                  tm_route=PEER_ROUTE_TILE, tm_dense=PEER_DENSE_TILE):
    h, n1, c1, r2, e2 = peer_route(x, shift, scale, wq_bf16, keys_bf16, rows_per_mod, tm_route)
    return peer_dense(h, u_bf16, vt_tiles, n1, c1, r2, e2, x, gate, ln_g, ln_b, rows_per_mod, tm_dense,
                      PEER_SLABS)


POOL_HALO = 8
POOL_ROWS = 256


def _pool_body(u_ref, w_ref, scale_ref, o_ref, pad_ref, *, seq_len):
    L = seq_len
    chunk = min(POOL_ROWS, L)
    zeros = jnp.zeros((POOL_HALO, W_GRP), F32)
    pad_ref[0:POOL_HALO, :] = zeros
    pad_ref[POOL_HALO + L:2 * POOL_HALO + L, :] = zeros
    pad_ref[POOL_HALO:POOL_HALO + L, :] = u_ref[0]
    for r0 in range(0, L, chunk):
        t = r0 + lax.broadcasted_iota(jnp.int32, (chunk, POOL_C), 0)
        for j, w in enumerate(POOL_WINDOWS):
            cols = slice(j * POOL_C, (j + 1) * POOL_C)
            acc = None
            for o in range(-(w // 2), w - w // 2):
                s = pad_ref[POOL_HALO + r0 + o:POOL_HALO + r0 + o + chunk, cols]
                acc = s if acc is None else acc + s
            lo = jnp.maximum(t - w // 2, 0)
            hi = jnp.minimum(t - w // 2 + w - 1, L - 1)
            cnt = (hi - lo + 1).astype(F32)
            pooled = acc / cnt - u_ref[0, r0:r0 + chunk, cols]
            y = jnp.dot(pooled.astype(BF16), w_ref[j], preferred_element_type=F32)
            o_ref[0, r0:r0 + chunk, cols] = y * scale_ref[:, cols]


def pool_mix_pallas(u, pool_w, pool_scale):
    B_, L, W = u.shape
    assert max(POOL_WINDOWS) // 2 <= POOL_HALO and L % min(POOL_ROWS, L) == 0
    blk = pl.BlockSpec((1, L, W), lambda b: (b, 0, 0))
    return pl.pallas_call(
        functools.partial(_pool_body, seq_len=L),
        out_shape=jax.ShapeDtypeStruct((B_, L, W), F32),
        grid=(B_,),
        in_specs=[blk, pl.BlockSpec(pool_w.shape, lambda b: (0, 0, 0)), pl.BlockSpec((1, W), lambda b: (0, 0))],
        out_specs=blk,
        scratch_shapes=[pltpu.VMEM((L + 2 * POOL_HALO, W), F32)],
        compiler_params=pltpu.CompilerParams(dimension_semantics=("arbitrary",),
                                             vmem_limit_bytes=V7X_VMEM_LIMIT_BYTES),
        name="pool_mix",
    )(u, pool_w.astype(BF16), pool_scale.reshape(1, W))


NA_TILE_R = 4
NA_TILE = NA_TILE_R * GRID_W
NA_SCALE = NA_HD ** -0.5


def _na_bias_table(rpb, n_rows):
    n_tiles = n_rows // NA_TILE_R
    KR = min(NA_WIN_R, n_rows)
    a = np.array([0, min(2, n_tiles - 1), n_tiles - 1]).reshape(3, 1, 1, 1, 1, 1)
    d = np.arange(3).reshape(1, 3, 1, 1, 1, 1)
    i = np.arange(NA_TILE_R).reshape(1, 1, NA_TILE_R, 1, 1, 1)
    qc = np.arange(GRID_W).reshape(1, 1, 1, GRID_W, 1, 1)
    j = np.arange(NA_TILE_R).reshape(1, 1, 1, 1, NA_TILE_R, 1)
    kc = np.arange(GRID_W).reshape(1, 1, 1, 1, 1, GRID_W)
    qr = NA_TILE_R * a + i
    kr = NA_TILE_R * (a + d - 1) + j
    rs = np.clip(qr - KR // 2, 0, n_rows - KR)
    c0 = np.clip(qc - NA_WIN_C // 2, 0, GRID_W - NA_WIN_C)
    ok = (kr >= rs) & (kr < rs + KR) & (kr >= 0) & (kr < n_rows) & (kc >= c0) & (kc < c0 + NA_WIN_C)
    row_rel = np.clip(kr - qr + NA_WIN_R - 1, 0, 2 * NA_WIN_R - 2)[0, :, :, 0, :, 0]
    col_rel = (np.clip(kc - qc, -(NA_WIN_C - 1), NA_WIN_C - 1) + NA_WIN_C - 1)[0, 0, 0, :, 0, :]
    onehot = (col_rel[None] == np.arange(2 * NA_WIN_C - 1)[:, None, None]).astype(np.float32)
    bias_rc = jnp.einsum('hrc,cqk->hrqk', rpb.astype(F32), onehot, precision=lax.Precision.HIGHEST)
    tab = jnp.stack([bias_rc[:, int(r)] for r in row_rel.reshape(-1)], axis=1)
    tab = tab.reshape(NA_H, 3, NA_TILE_R, NA_TILE_R, GRID_W, GRID_W).transpose(0, 1, 2, 4, 3, 5)
    tab = jnp.where(ok[:, None], tab[None], -jnp.inf)
    return tab.reshape(3, NA_H, 3, NA_TILE, NA_TILE)


def _na_body(q_ref, k_ref, v_ref, kc_ref, vc_ref, t_ref, o_ref, *, n_tiles, with_grid):
    a = pl.program_id(1)
    nt = (((1,), (1,)), ((), ()))
    for h in range(NA_H):
        hs = slice(h * NA_HD, (h + 1) * NA_HD)
        qh = q_ref[:, hs]
        scores = [lax.dot_general(qh, kc_ref[:, hs], nt, preferred_element_type=F32) * NA_SCALE]
        vals = [vc_ref[:, hs]]
        if with_grid:
            for d in range(3):
                ti = jnp.clip(a + d - 1, 0, n_tiles - 1)
                rows = pl.ds(pl.multiple_of(ti * NA_TILE, NA_TILE), NA_TILE)
                s = lax.dot_general(qh, k_ref[rows, hs], nt, preferred_element_type=F32)
                scores.append(s * NA_SCALE + t_ref[0, h, d])
                vals.append(v_ref[rows, hs])
        m = scores[0].max(axis=-1, keepdims=True)
        for s in scores[1:]:
            m = jnp.maximum(m, s.max(axis=-1, keepdims=True))
        den = None
        acc = None
        for s, vv in zip(scores, vals):
            p = jnp.exp(s - m)
            l = p.sum(axis=-1, keepdims=True)
            o = jnp.dot(p.astype(BF16), vv, preferred_element_type=F32)
            den = l if den is None else den + l
            acc = o if acc is None else acc + o
        o_ref[:, hs] = acc / den


def na_attention(q, k, v, kc, vc, table, seq_len, ctx_len, with_grid):
    W = q.shape[1]
    n_b = kc.shape[0] // ctx_len
    lq = q.shape[0] // n_b
    assert lq % NA_TILE == 0
    q_tiles = lq // NA_TILE
    n_tiles = seq_len // NA_TILE
    pat = lambda b, a: (jnp.where(a == 0, 0, jnp.where(a == n_tiles - 1, 2, 1)), 0, 0, 0, 0)
    qo_spec = pl.BlockSpec((NA_TILE, W), lambda b, a: (b * q_tiles + a, 0))
    return pl.pallas_call(
        functools.partial(_na_body, n_tiles=n_tiles, with_grid=with_grid),
        out_shape=jax.ShapeDtypeStruct(q.shape, F32),
        grid=(n_b, q_tiles),
        in_specs=[qo_spec,
                  pl.BlockSpec((seq_len, W), lambda b, a: (b, 0)), pl.BlockSpec((seq_len, W), lambda b, a: (b, 0)),
                  pl.BlockSpec((ctx_len, W), lambda b, a: (b, 0)), pl.BlockSpec((ctx_len, W), lambda b, a: (b, 0)),
                  pl.BlockSpec((1,) + table.shape[1:], pat)],
        out_specs=qo_spec,
        compiler_params=pltpu.CompilerParams(dimension_semantics=("arbitrary", "arbitrary"),
                                             vmem_limit_bytes=V7X_VMEM_LIMIT_BYTES),
        name="na_attention",
    )(q, k, v, kc, vc, table)


SSM_HALO = 8
SSM_BC = SSM_G * SSM_N


def _ssm_conv_body(prev_ref, cur_ref, next_ref, w_ref, b_ref, o_ref, pad_ref, *, tiles_per_seq):
    pos = pl.program_id(0) % tiles_per_seq
    tm = cur_ref.shape[0]
    pad_ref[0:SSM_HALO, :] = jnp.where(pos == 0, 0.0, prev_ref[...])
    pad_ref[SSM_HALO:SSM_HALO + tm, :] = cur_ref[...]
    pad_ref[SSM_HALO + tm:2 * SSM_HALO + tm, :] = jnp.where(pos == tiles_per_seq - 1, 0.0, next_ref[...])
    lead = (SSM_CONV - 1) // 2
    y = b_ref[...]
    for k in range(SSM_CONV):
        y = y + w_ref[k:k + 1, :] * pad_ref[SSM_HALO - lead + k:SSM_HALO - lead + k + tm, :]
    o_ref[...] = jax.nn.silu(y)


def ssm_conv(xbc, conv_w, conv_b, seq_len, tm):
    T, CH = xbc.shape
    assert seq_len % tm == 0 and tm % SSM_HALO == 0
    hb = tm // SSM_HALO
    n_hb = T // SSM_HALO
    return pl.pallas_call(
        functools.partial(_ssm_conv_body, tiles_per_seq=seq_len // tm),
        out_shape=jax.ShapeDtypeStruct((T, CH), F32),
        grid=(T // tm,),
        in_specs=[pl.BlockSpec((SSM_HALO, CH), lambda i: (jnp.maximum(i * hb - 1, 0), 0)),
                  pl.BlockSpec((tm, CH), lambda i: (i, 0)),
                  pl.BlockSpec((SSM_HALO, CH), lambda i: (jnp.minimum((i + 1) * hb, n_hb - 1), 0)),
                  pl.BlockSpec((SSM_CONV, CH), lambda i: (0, 0)), pl.BlockSpec((1, CH), lambda i: (0, 0))],
        out_specs=pl.BlockSpec((tm, CH), lambda i: (i, 0)),
        scratch_shapes=[pltpu.VMEM((tm + 2 * SSM_HALO, CH), F32)],
        compiler_params=pltpu.CompilerParams(dimension_semantics=("arbitrary",),
                                             vmem_limit_bytes=V7X_VMEM_LIMIT_BYTES),
        name="ssm_conv",
    )(xbc, xbc, xbc, conv_w, conv_b.reshape(1, CH))


def _ssd_chunk_prep(xbc_ref, dt_ref, bias_ref, a_ref, *, di):
    rev = di == 1
    Q = xbc_ref.shape[0]
    dt_all = jax.nn.softplus(dt_ref[...] + bias_ref[...])
    a_cs = dt_all * a_ref[...]
    row_id = lax.broadcasted_iota(jnp.int32, a_cs.shape, 0)
    sh = 1
    while sh < Q:
        if rev:
            a_cs = a_cs + jnp.where(row_id < Q - sh, pltpu.roll(a_cs, Q - sh, 0), 0.0)
        else:
            a_cs = a_cs + jnp.where(row_id >= sh, pltpu.roll(a_cs, sh, 0), 0.0)
        sh *= 2
    a_cs_t = a_cs.T
    a_tot = a_cs[0:1, :] if rev else a_cs[Q - 1:Q, :]
    l_id = lax.broadcasted_iota(jnp.int32, (Q, Q), 0)
    s_id = lax.broadcasted_iota(jnp.int32, (Q, Q), 1)
    causal = (l_id <= s_id) if rev else (l_id >= s_id)
    cb = []
    for g in range(SSM_G):
        bg = xbc_ref[:, W_GRP + g * SSM_N:W_GRP + (g + 1) * SSM_N].astype(BF16)
        cg = xbc_ref[:, W_GRP + SSM_BC + g * SSM_N:W_GRP + SSM_BC + (g + 1) * SSM_N].astype(BF16)
        cb.append((bg, cg, lax.dot_general(cg, bg, (((1,), (1,)), ((), ())), preferred_element_type=F32)))

    src = lax.broadcasted_iota(jnp.int32, (V7X_LANES, W_GRP), 0)
    dst = lax.broadcasted_iota(jnp.int32, (V7X_LANES, W_GRP), 1)
    spread = jnp.where(src == di * SSM_H + dst // SSM_HD, 1.0, 0.0).astype(BF16)

    def per_channel(m):
        hi = m.astype(BF16)
        lo = (m - hi.astype(F32)).astype(BF16)
        return (jnp.dot(hi, spread, preferred_element_type=F32) + jnp.dot(lo, spread, preferred_element_type=F32))

    x_dt = xbc_ref[:, :W_GRP] * per_channel(dt_all)
    x_in = x_dt.astype(BF16)
    x_out = (x_dt * per_channel(jnp.exp(a_tot - a_cs))).astype(BF16)
    e_in = per_channel(jnp.exp(a_cs))
    return a_cs, a_cs_t, a_tot, causal, cb, x_in, x_out, e_in


def _ssd_head(prep, h_scr, y_ref, *, di, h):
    a_cs, a_cs_t, a_tot, causal, cb, x_in, x_out, e_in = prep
    nt = (((1,), (1,)), ((), ()))
    tn = (((0,), (0,)), ((), ()))
    c = di * SSM_H + h
    ch = slice(h * SSM_HD, (h + 1) * SSM_HD)
    bg, cg, cbg = cb[h // (SSM_H // SSM_G)]
    lm = jnp.exp(jnp.where(causal, a_cs[:, c:c + 1] - a_cs_t[c:c + 1, :], -jnp.inf))
    hp = h_scr[di, h]
    yd = jnp.dot((cbg * lm).astype(BF16), x_in[:, ch], preferred_element_type=F32)
    yo = lax.dot_general(cg, hp.astype(BF16), nt, preferred_element_type=F32) * e_in[:, ch]
    y_ref[:, ch] = yd + yo
    h_scr[di, h] = (jnp.exp(a_tot[:, c:c + 1]) * hp
                    + lax.dot_general(x_out[:, ch], bg, tn, preferred_element_type=F32))


def _ssd_body(xf_ref, dtf_ref, xb_ref, dtb_ref, bias_ref, a_ref, h0_ref, yf_ref, yb_ref, hfin_ref, h_scr):
    ci = pl.program_id(1)

    @pl.when(ci == 0)
    def _():
        h_scr[...] = h0_ref[:, 0]

    prep_f = _ssd_chunk_prep(xf_ref, dtf_ref, bias_ref, a_ref, di=0)
    prep_b = _ssd_chunk_prep(xb_ref, dtb_ref, bias_ref, a_ref, di=1)
    for h in range(SSM_H):
        _ssd_head(prep_f, h_scr, yf_ref, di=0, h=h)
        _ssd_head(prep_b, h_scr, yb_ref, di=1, h=h)

    @pl.when(ci == pl.num_programs(1) - 1)
    def _():
        hfin_ref[:, 0] = h_scr[...]


def ssd_scan(xbc_act, dt_raw, bias128, a128, h0, seq_len):
    T = xbc_act.shape[0]
    n_b = T // seq_len
    Q = min(SSM_CHUNK, seq_len)
    nc = seq_len // Q
    fwd = lambda b, i: (b * nc + i, 0)
    bwd = lambda b, i: (b * nc + nc - 1 - i, 0)
    vec = pl.BlockSpec((1, V7X_LANES), lambda b, i: (0, 0))
    st = pl.BlockSpec((2, 1, SSM_H, SSM_HD, SSM_N), lambda b, i: (0, b, 0, 0, 0))
    y_shape = jax.ShapeDtypeStruct((T, W_GRP), F32)
    return pl.pallas_call(
        _ssd_body,
        out_shape=[y_shape, y_shape, jax.ShapeDtypeStruct(h0.shape, F32)],
        grid=(n_b, nc),
        in_specs=[pl.BlockSpec((Q, SSM_CONV_CH), fwd), pl.BlockSpec((Q, V7X_LANES), fwd),
                  pl.BlockSpec((Q, SSM_CONV_CH), bwd), pl.BlockSpec((Q, V7X_LANES), bwd), vec, vec, st],
        out_specs=[pl.BlockSpec((Q, W_GRP), fwd), pl.BlockSpec((Q, W_GRP), bwd), st],
        scratch_shapes=[pltpu.VMEM((2, SSM_H, SSM_HD, SSM_N), F32)],
        compiler_params=pltpu.CompilerParams(dimension_semantics=("arbitrary", "arbitrary"),
                                             vmem_limit_bytes=V7X_VMEM_LIMIT_BYTES),
        name="ssd_scan",
    )(xbc_act, dt_raw, xbc_act, dt_raw, bias128, a128, h0)


def _ssm_out_body(xbc_ref, yf_ref, yb_ref, z_ref, d_ref, nw_ref, o_ref):
    y = d_ref[...] * xbc_ref[:, :W_GRP] + yf_ref[...] + yb_ref[...]
    g = y * jax.nn.silu(z_ref[...])
    gw = W_GRP // SSM_G
    for k in range(SSM_G):
        gk = g[:, k * gw:(k + 1) * gw]
        r = lax.rsqrt(jnp.mean(gk * gk, -1, keepdims=True) + LN_EPS)
        o_ref[:, k * gw:(k + 1) * gw] = gk * r * nw_ref[:, k * gw:(k + 1) * gw]


def ssm_out(xbc_act, yf, yb, z, d512, norm_w, tm):
    T = z.shape[0]
    assert T % tm == 0
    row = pl.BlockSpec((tm, W_GRP), lambda i: (i, 0))
    vec = pl.BlockSpec((1, W_GRP), lambda i: (0, 0))
    return pl.pallas_call(
        _ssm_out_body,
        out_shape=jax.ShapeDtypeStruct((T, W_GRP), F32),
        grid=(T // tm,),
        in_specs=[pl.BlockSpec((tm, SSM_CONV_CH), lambda i: (i, 0)), row, row, row, vec, vec],
        out_specs=row,
        compiler_params=pltpu.CompilerParams(dimension_semantics=("arbitrary",),
                                             vmem_limit_bytes=V7X_VMEM_LIMIT_BYTES),
        name="ssm_out",
    )(xbc_act, yf, yb, z, d512, norm_w.reshape(1, W_GRP))


def ssm_mixer_pallas(z_l, xbc_l, dt_l, z_c, xbc_c, dt_c, conv_w, conv_b, dt_bias, a_log, d, norm_w,
                     seq_len, ctx_len, with_ctx_out, tm=ROW_TILE):
    n_b = z_l.shape[0] // seq_len
    pad = V7X_LANES - 2 * SSM_H
    bias128 = jnp.pad(dt_bias.astype(F32).reshape(1, 2 * SSM_H), ((0, 0), (0, pad)))
    a128 = jnp.pad(-jnp.exp(a_log.astype(F32)).reshape(1, 2 * SSM_H), ((0, 0), (0, pad)))
    d512 = jnp.repeat(d.astype(F32), SSM_HD).reshape(1, W_GRP)
    act_c = ssm_conv(xbc_c, conv_w, conv_b, ctx_len, min(tm, ctx_len))
    act_l = ssm_conv(xbc_l, conv_w, conv_b, seq_len, tm)
    zero = jnp.zeros((2, n_b, SSM_H, SSM_HD, SSM_N), F32)
    ycf, ycb, hc = ssd_scan(act_c, dt_c, bias128, a128, zero, ctx_len)
    ylf, ylb, _ = ssd_scan(act_l, dt_l, bias128, a128, hc, seq_len)
    out_l = ssm_out(act_l, ylf, ylb, z_l, d512, norm_w, tm)
    out_c = ssm_out(act_c, ycf, ycb, z_c, d512, norm_w, min(tm, z_c.shape[0])) if with_ctx_out else None
    return out_l, out_c


def s5_discretize(a_re, a_im, log_dt, b_re, b_im):
    a_re, a_im = a_re.astype(F32), a_im.astype(F32)
    b_re, b_im = b_re.astype(F32), b_im.astype(F32)
    dt = jnp.exp(log_dt.astype(F32))[:, None]
    mag = jnp.exp(a_re * dt)
    ab_re = mag * jnp.cos(a_im * dt)
    ab_im = mag * jnp.sin(a_im * dt)
    den = a_re * a_re + a_im * a_im
    f_re = ((ab_re - 1) * a_re + ab_im * a_im) / den
    f_im = (ab_im * a_re - (ab_re - 1) * a_im) / den
    bb_re = f_re[..., None] * b_re - f_im[..., None] * b_im
    bb_im = f_re[..., None] * b_im + f_im[..., None] * b_re
    return ab_re, ab_im, bb_re, bb_im


def _mixers(p_l, p_c, prm, with_ctx_out):
    s5_l, pool_l, q_l, k_l, v_l, z_l, xbc_l, dt_l, s5r_l = p_l
    s5_c, pool_c, q_c, k_c, v_c, z_c, xbc_c, dt_c, s5r_c = p_c
    ya_l, ya_c = s5_mixer_pallas(s5_l, s5_c, s5r_l, s5r_c, prm["s5_a_re"], prm["s5_a_im"], prm["s5_log_dt"],
                                 prm["s5_b_re"], prm["s5_b_im"], prm["s5_c_re"], prm["s5_c_im"], prm["s5_d"],
                                 prm["s5_w_glu"], prm["s5_b_glu"], with_ctx_out)
    yb_l = pool_mix_pallas(pool_l, prm["pool_w"], prm["pool_scale"])
    yb_c = pool_mix_pallas(pool_c, prm["pool_w"], prm["pool_scale"]) if with_ctx_out else None
    B_, L, _ = s5_l.shape
    C_ = s5_c.shape[1]
    flat = lambda t: t.reshape(-1, t.shape[-1])
    table = _na_bias_table(prm["na_rpb"], L // GRID_W)
    yc_l = na_attention(flat(q_l), flat(k_l), flat(v_l), flat(k_c), flat(v_c), table, L, C_, True)
    yc_c = (na_attention(flat(q_c), flat(k_c), flat(v_c), flat(k_c), flat(v_c), table, C_, C_, False)
            if with_ctx_out else None)
    yd_l, yd_c = ssm_mixer_pallas(flat(z_l), flat(xbc_l), flat(dt_l), flat(z_c), flat(xbc_c), flat(dt_c),
                                  prm["ssm_conv_w"], prm["ssm_conv_b"], prm["ssm_dt_bias"], prm["ssm_a_log"],
                                  prm["ssm_d"], prm["ssm_norm_w"], L, C_, with_ctx_out)
    return (ya_l, yb_l, yc_l, yd_l), (ya_c, yb_c, yc_c, yd_c)


def kernel(x, c, ctx, c_ctx, w_ada, b_ada, w_in, w_out, s5_a_re, s5_a_im, s5_log_dt, s5_b_re, s5_b_im, s5_c_re, s5_c_im, s5_d, s5_w_glu, s5_b_glu, pool_w, pool_scale, na_rpb, ssm_conv_w, ssm_conv_b, ssm_dt_bias, ssm_a_log, ssm_d, ssm_norm_w, ln1_g, ln1_b, ln2_g, ln2_b, peer_w_q, peer_sub_keys, peer_u, peer_v):
    B_, L, D = x.shape
    C_ = ctx.shape[1]
    x_lat = x.reshape(B_ * L, D)
    x_ctx = ctx.reshape(B_ * C_, D)
    assert B_ + 1 <= ADA_ROWS
    cond = jnp.concatenate([c, c_ctx[None], jnp.zeros((ADA_ROWS - B_ - 1, D), F32)], axis=0)
    dt_pad = V7X_LANES - IN_SPLITS[-1]
    splits = IN_SPLITS[:-1] + (V7X_LANES,)
    in_dtypes = (F32, F32, BF16, BF16, BF16, F32, F32, F32)
    TM = ROW_TILE
    for l in range(DEPTH):
        last = l == DEPTH - 1
        prm = dict(s5_a_re=s5_a_re[l], s5_a_im=s5_a_im[l], s5_log_dt=s5_log_dt[l], s5_b_re=s5_b_re[l],
                   s5_b_im=s5_b_im[l], s5_c_re=s5_c_re[l], s5_c_im=s5_c_im[l], s5_d=s5_d[l],
                   s5_w_glu=s5_w_glu[l], s5_b_glu=s5_b_glu[l], pool_w=pool_w[l], pool_scale=pool_scale[l],
                   na_rpb=na_rpb[l], ssm_conv_w=ssm_conv_w[l], ssm_conv_b=ssm_conv_b[l],
                   ssm_dt_bias=ssm_dt_bias[l], ssm_a_log=ssm_a_log[l], ssm_d=ssm_d[l],
                   ssm_norm_w=ssm_norm_w[l])
        mod = ada_modulation(cond, w_ada[l], b_ada[l])
        m_lat = mod[:B_].reshape(B_, 6, 1, D)
        m_ctx = mod[B_:B_ + 1].reshape(1, 6, 1, D)
        w_in_b = jnp.pad(w_in[l], ((0, 0), (0, dt_pad))).astype(BF16)
        w_out_b = w_out[l].astype(BF16)

        p_l = modulated_matmul(x_lat, m_lat[:, 0], m_lat[:, 1], w_in_b, splits, in_dtypes, L, L, TM)
        p_c = modulated_matmul(x_ctx, m_ctx[:, 0], m_ctx[:, 1], w_in_b, splits, in_dtypes, B_ * C_, C_, TM)
        p_l = [a.reshape(B_, L, -1) for a in p_l]
        p_c = [a.reshape(B_, C_, -1) for a in p_c]
        y_l, y_c = _mixers(p_l, p_c, prm, not last)

        x_lat = proj_residual_ln([a.reshape(B_ * L, -1) for a in y_l], x_lat, m_lat[:, 2], w_out_b,
                                 ln1_g[l], ln1_b[l], L, TM)
        wq_b = peer_w_q[l].astype(BF16)
        keys_b = peer_sub_keys[l].reshape(2 * PEER_HEADS, PEER_NK, PEER_HALF).astype(BF16)
        u_b = peer_u[l].astype(BF16)
        vt_b = peer_v_tiles(peer_v[l])
        x_lat = peer_sublayer(x_lat, m_lat[:, 3], m_lat[:, 4], m_lat[:, 5], wq_b, keys_b, u_b, vt_b,
                              ln2_g[l], ln2_b[l], L)
        if not last:
            x_ctx = proj_residual_ln([a.reshape(B_ * C_, -1) for a in y_c], x_ctx, m_ctx[:, 2], w_out_b,
                                     ln1_g[l], ln1_b[l], B_ * C_, TM)
            x_ctx = peer_sublayer(x_ctx, m_ctx[:, 3], m_ctx[:, 4], m_ctx[:, 5], wq_b, keys_b, u_b, vt_b,
                                  ln2_g[l], ln2_b[l], B_ * C_)
    return x_lat.reshape(B_, L, D)
```

```python
import functools

import jax
import jax.numpy as jnp
import numpy as np
from jax import lax
from jax.experimental import pallas as pl
from jax.experimental.pallas import tpu as pltpu

D_MODEL = 2048
BATCH = 4
SEQ = 4096
DEPTH = 2

GRID_W = 64
CTX_LEN = 256
N_MIXERS = 4
D_MIX = D_MODEL
W_GRP = D_MIX // N_MIXERS

S5_P = 16
S5_G = W_GRP // S5_P
S5_N = 64

POOL_WINDOWS = (2, 4, 8, 16)
POOL_C = W_GRP // len(POOL_WINDOWS)

NA_HD = 64
NA_H = W_GRP // NA_HD
NA_WIN_R = 8
NA_WIN_C = 16

SSM_HD = 64
SSM_H = W_GRP // SSM_HD
SSM_G = 2
SSM_N = 128
SSM_CONV = 4
SSM_CHUNK = 128
SSM_CONV_CH = W_GRP + 2 * SSM_G * SSM_N

PEER_HEADS = 8
PEER_NK = 128
PEER_NE = PEER_NK * PEER_NK
PEER_QDIM = 256
PEER_TOPK = 16
PEER_BLOCK = 128

IN_SPLITS = (W_GRP, W_GRP, W_GRP, W_GRP, W_GRP, W_GRP, SSM_CONV_CH, 2 * SSM_H)
D_IN = sum(IN_SPLITS)

DEEPNORM_ALPHA = (2 * DEPTH) ** 0.25
DEEPNORM_BETA = (8 * DEPTH) ** -0.25
LN_EPS = 1e-5
F32 = jnp.float32
BF16 = jnp.bfloat16

V7X_LANES = 128
V7X_VMEM_BYTES = 64 * 1024 * 1024
V7X_VMEM_LIMIT_BYTES = V7X_VMEM_BYTES * 7 // 8

ROW_TILE = 256
S5_STEPS = 128
S5_GLU_TILE = 256
PEER_ROUTE_TILE = 512
PEER_DENSE_TILE = 512


ADA_ROWS = 8
ADA_TN = 1536


def _ada_body(c_ref, w_ref, b_ref, o_ref):
    act = jax.nn.silu(c_ref[...]).astype(BF16)
    o_ref[...] = jnp.dot(act, w_ref[...].astype(BF16), preferred_element_type=F32) + b_ref[...]


def ada_modulation(c8, w, b):
    R, D = c8.shape
    N = w.shape[1]
    assert N % ADA_TN == 0
    return pl.pallas_call(
        _ada_body,
        out_shape=jax.ShapeDtypeStruct((R, N), F32),
        grid=(N // ADA_TN,),
        in_specs=[pl.BlockSpec((R, D), lambda j: (0, 0)), pl.BlockSpec((D, ADA_TN), lambda j: (0, j)),
                  pl.BlockSpec((1, ADA_TN), lambda j: (0, j))],
        out_specs=pl.BlockSpec((R, ADA_TN), lambda j: (0, j)),
        compiler_params=pltpu.CompilerParams(dimension_semantics=("arbitrary",),
                                             vmem_limit_bytes=V7X_VMEM_LIMIT_BYTES),
        name="ada_modulation",
    )(c8, w, b.reshape(1, N))
def _modmm_body(x_ref, shift_ref, scale_ref, w_ref, *out_refs, col_splits):
    xm = (x_ref[...] * (1.0 + scale_ref[0]) + shift_ref[0]).astype(BF16)
    o = 0
    for j, (ref, n) in enumerate(zip(out_refs, col_splits)):
        val = jnp.dot(xm, w_ref[:, o:o + n], preferred_element_type=F32)
        ref[...] = val.astype(ref.dtype)
        if j == 0:
            tm = val.shape[0]
            r = lax.broadcasted_iota(jnp.int32, (tm, tm), 0)
            c = lax.broadcasted_iota(jnp.int32, (tm, tm), 1)
            flip = jnp.where(r + c == tm - 1, 1.0, 0.0).astype(BF16)
            out_refs[-1][...] = jnp.dot(flip, val.astype(BF16), preferred_element_type=F32).astype(BF16)
        o += n


def modulated_matmul(x, shift, scale, w_bf16, col_splits, out_dtypes, rows_per_mod, seq_len, tm):
    T, K = x.shape
    assert T % tm == 0 and rows_per_mod % tm == 0 and seq_len % tm == 0
    tiles_per_mod = rows_per_mod // tm
    tps = seq_len // tm
    n_tot = sum(col_splits)
    assert w_bf16.shape == (K, n_tot)
    mod_spec = pl.BlockSpec((1, 1, K), lambda i: (i // tiles_per_mod, 0, 0))
    shapes = [jax.ShapeDtypeStruct((T, n), dt) for n, dt in zip(col_splits, out_dtypes, strict=True)]
    specs = [pl.BlockSpec((tm, n), lambda i: (i, 0)) for n in col_splits]
    shapes.append(jax.ShapeDtypeStruct((T, col_splits[0]), BF16))
    specs.append(pl.BlockSpec((tm, col_splits[0]), lambda i: ((i // tps) * tps + tps - 1 - i % tps, 0)))
    return pl.pallas_call(
        functools.partial(_modmm_body, col_splits=tuple(col_splits)),
        out_shape=shapes,
        grid=(T // tm,),
        in_specs=[pl.BlockSpec((tm, K), lambda i: (i, 0)), mod_spec, mod_spec,
                  pl.BlockSpec((K, n_tot), lambda i: (0, 0))],
        out_specs=specs,
        compiler_params=pltpu.CompilerParams(dimension_semantics=("arbitrary",),
                                             vmem_limit_bytes=V7X_VMEM_LIMIT_BYTES),
        name="modulated_matmul",
    )(x, shift, scale, w_bf16)


def _proj_ln_body(*refs, n_parts, alpha):
    part_refs = refs[:n_parts]
    x_ref, gate_ref, w_ref, g_ref, b_ref, o_ref = refs[n_parts:]
    acc = None
    o = 0
    for pr in part_refs:
        n = pr.shape[-1]
        d = jnp.dot(pr[...].astype(BF16), w_ref[o:o + n, :], preferred_element_type=F32)
        acc = d if acc is None else acc + d
        o += n
    h = alpha * x_ref[...] + gate_ref[0] * acc
    mu = jnp.mean(h, -1, keepdims=True)
    hc = h - mu
    var = jnp.mean(hc * hc, -1, keepdims=True)
    o_ref[...] = hc * lax.rsqrt(var + LN_EPS) * g_ref[...] + b_ref[...]


def proj_residual_ln(parts, x, gate, w_bf16, g, b, rows_per_mod, tm):
    T, D = x.shape
    assert T % tm == 0 and rows_per_mod % tm == 0
    tiles_per_mod = rows_per_mod // tm
    k_tot = sum(p.shape[-1] for p in parts)
    assert w_bf16.shape == (k_tot, D)
    row = lambda n: pl.BlockSpec((tm, n), lambda i: (i, 0))
    vec = pl.BlockSpec((1, D), lambda i: (0, 0))
    return pl.pallas_call(
        functools.partial(_proj_ln_body, n_parts=len(parts), alpha=DEEPNORM_ALPHA),
        out_shape=jax.ShapeDtypeStruct((T, D), F32),
        grid=(T // tm,),
        in_specs=[row(p.shape[-1]) for p in parts] + [
            row(D), pl.BlockSpec((1, 1, D), lambda i: (i // tiles_per_mod, 0, 0)),
            pl.BlockSpec((k_tot, D), lambda i: (0, 0)), vec, vec],
        out_specs=row(D),
        compiler_params=pltpu.CompilerParams(dimension_semantics=("arbitrary",),
                                             vmem_limit_bytes=V7X_VMEM_LIMIT_BYTES),
        name="proj_residual_ln",
    )(*parts, x, gate, w_bf16, g.reshape(1, D), b.reshape(1, D))


S5_CHAINS = 2 * BATCH
S5_STATE = S5_G * S5_N
S5_SCAN_COLS = 512
S5_SB_IN = S5_SCAN_COLS // S5_N * S5_P


def _s5_scan_body(u_ref, wb_ref, wc_ref, are_ref, aim_ref, y_ref, bu_ref, h_ref, *, steps):
    rows = steps * S5_CHAINS

    @pl.when(pl.program_id(0) == 0)
    def _():
        h_ref[...] = jnp.zeros_like(h_ref)

    u = u_ref[...]
    chain = lax.broadcasted_iota(jnp.int32, u.shape, 0) % S5_CHAINS
    fwd = chain < BATCH
    zero = jnp.zeros_like(u)
    uf = jnp.where(fwd, u, zero)
    ub = jnp.where(fwd, zero, u)
    is_fwd_y = lax.broadcasted_iota(jnp.int32, (rows, S5_SB_IN), 0) % S5_CHAINS < BATCH

    for cb in range(S5_STATE // S5_SCAN_COLS):
        cin = slice(cb * S5_SB_IN, (cb + 1) * S5_SB_IN)
        re = pl.ds(cb * S5_SCAN_COLS, S5_SCAN_COLS)
        im = pl.ds(S5_STATE + cb * S5_SCAN_COLS, S5_SCAN_COLS)
        bu = jnp.dot(jnp.concatenate([uf[:, cin], ub[:, cin]], axis=1), wb_ref[cb], preferred_element_type=F32)
        bu_ref[:, re] = bu[:, :S5_SCAN_COLS]
        bu_ref[:, im] = bu[:, S5_SCAN_COLS:]
        a_re = are_ref[:, re]
        a_im = aim_ref[:, re]

        def step(s, carry):
            h_re, h_im = carry
            r = pl.ds(pl.multiple_of(s * S5_CHAINS, S5_CHAINS), S5_CHAINS)
            n_re = a_re * h_re - a_im * h_im + bu_ref[r, re]
            n_im = a_re * h_im + a_im * h_re + bu_ref[r, im]
            bu_ref[r, re] = n_re
            bu_ref[r, im] = n_im
            return n_re, n_im

        h_re, h_im = lax.fori_loop(0, steps, step, (h_ref[:, re], h_ref[:, im]), unroll=8)
        h_ref[:, re] = h_re
        h_ref[:, im] = h_im
        hb = jnp.concatenate([bu_ref[:, re], bu_ref[:, im]], axis=1).astype(BF16)
        y2 = jnp.dot(hb, wc_ref[cb], preferred_element_type=F32)
        y_ref[cb] = jnp.where(is_fwd_y, y2[:, :S5_SB_IN], y2[:, S5_SB_IN:])


def s5_scan(u8, wb, wc, a_re8, a_im8, steps):
    n_rows = u8.shape[0]
    rows = steps * S5_CHAINS
    assert n_rows % rows == 0
    full = lambda a: pl.BlockSpec(a.shape, lambda i: (0,) * a.ndim)
    return pl.pallas_call(
        functools.partial(_s5_scan_body, steps=steps),
        out_shape=jax.ShapeDtypeStruct((W_GRP // S5_SB_IN, n_rows, S5_SB_IN), F32),
        grid=(n_rows // rows,),
        in_specs=[pl.BlockSpec((rows, W_GRP), lambda i: (i, 0)), full(wb), full(wc), full(a_re8), full(a_im8)],
        out_specs=pl.BlockSpec((W_GRP // S5_SB_IN, rows, S5_SB_IN), lambda i: (0, i, 0)),
        scratch_shapes=[pltpu.VMEM((rows, 2 * S5_STATE), F32), pltpu.VMEM((S5_CHAINS, 2 * S5_STATE), F32)],
        compiler_params=pltpu.CompilerParams(dimension_semantics=("arbitrary",),
                                             vmem_limit_bytes=V7X_VMEM_LIMIT_BYTES),
        name="s5_scan",
    )(u8, wb, wc, a_re8, a_im8)


def _s5_glu_body(u_ref, yf_ref, yb_ref, d_ref, w_ref, b_ref, o_ref):
    tm = u_ref.shape[1]
    r = lax.broadcasted_iota(jnp.int32, (tm, tm), 0)
    c = lax.broadcasted_iota(jnp.int32, (tm, tm), 1)
    flip = jnp.where(r + c == tm - 1, 1.0, 0.0).astype(BF16)
    for b in range(BATCH):
        n_cb = yf_ref.shape[0]
        yf = jnp.concatenate([yf_ref[cb, pl.ds(b, tm, stride=S5_CHAINS), :] for cb in range(n_cb)], axis=1)
        yb = jnp.concatenate([yb_ref[cb, pl.ds(BATCH + b, tm, stride=S5_CHAINS), :] for cb in range(n_cb)],
                             axis=1)
        hi = yb.astype(BF16)
        lo = (yb - hi.astype(F32)).astype(BF16)
        yb = jnp.dot(flip, hi, preferred_element_type=F32) + jnp.dot(flip, lo, preferred_element_type=F32)
        y = d_ref[...] * u_ref[b] + yf + yb
        g = jax.nn.gelu(y)
        z = jnp.dot(g.astype(BF16), w_ref[...], preferred_element_type=F32) + b_ref[...]
        o_ref[b] = g * jax.nn.sigmoid(z)


def s5_glu_pallas(u, y8, first_step, d, w_bf16, b, tm):
    B_, n, W = u.shape
    assert B_ == BATCH and n % tm == 0 and first_step % tm == 0
    nt = n // tm
    off = first_step // tm
    row = pl.BlockSpec((B_, tm, W), lambda i: (0, i, 0))
    vec = pl.BlockSpec((1, W), lambda i: (0, 0))
    return pl.pallas_call(
        _s5_glu_body,
        out_shape=jax.ShapeDtypeStruct((B_, n, W), F32),
        grid=(nt,),
        in_specs=[row, pl.BlockSpec((y8.shape[0], tm * S5_CHAINS, y8.shape[2]), lambda i: (0, off + i, 0)),
                  pl.BlockSpec((y8.shape[0], tm * S5_CHAINS, y8.shape[2]), lambda i: (0, off + nt - 1 - i, 0)),
                  vec, pl.BlockSpec((W, W), lambda i: (0, 0)), vec],
        out_specs=row,
        compiler_params=pltpu.CompilerParams(dimension_semantics=("arbitrary",),
                                             vmem_limit_bytes=V7X_VMEM_LIMIT_BYTES),
        name="s5_glu",
    )(u, y8, y8, d.reshape(1, W), w_bf16, b.reshape(1, W))


def _s5_weights(a_re, a_im, log_dt, b_re, b_im, c_re, c_im):
    eye = jnp.eye(S5_G, dtype=F32)
    wb, a8 = [], []
    wc = []
    for di in range(2):
        ab_re, ab_im, bb_re, bb_im = s5_discretize(a_re[di], a_im[di], log_dt[di], b_re[di], b_im[di])
        blk = lambda m: jnp.einsum('gnp,gh->gphn', m, eye).reshape(W_GRP, S5_STATE)
        wb.append(jnp.concatenate([blk(bb_re), blk(bb_im)], axis=1))
        a8.append((jnp.broadcast_to(ab_re.reshape(1, S5_STATE), (BATCH, S5_STATE)),
                   jnp.broadcast_to(ab_im.reshape(1, S5_STATE), (BATCH, S5_STATE))))
        cblk = lambda m: jnp.einsum('gpn,gh->gnhp', m.astype(F32), eye).reshape(S5_STATE, W_GRP)
        wc.append(jnp.concatenate([cblk(c_re[di]), -cblk(c_im[di])], axis=0))
    wb_sb, wc_sb = [], []
    for sb in range(S5_STATE // S5_SCAN_COLS):
        cin = slice(sb * S5_SB_IN, (sb + 1) * S5_SB_IN)
        re = slice(sb * S5_SCAN_COLS, (sb + 1) * S5_SCAN_COLS)
        im = slice(S5_STATE + sb * S5_SCAN_COLS, S5_STATE + (sb + 1) * S5_SCAN_COLS)
        wb_sb.append(jnp.concatenate([jnp.concatenate([w[cin, re], w[cin, im]], axis=1) for w in wb], axis=0))
        wc_sb.append(jnp.concatenate([jnp.concatenate([w[re, cin], w[im, cin]], axis=0) for w in wc], axis=1))
    wb_sb = jnp.stack(wb_sb).astype(BF16)
    wc_sb = jnp.stack(wc_sb).astype(BF16)
    a_re8 = jnp.concatenate([a8[0][0], a8[1][0]], axis=0)
    a_im8 = jnp.concatenate([a8[0][1], a8[1][1]], axis=0)
    return wb_sb, wc_sb, a_re8, a_im8


def s5_mixer_pallas(u_lat, u_ctx, rev_lat, rev_ctx, a_re, a_im, log_dt, b_re, b_im, c_re, c_im, d, w_glu,
                    b_glu, with_ctx_out, steps=S5_STEPS, tm=S5_GLU_TILE):
    B_, L, W = u_lat.shape
    C_ = u_ctx.shape[1]
    assert B_ == BATCH and W == W_GRP
    wb, wc, a_re8, a_im8 = _s5_weights(a_re, a_im, log_dt, b_re, b_im, c_re, c_im)
    seq_f = jnp.concatenate([u_ctx.astype(BF16), u_lat.astype(BF16)], axis=1)
    seq_b = jnp.concatenate([rev_ctx, rev_lat], axis=1)
    u8 = jnp.concatenate([seq_f, seq_b], axis=0).transpose(1, 0, 2).reshape((C_ + L) * S5_CHAINS, W)
    y8 = s5_scan(u8, wb, wc, a_re8, a_im8, steps)
    w_glu_b = w_glu.astype(BF16)
    tm = min(tm, C_)
    out_lat = s5_glu_pallas(u_lat, y8, C_, d, w_glu_b, b_glu, tm).reshape(B_ * L, W)
    out_ctx = s5_glu_pallas(u_ctx, y8, 0, d, w_glu_b, b_glu, tm).reshape(B_ * C_, W) if with_ctx_out else None
    return out_lat, out_ctx


PEER_HALF = PEER_QDIM // 2


def _argmax_rows(v, r):
    while v.shape[0] > 1:
        half = v.shape[0] // 2
        take_hi = v[half:] > v[:half]
        r = jnp.where(take_hi, r[half:], r[:half])
        v = jnp.maximum(v[:half], v[half:])
    return v, r


def _topk_rows(xs, k, with_rank):
    n, lanes = xs[0].shape
    row = lax.broadcasted_iota(jnp.int32, (n, lanes), 0).astype(F32)
    krow = lax.broadcasted_iota(jnp.int32, (k, lanes), 0)

    def body(it, carry):
        out = []
        for x, vals, aux in carry:
            m, first = _argmax_rows(x, row)
            hit = row == first
            sel = krow == it
            aux = jnp.where(hit, it.astype(F32), aux) if with_rank else jnp.where(sel, first, aux)
            out.append((jnp.where(hit, -jnp.inf, x), jnp.where(sel, m, vals), aux))
        return tuple(out)

    zeros = jnp.zeros((k, lanes), F32)
    aux0 = jnp.full((n, lanes), float(k), F32) if with_rank else zeros
    res = lax.fori_loop(0, k, body, tuple((x, zeros, aux0) for x in xs), unroll=1 if with_rank else 2)
    return [(vals, aux) for _, vals, aux in res]


PEER_GRID_COLS = tuple(PEER_TOPK // (i + 1) for i in range(PEER_TOPK))
PEER_GRID_ROWS = 64


def _peer_route_body(x_ref, shift_ref, scale_ref, wq_ref, keys_ref, h_ref, n1_ref, c1_ref, r2_ref, e2_ref,
                     q_ref, *, lane_tiles):
    hm = (x_ref[...] * (1.0 + scale_ref[0]) + shift_ref[0]).astype(BF16)
    h_ref[...] = hm
    q_ref[...] = jnp.dot(hm, wq_ref[...], preferred_element_type=F32).astype(BF16)
    K = PEER_TOPK
    key = lax.broadcasted_iota(jnp.int32, (PEER_NK, V7X_LANES), 0).astype(F32)
    for hd in range(PEER_HEADS):
        for lt in range(lane_tiles):
            tok = pl.ds(lt * V7X_LANES, V7X_LANES)
            sc = []
            for side in range(2):
                col = (2 * hd + side) * PEER_HALF
                qs = q_ref[tok, col:col + PEER_HALF]
                sc.append(lax.dot_general(keys_ref[2 * hd + side], qs, (((1,), (1,)), ((), ())),
                                          preferred_element_type=F32))
            (v1, idx1), (v2, idx2) = _topk_rows(sc, K, with_rank=False)
            cells = [v1[i:i + 1] + v2[:PEER_GRID_COLS[i]] for i in range(K)]
            cells.append(jnp.full((PEER_GRID_ROWS - sum(PEER_GRID_COLS), V7X_LANES), -jnp.inf, F32))
            (vc, rc), = _topk_rows([jnp.concatenate(cells, axis=0)], K, with_rank=True)
            z = jnp.sum(jnp.exp(vc - vc[0:1]), axis=0, keepdims=True)
            chosen = jnp.where(rc < float(K), 1.0, 0.0)
            n1 = jnp.zeros_like(key)
            r2 = jnp.full(key.shape, float(K), F32)
            off = 0
            for i in range(K):
                n_i = jnp.sum(chosen[off:off + PEER_GRID_COLS[i]], axis=0, keepdims=True)
                n1 = jnp.where(key == idx1[i:i + 1], n_i, n1)
                r2 = jnp.where(key == idx2[i:i + 1], float(i), r2)
                off += PEER_GRID_COLS[i]
            n1_ref[hd, :, tok] = n1
            c1_ref[hd, :, tok] = jnp.exp(sc[0] - v1[0:1]) / z
            r2_ref[hd, :, tok] = r2.astype(BF16)
            e2_ref[hd, :, tok] = jnp.exp(sc[1] - v2[0:1]).astype(BF16)


def peer_route(x, shift, scale, wq_bf16, keys_bf16, rows_per_mod, tm):
    T, D = x.shape
    assert T % tm == 0 and rows_per_mod % tm == 0 and tm % V7X_LANES == 0
    tiles_per_mod = rows_per_mod // tm
    mod_spec = pl.BlockSpec((1, 1, D), lambda i: (i // tiles_per_mod, 0, 0))
    tab = lambda dt: jax.ShapeDtypeStruct((PEER_HEADS, PEER_NK, T), dt)
    tab_spec = pl.BlockSpec((PEER_HEADS, PEER_NK, tm), lambda i: (0, 0, i))
    return pl.pallas_call(
        functools.partial(_peer_route_body, lane_tiles=tm // V7X_LANES),
        out_shape=[jax.ShapeDtypeStruct((T, D), BF16), tab(F32), tab(F32), tab(BF16), tab(BF16)],
        grid=(T // tm,),
        in_specs=[pl.BlockSpec((tm, D), lambda i: (i, 0)), mod_spec, mod_spec,
                  pl.BlockSpec(wq_bf16.shape, lambda i: (0, 0)),
                  pl.BlockSpec(keys_bf16.shape, lambda i: (0, 0, 0))],
        out_specs=[pl.BlockSpec((tm, D), lambda i: (i, 0)), tab_spec, tab_spec, tab_spec, tab_spec],
        scratch_shapes=[pltpu.VMEM((tm, PEER_HEADS * PEER_QDIM), BF16)],
        compiler_params=pltpu.CompilerParams(dimension_semantics=("arbitrary",),
                                             vmem_limit_bytes=V7X_VMEM_LIMIT_BYTES),
        name="peer_route",
    )(x, shift, scale, wq_bf16, keys_bf16)


def _peer_dense_body(h_ref, u_ref, vt_ref, n1_ref, c1_ref, r2_ref, e2_ref, x_ref, gate_ref, g_ref, b_ref,
                     o_ref, acc_ref, a_ref, *, n_slab):
    j = pl.program_id(1)

    @pl.when(j == 0)
    def _():
        acc_ref[...] = jnp.zeros_like(acc_ref)

    rows = PEER_SCORE_SLABS * PEER_NK
    for p in range(n_slab // PEER_SCORE_SLABS):
        s = lax.dot_general(u_ref[p * rows:(p + 1) * rows, :], h_ref[...], (((1,), (1,)), ((), ())),
                            preferred_element_type=F32)
        for kk in range(PEER_SCORE_SLABS):
            k = PEER_SCORE_SLABS * p + kk
            for lt in range(h_ref.shape[0] // V7X_LANES):
                lanes = slice(lt * V7X_LANES, (lt + 1) * V7X_LANES)
                g = None
                for hd in range(PEER_HEADS):
                    n1row = n1_ref[hd, k:k + 1, lanes].astype(BF16)
                    c1row = c1_ref[hd, k:k + 1, lanes].astype(BF16)
                    sel = lax.clamp(jnp.zeros((), BF16), n1row - r2_ref[hd, :, lanes], jnp.ones((), BF16))
                    gh = sel * e2_ref[hd, :, lanes] * c1row
                    g = gh if g is None else g + gh
                sk = s[kk * PEER_NK:(kk + 1) * PEER_NK, lanes]
                a_ref[k * PEER_NK:(k + 1) * PEER_NK, lanes] = jax.nn.gelu(sk).astype(BF16) * g
    acc_ref[...] += jnp.dot(vt_ref[0], a_ref[...], preferred_element_type=F32)

    @pl.when(j == pl.num_programs(1) - 1)
    def _():
        hres = DEEPNORM_ALPHA * x_ref[...] + gate_ref[0] * acc_ref[...].T
        mu = jnp.mean(hres, -1, keepdims=True)
        hc = hres - mu
        var = jnp.mean(hc * hc, -1, keepdims=True)
        o_ref[...] = hc * lax.rsqrt(var + LN_EPS) * g_ref[...] + b_ref[...]


def peer_dense(h_bf16, u_bf16, vt_tiles, n1, c1, r2, e2, x, gate, ln_g, ln_b, rows_per_mod, tm, n_slab):
    T, D = h_bf16.shape
    NE = u_bf16.shape[0]
    e_tile = n_slab * PEER_NK
    n_e = NE // e_tile
    assert T % tm == 0 and NE % e_tile == 0 and vt_tiles.shape == (n_e, D, e_tile)
    slab_spec = pl.BlockSpec((PEER_HEADS, n_slab, tm), lambda i, j: (0, j, i))
    tok_spec = pl.BlockSpec((PEER_HEADS, PEER_NK, tm), lambda i, j: (0, 0, i))
    return pl.pallas_call(
        functools.partial(_peer_dense_body, n_slab=n_slab),
        out_shape=jax.ShapeDtypeStruct((T, D), F32),
        grid=(T // tm, n_e),
        in_specs=[pl.BlockSpec((tm, D), lambda i, j: (i, 0)),
                  pl.BlockSpec((e_tile, D), lambda i, j: (j, 0)),
                  pl.BlockSpec((1, D, e_tile), lambda i, j: (j, 0, 0)),
                  slab_spec, slab_spec, tok_spec, tok_spec,
                  pl.BlockSpec((tm, D), lambda i, j: (i, 0)),
                  pl.BlockSpec((1, 1, D), lambda i, j: (i // (rows_per_mod // tm), 0, 0)),
                  pl.BlockSpec((1, D), lambda i, j: (0, 0)), pl.BlockSpec((1, D), lambda i, j: (0, 0))],
        out_specs=pl.BlockSpec((tm, D), lambda i, j: (i, 0)),
        scratch_shapes=[pltpu.VMEM((D, tm), F32), pltpu.VMEM((e_tile, tm), BF16)],
        compiler_params=pltpu.CompilerParams(dimension_semantics=("arbitrary", "arbitrary"),
                                             vmem_limit_bytes=V7X_VMEM_LIMIT_BYTES),
        name="peer_dense",
    )(h_bf16, u_bf16, vt_tiles, n1, c1, r2, e2, x, gate, ln_g.reshape(1, D), ln_b.reshape(1, D))


PEER_SLABS = 8
PEER_SCORE_SLABS = 4


def peer_v_tiles(v_tab):
    e_tile = PEER_SLABS * PEER_NK
    return v_tab.astype(BF16).reshape(v_tab.shape[0] // e_tile, e_tile, v_tab.shape[1]).transpose(0, 2, 1)


def peer_sublayer(x, shift, scale, gate, wq_bf16, keys_bf16, u_bf16, vt_tiles, ln_g, ln_b, rows_per_mod,
                  tm_route=PEER_ROUTE_TILE, tm_dense=PEER_DENSE_TILE):
    h, n1, c1, r2, e2 = peer_route(x, shift, scale, wq_bf16, keys_bf16, rows_per_mod, tm_route)
    return peer_dense(h, u_bf16, vt_tiles, n1, c1, r2, e2, x, gate, ln_g, ln_b, rows_per_mod, tm_dense,
                      PEER_SLABS)


POOL_HALO = 8
POOL_ROWS = 256


def _pool_body(u_ref, w_ref, scale_ref, o_ref, pad_ref, *, seq_len):
    L = seq_len
    chunk = min(POOL_ROWS, L)
    zeros = jnp.zeros((POOL_HALO, W_GRP), F32)
    pad_ref[0:POOL_HALO, :] = zeros
    pad_ref[POOL_HALO + L:2 * POOL_HALO + L, :] = zeros
    pad_ref[POOL_HALO:POOL_HALO + L, :] = u_ref[0]
    for r0 in range(0, L, chunk):
        t = r0 + lax.broadcasted_iota(jnp.int32, (chunk, POOL_C), 0)
        for j, w in enumerate(POOL_WINDOWS):
            cols = slice(j * POOL_C, (j + 1) * POOL_C)
            acc = None
            for o in range(-(w // 2), w - w // 2):
                s = pad_ref[POOL_HALO + r0 + o:POOL_HALO + r0 + o + chunk, cols]
                acc = s if acc is None else acc + s
            lo = jnp.maximum(t - w // 2, 0)
            hi = jnp.minimum(t - w // 2 + w - 1, L - 1)
            cnt = (hi - lo + 1).astype(F32)
            pooled = acc / cnt - u_ref[0, r0:r0 + chunk, cols]
            y = jnp.dot(pooled.astype(BF16), w_ref[j], preferred_element_type=F32)
            o_ref[0, r0:r0 + chunk, cols] = y * scale_ref[:, cols]


def pool_mix_pallas(u, pool_w, pool_scale):
    B_, L, W = u.shape
    assert max(POOL_WINDOWS) // 2 <= POOL_HALO and L % min(POOL_ROWS, L) == 0
    blk = pl.BlockSpec((1, L, W), lambda b: (b, 0, 0))
    return pl.pallas_call(
        functools.partial(_pool_body, seq_len=L),
        out_shape=jax.ShapeDtypeStruct((B_, L, W), F32),
        grid=(B_,),
        in_specs=[blk, pl.BlockSpec(pool_w.shape, lambda b: (0, 0, 0)), pl.BlockSpec((1, W), lambda b: (0, 0))],
        out_specs=blk,
        scratch_shapes=[pltpu.VMEM((L + 2 * POOL_HALO, W), F32)],
        compiler_params=pltpu.CompilerParams(dimension_semantics=("arbitrary",),
                                             vmem_limit_bytes=V7X_VMEM_LIMIT_BYTES),
        name="pool_mix",
    )(u, pool_w.astype(BF16), pool_scale.reshape(1, W))


NA_TILE_R = 4
NA_TILE = NA_TILE_R * GRID_W
NA_SCALE = NA_HD ** -0.5


def _na_bias_table(rpb, n_rows):
    n_tiles = n_rows // NA_TILE_R
    KR = min(NA_WIN_R, n_rows)
    a = np.array([0, min(2, n_tiles - 1), n_tiles - 1]).reshape(3, 1, 1, 1, 1, 1)
    d = np.arange(3).reshape(1, 3, 1, 1, 1, 1)
    i = np.arange(NA_TILE_R).reshape(1, 1, NA_TILE_R, 1, 1, 1)
    qc = np.arange(GRID_W).reshape(1, 1, 1, GRID_W, 1, 1)
    j = np.arange(NA_TILE_R).reshape(1, 1, 1, 1, NA_TILE_R, 1)
    kc = np.arange(GRID_W).reshape(1, 1, 1, 1, 1, GRID_W)
    qr = NA_TILE_R * a + i
    kr = NA_TILE_R * (a + d - 1) + j
    rs = np.clip(qr - KR // 2, 0, n_rows - KR)
    c0 = np.clip(qc - NA_WIN_C // 2, 0, GRID_W - NA_WIN_C)
    ok = (kr >= rs) & (kr < rs + KR) & (kr >= 0) & (kr < n_rows) & (kc >= c0) & (kc < c0 + NA_WIN_C)
    row_rel = np.clip(kr - qr + NA_WIN_R - 1, 0, 2 * NA_WIN_R - 2)[0, :, :, 0, :, 0]
    col_rel = (np.clip(kc - qc, -(NA_WIN_C - 1), NA_WIN_C - 1) + NA_WIN_C - 1)[0, 0, 0, :, 0, :]
    onehot = (col_rel[None] == np.arange(2 * NA_WIN_C - 1)[:, None, None]).astype(np.float32)
    bias_rc = jnp.einsum('hrc,cqk->hrqk', rpb.astype(F32), onehot, precision=lax.Precision.HIGHEST)
    tab = jnp.stack([bias_rc[:, int(r)] for r in row_rel.reshape(-1)], axis=1)
    tab = tab.reshape(NA_H, 3, NA_TILE_R, NA_TILE_R, GRID_W, GRID_W).transpose(0, 1, 2, 4, 3, 5)
    tab = jnp.where(ok[:, None], tab[None], -jnp.inf)
    return tab.reshape(3, NA_H, 3, NA_TILE, NA_TILE)


def _na_body(q_ref, k_ref, v_ref, kc_ref, vc_ref, t_ref, o_ref, *, n_tiles, with_grid):
    a = pl.program_id(1)
    nt = (((1,), (1,)), ((), ()))
    for h in range(NA_H):
        hs = slice(h * NA_HD, (h + 1) * NA_HD)
        qh = q_ref[:, hs]
        scores = [lax.dot_general(qh, kc_ref[:, hs], nt, preferred_element_type=F32) * NA_SCALE]
        vals = [vc_ref[:, hs]]
        if with_grid:
            for d in range(3):
                ti = jnp.clip(a + d - 1, 0, n_tiles - 1)
                rows = pl.ds(pl.multiple_of(ti * NA_TILE, NA_TILE), NA_TILE)
                s = lax.dot_general(qh, k_ref[rows, hs], nt, preferred_element_type=F32)
                scores.append(s * NA_SCALE + t_ref[0, h, d])
                vals.append(v_ref[rows, hs])
        m_el = scores[0]
        for s in scores[1:]:
            m_el = jnp.maximum(m_el, s)
        m = m_el.max(axis=-1, keepdims=True)
        p_el = None
        acc = None
        for s, vv in zip(scores, vals):
            p = jnp.exp(s - m)
            o = jnp.dot(p.astype(BF16), vv, preferred_element_type=F32)
            p_el = p if p_el is None else p_el + p
            acc = o if acc is None else acc + o
        o_ref[:, hs] = acc / p_el.sum(axis=-1, keepdims=True)


def na_attention(q, k, v, kc, vc, table, seq_len, ctx_len, with_grid):
    W = q.shape[1]
    n_b = kc.shape[0] // ctx_len
    lq = q.shape[0] // n_b
    assert lq % NA_TILE == 0 and ctx_len == NA_TILE
    q_tiles = lq // NA_TILE
    n_tiles = seq_len // NA_TILE
    pat = lambda b, a: (jnp.where(a == 0, 0, jnp.where(a == n_tiles - 1, 2, 1)), 0, 0, 0, 0)
    qo_spec = pl.BlockSpec((NA_TILE, W), lambda b, a: (b * q_tiles + a, 0))
    return pl.pallas_call(
        functools.partial(_na_body, n_tiles=n_tiles, with_grid=with_grid),
        out_shape=jax.ShapeDtypeStruct(q.shape, F32),
        grid=(n_b, q_tiles),
        in_specs=[qo_spec,
                  pl.BlockSpec((seq_len, W), lambda b, a: (b, 0)), pl.BlockSpec((seq_len, W), lambda b, a: (b, 0)),
                  pl.BlockSpec((ctx_len, W), lambda b, a: (b, 0)), pl.BlockSpec((ctx_len, W), lambda b, a: (b, 0)),
                  pl.BlockSpec((1,) + table.shape[1:], pat)],
        out_specs=qo_spec,
        compiler_params=pltpu.CompilerParams(dimension_semantics=("arbitrary", "arbitrary"),
                                             vmem_limit_bytes=V7X_VMEM_LIMIT_BYTES),
        name="na_attention",
    )(q, k, v, kc, vc, table)


SSM_HALO = 8
SSM_BC = SSM_G * SSM_N


def _ssm_conv_body(prev_ref, cur_ref, next_ref, w_ref, b_ref, o_ref, pad_ref, *, tiles_per_seq):
    pos = pl.program_id(0) % tiles_per_seq
    tm = cur_ref.shape[0]
    pad_ref[0:SSM_HALO, :] = jnp.where(pos == 0, 0.0, prev_ref[...])
    pad_ref[SSM_HALO:SSM_HALO + tm, :] = cur_ref[...]
    pad_ref[SSM_HALO + tm:2 * SSM_HALO + tm, :] = jnp.where(pos == tiles_per_seq - 1, 0.0, next_ref[...])
    lead = (SSM_CONV - 1) // 2
    y = b_ref[...]
    for k in range(SSM_CONV):
        y = y + w_ref[k:k + 1, :] * pad_ref[SSM_HALO - lead + k:SSM_HALO - lead + k + tm, :]
    o_ref[...] = jax.nn.silu(y)


def ssm_conv(xbc, conv_w, conv_b, seq_len, tm):
    T, CH = xbc.shape
    assert seq_len % tm == 0 and tm % SSM_HALO == 0
    hb = tm // SSM_HALO
    n_hb = T // SSM_HALO
    return pl.pallas_call(
        functools.partial(_ssm_conv_body, tiles_per_seq=seq_len // tm),
        out_shape=jax.ShapeDtypeStruct((T, CH), F32),
        grid=(T // tm,),
        in_specs=[pl.BlockSpec((SSM_HALO, CH), lambda i: (jnp.maximum(i * hb - 1, 0), 0)),
                  pl.BlockSpec((tm, CH), lambda i: (i, 0)),
                  pl.BlockSpec((SSM_HALO, CH), lambda i: (jnp.minimum((i + 1) * hb, n_hb - 1), 0)),
                  pl.BlockSpec((SSM_CONV, CH), lambda i: (0, 0)), pl.BlockSpec((1, CH), lambda i: (0, 0))],
        out_specs=pl.BlockSpec((tm, CH), lambda i: (i, 0)),
        scratch_shapes=[pltpu.VMEM((tm + 2 * SSM_HALO, CH), F32)],
        compiler_params=pltpu.CompilerParams(dimension_semantics=("arbitrary",),
                                             vmem_limit_bytes=V7X_VMEM_LIMIT_BYTES),
        name="ssm_conv",
    )(xbc, xbc, xbc, conv_w, conv_b.reshape(1, CH))


def _ssd_chunk_prep(xbc_ref, dt_ref, bias_ref, a_ref, *, di):
    rev = di == 1
    Q = xbc_ref.shape[0]
    dt_all = jax.nn.softplus(dt_ref[...] + bias_ref[...])
    a_cs = dt_all * a_ref[...]
    row_id = lax.broadcasted_iota(jnp.int32, a_cs.shape, 0)
    sh = 1
    while sh < Q:
        if rev:
            a_cs = a_cs + jnp.where(row_id < Q - sh, pltpu.roll(a_cs, Q - sh, 0), 0.0)
        else:
            a_cs = a_cs + jnp.where(row_id >= sh, pltpu.roll(a_cs, sh, 0), 0.0)
        sh *= 2
    a_cs_t = a_cs.T
    a_tot = a_cs[0:1, :] if rev else a_cs[Q - 1:Q, :]
    l_id = lax.broadcasted_iota(jnp.int32, (Q, Q), 0)
    s_id = lax.broadcasted_iota(jnp.int32, (Q, Q), 1)
    causal = (l_id <= s_id) if rev else (l_id >= s_id)
    cb = []
    for g in range(SSM_G):
        bg = xbc_ref[:, W_GRP + g * SSM_N:W_GRP + (g + 1) * SSM_N].astype(BF16)
        cg = xbc_ref[:, W_GRP + SSM_BC + g * SSM_N:W_GRP + SSM_BC + (g + 1) * SSM_N].astype(BF16)
        cb.append((bg, cg, lax.dot_general(cg, bg, (((1,), (1,)), ((), ())), preferred_element_type=F32)))

    src = lax.broadcasted_iota(jnp.int32, (V7X_LANES, W_GRP), 0)
    dst = lax.broadcasted_iota(jnp.int32, (V7X_LANES, W_GRP), 1)
    spread = jnp.where(src == di * SSM_H + dst // SSM_HD, 1.0, 0.0).astype(BF16)

    def per_channel(m):
        hi = m.astype(BF16)
        lo = (m - hi.astype(F32)).astype(BF16)
        return (jnp.dot(hi, spread, preferred_element_type=F32) + jnp.dot(lo, spread, preferred_element_type=F32))

    x_dt = xbc_ref[:, :W_GRP] * per_channel(dt_all)
    x_in = x_dt.astype(BF16)
    x_out = (x_dt * per_channel(jnp.exp(a_tot - a_cs))).astype(BF16)
    e_in = per_channel(jnp.exp(a_cs))
    return a_cs, a_cs_t, a_tot, causal, cb, x_in, x_out, e_in


def _ssd_head(prep, h_scr, y_ref, *, di, h):
    a_cs, a_cs_t, a_tot, causal, cb, x_in, x_out, e_in = prep
    nt = (((1,), (1,)), ((), ()))
    tn = (((0,), (0,)), ((), ()))
    c = di * SSM_H + h
    ch = slice(h * SSM_HD, (h + 1) * SSM_HD)
    bg, cg, cbg = cb[h // (SSM_H // SSM_G)]
    lm = jnp.exp(jnp.where(causal, a_cs[:, c:c + 1] - a_cs_t[c:c + 1, :], -jnp.inf))
    hp = h_scr[di, h]
    yd = jnp.dot((cbg * lm).astype(BF16), x_in[:, ch], preferred_element_type=F32)
    yo = lax.dot_general(cg, hp.astype(BF16), nt, preferred_element_type=F32) * e_in[:, ch]
    y_ref[:, ch] = yd + yo
    h_scr[di, h] = (jnp.exp(a_tot[:, c:c + 1]) * hp
                    + lax.dot_general(x_out[:, ch], bg, tn, preferred_element_type=F32))


def _ssd_body(xf_ref, dtf_ref, xb_ref, dtb_ref, bias_ref, a_ref, h0_ref, yf_ref, yb_ref, hfin_ref, h_scr):
    ci = pl.program_id(1)

    @pl.when(ci == 0)
    def _():
        h_scr[...] = h0_ref[:, 0]

    prep_f = _ssd_chunk_prep(xf_ref, dtf_ref, bias_ref, a_ref, di=0)
    prep_b = _ssd_chunk_prep(xb_ref, dtb_ref, bias_ref, a_ref, di=1)
    for h in range(SSM_H):
        _ssd_head(prep_f, h_scr, yf_ref, di=0, h=h)
        _ssd_head(prep_b, h_scr, yb_ref, di=1, h=h)

    @pl.when(ci == pl.num_programs(1) - 1)
    def _():
        hfin_ref[:, 0] = h_scr[...]


def ssd_scan(xbc_act, dt_raw, bias128, a128, h0, seq_len):
    T = xbc_act.shape[0]
    n_b = T // seq_len
    Q = min(SSM_CHUNK, seq_len)
    nc = seq_len // Q
    fwd = lambda b, i: (b * nc + i, 0)
    bwd = lambda b, i: (b * nc + nc - 1 - i, 0)
    vec = pl.BlockSpec((1, V7X_LANES), lambda b, i: (0, 0))
    st = pl.BlockSpec((2, 1, SSM_H, SSM_HD, SSM_N), lambda b, i: (0, b, 0, 0, 0))
    y_shape = jax.ShapeDtypeStruct((T, W_GRP), F32)
    return pl.pallas_call(
        _ssd_body,
        out_shape=[y_shape, y_shape, jax.ShapeDtypeStruct(h0.shape, F32)],
        grid=(n_b, nc),
        in_specs=[pl.BlockSpec((Q, SSM_CONV_CH), fwd), pl.BlockSpec((Q, V7X_LANES), fwd),
                  pl.BlockSpec((Q, SSM_CONV_CH), bwd), pl.BlockSpec((Q, V7X_LANES), bwd), vec, vec, st],
        out_specs=[pl.BlockSpec((Q, W_GRP), fwd), pl.BlockSpec((Q, W_GRP), bwd), st],
        scratch_shapes=[pltpu.VMEM((2, SSM_H, SSM_HD, SSM_N), F32)],
        compiler_params=pltpu.CompilerParams(dimension_semantics=("arbitrary", "arbitrary"),
                                             vmem_limit_bytes=V7X_VMEM_LIMIT_BYTES),
        name="ssd_scan",
    )(xbc_act, dt_raw, xbc_act, dt_raw, bias128, a128, h0)


def _ssm_out_body(xbc_ref, yf_ref, yb_ref, z_ref, d_ref, nw_ref, o_ref):
    y = d_ref[...] * xbc_ref[:, :W_GRP] + yf_ref[...] + yb_ref[...]
    g = y * jax.nn.silu(z_ref[...])
    gw = W_GRP // SSM_G
    for k in range(SSM_G):
        gk = g[:, k * gw:(k + 1) * gw]
        r = lax.rsqrt(jnp.mean(gk * gk, -1, keepdims=True) + LN_EPS)
        o_ref[:, k * gw:(k + 1) * gw] = gk * r * nw_ref[:, k * gw:(k + 1) * gw]


def ssm_out(xbc_act, yf, yb, z, d512, norm_w, tm):
    T = z.shape[0]
    assert T % tm == 0
    row = pl.BlockSpec((tm, W_GRP), lambda i: (i, 0))
    vec = pl.BlockSpec((1, W_GRP), lambda i: (0, 0))
    return pl.pallas_call(
        _ssm_out_body,
        out_shape=jax.ShapeDtypeStruct((T, W_GRP), F32),
        grid=(T // tm,),
        in_specs=[pl.BlockSpec((tm, SSM_CONV_CH), lambda i: (i, 0)), row, row, row, vec, vec],
        out_specs=row,
        compiler_params=pltpu.CompilerParams(dimension_semantics=("arbitrary",),
                                             vmem_limit_bytes=V7X_VMEM_LIMIT_BYTES),
        name="ssm_out",
    )(xbc_act, yf, yb, z, d512, norm_w.reshape(1, W_GRP))


def ssm_mixer_pallas(z_l, xbc_l, dt_l, z_c, xbc_c, dt_c, conv_w, conv_b, dt_bias, a_log, d, norm_w,
                     seq_len, ctx_len, with_ctx_out, tm=ROW_TILE):
    n_b = z_l.shape[0] // seq_len
    pad = V7X_LANES - 2 * SSM_H
    bias128 = jnp.pad(dt_bias.astype(F32).reshape(1, 2 * SSM_H), ((0, 0), (0, pad)))
    a128 = jnp.pad(-jnp.exp(a_log.astype(F32)).reshape(1, 2 * SSM_H), ((0, 0), (0, pad)))
    d512 = jnp.repeat(d.astype(F32), SSM_HD).reshape(1, W_GRP)
    act_c = ssm_conv(xbc_c, conv_w, conv_b, ctx_len, min(tm, ctx_len))
    act_l = ssm_conv(xbc_l, conv_w, conv_b, seq_len, tm)
    zero = jnp.zeros((2, n_b, SSM_H, SSM_HD, SSM_N), F32)
    ycf, ycb, hc = ssd_scan(act_c, dt_c, bias128, a128, zero, ctx_len)
    ylf, ylb, _ = ssd_scan(act_l, dt_l, bias128, a128, hc, seq_len)
    out_l = ssm_out(act_l, ylf, ylb, z_l, d512, norm_w, tm)
    out_c = ssm_out(act_c, ycf, ycb, z_c, d512, norm_w, min(tm, z_c.shape[0])) if with_ctx_out else None
    return out_l, out_c


def s5_discretize(a_re, a_im, log_dt, b_re, b_im):
    a_re, a_im = a_re.astype(F32), a_im.astype(F32)
    b_re, b_im = b_re.astype(F32), b_im.astype(F32)
    dt = jnp.exp(log_dt.astype(F32))[:, None]
    mag = jnp.exp(a_re * dt)
    ab_re = mag * jnp.cos(a_im * dt)
    ab_im = mag * jnp.sin(a_im * dt)
    den = a_re * a_re + a_im * a_im
    f_re = ((ab_re - 1) * a_re + ab_im * a_im) / den
    f_im = (ab_im * a_re - (ab_re - 1) * a_im) / den
    bb_re = f_re[..., None] * b_re - f_im[..., None] * b_im
    bb_im = f_re[..., None] * b_im + f_im[..., None] * b_re
    return ab_re, ab_im, bb_re, bb_im


def _mixers(p_l, p_c, prm, with_ctx_out):
    s5_l, pool_l, q_l, k_l, v_l, z_l, xbc_l, dt_l, s5r_l = p_l
    s5_c, pool_c, q_c, k_c, v_c, z_c, xbc_c, dt_c, s5r_c = p_c
    ya_l, ya_c = s5_mixer_pallas(s5_l, s5_c, s5r_l, s5r_c, prm["s5_a_re"], prm["s5_a_im"], prm["s5_log_dt"],
                                 prm["s5_b_re"], prm["s5_b_im"], prm["s5_c_re"], prm["s5_c_im"], prm["s5_d"],
                                 prm["s5_w_glu"], prm["s5_b_glu"], with_ctx_out)
    yb_l = pool_mix_pallas(pool_l, prm["pool_w"], prm["pool_scale"])
    yb_c = pool_mix_pallas(pool_c, prm["pool_w"], prm["pool_scale"]) if with_ctx_out else None
    B_, L, _ = s5_l.shape
    C_ = s5_c.shape[1]
    flat = lambda t: t.reshape(-1, t.shape[-1])
    table = _na_bias_table(prm["na_rpb"], L // GRID_W)
    yc_l = na_attention(flat(q_l), flat(k_l), flat(v_l), flat(k_c), flat(v_c), table, L, C_, True)
    yc_c = (na_attention(flat(q_c), flat(k_c), flat(v_c), flat(k_c), flat(v_c), table, C_, C_, False)
            if with_ctx_out else None)
    yd_l, yd_c = ssm_mixer_pallas(flat(z_l), flat(xbc_l), flat(dt_l), flat(z_c), flat(xbc_c), flat(dt_c),
                                  prm["ssm_conv_w"], prm["ssm_conv_b"], prm["ssm_dt_bias"], prm["ssm_a_log"],
                                  prm["ssm_d"], prm["ssm_norm_w"], L, C_, with_ctx_out)
    return (ya_l, yb_l, yc_l, yd_l), (ya_c, yb_c, yc_c, yd_c)


def kernel(x, c, ctx, c_ctx, w_ada, b_ada, w_in, w_out, s5_a_re, s5_a_im, s5_log_dt, s5_b_re, s5_b_im, s5_c_re, s5_c_im, s5_d, s5_w_glu, s5_b_glu, pool_w, pool_scale, na_rpb, ssm_conv_w, ssm_conv_b, ssm_dt_bias, ssm_a_log, ssm_d, ssm_norm_w, ln1_g, ln1_b, ln2_g, ln2_b, peer_w_q, peer_sub_keys, peer_u, peer_v):
    B_, L, D = x.shape
    C_ = ctx.shape[1]
    x_lat = x.reshape(B_ * L, D)
    x_ctx = ctx.reshape(B_ * C_, D)
    assert B_ + 1 <= ADA_ROWS
    cond = jnp.concatenate([c, c_ctx[None], jnp.zeros((ADA_ROWS - B_ - 1, D), F32)], axis=0)
    dt_pad = V7X_LANES - IN_SPLITS[-1]
    splits = IN_SPLITS[:-1] + (V7X_LANES,)
    in_dtypes = (F32, F32, BF16, BF16, BF16, F32, F32, F32)
    TM = ROW_TILE
    for l in range(DEPTH):
        last = l == DEPTH - 1
        prm = dict(s5_a_re=s5_a_re[l], s5_a_im=s5_a_im[l], s5_log_dt=s5_log_dt[l], s5_b_re=s5_b_re[l],
                   s5_b_im=s5_b_im[l], s5_c_re=s5_c_re[l], s5_c_im=s5_c_im[l], s5_d=s5_d[l],
                   s5_w_glu=s5_w_glu[l], s5_b_glu=s5_b_glu[l], pool_w=pool_w[l], pool_scale=pool_scale[l],
                   na_rpb=na_rpb[l], ssm_conv_w=ssm_conv_w[l], ssm_conv_b=ssm_conv_b[l],
                   ssm_dt_bias=ssm_dt_bias[l], ssm_a_log=ssm_a_log[l], ssm_d=ssm_d[l],
                   ssm_norm_w=ssm_norm_w[l])
        mod = ada_modulation(cond, w_ada[l], b_ada[l])
        m_lat = mod[:B_].reshape(B_, 6, 1, D)
        m_ctx = mod[B_:B_ + 1].reshape(1, 6, 1, D)
        w_in_b = jnp.pad(w_in[l], ((0, 0), (0, dt_pad))).astype(BF16)
        w_out_b = w_out[l].astype(BF16)

        p_l = modulated_matmul(x_lat, m_lat[:, 0], m_lat[:, 1], w_in_b, splits, in_dtypes, L, L, TM)
        p_c = modulated_matmul(x_ctx, m_ctx[:, 0], m_ctx[:, 1], w_in_b, splits, in_dtypes, B_ * C_, C_, TM)
        p_l = [a.reshape(B_, L, -1) for a in p_l]
        p_c = [a.reshape(B_, C_, -1) for a in p_c]
        y_l, y_c = _mixers(p_l, p_c, prm, not last)

        x_lat = proj_residual_ln([a.reshape(B_ * L, -1) for a in y_l], x_lat, m_lat[:, 2], w_out_b,
                                 ln1_g[l], ln1_b[l], L, TM)
        wq_b = peer_w_q[l].astype(BF16)
        keys_b = peer_sub_keys[l].reshape(2 * PEER_HEADS, PEER_NK, PEER_HALF).astype(BF16)
        u_b = peer_u[l].astype(BF16)
        vt_b = peer_v_tiles(peer_v[l])
        x_lat = peer_sublayer(x_lat, m_lat[:, 3], m_lat[:, 4], m_lat[:, 5], wq_b, keys_b, u_b, vt_b,
                              ln2_g[l], ln2_b[l], L)
        if not last:
            x_ctx = proj_residual_ln([a.reshape(B_ * C_, -1) for a in y_c], x_ctx, m_ctx[:, 2], w_out_b,
                                     ln1_g[l], ln1_b[l], B_ * C_, TM)
            x_ctx = peer_sublayer(x_ctx, m_ctx[:, 3], m_ctx[:, 4], m_ctx[:, 5], wq_b, keys_b, u_b, vt_b,
                                  ln2_g[l], ln2_b[l], B_ * C_)
    return x_lat.reshape(B_, L, D)
```

```python
import functools

import jax
import jax.numpy as jnp
import numpy as np
from jax import lax
from jax.experimental import pallas as pl
from jax.experimental.pallas import tpu as pltpu

D_MODEL = 2048
BATCH = 4
SEQ = 4096
DEPTH = 2

GRID_W = 64
CTX_LEN = 256
N_MIXERS = 4
D_MIX = D_MODEL
W_GRP = D_MIX // N_MIXERS

S5_P = 16
S5_G = W_GRP // S5_P
S5_N = 64

POOL_WINDOWS = (2, 4, 8, 16)
POOL_C = W_GRP // len(POOL_WINDOWS)

NA_HD = 64
NA_H = W_GRP // NA_HD
NA_WIN_R = 8
NA_WIN_C = 16

SSM_HD = 64
SSM_H = W_GRP // SSM_HD
SSM_G = 2
SSM_N = 128
SSM_CONV = 4
SSM_CHUNK = 128
SSM_CONV_CH = W_GRP + 2 * SSM_G * SSM_N

PEER_HEADS = 8
PEER_NK = 128
PEER_NE = PEER_NK * PEER_NK
PEER_QDIM = 256
PEER_TOPK = 16
PEER_BLOCK = 128

IN_SPLITS = (W_GRP, W_GRP, W_GRP, W_GRP, W_GRP, W_GRP, SSM_CONV_CH, 2 * SSM_H)
D_IN = sum(IN_SPLITS)

DEEPNORM_ALPHA = (2 * DEPTH) ** 0.25
DEEPNORM_BETA = (8 * DEPTH) ** -0.25
LN_EPS = 1e-5
F32 = jnp.float32
BF16 = jnp.bfloat16

V7X_LANES = 128
V7X_VMEM_BYTES = 64 * 1024 * 1024
V7X_VMEM_LIMIT_BYTES = V7X_VMEM_BYTES * 7 // 8

ROW_TILE = 256
S5_STEPS = 128
S5_GLU_TILE = 256
PEER_ROUTE_TILE = 512
PEER_DENSE_TILE = 512
OUT_PROJ_TILE = 512


ADA_ROWS = 8
ADA_TN = 1536


def _ada_body(c_ref, w_ref, b_ref, o_ref):
    act = jax.nn.silu(c_ref[...]).astype(BF16)
    o_ref[...] = jnp.dot(act, w_ref[...].astype(BF16), preferred_element_type=F32) + b_ref[...]


def ada_modulation(c8, w, b):
    R, D = c8.shape
    N = w.shape[1]
    assert N % ADA_TN == 0
    return pl.pallas_call(
        _ada_body,
        out_shape=jax.ShapeDtypeStruct((R, N), F32),
        grid=(N // ADA_TN,),
        in_specs=[pl.BlockSpec((R, D), lambda j: (0, 0)), pl.BlockSpec((D, ADA_TN), lambda j: (0, j)),
                  pl.BlockSpec((1, ADA_TN), lambda j: (0, j))],
        out_specs=pl.BlockSpec((R, ADA_TN), lambda j: (0, j)),
        compiler_params=pltpu.CompilerParams(dimension_semantics=("arbitrary",),
                                             vmem_limit_bytes=V7X_VMEM_LIMIT_BYTES),
        name="ada_modulation",
    )(c8, w, b.reshape(1, N))
def _modmm_body(x_ref, shift_ref, scale_ref, w_ref, *out_refs, col_splits):
    xm = (x_ref[...] * (1.0 + scale_ref[0]) + shift_ref[0]).astype(BF16)
    o = 0
    for j, (ref, n) in enumerate(zip(out_refs, col_splits)):
        val = jnp.dot(xm, w_ref[:, o:o + n], preferred_element_type=F32)
        ref[...] = val.astype(ref.dtype)
        if j == 0:
            tm = val.shape[0]
            r = lax.broadcasted_iota(jnp.int32, (tm, tm), 0)
            c = lax.broadcasted_iota(jnp.int32, (tm, tm), 1)
            flip = jnp.where(r + c == tm - 1, 1.0, 0.0).astype(BF16)
            out_refs[-1][...] = jnp.dot(flip, val.astype(BF16), preferred_element_type=F32).astype(BF16)
        o += n


def modulated_matmul(x, shift, scale, w_bf16, col_splits, out_dtypes, rows_per_mod, seq_len, tm):
    T, K = x.shape
    assert T % tm == 0 and rows_per_mod % tm == 0 and seq_len % tm == 0
    tiles_per_mod = rows_per_mod // tm
    tps = seq_len // tm
    n_tot = sum(col_splits)
    assert w_bf16.shape == (K, n_tot)
    mod_spec = pl.BlockSpec((1, 1, K), lambda i: (i // tiles_per_mod, 0, 0))
    shapes = [jax.ShapeDtypeStruct((T, n), dt) for n, dt in zip(col_splits, out_dtypes, strict=True)]
    specs = [pl.BlockSpec((tm, n), lambda i: (i, 0)) for n in col_splits]
    shapes.append(jax.ShapeDtypeStruct((T, col_splits[0]), BF16))
    specs.append(pl.BlockSpec((tm, col_splits[0]), lambda i: ((i // tps) * tps + tps - 1 - i % tps, 0)))
    return pl.pallas_call(
        functools.partial(_modmm_body, col_splits=tuple(col_splits)),
        out_shape=shapes,
        grid=(T // tm,),
        in_specs=[pl.BlockSpec((tm, K), lambda i: (i, 0)), mod_spec, mod_spec,
                  pl.BlockSpec((K, n_tot), lambda i: (0, 0))],
        out_specs=specs,
        compiler_params=pltpu.CompilerParams(dimension_semantics=("arbitrary",),
                                             vmem_limit_bytes=V7X_VMEM_LIMIT_BYTES),
        name="modulated_matmul",
    )(x, shift, scale, w_bf16)


def _proj_ln_body(*refs, n_parts, alpha):
    part_refs = refs[:n_parts]
    x_ref, gate_ref, w_ref, g_ref, b_ref, o_ref = refs[n_parts:]
    acc = None
    o = 0
    for pr in part_refs:
        n = pr.shape[-1]
        d = jnp.dot(pr[...].astype(BF16), w_ref[o:o + n, :], preferred_element_type=F32)
        acc = d if acc is None else acc + d
        o += n
    h = alpha * x_ref[...] + gate_ref[0] * acc
    mu = jnp.mean(h, -1, keepdims=True)
    hc = h - mu
    var = jnp.mean(hc * hc, -1, keepdims=True)
    o_ref[...] = hc * lax.rsqrt(var + LN_EPS) * g_ref[...] + b_ref[...]


def proj_residual_ln(parts, x, gate, w_bf16, g, b, rows_per_mod, tm):
    T, D = x.shape
    assert T % tm == 0 and rows_per_mod % tm == 0
    tiles_per_mod = rows_per_mod // tm
    k_tot = sum(p.shape[-1] for p in parts)
    assert w_bf16.shape == (k_tot, D)
    row = lambda n: pl.BlockSpec((tm, n), lambda i: (i, 0))
    vec = pl.BlockSpec((1, D), lambda i: (0, 0))
    return pl.pallas_call(
        functools.partial(_proj_ln_body, n_parts=len(parts), alpha=DEEPNORM_ALPHA),
        out_shape=jax.ShapeDtypeStruct((T, D), F32),
        grid=(T // tm,),
        in_specs=[row(p.shape[-1]) for p in parts] + [
            row(D), pl.BlockSpec((1, 1, D), lambda i: (i // tiles_per_mod, 0, 0)),
            pl.BlockSpec((k_tot, D), lambda i: (0, 0)), vec, vec],
        out_specs=row(D),
        compiler_params=pltpu.CompilerParams(dimension_semantics=("arbitrary",),
                                             vmem_limit_bytes=V7X_VMEM_LIMIT_BYTES),
        name="proj_residual_ln",
    )(*parts, x, gate, w_bf16, g.reshape(1, D), b.reshape(1, D))


S5_CHAINS = 2 * BATCH
S5_STATE = S5_G * S5_N
S5_SCAN_COLS = 512
S5_SB_IN = S5_SCAN_COLS // S5_N * S5_P


def _s5_scan_body(u_ref, wb_ref, wc_ref, are_ref, aim_ref, y_ref, bu_ref, h_ref, *, steps):
    rows = steps * S5_CHAINS

    @pl.when(pl.program_id(0) == 0)
    def _():
        h_ref[...] = jnp.zeros_like(h_ref)

    u = u_ref[...]
    chain = lax.broadcasted_iota(jnp.int32, u.shape, 0) % S5_CHAINS
    fwd = chain < BATCH
    zero = jnp.zeros_like(u)
    uf = jnp.where(fwd, u, zero)
    ub = jnp.where(fwd, zero, u)
    is_fwd_y = lax.broadcasted_iota(jnp.int32, (rows, S5_SB_IN), 0) % S5_CHAINS < BATCH

    for cb in range(S5_STATE // S5_SCAN_COLS):
        cin = slice(cb * S5_SB_IN, (cb + 1) * S5_SB_IN)
        re = pl.ds(cb * S5_SCAN_COLS, S5_SCAN_COLS)
        im = pl.ds(S5_STATE + cb * S5_SCAN_COLS, S5_SCAN_COLS)
        bu = jnp.dot(jnp.concatenate([uf[:, cin], ub[:, cin]], axis=1), wb_ref[cb], preferred_element_type=F32)
        bu_ref[:, re] = bu[:, :S5_SCAN_COLS]
        bu_ref[:, im] = bu[:, S5_SCAN_COLS:]
        a_re = are_ref[:, re]
        a_im = aim_ref[:, re]

        def step(s, carry):
            h_re, h_im = carry
            r = pl.ds(pl.multiple_of(s * S5_CHAINS, S5_CHAINS), S5_CHAINS)
            n_re = a_re * h_re - a_im * h_im + bu_ref[r, re]
            n_im = a_re * h_im + a_im * h_re + bu_ref[r, im]
            bu_ref[r, re] = n_re
            bu_ref[r, im] = n_im
            return n_re, n_im

        h_re, h_im = lax.fori_loop(0, steps, step, (h_ref[:, re], h_ref[:, im]), unroll=8)
        h_ref[:, re] = h_re
        h_ref[:, im] = h_im
        hb = jnp.concatenate([bu_ref[:, re], bu_ref[:, im]], axis=1).astype(BF16)
        y2 = jnp.dot(hb, wc_ref[cb], preferred_element_type=F32)
        y_ref[cb] = jnp.where(is_fwd_y, y2[:, :S5_SB_IN], y2[:, S5_SB_IN:])


def s5_scan(u8, wb, wc, a_re8, a_im8, steps):
    n_rows = u8.shape[0]
    rows = steps * S5_CHAINS
    assert n_rows % rows == 0
    full = lambda a: pl.BlockSpec(a.shape, lambda i: (0,) * a.ndim)
    return pl.pallas_call(
        functools.partial(_s5_scan_body, steps=steps),
        out_shape=jax.ShapeDtypeStruct((W_GRP // S5_SB_IN, n_rows, S5_SB_IN), F32),
        grid=(n_rows // rows,),
        in_specs=[pl.BlockSpec((rows, W_GRP), lambda i: (i, 0)), full(wb), full(wc), full(a_re8), full(a_im8)],
        out_specs=pl.BlockSpec((W_GRP // S5_SB_IN, rows, S5_SB_IN), lambda i: (0, i, 0)),
        scratch_shapes=[pltpu.VMEM((rows, 2 * S5_STATE), F32), pltpu.VMEM((S5_CHAINS, 2 * S5_STATE), F32)],
        compiler_params=pltpu.CompilerParams(dimension_semantics=("arbitrary",),
                                             vmem_limit_bytes=V7X_VMEM_LIMIT_BYTES),
        name="s5_scan",
    )(u8, wb, wc, a_re8, a_im8)


def _s5_glu_body(u_ref, yf_ref, yb_ref, d_ref, w_ref, b_ref, o_ref):
    tm = u_ref.shape[1]
    r = lax.broadcasted_iota(jnp.int32, (tm, tm), 0)
    c = lax.broadcasted_iota(jnp.int32, (tm, tm), 1)
    flip = jnp.where(r + c == tm - 1, 1.0, 0.0).astype(BF16)
    for b in range(BATCH):
        n_cb = yf_ref.shape[0]
        yf = jnp.concatenate([yf_ref[cb, pl.ds(b, tm, stride=S5_CHAINS), :] for cb in range(n_cb)], axis=1)
        yb = jnp.concatenate([yb_ref[cb, pl.ds(BATCH + b, tm, stride=S5_CHAINS), :] for cb in range(n_cb)],
                             axis=1)
        hi = yb.astype(BF16)
        lo = (yb - hi.astype(F32)).astype(BF16)
        yb = jnp.dot(flip, hi, preferred_element_type=F32) + jnp.dot(flip, lo, preferred_element_type=F32)
        y = d_ref[...] * u_ref[b] + yf + yb
        g = jax.nn.gelu(y)
        z = jnp.dot(g.astype(BF16), w_ref[...], preferred_element_type=F32) + b_ref[...]
        o_ref[b] = g * jax.nn.sigmoid(z)


def s5_glu_pallas(u, y8, first_step, d, w_bf16, b, tm):
    B_, n, W = u.shape
    assert B_ == BATCH and n % tm == 0 and first_step % tm == 0
    nt = n // tm
    off = first_step // tm
    row = pl.BlockSpec((B_, tm, W), lambda i: (0, i, 0))
    vec = pl.BlockSpec((1, W), lambda i: (0, 0))
    return pl.pallas_call(
        _s5_glu_body,
        out_shape=jax.ShapeDtypeStruct((B_, n, W), F32),
        grid=(nt,),
        in_specs=[row, pl.BlockSpec((y8.shape[0], tm * S5_CHAINS, y8.shape[2]), lambda i: (0, off + i, 0)),
                  pl.BlockSpec((y8.shape[0], tm * S5_CHAINS, y8.shape[2]), lambda i: (0, off + nt - 1 - i, 0)),
                  vec, pl.BlockSpec((W, W), lambda i: (0, 0)), vec],
        out_specs=row,
        compiler_params=pltpu.CompilerParams(dimension_semantics=("arbitrary",),
                                             vmem_limit_bytes=V7X_VMEM_LIMIT_BYTES),
        name="s5_glu",
    )(u, y8, y8, d.reshape(1, W), w_bf16, b.reshape(1, W))


def _s5_weights(a_re, a_im, log_dt, b_re, b_im, c_re, c_im):
    eye = jnp.eye(S5_G, dtype=F32)
    wb, a8 = [], []
    wc = []
    for di in range(2):
        ab_re, ab_im, bb_re, bb_im = s5_discretize(a_re[di], a_im[di], log_dt[di], b_re[di], b_im[di])
        blk = lambda m: jnp.einsum('gnp,gh->gphn', m, eye).reshape(W_GRP, S5_STATE)
        wb.append(jnp.concatenate([blk(bb_re), blk(bb_im)], axis=1))
        a8.append((jnp.broadcast_to(ab_re.reshape(1, S5_STATE), (BATCH, S5_STATE)),
                   jnp.broadcast_to(ab_im.reshape(1, S5_STATE), (BATCH, S5_STATE))))
        cblk = lambda m: jnp.einsum('gpn,gh->gnhp', m.astype(F32), eye).reshape(S5_STATE, W_GRP)
        wc.append(jnp.concatenate([cblk(c_re[di]), -cblk(c_im[di])], axis=0))
    wb_sb, wc_sb = [], []
    for sb in range(S5_STATE // S5_SCAN_COLS):
        cin = slice(sb * S5_SB_IN, (sb + 1) * S5_SB_IN)
        re = slice(sb * S5_SCAN_COLS, (sb + 1) * S5_SCAN_COLS)
        im = slice(S5_STATE + sb * S5_SCAN_COLS, S5_STATE + (sb + 1) * S5_SCAN_COLS)
        wb_sb.append(jnp.concatenate([jnp.concatenate([w[cin, re], w[cin, im]], axis=1) for w in wb], axis=0))
        wc_sb.append(jnp.concatenate([jnp.concatenate([w[re, cin], w[im, cin]], axis=0) for w in wc], axis=1))
    wb_sb = jnp.stack(wb_sb).astype(BF16)
    wc_sb = jnp.stack(wc_sb).astype(BF16)
    a_re8 = jnp.concatenate([a8[0][0], a8[1][0]], axis=0)
    a_im8 = jnp.concatenate([a8[0][1], a8[1][1]], axis=0)
    return wb_sb, wc_sb, a_re8, a_im8


def s5_mixer_pallas(u_lat, u_ctx, rev_lat, rev_ctx, a_re, a_im, log_dt, b_re, b_im, c_re, c_im, d, w_glu,
                    b_glu, with_ctx_out, steps=S5_STEPS, tm=S5_GLU_TILE):
    B_, L, W = u_lat.shape
    C_ = u_ctx.shape[1]
    assert B_ == BATCH and W == W_GRP
    wb, wc, a_re8, a_im8 = _s5_weights(a_re, a_im, log_dt, b_re, b_im, c_re, c_im)
    seq_f = jnp.concatenate([u_ctx.astype(BF16), u_lat.astype(BF16)], axis=1)
    seq_b = jnp.concatenate([rev_ctx, rev_lat], axis=1)
    u8 = jnp.concatenate([seq_f, seq_b], axis=0).transpose(1, 0, 2).reshape((C_ + L) * S5_CHAINS, W)
    y8 = s5_scan(u8, wb, wc, a_re8, a_im8, steps)
    w_glu_b = w_glu.astype(BF16)
    tm = min(tm, C_)
    out_lat = s5_glu_pallas(u_lat, y8, C_, d, w_glu_b, b_glu, tm).reshape(B_ * L, W)
    out_ctx = s5_glu_pallas(u_ctx, y8, 0, d, w_glu_b, b_glu, tm).reshape(B_ * C_, W) if with_ctx_out else None
    return out_lat, out_ctx


PEER_HALF = PEER_QDIM // 2


def _gelu_tanh(x):
    c = (2.0 / np.pi) ** 0.5
    return x * (0.5 + 0.5 * jnp.tanh(x * (c + (0.044715 * c) * (x * x))))


def _argmax_rows(v, r):
    while v.shape[0] > 1:
        half = v.shape[0] // 2
        take_hi = v[half:] > v[:half]
        r = jnp.where(take_hi, r[half:], r[:half])
        v = jnp.maximum(v[:half], v[half:])
    return v, r


def _topk_rows(xs, k, with_rank):
    n, lanes = xs[0].shape
    row = lax.broadcasted_iota(jnp.int32, (n, lanes), 0).astype(F32)
    krow = lax.broadcasted_iota(jnp.int32, (k, lanes), 0)

    def body(it, carry):
        out = []
        for x, vals, aux in carry:
            m, first = _argmax_rows(x, row)
            hit = row == first
            sel = krow == it
            aux = jnp.where(hit, it.astype(F32), aux) if with_rank else jnp.where(sel, first, aux)
            out.append((jnp.where(hit, -jnp.inf, x), jnp.where(sel, m, vals), aux))
        return tuple(out)

    zeros = jnp.zeros((k, lanes), F32)
    aux0 = jnp.full((n, lanes), float(k), F32) if with_rank else zeros
    res = lax.fori_loop(0, k, body, tuple((x, zeros, aux0) for x in xs), unroll=1 if with_rank else 2)
    return [(vals, aux) for _, vals, aux in res]


PEER_GRID_COLS = tuple(PEER_TOPK // (i + 1) for i in range(PEER_TOPK))
PEER_GRID_ROWS = 64


def _peer_route_body(x_ref, shift_ref, scale_ref, wq_ref, keys_ref, h_ref, n1_ref, c1_ref, r2_ref, e2_ref,
                     q_ref, *, lane_tiles):
    hm = (x_ref[...] * (1.0 + scale_ref[0]) + shift_ref[0]).astype(BF16)
    h_ref[...] = hm
    q_ref[...] = jnp.dot(hm, wq_ref[...], preferred_element_type=F32).astype(BF16)
    K = PEER_TOPK
    key = lax.broadcasted_iota(jnp.int32, (PEER_NK, V7X_LANES), 0).astype(F32)
    for hd in range(PEER_HEADS):
        for lt in range(lane_tiles):
            tok = pl.ds(lt * V7X_LANES, V7X_LANES)
            sc = []
            for side in range(2):
                col = (2 * hd + side) * PEER_HALF
                qs = q_ref[tok, col:col + PEER_HALF]
                sc.append(lax.dot_general(keys_ref[2 * hd + side], qs, (((1,), (1,)), ((), ())),
                                          preferred_element_type=F32))
            (v1, idx1), (v2, idx2) = _topk_rows(sc, K, with_rank=False)
            cells = [v1[i:i + 1] + v2[:PEER_GRID_COLS[i]] for i in range(K)]
            cells.append(jnp.full((PEER_GRID_ROWS - sum(PEER_GRID_COLS), V7X_LANES), -jnp.inf, F32))
            (vc, rc), = _topk_rows([jnp.concatenate(cells, axis=0)], K, with_rank=True)
            z = jnp.sum(jnp.exp(vc - vc[0:1]), axis=0, keepdims=True)
            chosen = jnp.where(rc < float(K), 1.0, 0.0)
            n1 = jnp.zeros_like(key)
            r2 = jnp.full(key.shape, float(K), F32)
            off = 0
            for i in range(K):
                n_i = jnp.sum(chosen[off:off + PEER_GRID_COLS[i]], axis=0, keepdims=True)
                n1 = jnp.where(key == idx1[i:i + 1], n_i, n1)
                r2 = jnp.where(key == idx2[i:i + 1], float(i), r2)
                off += PEER_GRID_COLS[i]
            n1_ref[hd, :, tok] = n1
            c1_ref[hd, :, tok] = jnp.exp(sc[0] - v1[0:1]) / z
            r2_ref[hd, :, tok] = r2.astype(BF16)
            e2_ref[hd, :, tok] = jnp.exp(sc[1] - v2[0:1]).astype(BF16)


def peer_route(x, shift, scale, wq_bf16, keys_bf16, rows_per_mod, tm):
    T, D = x.shape
    assert T % tm == 0 and rows_per_mod % tm == 0 and tm % V7X_LANES == 0
    tiles_per_mod = rows_per_mod // tm
    mod_spec = pl.BlockSpec((1, 1, D), lambda i: (i // tiles_per_mod, 0, 0))
    tab = lambda dt: jax.ShapeDtypeStruct((PEER_HEADS, PEER_NK, T), dt)
    tab_spec = pl.BlockSpec((PEER_HEADS, PEER_NK, tm), lambda i: (0, 0, i))
    return pl.pallas_call(
        functools.partial(_peer_route_body, lane_tiles=tm // V7X_LANES),
        out_shape=[jax.ShapeDtypeStruct((T, D), BF16), tab(F32), tab(F32), tab(BF16), tab(BF16)],
        grid=(T // tm,),
        in_specs=[pl.BlockSpec((tm, D), lambda i: (i, 0)), mod_spec, mod_spec,
                  pl.BlockSpec(wq_bf16.shape, lambda i: (0, 0)),
                  pl.BlockSpec(keys_bf16.shape, lambda i: (0, 0, 0))],
        out_specs=[pl.BlockSpec((tm, D), lambda i: (i, 0)), tab_spec, tab_spec, tab_spec, tab_spec],
        scratch_shapes=[pltpu.VMEM((tm, PEER_HEADS * PEER_QDIM), BF16)],
        compiler_params=pltpu.CompilerParams(dimension_semantics=("arbitrary",),
                                             vmem_limit_bytes=V7X_VMEM_LIMIT_BYTES),
        name="peer_route",
    )(x, shift, scale, wq_bf16, keys_bf16)


def _peer_dense_body(h_ref, u_ref, vt_ref, n1_ref, c1_ref, r2_ref, e2_ref, x_ref, gate_ref, g_ref, b_ref,
                     o_ref, acc_ref, a_ref, *, n_slab):
    j = pl.program_id(1)

    @pl.when(j == 0)
    def _():
        acc_ref[...] = jnp.zeros_like(acc_ref)

    rows = PEER_SCORE_SLABS * PEER_NK
    for p in range(n_slab // PEER_SCORE_SLABS):
        s = lax.dot_general(u_ref[p * rows:(p + 1) * rows, :], h_ref[...], (((1,), (1,)), ((), ())),
                            preferred_element_type=F32)
        for kk in range(PEER_SCORE_SLABS):
            k = PEER_SCORE_SLABS * p + kk
            for lt in range(h_ref.shape[0] // V7X_LANES):
                lanes = slice(lt * V7X_LANES, (lt + 1) * V7X_LANES)
                g = None
                for hd in range(PEER_HEADS):
                    n1row = n1_ref[hd, k:k + 1, lanes].astype(BF16)
                    c1row = c1_ref[hd, k:k + 1, lanes].astype(BF16)
                    sel = lax.clamp(jnp.zeros((), BF16), n1row - r2_ref[hd, :, lanes], jnp.ones((), BF16))
                    gh = sel * e2_ref[hd, :, lanes] * c1row
                    g = gh if g is None else g + gh
                sk = s[kk * PEER_NK:(kk + 1) * PEER_NK, lanes]
                a_ref[k * PEER_NK:(k + 1) * PEER_NK, lanes] = _gelu_tanh(sk).astype(BF16) * g
    acc_ref[...] += jnp.dot(vt_ref[0], a_ref[...], preferred_element_type=F32)

    @pl.when(j == pl.num_programs(1) - 1)
    def _():
        hres = DEEPNORM_ALPHA * x_ref[...] + gate_ref[0] * acc_ref[...].T
        mu = jnp.mean(hres, -1, keepdims=True)
        hc = hres - mu
        var = jnp.mean(hc * hc, -1, keepdims=True)
        o_ref[...] = hc * lax.rsqrt(var + LN_EPS) * g_ref[...] + b_ref[...]


def peer_dense(h_bf16, u_bf16, vt_tiles, n1, c1, r2, e2, x, gate, ln_g, ln_b, rows_per_mod, tm, n_slab):
    T, D = h_bf16.shape
    NE = u_bf16.shape[0]
    e_tile = n_slab * PEER_NK
    n_e = NE // e_tile
    assert T % tm == 0 and NE % e_tile == 0 and vt_tiles.shape == (n_e, D, e_tile)
    slab_spec = pl.BlockSpec((PEER_HEADS, n_slab, tm), lambda i, j: (0, j, i))
    tok_spec = pl.BlockSpec((PEER_HEADS, PEER_NK, tm), lambda i, j: (0, 0, i))
    return pl.pallas_call(
        functools.partial(_peer_dense_body, n_slab=n_slab),
        out_shape=jax.ShapeDtypeStruct((T, D), F32),
        grid=(T // tm, n_e),
        in_specs=[pl.BlockSpec((tm, D), lambda i, j: (i, 0)),
                  pl.BlockSpec((e_tile, D), lambda i, j: (j, 0)),
                  pl.BlockSpec((1, D, e_tile), lambda i, j: (j, 0, 0)),
                  slab_spec, slab_spec, tok_spec, tok_spec,
                  pl.BlockSpec((tm, D), lambda i, j: (i, 0)),
                  pl.BlockSpec((1, 1, D), lambda i, j: (i // (rows_per_mod // tm), 0, 0)),
                  pl.BlockSpec((1, D), lambda i, j: (0, 0)), pl.BlockSpec((1, D), lambda i, j: (0, 0))],
        out_specs=pl.BlockSpec((tm, D), lambda i, j: (i, 0)),
        scratch_shapes=[pltpu.VMEM((D, tm), F32), pltpu.VMEM((e_tile, tm), BF16)],
        compiler_params=pltpu.CompilerParams(dimension_semantics=("arbitrary", "arbitrary"),
                                             vmem_limit_bytes=V7X_VMEM_LIMIT_BYTES),
        name="peer_dense",
    )(h_bf16, u_bf16, vt_tiles, n1, c1, r2, e2, x, gate, ln_g.reshape(1, D), ln_b.reshape(1, D))


PEER_SLABS = 8
PEER_SCORE_SLABS = 4


def peer_v_tiles(v_tab):
    e_tile = PEER_SLABS * PEER_NK
    return v_tab.astype(BF16).reshape(v_tab.shape[0] // e_tile, e_tile, v_tab.shape[1]).transpose(0, 2, 1)


def peer_sublayer(x, shift, scale, gate, wq_bf16, keys_bf16, u_bf16, vt_tiles, ln_g, ln_b, rows_per_mod,
                  tm_route=PEER_ROUTE_TILE, tm_dense=PEER_DENSE_TILE):
    h, n1, c1, r2, e2 = peer_route(x, shift, scale, wq_bf16, keys_bf16, rows_per_mod, tm_route)
    return peer_dense(h, u_bf16, vt_tiles, n1, c1, r2, e2, x, gate, ln_g, ln_b, rows_per_mod, tm_dense,
                      PEER_SLABS)


POOL_HALO = 8
POOL_ROWS = 256


def _pool_body(u_ref, w_ref, scale_ref, o_ref, pad_ref, *, seq_len):
    L = seq_len
    chunk = min(POOL_ROWS, L)
    zeros = jnp.zeros((POOL_HALO, W_GRP), F32)
    pad_ref[0:POOL_HALO, :] = zeros
    pad_ref[POOL_HALO + L:2 * POOL_HALO + L, :] = zeros
    pad_ref[POOL_HALO:POOL_HALO + L, :] = u_ref[0]
    for r0 in range(0, L, chunk):
        t = r0 + lax.broadcasted_iota(jnp.int32, (chunk, POOL_C), 0)
        for j, w in enumerate(POOL_WINDOWS):
            cols = slice(j * POOL_C, (j + 1) * POOL_C)
            acc = None
            for o in range(-(w // 2), w - w // 2):
                s = pad_ref[POOL_HALO + r0 + o:POOL_HALO + r0 + o + chunk, cols]
                acc = s if acc is None else acc + s
            lo = jnp.maximum(t - w // 2, 0)
            hi = jnp.minimum(t - w // 2 + w - 1, L - 1)
            cnt = (hi - lo + 1).astype(F32)
            pooled = acc / cnt - u_ref[0, r0:r0 + chunk, cols]
            y = jnp.dot(pooled.astype(BF16), w_ref[j], preferred_element_type=F32)
            o_ref[0, r0:r0 + chunk, cols] = y * scale_ref[:, cols]


def pool_mix_pallas(u, pool_w, pool_scale):
    B_, L, W = u.shape
    assert max(POOL_WINDOWS) // 2 <= POOL_HALO and L % min(POOL_ROWS, L) == 0
    blk = pl.BlockSpec((1, L, W), lambda b: (b, 0, 0))
    return pl.pallas_call(
        functools.partial(_pool_body, seq_len=L),
        out_shape=jax.ShapeDtypeStruct((B_, L, W), F32),
        grid=(B_,),
        in_specs=[blk, pl.BlockSpec(pool_w.shape, lambda b: (0, 0, 0)), pl.BlockSpec((1, W), lambda b: (0, 0))],
        out_specs=blk,
        scratch_shapes=[pltpu.VMEM((L + 2 * POOL_HALO, W), F32)],
        compiler_params=pltpu.CompilerParams(dimension_semantics=("arbitrary",),
                                             vmem_limit_bytes=V7X_VMEM_LIMIT_BYTES),
        name="pool_mix",
    )(u, pool_w.astype(BF16), pool_scale.reshape(1, W))


NA_TILE_R = 4
NA_TILE = NA_TILE_R * GRID_W
NA_SCALE = NA_HD ** -0.5


def _na_bias_table(rpb, n_rows):
    n_tiles = n_rows // NA_TILE_R
    KR = min(NA_WIN_R, n_rows)
    a = np.array([0, min(2, n_tiles - 1), n_tiles - 1]).reshape(3, 1, 1, 1, 1, 1)
    d = np.arange(3).reshape(1, 3, 1, 1, 1, 1)
    i = np.arange(NA_TILE_R).reshape(1, 1, NA_TILE_R, 1, 1, 1)
    qc = np.arange(GRID_W).reshape(1, 1, 1, GRID_W, 1, 1)
    j = np.arange(NA_TILE_R).reshape(1, 1, 1, 1, NA_TILE_R, 1)
    kc = np.arange(GRID_W).reshape(1, 1, 1, 1, 1, GRID_W)
    qr = NA_TILE_R * a + i
    kr = NA_TILE_R * (a + d - 1) + j
    rs = np.clip(qr - KR // 2, 0, n_rows - KR)
    c0 = np.clip(qc - NA_WIN_C // 2, 0, GRID_W - NA_WIN_C)
    ok = (kr >= rs) & (kr < rs + KR) & (kr >= 0) & (kr < n_rows) & (kc >= c0) & (kc < c0 + NA_WIN_C)
    row_rel = np.clip(kr - qr + NA_WIN_R - 1, 0, 2 * NA_WIN_R - 2)[0, :, :, 0, :, 0]
    col_rel = (np.clip(kc - qc, -(NA_WIN_C - 1), NA_WIN_C - 1) + NA_WIN_C - 1)[0, 0, 0, :, 0, :]
    onehot = (col_rel[None] == np.arange(2 * NA_WIN_C - 1)[:, None, None]).astype(np.float32)
    bias_rc = jnp.einsum('hrc,cqk->hrqk', rpb.astype(F32), onehot, precision=lax.Precision.HIGHEST)
    tab = jnp.stack([bias_rc[:, int(r)] for r in row_rel.reshape(-1)], axis=1)
    tab = tab.reshape(NA_H, 3, NA_TILE_R, NA_TILE_R, GRID_W, GRID_W).transpose(0, 1, 2, 4, 3, 5)
    tab = jnp.where(ok[:, None], tab[None], -jnp.inf)
    return tab.reshape(3, NA_H, 3, NA_TILE, NA_TILE)


def _na_body(q_ref, k_ref, v_ref, kc_ref, vc_ref, t_ref, o_ref, *, n_tiles, with_grid):
    a = pl.program_id(1)
    nt = (((1,), (1,)), ((), ()))
    for h in range(NA_H):
        hs = slice(h * NA_HD, (h + 1) * NA_HD)
        qh = q_ref[:, hs]
        scores = [lax.dot_general(qh, kc_ref[:, hs], nt, preferred_element_type=F32) * NA_SCALE]
        vals = [vc_ref[:, hs]]
        if with_grid:
            for d in range(3):
                ti = jnp.clip(a + d - 1, 0, n_tiles - 1)
                rows = pl.ds(pl.multiple_of(ti * NA_TILE, NA_TILE), NA_TILE)
                s = lax.dot_general(qh, k_ref[rows, hs], nt, preferred_element_type=F32)
                scores.append(s * NA_SCALE + t_ref[0, h, d])
                vals.append(v_ref[rows, hs])
        m_el = scores[0]
        for s in scores[1:]:
            m_el = jnp.maximum(m_el, s)
        m = m_el.max(axis=-1, keepdims=True)
        p_el = None
        acc = None
        for s, vv in zip(scores, vals):
            p = jnp.exp(s - m)
            o = jnp.dot(p.astype(BF16), vv, preferred_element_type=F32)
            p_el = p if p_el is None else p_el + p
            acc = o if acc is None else acc + o
        o_ref[:, hs] = acc / p_el.sum(axis=-1, keepdims=True)


def na_attention(q, k, v, kc, vc, table, seq_len, ctx_len, with_grid):
    W = q.shape[1]
    n_b = kc.shape[0] // ctx_len
    lq = q.shape[0] // n_b
    assert lq % NA_TILE == 0 and ctx_len == NA_TILE
    q_tiles = lq // NA_TILE
    n_tiles = seq_len // NA_TILE
    pat = lambda b, a: (jnp.where(a == 0, 0, jnp.where(a == n_tiles - 1, 2, 1)), 0, 0, 0, 0)
    qo_spec = pl.BlockSpec((NA_TILE, W), lambda b, a: (b * q_tiles + a, 0))
    return pl.pallas_call(
        functools.partial(_na_body, n_tiles=n_tiles, with_grid=with_grid),
        out_shape=jax.ShapeDtypeStruct(q.shape, F32),
        grid=(n_b, q_tiles),
        in_specs=[qo_spec,
                  pl.BlockSpec((seq_len, W), lambda b, a: (b, 0)), pl.BlockSpec((seq_len, W), lambda b, a: (b, 0)),
                  pl.BlockSpec((ctx_len, W), lambda b, a: (b, 0)), pl.BlockSpec((ctx_len, W), lambda b, a: (b, 0)),
                  pl.BlockSpec((1,) + table.shape[1:], pat)],
        out_specs=qo_spec,
        compiler_params=pltpu.CompilerParams(dimension_semantics=("arbitrary", "arbitrary"),
                                             vmem_limit_bytes=V7X_VMEM_LIMIT_BYTES),
        name="na_attention",
    )(q, k, v, kc, vc, table)


SSM_HALO = 8
SSM_BC = SSM_G * SSM_N


def _ssm_conv_body(prev_ref, cur_ref, next_ref, w_ref, b_ref, o_ref, pad_ref, *, tiles_per_seq):
    pos = pl.program_id(0) % tiles_per_seq
    tm = cur_ref.shape[0]
    pad_ref[0:SSM_HALO, :] = jnp.where(pos == 0, 0.0, prev_ref[...])
    pad_ref[SSM_HALO:SSM_HALO + tm, :] = cur_ref[...]
    pad_ref[SSM_HALO + tm:2 * SSM_HALO + tm, :] = jnp.where(pos == tiles_per_seq - 1, 0.0, next_ref[...])
    lead = (SSM_CONV - 1) // 2
    y = b_ref[...]
    for k in range(SSM_CONV):
        y = y + w_ref[k:k + 1, :] * pad_ref[SSM_HALO - lead + k:SSM_HALO - lead + k + tm, :]
    o_ref[...] = jax.nn.silu(y)


def ssm_conv(xbc, conv_w, conv_b, seq_len, tm):
    T, CH = xbc.shape
    assert seq_len % tm == 0 and tm % SSM_HALO == 0
    hb = tm // SSM_HALO
    n_hb = T // SSM_HALO
    return pl.pallas_call(
        functools.partial(_ssm_conv_body, tiles_per_seq=seq_len // tm),
        out_shape=jax.ShapeDtypeStruct((T, CH), F32),
        grid=(T // tm,),
        in_specs=[pl.BlockSpec((SSM_HALO, CH), lambda i: (jnp.maximum(i * hb - 1, 0), 0)),
                  pl.BlockSpec((tm, CH), lambda i: (i, 0)),
                  pl.BlockSpec((SSM_HALO, CH), lambda i: (jnp.minimum((i + 1) * hb, n_hb - 1), 0)),
                  pl.BlockSpec((SSM_CONV, CH), lambda i: (0, 0)), pl.BlockSpec((1, CH), lambda i: (0, 0))],
        out_specs=pl.BlockSpec((tm, CH), lambda i: (i, 0)),
        scratch_shapes=[pltpu.VMEM((tm + 2 * SSM_HALO, CH), F32)],
        compiler_params=pltpu.CompilerParams(dimension_semantics=("arbitrary",),
                                             vmem_limit_bytes=V7X_VMEM_LIMIT_BYTES),
        name="ssm_conv",
    )(xbc, xbc, xbc, conv_w, conv_b.reshape(1, CH))


def _ssd_chunk_prep(xbc_ref, dt_ref, bias_ref, a_ref, *, di):
    rev = di == 1
    Q = xbc_ref.shape[0]
    dt_all = jax.nn.softplus(dt_ref[...] + bias_ref[...])
    a_cs = dt_all * a_ref[...]
    row_id = lax.broadcasted_iota(jnp.int32, a_cs.shape, 0)
    sh = 1
    while sh < Q:
        if rev:
            a_cs = a_cs + jnp.where(row_id < Q - sh, pltpu.roll(a_cs, Q - sh, 0), 0.0)
        else:
            a_cs = a_cs + jnp.where(row_id >= sh, pltpu.roll(a_cs, sh, 0), 0.0)
        sh *= 2
    a_cs_t = a_cs.T
    a_tot = a_cs[0:1, :] if rev else a_cs[Q - 1:Q, :]
    l_id = lax.broadcasted_iota(jnp.int32, (Q, Q), 0)
    s_id = lax.broadcasted_iota(jnp.int32, (Q, Q), 1)
    causal = (l_id <= s_id) if rev else (l_id >= s_id)
    cb = []
    for g in range(SSM_G):
        bg = xbc_ref[:, W_GRP + g * SSM_N:W_GRP + (g + 1) * SSM_N].astype(BF16)
        cg = xbc_ref[:, W_GRP + SSM_BC + g * SSM_N:W_GRP + SSM_BC + (g + 1) * SSM_N].astype(BF16)
        cb.append((bg, cg, lax.dot_general(cg, bg, (((1,), (1,)), ((), ())), preferred_element_type=F32)))

    src = lax.broadcasted_iota(jnp.int32, (V7X_LANES, W_GRP), 0)
    dst = lax.broadcasted_iota(jnp.int32, (V7X_LANES, W_GRP), 1)
    spread = jnp.where(src == di * SSM_H + dst // SSM_HD, 1.0, 0.0).astype(BF16)

    def per_channel(m):
        hi = m.astype(BF16)
        lo = (m - hi.astype(F32)).astype(BF16)
        return (jnp.dot(hi, spread, preferred_element_type=F32) + jnp.dot(lo, spread, preferred_element_type=F32))

    x_dt = xbc_ref[:, :W_GRP] * per_channel(dt_all)
    x_in = x_dt.astype(BF16)
    x_out = (x_dt * per_channel(jnp.exp(a_tot - a_cs))).astype(BF16)
    e_in = per_channel(jnp.exp(a_cs))
    return a_cs, a_cs_t, a_tot, causal, cb, x_in, x_out, e_in


def _ssd_head(prep, h_scr, y_ref, *, di, h):
    a_cs, a_cs_t, a_tot, causal, cb, x_in, x_out, e_in = prep
    nt = (((1,), (1,)), ((), ()))
    tn = (((0,), (0,)), ((), ()))
    c = di * SSM_H + h
    ch = slice(h * SSM_HD, (h + 1) * SSM_HD)
    bg, cg, cbg = cb[h // (SSM_H // SSM_G)]
    lm = jnp.exp(jnp.where(causal, a_cs[:, c:c + 1] - a_cs_t[c:c + 1, :], -jnp.inf))
    hp = h_scr[di, h]
    yd = jnp.dot((cbg * lm).astype(BF16), x_in[:, ch], preferred_element_type=F32)
    yo = lax.dot_general(cg, hp.astype(BF16), nt, preferred_element_type=F32) * e_in[:, ch]
    y_ref[:, ch] = yd + yo
    h_scr[di, h] = (jnp.exp(a_tot[:, c:c + 1]) * hp
                    + lax.dot_general(x_out[:, ch], bg, tn, preferred_element_type=F32))


def _ssd_body(xf_ref, dtf_ref, xb_ref, dtb_ref, bias_ref, a_ref, h0_ref, yf_ref, yb_ref, hfin_ref, h_scr):
    ci = pl.program_id(1)

    @pl.when(ci == 0)
    def _():
        h_scr[...] = h0_ref[:, 0]

    prep_f = _ssd_chunk_prep(xf_ref, dtf_ref, bias_ref, a_ref, di=0)
    prep_b = _ssd_chunk_prep(xb_ref, dtb_ref, bias_ref, a_ref, di=1)
    for h in range(SSM_H):
        _ssd_head(prep_f, h_scr, yf_ref, di=0, h=h)
        _ssd_head(prep_b, h_scr, yb_ref, di=1, h=h)

    @pl.when(ci == pl.num_programs(1) - 1)
    def _():
        hfin_ref[:, 0] = h_scr[...]


def ssd_scan(xbc_act, dt_raw, bias128, a128, h0, seq_len):
    T = xbc_act.shape[0]
    n_b = T // seq_len
    Q = min(SSM_CHUNK, seq_len)
    nc = seq_len // Q
    fwd = lambda b, i: (b * nc + i, 0)
    bwd = lambda b, i: (b * nc + nc - 1 - i, 0)
    vec = pl.BlockSpec((1, V7X_LANES), lambda b, i: (0, 0))
    st = pl.BlockSpec((2, 1, SSM_H, SSM_HD, SSM_N), lambda b, i: (0, b, 0, 0, 0))
    y_shape = jax.ShapeDtypeStruct((T, W_GRP), F32)
    return pl.pallas_call(
        _ssd_body,
        out_shape=[y_shape, y_shape, jax.ShapeDtypeStruct(h0.shape, F32)],
        grid=(n_b, nc),
        in_specs=[pl.BlockSpec((Q, SSM_CONV_CH), fwd), pl.BlockSpec((Q, V7X_LANES), fwd),
                  pl.BlockSpec((Q, SSM_CONV_CH), bwd), pl.BlockSpec((Q, V7X_LANES), bwd), vec, vec, st],
        out_specs=[pl.BlockSpec((Q, W_GRP), fwd), pl.BlockSpec((Q, W_GRP), bwd), st],
        scratch_shapes=[pltpu.VMEM((2, SSM_H, SSM_HD, SSM_N), F32)],
        compiler_params=pltpu.CompilerParams(dimension_semantics=("arbitrary", "arbitrary"),
                                             vmem_limit_bytes=V7X_VMEM_LIMIT_BYTES),
        name="ssd_scan",
    )(xbc_act, dt_raw, xbc_act, dt_raw, bias128, a128, h0)


def _ssm_out_body(xbc_ref, yf_ref, yb_ref, z_ref, d_ref, nw_ref, o_ref):
    y = d_ref[...] * xbc_ref[:, :W_GRP] + yf_ref[...] + yb_ref[...]
    g = y * jax.nn.silu(z_ref[...])
    gw = W_GRP // SSM_G
    for k in range(SSM_G):
        gk = g[:, k * gw:(k + 1) * gw]
        r = lax.rsqrt(jnp.mean(gk * gk, -1, keepdims=True) + LN_EPS)
        o_ref[:, k * gw:(k + 1) * gw] = gk * r * nw_ref[:, k * gw:(k + 1) * gw]


def ssm_out(xbc_act, yf, yb, z, d512, norm_w, tm):
    T = z.shape[0]
    assert T % tm == 0
    row = pl.BlockSpec((tm, W_GRP), lambda i: (i, 0))
    vec = pl.BlockSpec((1, W_GRP), lambda i: (0, 0))
    return pl.pallas_call(
        _ssm_out_body,
        out_shape=jax.ShapeDtypeStruct((T, W_GRP), F32),
        grid=(T // tm,),
        in_specs=[pl.BlockSpec((tm, SSM_CONV_CH), lambda i: (i, 0)), row, row, row, vec, vec],
        out_specs=row,
        compiler_params=pltpu.CompilerParams(dimension_semantics=("arbitrary",),
                                             vmem_limit_bytes=V7X_VMEM_LIMIT_BYTES),
        name="ssm_out",
    )(xbc_act, yf, yb, z, d512, norm_w.reshape(1, W_GRP))


def ssm_mixer_pallas(z_l, xbc_l, dt_l, z_c, xbc_c, dt_c, conv_w, conv_b, dt_bias, a_log, d, norm_w,
                     seq_len, ctx_len, with_ctx_out, tm=ROW_TILE):
    n_b = z_l.shape[0] // seq_len
    pad = V7X_LANES - 2 * SSM_H
    bias128 = jnp.pad(dt_bias.astype(F32).reshape(1, 2 * SSM_H), ((0, 0), (0, pad)))
    a128 = jnp.pad(-jnp.exp(a_log.astype(F32)).reshape(1, 2 * SSM_H), ((0, 0), (0, pad)))
    d512 = jnp.repeat(d.astype(F32), SSM_HD).reshape(1, W_GRP)
    act_c = ssm_conv(xbc_c, conv_w, conv_b, ctx_len, min(tm, ctx_len))
    act_l = ssm_conv(xbc_l, conv_w, conv_b, seq_len, tm)
    zero = jnp.zeros((2, n_b, SSM_H, SSM_HD, SSM_N), F32)
    ycf, ycb, hc = ssd_scan(act_c, dt_c, bias128, a128, zero, ctx_len)
    ylf, ylb, _ = ssd_scan(act_l, dt_l, bias128, a128, hc, seq_len)
    out_l = ssm_out(act_l, ylf, ylb, z_l, d512, norm_w, tm)
    out_c = ssm_out(act_c, ycf, ycb, z_c, d512, norm_w, min(tm, z_c.shape[0])) if with_ctx_out else None
    return out_l, out_c


def s5_discretize(a_re, a_im, log_dt, b_re, b_im):
    a_re, a_im = a_re.astype(F32), a_im.astype(F32)
    b_re, b_im = b_re.astype(F32), b_im.astype(F32)
    dt = jnp.exp(log_dt.astype(F32))[:, None]
    mag = jnp.exp(a_re * dt)
    ab_re = mag * jnp.cos(a_im * dt)
    ab_im = mag * jnp.sin(a_im * dt)
    den = a_re * a_re + a_im * a_im
    f_re = ((ab_re - 1) * a_re + ab_im * a_im) / den
    f_im = (ab_im * a_re - (ab_re - 1) * a_im) / den
    bb_re = f_re[..., None] * b_re - f_im[..., None] * b_im
    bb_im = f_re[..., None] * b_im + f_im[..., None] * b_re
    return ab_re, ab_im, bb_re, bb_im


def _mixers(p_l, p_c, prm, with_ctx_out):
    s5_l, pool_l, q_l, k_l, v_l, z_l, xbc_l, dt_l, s5r_l = p_l
    s5_c, pool_c, q_c, k_c, v_c, z_c, xbc_c, dt_c, s5r_c = p_c
    ya_l, ya_c = s5_mixer_pallas(s5_l, s5_c, s5r_l, s5r_c, prm["s5_a_re"], prm["s5_a_im"], prm["s5_log_dt"],
                                 prm["s5_b_re"], prm["s5_b_im"], prm["s5_c_re"], prm["s5_c_im"], prm["s5_d"],
                                 prm["s5_w_glu"], prm["s5_b_glu"], with_ctx_out)
    yb_l = pool_mix_pallas(pool_l, prm["pool_w"], prm["pool_scale"])
    yb_c = pool_mix_pallas(pool_c, prm["pool_w"], prm["pool_scale"]) if with_ctx_out else None
    B_, L, _ = s5_l.shape
    C_ = s5_c.shape[1]
    flat = lambda t: t.reshape(-1, t.shape[-1])
    table = _na_bias_table(prm["na_rpb"], L // GRID_W)
    yc_l = na_attention(flat(q_l), flat(k_l), flat(v_l), flat(k_c), flat(v_c), table, L, C_, True)
    yc_c = (na_attention(flat(q_c), flat(k_c), flat(v_c), flat(k_c), flat(v_c), table, C_, C_, False)
            if with_ctx_out else None)
    yd_l, yd_c = ssm_mixer_pallas(flat(z_l), flat(xbc_l), flat(dt_l), flat(z_c), flat(xbc_c), flat(dt_c),
                                  prm["ssm_conv_w"], prm["ssm_conv_b"], prm["ssm_dt_bias"], prm["ssm_a_log"],
                                  prm["ssm_d"], prm["ssm_norm_w"], L, C_, with_ctx_out)
    return (ya_l, yb_l, yc_l, yd_l), (ya_c, yb_c, yc_c, yd_c)


def kernel(x, c, ctx, c_ctx, w_ada, b_ada, w_in, w_out, s5_a_re, s5_a_im, s5_log_dt, s5_b_re, s5_b_im, s5_c_re, s5_c_im, s5_d, s5_w_glu, s5_b_glu, pool_w, pool_scale, na_rpb, ssm_conv_w, ssm_conv_b, ssm_dt_bias, ssm_a_log, ssm_d, ssm_norm_w, ln1_g, ln1_b, ln2_g, ln2_b, peer_w_q, peer_sub_keys, peer_u, peer_v):
    B_, L, D = x.shape
    C_ = ctx.shape[1]
    x_lat = x.reshape(B_ * L, D)
    x_ctx = ctx.reshape(B_ * C_, D)
    assert B_ + 1 <= ADA_ROWS
    cond = jnp.concatenate([c, c_ctx[None], jnp.zeros((ADA_ROWS - B_ - 1, D), F32)], axis=0)
    dt_pad = V7X_LANES - IN_SPLITS[-1]
    splits = IN_SPLITS[:-1] + (V7X_LANES,)
    in_dtypes = (F32, F32, BF16, BF16, BF16, F32, F32, F32)
    TM = ROW_TILE
    for l in range(DEPTH):
        last = l == DEPTH - 1
        prm = dict(s5_a_re=s5_a_re[l], s5_a_im=s5_a_im[l], s5_log_dt=s5_log_dt[l], s5_b_re=s5_b_re[l],
                   s5_b_im=s5_b_im[l], s5_c_re=s5_c_re[l], s5_c_im=s5_c_im[l], s5_d=s5_d[l],
                   s5_w_glu=s5_w_glu[l], s5_b_glu=s5_b_glu[l], pool_w=pool_w[l], pool_scale=pool_scale[l],
                   na_rpb=na_rpb[l], ssm_conv_w=ssm_conv_w[l], ssm_conv_b=ssm_conv_b[l],
                   ssm_dt_bias=ssm_dt_bias[l], ssm_a_log=ssm_a_log[l], ssm_d=ssm_d[l],
                   ssm_norm_w=ssm_norm_w[l])
        mod = ada_modulation(cond, w_ada[l], b_ada[l])
        m_lat = mod[:B_].reshape(B_, 6, 1, D)
        m_ctx = mod[B_:B_ + 1].reshape(1, 6, 1, D)
        w_in_b = jnp.pad(w_in[l], ((0, 0), (0, dt_pad))).astype(BF16)
        w_out_b = w_out[l].astype(BF16)

        p_l = modulated_matmul(x_lat, m_lat[:, 0], m_lat[:, 1], w_in_b, splits, in_dtypes, L, L, TM)
        p_c = modulated_matmul(x_ctx, m_ctx[:, 0], m_ctx[:, 1], w_in_b, splits, in_dtypes, B_ * C_, C_, TM)
        p_l = [a.reshape(B_, L, -1) for a in p_l]
        p_c = [a.reshape(B_, C_, -1) for a in p_c]
        y_l, y_c = _mixers(p_l, p_c, prm, not last)

        x_lat = proj_residual_ln([a.reshape(B_ * L, -1) for a in y_l], x_lat, m_lat[:, 2], w_out_b,
                                 ln1_g[l], ln1_b[l], L, OUT_PROJ_TILE)
        wq_b = peer_w_q[l].astype(BF16)
        keys_b = peer_sub_keys[l].reshape(2 * PEER_HEADS, PEER_NK, PEER_HALF).astype(BF16)
        u_b = peer_u[l].astype(BF16)
        vt_b = peer_v_tiles(peer_v[l])
        x_lat = peer_sublayer(x_lat, m_lat[:, 3], m_lat[:, 4], m_lat[:, 5], wq_b, keys_b, u_b, vt_b,
                              ln2_g[l], ln2_b[l], L)
        if not last:
            x_ctx = proj_residual_ln([a.reshape(B_ * C_, -1) for a in y_c], x_ctx, m_ctx[:, 2], w_out_b,
                                     ln1_g[l], ln1_b[l], B_ * C_, OUT_PROJ_TILE)
            x_ctx = peer_sublayer(x_ctx, m_ctx[:, 3], m_ctx[:, 4], m_ctx[:, 5], wq_b, keys_b, u_b, vt_b,
                                  ln2_g[l], ln2_b[l], B_ * C_)
    return x_lat.reshape(B_, L, D)
```
